```python
import numpy as np
import jax, jax.numpy as jnp
from jax import lax

D_MODEL = 2048
BATCH = 2
SEQ = 16384
DEPTH = 2

HEAD_DIM = 128
ROPE_THETA = 10000.0
NORM_EPS = 1e-6
NEG_INF = -1e30
BIG = 1e9
Q_BLOCK = 128
NSA_HEADS = 4
MLA_HEADS = 6
DIL_HEADS = 6
MIX_WIDTH = (NSA_HEADS + MLA_HEADS + DIL_HEADS) * HEAD_DIM
CMP_LEN = 32
CMP_STRIDE = 16
CMP_HIDDEN = 128
SLC_BLOCK = 64
SLC_TOPK = 16
NSA_WINDOW = 512
Q_LORA = 512
KV_LORA = 512
QK_NOPE = 128
QK_ROPE = 64
V_HEAD = 128
DIL_PATTERNS = ((128, 1), (512, 4), (2048, 16))
MEM_LEN = 256
XATTN_HEADS = 4
XATTN_DIM = XATTN_HEADS * HEAD_DIM
D_FF = 4 * D_MODEL
NSA_Q_W = NSA_HEADS * HEAD_DIM
NSA_KV_W = 6 * HEAD_DIM
NSA_GATE_W = 3 * NSA_HEADS
DIL_W = 3 * DIL_HEADS * HEAD_DIM
IN_SPLITS = (NSA_Q_W, NSA_KV_W, NSA_GATE_W, Q_LORA, KV_LORA, QK_ROPE, DIL_W)
D_IN = NSA_Q_W + NSA_KV_W + NSA_GATE_W + Q_LORA + KV_LORA + QK_ROPE + DIL_W

kernel_name = 'hybrid_nsa_mla_dilated_trunk'


def rms_norm(x, g):
    xf = x.astype(jnp.float32)
    y = xf * lax.rsqrt(jnp.mean(xf * xf, axis=-1, keepdims=True) + NORM_EPS)
    return (y * g.astype(jnp.float32)).astype(x.dtype)


def rope_tables(pos, dim):
    inv = ROPE_THETA ** (-jnp.arange(0, dim, 2, dtype=jnp.float32) / dim)
    ang = pos.astype(jnp.float32)[..., None] * inv
    return jnp.cos(ang), jnp.sin(ang)


def apply_rope(x, cos, sin):
    xf = x.astype(jnp.float32)
    x1, x2 = jnp.split(xf, 2, axis=-1)
    c, s = cos[:, None], sin[:, None]
    return jnp.concatenate([x1 * c - x2 * s, x2 * c + x1 * s], axis=-1).astype(x.dtype)


def heads(t, n):
    B, S, _ = t.shape
    return t.reshape(B, S, n, -1).transpose(0, 2, 1, 3)


def merge_heads(t):
    B, H, S, d = t.shape
    return t.transpose(0, 2, 1, 3).reshape(B, S, H * d)


def banded_attention(q, k, v, window, block=Q_BLOCK):
    L, dh = q.shape[-2], q.shape[-1]
    nb = -(-L // block)
    Lp = nb * block
    n_prev = -(-window // block)

    def to_blocks(t):
        t = jnp.pad(t, [(0, 0)] * (t.ndim - 2) + [(0, Lp - L), (0, 0)])
        return t.reshape(t.shape[:-2] + (nb, block, dh))

    def band(t):
        tp = jnp.pad(t, [(0, 0)] * (t.ndim - 3) + [(n_prev, 0), (0, 0), (0, 0)])
        return jnp.concatenate([tp[..., j:j + nb, :, :] for j in range(n_prev + 1)], axis=-2)

    qb = to_blocks(q)
    kband = band(to_blocks(k))
    vband = band(to_blocks(v))
    qpos = jnp.arange(Lp).reshape(nb, block)[:, :, None]
    kpos = (jnp.arange(nb)[:, None] - n_prev) * block + jnp.arange((n_prev + 1) * block)[None, :]
    dist = qpos - kpos[:, None, :]
    mask = (dist >= 0) & (dist <= window) & (kpos[:, None, :] >= 0)
    s = jnp.einsum('...nqd,...nkd->...nqk', qb, kband, preferred_element_type=jnp.float32) * dh ** -0.5
    s = jnp.where(mask, s, NEG_INF)
    lse = jax.nn.logsumexp(s, axis=-1)
    p = jnp.exp(s - lse[..., None])
    o = jnp.einsum('...nqk,...nkd->...nqd', p.astype(v.dtype), vband)
    o = o.reshape(o.shape[:-3] + (Lp, dh))[..., :L, :]
    lse = lse.reshape(lse.shape[:-2] + (Lp,))[..., :L]
    return o, lse


def causal_attention_blocked(q, k, v):
    B, H, S, dq = q.shape
    nb = S // Q_BLOCK
    qb = q.reshape(B, H, nb, Q_BLOCK, dq).transpose(2, 0, 1, 3, 4)
    kpos = jnp.arange(S)

    def one(args):
        qi, b = args
        s = jnp.einsum('bhqd,bhkd->bhqk', qi, k, preferred_element_type=jnp.float32) * dq ** -0.5
        tq = b * Q_BLOCK + jnp.arange(Q_BLOCK)
        s = jnp.where(kpos[None, :] <= tq[:, None], s, NEG_INF)
        p = jax.nn.softmax(s, axis=-1)
        return jnp.einsum('bhqk,bhkd->bhqd', p.astype(v.dtype), v)

    o = lax.map(one, (qb, jnp.arange(nb)))
    return o.transpose(1, 2, 0, 3, 4).reshape(B, H, S, v.shape[-1])


def nsa_attention(q, kv, gate_logits, positions, cos, sin, cmp_pos_emb, w_ck1, w_ck2, w_cv1, w_cv2):
    B, H, S, dh = q.shape
    k_cr, v_cr, k_sr, v_sr, k_wr, v_wr = jnp.split(kv, 6, axis=-1)

    n_cmp = (S - CMP_LEN) // CMP_STRIDE + 1
    cmp_idx = np.arange(n_cmp)[:, None] * CMP_STRIDE + np.arange(CMP_LEN)[None, :]
    cmp_end = cmp_idx[:, -1]

    def compress(t, w1, w2):
        blk = t[:, cmp_idx] + cmp_pos_emb
        return jax.nn.gelu(blk.reshape(B, n_cmp, CMP_LEN * dh) @ w1) @ w2

    cos_c, sin_c = rope_tables(positions[:, cmp_end], dh)
    k_c = apply_rope(compress(k_cr, w_ck1, w_ck2)[:, None], cos_c, sin_c)[:, 0]
    v_c = compress(v_cr, w_cv1, w_cv2)

    n_slc = S // SLC_BLOCK
    n_sel = min(SLC_TOPK, n_slc)
    k_s = apply_rope(k_sr[:, None], cos, sin)[:, 0].reshape(B, n_slc, SLC_BLOCK, dh)
    v_s = v_sr.reshape(B, n_slc, SLC_BLOCK, dh)
    slc_start = np.arange(n_slc) * SLC_BLOCK
    overlap = ((cmp_idx[:, 0, None] < slc_start[None, :] + SLC_BLOCK) &
               (cmp_end[:, None] >= slc_start[None, :])).astype(np.float32)
    cmp_end_j = jnp.asarray(cmp_end)
    blk_ids = jnp.arange(n_slc)
    bidx = jnp.arange(B)[:, None, None]
    scale = dh ** -0.5
    nqb = S // Q_BLOCK
    qb = q.reshape(B, H, nqb, Q_BLOCK, dh).transpose(2, 0, 1, 3, 4)

    def one(args):
        qi, b = args
        tq = b * Q_BLOCK + jnp.arange(Q_BLOCK)
        s = jnp.einsum('bhqd,bcd->bhqc', qi, k_c, preferred_element_type=jnp.float32) * scale
        cmask = cmp_end_j[None, :] <= tq[:, None]
        p = jnp.where(cmask, jax.nn.softmax(jnp.where(cmask, s, NEG_INF), axis=-1), 0.0)
        o_c = jnp.einsum('bhqc,bcd->bhqd', p.astype(v_c.dtype), v_c)
        imp = jnp.einsum('bhqc,cj->bqj', p, overlap)
        cur = tq // SLC_BLOCK
        forced = (blk_ids[None, :] == 0) | (blk_ids[None, :] == cur[:, None]) | (blk_ids[None, :] == cur[:, None] - 1)
        valid = blk_ids[None, :] <= cur[:, None]
        imp = jnp.where(forced, BIG, jnp.where(valid, imp, -BIG))
        _, sel = lax.top_k(imp, n_sel)
        ks = k_s[bidx, sel]
        vs = v_s[bidx, sel]
        tok = sel[..., None] * SLC_BLOCK + jnp.arange(SLC_BLOCK)
        smask = (tok <= tq[None, :, None, None]).reshape(B, 1, Q_BLOCK, n_sel * SLC_BLOCK)
        s2 = jnp.einsum('bhqd,bqnkd->bhqnk', qi, ks, preferred_element_type=jnp.float32)
        s2 = jnp.where(smask, s2.reshape(B, H, Q_BLOCK, n_sel * SLC_BLOCK) * scale, NEG_INF)
        p2 = jax.nn.softmax(s2, axis=-1).reshape(B, H, Q_BLOCK, n_sel, SLC_BLOCK)
        o_s = jnp.einsum('bhqnk,bqnkd->bhqd', p2.astype(vs.dtype), vs)
        return o_c, o_s

    o_c, o_s = lax.map(one, (qb, jnp.arange(nqb)))
    o_c = o_c.transpose(1, 2, 0, 3, 4).reshape(B, H, S, dh)
    o_s = o_s.transpose(1, 2, 0, 3, 4).reshape(B, H, S, dh)

    k_w = jnp.broadcast_to(apply_rope(k_wr[:, None], cos, sin), (B, H, S, dh))
    v_w = jnp.broadcast_to(v_wr[:, None], (B, H, S, dh))
    o_w, _ = banded_attention(q, k_w, v_w, NSA_WINDOW)

    g = jax.nn.sigmoid(gate_logits).reshape(B, S, H, 3).transpose(0, 2, 1, 3)
    return g[..., 0:1] * o_c + g[..., 1:2] * o_s + g[..., 2:3] * o_w


def dilated_attention(q, k, v):
    B, H, S, dh = q.shape
    outs, lses = [], []
    for window, dil in DIL_PATTERNS:
        L = S // dil

        def sub(t):
            return t.reshape(B, H, L, dil, dh).transpose(0, 1, 3, 2, 4)

        o, lse = banded_attention(sub(q), sub(k), sub(v), window // dil)
        outs.append(o.transpose(0, 1, 3, 2, 4).reshape(B, H, S, dh))
        lses.append(lse.transpose(0, 1, 3, 2).reshape(B, H, S))
    w = jax.nn.softmax(jnp.stack(lses, axis=0), axis=0)
    o = w[0][..., None] * outs[0].astype(jnp.float32) + w[1][..., None] * outs[1].astype(jnp.float32) \
        + w[2][..., None] * outs[2].astype(jnp.float32)
    return o.astype(q.dtype)


def hybrid_mixer(h, positions, cos, sin, cos_r, sin_r, w_in, cmp_pos_emb, w_cmp_k1, w_cmp_k2, w_cmp_v1,
                 w_cmp_v2, g_q_lora, g_kv_lora, w_uq, w_ukv, w_out):
    proj = h @ w_in
    cuts = [int(c) for c in np.cumsum(IN_SPLITS)[:-1]]
    nsa_q, nsa_kv, nsa_g, cq, ckv, kr, dil_qkv = jnp.split(proj, cuts, axis=-1)

    q_a = apply_rope(heads(nsa_q, NSA_HEADS), cos, sin)
    o_a = nsa_attention(q_a, nsa_kv, nsa_g, positions, cos, sin, cmp_pos_emb, w_cmp_k1, w_cmp_k2, w_cmp_v1, w_cmp_v2)

    qm = heads(rms_norm(cq, g_q_lora) @ w_uq, MLA_HEADS)
    q_pe = apply_rope(qm[..., QK_NOPE:], cos_r, sin_r)
    kvm = heads(rms_norm(ckv, g_kv_lora) @ w_ukv, MLA_HEADS)
    k_nope, v_m = kvm[..., :QK_NOPE], kvm[..., QK_NOPE:]
    k_pe = apply_rope(kr[:, None], cos_r, sin_r)
    q_m = jnp.concatenate([qm[..., :QK_NOPE], q_pe], axis=-1)
    k_m = jnp.concatenate([k_nope, jnp.broadcast_to(k_pe, k_nope.shape[:-1] + (QK_ROPE,))], axis=-1)
    o_b = causal_attention_blocked(q_m, k_m, v_m)

    dq, dk, dv = jnp.split(dil_qkv, 3, axis=-1)
    o_c = dilated_attention(apply_rope(heads(dq, DIL_HEADS), cos, sin),
                            apply_rope(heads(dk, DIL_HEADS), cos, sin),
                            heads(dv, DIL_HEADS))

    o = jnp.concatenate([o_a, o_b, o_c], axis=1)
    return merge_heads(o) @ w_out


def memory_cross_attention(h, mem_n, w_xq, w_xkv, w_xo):
    q = heads(h @ w_xq, XATTN_HEADS)
    k, v = jnp.split(mem_n @ w_xkv, 2, axis=-1)
    k, v = heads(k, XATTN_HEADS), heads(v, XATTN_HEADS)
    s = jnp.einsum('bhqd,bhkd->bhqk', q, k, preferred_element_type=jnp.float32) * HEAD_DIM ** -0.5
    p = jax.nn.softmax(s, axis=-1)
    o = jnp.einsum('bhqk,bhkd->bhqd', p.astype(v.dtype), v)
    return merge_heads(o) @ w_xo


def squared_relu_mlp(h, w_up, w_down):
    return jnp.square(jax.nn.relu(h @ w_up)) @ w_down


def setup_inputs(seed: int = 0) -> dict:
    key = jax.random.key(seed)
    keys = list(jax.random.split(key, 32))
    it = iter(keys)

    def w(shape, fan_in):
        return jax.random.normal(next(it), shape, jnp.float32) * fan_in ** -0.5

    def gain(n):
        return 1.0 + 0.05 * jax.random.normal(next(it), (DEPTH, n), jnp.float32)

    x = jax.random.normal(next(it), (BATCH, SEQ, D_MODEL), jnp.float32)
    mem = jax.random.normal(next(it), (BATCH, MEM_LEN, D_MODEL), jnp.float32)
    offset = jax.random.randint(next(it), (BATCH, 1), 0, 4096, dtype=jnp.int32)
    positions = offset + jnp.arange(SEQ, dtype=jnp.int32)[None, :]
    return {
        'x': x,
        'mem': mem,
        'positions': positions,
        'g_mix_pre': gain(D_MODEL),
        'w_in': w((DEPTH, D_MODEL, D_IN), D_MODEL),
        'cmp_pos_emb': w((DEPTH, CMP_LEN, HEAD_DIM), 4),
        'w_cmp_k1': w((DEPTH, CMP_LEN * HEAD_DIM, CMP_HIDDEN), CMP_LEN * HEAD_DIM),
        'w_cmp_k2': w((DEPTH, CMP_HIDDEN, HEAD_DIM), CMP_HIDDEN),
        'w_cmp_v1': w((DEPTH, CMP_LEN * HEAD_DIM, CMP_HIDDEN), CMP_LEN * HEAD_DIM),
        'w_cmp_v2': w((DEPTH, CMP_HIDDEN, HEAD_DIM), CMP_HIDDEN),
        'g_q_lora': gain(Q_LORA),
        'g_kv_lora': gain(KV_LORA),
        'w_uq': w((DEPTH, Q_LORA, MLA_HEADS * (QK_NOPE + QK_ROPE)), Q_LORA),
        'w_ukv': w((DEPTH, KV_LORA, MLA_HEADS * (QK_NOPE + V_HEAD)), KV_LORA),
        'w_out': w((DEPTH, MIX_WIDTH, D_MODEL), MIX_WIDTH),
        'g_mix_post': gain(D_MODEL),
        'g_mem_pre': gain(D_MODEL),
        'g_mem_kv': gain(D_MODEL),
        'w_xq': w((DEPTH, D_MODEL, XATTN_DIM), D_MODEL),
        'w_xkv': w((DEPTH, D_MODEL, 2 * XATTN_DIM), D_MODEL),
        'w_xo': w((DEPTH, XATTN_DIM, D_MODEL), XATTN_DIM),
        'g_mem_post': gain(D_MODEL),
        'g_mlp_pre': gain(D_MODEL),
        'w_up': w((DEPTH, D_MODEL, D_FF), D_MODEL),
        'w_down': w((DEPTH, D_FF, D_MODEL), D_FF),
        'g_mlp_post': gain(D_MODEL),
    }


def reference(x, mem, positions, g_mix_pre, w_in, cmp_pos_emb, w_cmp_k1, w_cmp_k2, w_cmp_v1, w_cmp_v2,
              g_q_lora, g_kv_lora, w_uq, w_ukv, w_out, g_mix_post, g_mem_pre, g_mem_kv, w_xq, w_xkv, w_xo,
              g_mem_post, g_mlp_pre, w_up, w_down, g_mlp_post):
    cos, sin = rope_tables(positions, HEAD_DIM)
    cos_r, sin_r = rope_tables(positions, QK_ROPE)
    h = x
    for l in range(DEPTH):
        y = hybrid_mixer(rms_norm(h, g_mix_pre[l]), positions, cos, sin, cos_r, sin_r, w_in[l], cmp_pos_emb[l],
                         w_cmp_k1[l], w_cmp_k2[l], w_cmp_v1[l], w_cmp_v2[l], g_q_lora[l], g_kv_lora[l],
                         w_uq[l], w_ukv[l], w_out[l])
        h = h + rms_norm(y, g_mix_post[l])
        y = memory_cross_attention(rms_norm(h, g_mem_pre[l]), rms_norm(mem, g_mem_kv[l]), w_xq[l], w_xkv[l], w_xo[l])
        h = h + rms_norm(y, g_mem_post[l])
        y = squared_relu_mlp(rms_norm(h, g_mlp_pre[l]), w_up[l], w_down[l])
        h = h + rms_norm(y, g_mlp_post[l])
    return h
```

```python
import functools

import numpy as np
import jax
import jax.numpy as jnp
from jax import lax
from jax.experimental import pallas as pl
from jax.experimental.pallas import tpu as pltpu

F32 = jnp.float32
BF16 = jnp.bfloat16

LANE = 128
VMEM_LIMIT = 56 * 1024 * 1024

HEAD_DIM = 128
ROPE_THETA = 10000.0
NORM_EPS = 1e-6
NEG_INF = -1e30
BIG = 1e9
NSA_HEADS = 4
MLA_HEADS = 6
DIL_HEADS = 6
CMP_LEN = 32
CMP_STRIDE = 16
SLC_BLOCK = 64
SLC_TOPK = 16
NSA_WINDOW = 512
Q_LORA = 512
KV_LORA = 512
QK_NOPE = 128
QK_ROPE = 64
DIL_PATTERNS = ((128, 1), (512, 4), (2048, 16))
XATTN_HEADS = 4
PEN_CHUNK = LANE * SLC_BLOCK


def _cparams(n_grid):
    return pltpu.CompilerParams(
        dimension_semantics=("arbitrary",) * n_grid, vmem_limit_bytes=VMEM_LIMIT)


def _rms(x, g):
    return x * lax.rsqrt(jnp.mean(x * x, axis=-1, keepdims=True) + NORM_EPS) * g


def _dot_t(a, b):
    return lax.dot_general(a, b, (((1,), (1,)), ((), ())), preferred_element_type=F32)


def _proj_kernel(out_plan, has_norm, n_tab, chunk, *refs):
    x_ref, g_ref, w_ref = refs[:3]
    tab_refs = refs[3:3 + n_tab]
    out_refs = refs[3 + n_tab:]
    x = x_ref[...].astype(F32)
    if has_norm:
        x = _rms(x, g_ref[...])
    xb = x.astype(BF16)
    n_cols = w_ref.shape[1]
    flat = []
    for oi, (modes, scale) in enumerate(out_plan):
        for k, mode in enumerate(modes):
            flat.append((oi, k * LANE, mode, scale))
    for c0 in range(0, n_cols, chunk):
        c1 = min(c0 + chunk, n_cols)
        acc = jnp.dot(xb, w_ref[:, c0:c1], preferred_element_type=F32)
        for s in range((c1 - c0) // LANE):
            oi, off, mode, scale = flat[c0 // LANE + s]
            a = acc[:, s * LANE:(s + 1) * LANE]
            if mode == "rope":
                a = a * tab_refs[0][...] + pltpu.roll(a, LANE // 2, 1) * tab_refs[1][...]
            elif mode == "rope_r":
                a = a * tab_refs[2][...] + pltpu.roll(a, LANE // 2, 1) * tab_refs[3][...]
            if scale != 1.0:
                a = a * scale
            out_refs[oi][:, off:off + LANE] = a.astype(out_refs[oi].dtype)


def _proj(x, g, w, out_plan, out_dtypes, tabs=(), bm=512, chunk=512, name="proj"):
    T, K = x.shape
    N = w.shape[1]
    assert T % bm == 0 and N % LANE == 0
    assert sum(len(m) for m, _ in out_plan) * LANE == N
    has_norm = g is not None
    if g is None:
        g = jnp.ones((1, K), F32)
    in_specs = [pl.BlockSpec((bm, K), lambda i: (i, 0)),
                pl.BlockSpec((1, K), lambda i: (0, 0)),
                pl.BlockSpec((K, N), lambda i: (0, 0))]
    in_specs += [pl.BlockSpec((bm, LANE), lambda i: (i, 0)) for _ in tabs]
    out_shape = [jax.ShapeDtypeStruct((T, len(m) * LANE), dt)
                 for (m, _), dt in zip(out_plan, out_dtypes)]
    out_specs = [pl.BlockSpec((bm, len(m) * LANE), lambda i: (i, 0)) for m, _ in out_plan]
    return pl.pallas_call(
        functools.partial(_proj_kernel, out_plan, has_norm, len(tabs), chunk),
        out_shape=out_shape, grid=(T // bm,), in_specs=in_specs, out_specs=out_specs,
        compiler_params=_cparams(1), name=name,
    )(x, g, w, *tabs)


def _mlp_up_kernel(x_ref, g_ref, w_ref, o_ref, xn_ref):
    @pl.when(pl.program_id(1) == 0)
    def _():
        xn_ref[...] = _rms(x_ref[...], g_ref[...]).astype(BF16)
    a = jnp.dot(xn_ref[...], w_ref[...], preferred_element_type=F32)
    a = jnp.maximum(a, 0.0)
    o_ref[...] = (a * a).astype(o_ref.dtype)


def _mlp_up(x, g, w, bm=512, bn=1024):
    T, K = x.shape
    N = w.shape[1]
    return pl.pallas_call(
        _mlp_up_kernel,
        out_shape=jax.ShapeDtypeStruct((T, N), BF16),
        grid=(T // bm, N // bn),
        in_specs=[pl.BlockSpec((bm, K), lambda i, j: (i, 0)),
                  pl.BlockSpec((1, K), lambda i, j: (0, 0)),
                  pl.BlockSpec((K, bn), lambda i, j: (0, j))],
        out_specs=pl.BlockSpec((bm, bn), lambda i, j: (i, j)),
        scratch_shapes=[pltpu.VMEM((bm, K), BF16)],
        compiler_params=_cparams(2), name="mlp_up",
    )(x, g, w)


def _out_proj_kernel(n_k, a_ref, w_ref, h_ref, g_ref, o_ref, acc_ref):
    k = pl.program_id(1)
    part = jnp.dot(a_ref[...], w_ref[...], preferred_element_type=F32)

    def finish(y):
        o_ref[...] = h_ref[...] + _rms(y, g_ref[...])

    if n_k == 1:
        finish(part)
    else:
        @pl.when(k == 0)
        def _():
            acc_ref[...] = part

        @pl.when(jnp.logical_and(k > 0, k < n_k - 1))
        def _():
            acc_ref[...] += part

        @pl.when(k == n_k - 1)
        def _():
            finish(acc_ref[...] + part)


def _out_proj(a, w, h, g, bm=512, bk=2048, name="out_proj"):
    T, K = a.shape
    N = w.shape[1]
    bk = min(bk, K)
    n_k = K // bk
    return pl.pallas_call(
        functools.partial(_out_proj_kernel, n_k),
        out_shape=jax.ShapeDtypeStruct((T, N), F32),
        grid=(T // bm, n_k),
        in_specs=[pl.BlockSpec((bm, bk), lambda i, k: (i, k)),
                  pl.BlockSpec((bk, N), lambda i, k: (k, 0)),
                  pl.BlockSpec((bm, N), lambda i, k: (i, 0)),
                  pl.BlockSpec((1, N), lambda i, k: (0, 0))],
        out_specs=pl.BlockSpec((bm, N), lambda i, k: (i, 0)),
        scratch_shapes=[pltpu.VMEM((bm, N) if n_k > 1 else (8, LANE), F32)],
        compiler_params=_cparams(2), name=name,
    )(a, w, h, g)


def _online_step(carry, s, v):
    m, l, acc = carry
    m_new = jnp.maximum(m, jnp.max(s, axis=-1, keepdims=True))
    alpha = jnp.exp(m - m_new)
    p = jnp.exp(s - m_new)
    l = alpha * l + jnp.sum(p, axis=-1, keepdims=True)
    acc = alpha * acc + jnp.dot(p.astype(BF16), v, preferred_element_type=F32)
    return m_new, l, acc


def _online_init(rows, dv):
    return (jnp.full((rows, 1), NEG_INF, F32), jnp.zeros((rows, 1), F32),
            jnp.zeros((rows, dv), F32))


def _mla_kernel(tq, q_ref, k_ref, v_ref, o_ref):
    i = pl.program_id(2)
    q = q_ref[...]

    def tile(t, carry, diagonal):
        start = pl.multiple_of(t * tq, tq)
        s = _dot_t(q, k_ref[pl.ds(start, tq), :])
        if diagonal:
            row = lax.broadcasted_iota(jnp.int32, (tq, tq), 0)
            col = lax.broadcasted_iota(jnp.int32, (tq, tq), 1)
            s = jnp.where(col <= row, s, NEG_INF)
        return _online_step(carry, s, v_ref[pl.ds(start, tq), :])

    carry = lax.fori_loop(0, i, lambda t, c: tile(t, c, False), _online_init(tq, HEAD_DIM))
    _, l, acc = tile(i, carry, True)
    o_ref[...] = (acc / l).astype(o_ref.dtype)


def _mla_attention(q, k, v, n_heads, tq=512):
    B, S, _ = q.shape
    dq = q.shape[2] // n_heads
    return pl.pallas_call(
        functools.partial(_mla_kernel, tq),
        out_shape=jax.ShapeDtypeStruct((B, S, n_heads * HEAD_DIM), BF16),
        grid=(B, n_heads, S // tq),
        in_specs=[pl.BlockSpec((None, tq, dq), lambda b, h, i: (b, i, h)),
                  pl.BlockSpec((None, S, dq), lambda b, h, i: (b, 0, h)),
                  pl.BlockSpec((None, S, HEAD_DIM), lambda b, h, i: (b, 0, h))],
        out_specs=pl.BlockSpec((None, tq, HEAD_DIM), lambda b, h, i: (b, i, h)),
        compiler_params=_cparams(3), name="mla_attention",
    )(q, k, v)


def _band_block(q, kwin, vwin, window, base):
    R = q.shape[0]
    C = window + LANE
    s = _dot_t(q, kwin)
    r = lax.broadcasted_iota(jnp.int32, (R, C), 0) & (LANE - 1)
    c = lax.broadcasted_iota(jnp.int32, (R, C), 1)
    mask = (c >= r) & (c <= r + window) & (c >= window - base)
    s = jnp.where(mask, s, NEG_INF)
    m = jnp.max(s, axis=-1, keepdims=True)
    p = jnp.exp(s - m)
    l = jnp.sum(p, axis=-1, keepdims=True)
    o = jnp.dot(p.astype(BF16), vwin, preferred_element_type=F32) / l
    return o, m + jnp.log(l)


def _stack_heads(x, n):
    return jnp.concatenate([x[:, h * LANE:(h + 1) * LANE] for h in range(n)], axis=0)


def _nsa_window_kernel(tq, window, q_ref, kp_ref, kc_ref, vp_ref, vc_ref, o_ref):
    i = pl.program_id(1)
    kwin = jnp.concatenate([kp_ref[...], kc_ref[...]], axis=0)
    vwin = jnp.concatenate([vp_ref[...], vc_ref[...]], axis=0)
    for j in range(tq // LANE):
        q = _stack_heads(q_ref[j * LANE:(j + 1) * LANE, :], NSA_HEADS)
        lo = j * LANE
        o, _ = _band_block(q, kwin[lo:lo + window + LANE], vwin[lo:lo + window + LANE],
                           window, i * tq + lo)
        for h in range(NSA_HEADS):
            o_ref[lo:lo + LANE, h * LANE:(h + 1) * LANE] = (
                o[h * LANE:(h + 1) * LANE].astype(o_ref.dtype))


def _nsa_window_attention(q, k, v, window):
    B, S, W = q.shape
    tq = window
    prev = lambda b, i: (b, jnp.maximum(i - 1, 0), 0)
    cur = lambda b, i: (b, i, 0)
    return pl.pallas_call(
        functools.partial(_nsa_window_kernel, tq, window),
        out_shape=jax.ShapeDtypeStruct((B, S, W), BF16),
        grid=(B, S // tq),
        in_specs=[pl.BlockSpec((None, tq, W), cur),
                  pl.BlockSpec((None, window, HEAD_DIM), prev),
                  pl.BlockSpec((None, tq, HEAD_DIM), cur),
                  pl.BlockSpec((None, window, HEAD_DIM), prev),
                  pl.BlockSpec((None, tq, HEAD_DIM), cur)],
        out_specs=pl.BlockSpec((None, tq, W), cur),
        compiler_params=_cparams(2), name="nsa_window_attention",
    )(q, k, k, v, v)


def _dilated_kernel(tq, window, n_heads, q_ref, kp_ref, kc_ref, vp_ref, vc_ref, o_ref, lse_ref):
    i = pl.program_id(2)
    lse_ref[...] = jnp.zeros(lse_ref.shape, F32)
    for h in range(n_heads):
        hs = slice(h * LANE, (h + 1) * LANE)
        kwin = jnp.concatenate([kp_ref[:, hs], kc_ref[:, hs]], axis=0)
        vwin = jnp.concatenate([vp_ref[:, hs], vc_ref[:, hs]], axis=0)
        for j in range(tq // LANE):
            lo = j * LANE
            o, lse = _band_block(q_ref[lo:lo + LANE, hs], kwin[lo:lo + window + LANE],
                                 vwin[lo:lo + window + LANE], window, i * tq + lo)
            o_ref[lo:lo + LANE, hs] = o.astype(o_ref.dtype)
            lse_ref[lo:lo + LANE, h:h + 1] = lse


def _dilated_pattern(q, k, v, window, dil, n_heads, tq=512):
    B, S, W = q.shape
    L = S // dil
    tq = min(tq, L)
    w = window // dil
    assert w == LANE and L % tq == 0
    view = lambda t: t.reshape(B, L, dil * W)
    sub = tq // w
    prev = lambda b, r, i: (b, jnp.maximum(i * sub - 1, 0), r)
    cur = lambda b, r, i: (b, i, r)
    o, lse = pl.pallas_call(
        functools.partial(_dilated_kernel, tq, w, n_heads),
        out_shape=[jax.ShapeDtypeStruct((B, L, dil * W), BF16),
                   jax.ShapeDtypeStruct((B, L, dil * LANE), F32)],
        grid=(B, dil, L // tq),
        in_specs=[pl.BlockSpec((None, tq, W), cur),
                  pl.BlockSpec((None, w, W), prev),
                  pl.BlockSpec((None, tq, W), cur),
                  pl.BlockSpec((None, w, W), prev),
                  pl.BlockSpec((None, tq, W), cur)],
        out_specs=[pl.BlockSpec((None, tq, W), cur),
                   pl.BlockSpec((None, tq, LANE), cur)],
        compiler_params=_cparams(3), name="dilated_attention_d%d" % dil,
    )(view(q), view(k), view(k), view(v), view(v))
    return o.reshape(B, S, W), lse.reshape(B, S, LANE)


def _gelu_tanh(x):
    return 0.5 * x * (1.0 + jnp.tanh(0.7978845608028654 * (x + 0.044715 * (x * x * x))))


def _compress_kernel(rope, t_ref, pe_ref, w1a_ref, w1b_ref, w2_ref, cos_ref, sin_ref, o_ref):
    t = t_ref[...].astype(F32)
    n = t.shape[0]
    first = jnp.dot((t + pe_ref[0:1, :]).astype(BF16), w1a_ref[...], preferred_element_type=F32)
    second = jnp.dot((t + pe_ref[1:2, :]).astype(BF16), w1b_ref[...], preferred_element_type=F32)
    hid = first + pltpu.roll(second, n - 1, 0)
    out = jnp.dot(_gelu_tanh(hid).astype(BF16), w2_ref[...], preferred_element_type=F32)
    if rope:
        out = out * cos_ref[...] + pltpu.roll(out, LANE // 2, 1) * sin_ref[...]
    o_ref[...] = out.astype(o_ref.dtype)


def _compress(t, pe2, w1a, w1b, w2, cos_c, sin_c, rope):
    B, n, K = t.shape
    whole = lambda b: (0, 0)
    per_b = lambda b: (b, 0, 0)
    return pl.pallas_call(
        functools.partial(_compress_kernel, rope),
        out_shape=jax.ShapeDtypeStruct((B, n, HEAD_DIM), BF16),
        grid=(B,),
        in_specs=[pl.BlockSpec((None, n, K), per_b),
                  pl.BlockSpec((2, K), whole),
                  pl.BlockSpec((K, HEAD_DIM), whole),
                  pl.BlockSpec((K, HEAD_DIM), whole),
                  pl.BlockSpec((HEAD_DIM, HEAD_DIM), whole),
                  pl.BlockSpec((None, n, HEAD_DIM), per_b),
                  pl.BlockSpec((None, n, HEAD_DIM), per_b)],
        out_specs=pl.BlockSpec((None, n, HEAD_DIM), per_b),
        compiler_params=_cparams(1), name="nsa_compress",
    )(t, pe2, w1a, w1b, w2, cos_c, sin_c)


def _nsa_cmp_kernel(tq, n_slc, n_sel, q_ref, kc_ref, vc_ref, o_ref, pen_ref):
    i = pl.program_id(1)
    n_cmp = kc_ref.shape[0]
    per_blk = SLC_BLOCK // CMP_STRIDE
    slc_shift = n_slc.bit_length() - 1
    q = _stack_heads(q_ref[...], NSA_HEADS)
    s = _dot_t(q, kc_ref[...])
    R = NSA_HEADS * tq
    row = lax.broadcasted_iota(jnp.int32, (R, n_cmp), 0) & (tq - 1)
    col = lax.broadcasted_iota(jnp.int32, (R, n_cmp), 1)
    j_of = col & (n_slc - 1)
    r_of = col >> slc_shift
    cmp_end = (per_blk * j_of + r_of) * CMP_STRIDE + (CMP_LEN - 1)
    cmask = cmp_end <= row + i * tq
    s = jnp.where(cmask, s, NEG_INF)
    m = jnp.max(s, axis=-1, keepdims=True)
    e = jnp.exp(s - m)
    p = jnp.where(cmask, e / jnp.sum(e, axis=-1, keepdims=True), 0.0)
    o = jnp.dot(p.astype(BF16), vc_ref[...], preferred_element_type=F32)
    for h in range(NSA_HEADS):
        o_ref[:, h * LANE:(h + 1) * LANE] = o[h * tq:(h + 1) * tq].astype(o_ref.dtype)

    ph = p[0:tq]
    for h in range(1, NSA_HEADS):
        ph = ph + p[h * tq:(h + 1) * tq]
    groups = [ph[:, r * n_slc:(r + 1) * n_slc] for r in range(per_blk)]
    blk = lax.broadcasted_iota(jnp.int32, (tq, n_slc), 1)
    spill = jnp.where(blk == 0, 0.0, pltpu.roll(groups[per_blk - 1], 1, 1))
    imp = groups[0]
    for r in range(1, per_blk):
        imp = imp + groups[r]
    imp = imp + spill
    tpos = lax.broadcasted_iota(jnp.int32, (tq, n_slc), 0) + i * tq
    cur = tpos >> (SLC_BLOCK.bit_length() - 1)
    forced = (blk == 0) | (blk == cur) | (blk == cur - 1)
    valid = blk <= cur
    work = jnp.where(forced, BIG, jnp.where(valid, imp, -BIG))
    blk_f = blk.astype(F32)
    sel = jnp.zeros((tq, n_slc), F32)
    for _ in range(n_sel):
        mx = jnp.max(work, axis=-1, keepdims=True)
        first = jnp.min(jnp.where(work == mx, blk_f, float(n_slc)), axis=-1, keepdims=True)
        pick = blk_f == first
        sel = jnp.where(pick, 1.0, sel)
        work = jnp.where(pick, -jnp.inf, work)
    pen = jnp.where((sel > 0.0) & valid, 0.0, NEG_INF).astype(pen_ref.dtype)
    for c in range(n_slc // LANE):
        pen_ref[c] = pen[:, c * LANE:(c + 1) * LANE]


def _nsa_compressed(q, k_c, v_c, tq=128):
    B, S, W = q.shape
    n_cmp = k_c.shape[1]
    n_slc = S // SLC_BLOCK
    n_sel = min(SLC_TOPK, n_slc)
    assert n_slc % LANE == 0 and n_slc & (n_slc - 1) == 0
    assert n_cmp == 4 * n_slc and tq & (tq - 1) == 0
    n_pc = n_slc // LANE
    return pl.pallas_call(
        functools.partial(_nsa_cmp_kernel, tq, n_slc, n_sel),
        out_shape=[jax.ShapeDtypeStruct((B, S, W), BF16),
                   jax.ShapeDtypeStruct((B, n_pc, S, LANE), BF16)],
        grid=(B, S // tq),
        in_specs=[pl.BlockSpec((None, tq, W), lambda b, i: (b, i, 0)),
                  pl.BlockSpec((None, n_cmp, HEAD_DIM), lambda b, i: (b, 0, 0)),
                  pl.BlockSpec((None, n_cmp, HEAD_DIM), lambda b, i: (b, 0, 0))],
        out_specs=[pl.BlockSpec((None, tq, W), lambda b, i: (b, i, 0)),
                   pl.BlockSpec((None, n_pc, tq, LANE), lambda b, i: (b, 0, i, 0))],
        compiler_params=_cparams(2), name="nsa_compressed_topk",
    )(q, k_c, v_c)


def _nsa_sel_kernel(tq, tk, q_ref, pen_ref, k_ref, v_ref, o_ref):
    i = pl.program_id(1)
    t0 = i * tq
    q = _stack_heads(q_ref[...], NSA_HEADS)
    R = NSA_HEADS * tq
    last = t0 // tk
    carry = _online_init(R, HEAD_DIM)
    tiles_per_chunk = PEN_CHUNK // tk
    for c in range(pen_ref.shape[0]):
        pen = jnp.concatenate([pen_ref[c]] * NSA_HEADS, axis=0)
        qa = jnp.concatenate([q, pen], axis=1)

        def tile(t, carry, diagonal, qa=qa):
            start = pl.multiple_of(t * tk, tk)
            s = _dot_t(qa, k_ref[pl.ds(start, tk), :])
            if diagonal:
                row = (lax.broadcasted_iota(jnp.int32, (R, tk), 0) & (tq - 1)) + t0
                col = lax.broadcasted_iota(jnp.int32, (R, tk), 1) + start
                s = jnp.where(col <= row, s, NEG_INF)
            return _online_step(carry, s, v_ref[pl.ds(start, tk), :])

        lo = c * tiles_per_chunk
        hi = jnp.minimum(last, lo + tiles_per_chunk)
        carry = lax.fori_loop(lo, hi, lambda t, cr: tile(t, cr, False), carry)
        in_chunk = jnp.logical_and(last >= lo, last < lo + tiles_per_chunk)
        carry = lax.cond(in_chunk, lambda cr: tile(last, cr, True), lambda cr: cr, carry)
    _, l, acc = carry
    o = acc / l
    for h in range(NSA_HEADS):
        o_ref[:, h * LANE:(h + 1) * LANE] = o[h * tq:(h + 1) * tq].astype(o_ref.dtype)


def _nsa_selected(q, pen, k_aug, v, tq=128, tk=512):
    B, S, W = q.shape
    n_pc = pen.shape[1]
    assert tk % tq == 0 and PEN_CHUNK % tk == 0 and tq & (tq - 1) == 0
    return pl.pallas_call(
        functools.partial(_nsa_sel_kernel, tq, tk),
        out_shape=jax.ShapeDtypeStruct((B, S, W), BF16),
        grid=(B, S // tq),
        in_specs=[pl.BlockSpec((None, tq, W), lambda b, i: (b, i, 0)),
                  pl.BlockSpec((None, n_pc, tq, LANE), lambda b, i: (b, 0, i, 0)),
                  pl.BlockSpec((None, S, 2 * HEAD_DIM), lambda b, i: (b, 0, 0)),
                  pl.BlockSpec((None, S, HEAD_DIM), lambda b, i: (b, 0, 0))],
        out_specs=pl.BlockSpec((None, tq, W), lambda b, i: (b, i, 0)),
        compiler_params=_cparams(2), name="nsa_selected_attention",
    )(q, pen, k_aug, v)


def _combine_kernel(oc_ref, os_ref, ow_ref, gate_ref, ob_ref,
                    d0_ref, d1_ref, d2_ref, l0_ref, l1_ref, l2_ref, o_ref):
    g = jax.nn.sigmoid(gate_ref[...])
    for h in range(NSA_HEADS):
        hs = slice(h * LANE, (h + 1) * LANE)
        a = (g[:, 3 * h:3 * h + 1] * oc_ref[:, hs].astype(F32)
             + g[:, 3 * h + 1:3 * h + 2] * os_ref[:, hs].astype(F32)
             + g[:, 3 * h + 2:3 * h + 3] * ow_ref[:, hs].astype(F32))
        o_ref[:, hs] = a.astype(o_ref.dtype)
    off = NSA_HEADS * LANE
    o_ref[:, off:off + MLA_HEADS * LANE] = ob_ref[...]
    off += MLA_HEADS * LANE
    l0, l1, l2 = l0_ref[...], l1_ref[...], l2_ref[...]
    mx = jnp.maximum(jnp.maximum(l0, l1), l2)
    e0, e1, e2 = jnp.exp(l0 - mx), jnp.exp(l1 - mx), jnp.exp(l2 - mx)
    den = e0 + e1 + e2
    w0, w1, w2 = e0 / den, e1 / den, e2 / den
    for h in range(DIL_HEADS):
        hs = slice(h * LANE, (h + 1) * LANE)
        a = (w0[:, h:h + 1] * d0_ref[:, hs].astype(F32)
             + w1[:, h:h + 1] * d1_ref[:, hs].astype(F32)
             + w2[:, h:h + 1] * d2_ref[:, hs].astype(F32))
        o_ref[:, off + h * LANE:off + (h + 1) * LANE] = a.astype(o_ref.dtype)


def _combine(o_c, o_s, o_w, gate, o_b, d_outs, d_lses, bm=512):
    T = o_c.shape[0]
    width = (NSA_HEADS + MLA_HEADS + DIL_HEADS) * LANE
    args = (o_c, o_s, o_w, gate, o_b) + tuple(d_outs) + tuple(d_lses)
    return pl.pallas_call(
        _combine_kernel,
        out_shape=jax.ShapeDtypeStruct((T, width), BF16),
        grid=(T // bm,),
        in_specs=[pl.BlockSpec((bm, a.shape[1]), lambda i: (i, 0)) for a in args],
        out_specs=pl.BlockSpec((bm, width), lambda i: (i, 0)),
        compiler_params=_cparams(1), name="mixer_combine",
    )(*args)


def _xattn_kernel(n_heads, q_ref, k_ref, v_ref, o_ref):
    for h in range(n_heads):
        hs = slice(h * LANE, (h + 1) * LANE)
        s = _dot_t(q_ref[:, hs], k_ref[:, hs])
        m = jnp.max(s, axis=-1, keepdims=True)
        p = jnp.exp(s - m)
        l = jnp.sum(p, axis=-1, keepdims=True)
        o = jnp.dot(p.astype(BF16), v_ref[:, hs], preferred_element_type=F32) / l
        o_ref[:, hs] = o.astype(o_ref.dtype)


def _cross_attention(q, k, v, n_heads, tq=512):
    B, S, W = q.shape
    M = k.shape[1]
    return pl.pallas_call(
        functools.partial(_xattn_kernel, n_heads),
        out_shape=jax.ShapeDtypeStruct((B, S, W), BF16),
        grid=(B, S // tq),
        in_specs=[pl.BlockSpec((None, tq, W), lambda b, i: (b, i, 0)),
                  pl.BlockSpec((None, M, W), lambda b, i: (b, 0, 0)),
                  pl.BlockSpec((None, M, W), lambda b, i: (b, 0, 0))],
        out_specs=pl.BlockSpec((None, tq, W), lambda b, i: (b, i, 0)),
        compiler_params=_cparams(2), name="cross_attention",
    )(q, k, v)


def _rope_tables(pos, dim):
    inv = ROPE_THETA ** (-jnp.arange(0, dim, 2, dtype=F32) / dim)
    ang = pos.astype(F32)[..., None] * inv
    return jnp.cos(ang), jnp.sin(ang)


def _full_tables(pos):
    c, s = _rope_tables(pos, HEAD_DIM)
    return jnp.concatenate([c, c], -1), jnp.concatenate([-s, s], -1)


def _spread_rope64(t):
    z = jnp.zeros(t.shape[:-1] + (QK_ROPE // 2,), t.dtype)
    return jnp.concatenate([t[..., :QK_ROPE // 2], z, t[..., QK_ROPE // 2:], z], -1)


def _small_tables(pos):
    c, s = _rope_tables(pos, QK_ROPE)
    return _spread_rope64(jnp.concatenate([c, c], -1)), _spread_rope64(jnp.concatenate([-s, s], -1))


def _layer(h, mem, tabs, tabs_c, onehot, p):
    B, S, D = h.shape
    T = B * S
    dh = HEAD_DIM
    scale = dh ** -0.5
    h2 = h.reshape(T, D)

    w_in = p["w_in"]
    cuts = np.cumsum([NSA_HEADS * dh, 6 * dh, 3 * NSA_HEADS, Q_LORA, KV_LORA, QK_ROPE])
    w_q, w_kv, w_g, w_cq, w_ckv, w_kr, w_dil = jnp.split(w_in, cuts, axis=1)
    kv = [w_kv[:, k * dh:(k + 1) * dh] for k in range(6)]
    dw = DIL_HEADS * dh
    w_dq, w_dk, w_dv = w_dil[:, :dw], w_dil[:, dw:2 * dw], w_dil[:, 2 * dw:]
    w_rope = jnp.concatenate([w_q, kv[2], kv[4], w_dq, w_dk, _spread_rope64(w_kr)], 1).astype(BF16)
    w_gpad = jnp.pad(w_g, ((0, 0), (0, LANE - w_g.shape[1])))
    w_plain = jnp.concatenate([kv[0], kv[1], kv[3], kv[5], w_gpad, w_cq, w_ckv, w_dv], 1).astype(BF16)

    nsa_q, k_s, k_w, dq, dk, k_pe = _proj(
        h2, p["g_mix_pre"], w_rope,
        [(("rope",) * NSA_HEADS, scale), (("rope",), 1.0), (("rope",), 1.0),
         (("rope",) * DIL_HEADS, scale), (("rope",) * DIL_HEADS, 1.0), (("rope_r",), 1.0)],
        [BF16] * 6, tabs=tabs, name="in_proj_rope")
    k_cr, v_cr, v_s, v_w, gate, cq, ckv, dv = _proj(
        h2, p["g_mix_pre"], w_plain,
        [(("none",), 1.0)] * 4 + [(("none",), 1.0), (("none",) * 4, 1.0), (("none",) * 4, 1.0),
                                  (("none",) * DIL_HEADS, 1.0)],
        [BF16] * 4 + [F32, BF16, BF16, BF16], name="in_proj_plain")

    n_chunk = S // CMP_STRIDE
    half = CMP_LEN // 2
    pe2 = p["cmp_pos_emb"].reshape(2, half * dh)
    cos_c, sin_c = tabs_c

    def compress(t, w1, w2, rope):
        out = _compress(t.reshape(B, n_chunk, CMP_STRIDE * dh), pe2,
                        w1[:half * dh].astype(BF16), w1[half * dh:].astype(BF16),
                        w2.astype(BF16), cos_c, sin_c, rope)
        return out.reshape(B, n_chunk // 4, 4, dh).transpose(0, 2, 1, 3).reshape(B, n_chunk, dh)

    k_c = compress(k_cr, p["w_cmp_k1"], p["w_cmp_k2"], True)
    v_c = compress(v_cr, p["w_cmp_v1"], p["w_cmp_v2"], False)
    q3 = nsa_q.reshape(B, S, NSA_HEADS * dh)
    o_cmp, pen = _nsa_compressed(q3, k_c, v_c)
    k_aug = jnp.concatenate([k_s.reshape(B, S, dh), onehot], axis=-1)
    o_sel = _nsa_selected(q3, pen, k_aug, v_s.reshape(B, S, dh))
    o_win = _nsa_window_attention(q3, k_w.reshape(B, S, dh), v_w.reshape(B, S, dh), NSA_WINDOW)

    dqk = QK_NOPE + QK_ROPE
    w_uq = p["w_uq"].reshape(Q_LORA, MLA_HEADS, dqk)
    w_uq = jnp.concatenate([w_uq[..., :QK_NOPE], _spread_rope64(w_uq[..., QK_NOPE:])], -1)
    w_uq = w_uq.reshape(Q_LORA, MLA_HEADS * 2 * dh).astype(BF16)
    (q_m,) = _proj(cq, p["g_q_lora"], w_uq, [(("none", "rope_r") * MLA_HEADS, dqk ** -0.5)],
                   [BF16], tabs=tabs, name="mla_q_up")
    w_ukv = p["w_ukv"].reshape(KV_LORA, MLA_HEADS, 2 * dh)
    w_ukv = jnp.concatenate([w_ukv[..., :dh].reshape(KV_LORA, -1),
                             w_ukv[..., dh:].reshape(KV_LORA, -1)], 1).astype(BF16)
    k_nope, v_m = _proj(ckv, p["g_kv_lora"], w_ukv,
                        [(("none",) * MLA_HEADS, 1.0), (("none",) * MLA_HEADS, 1.0)],
                        [BF16, BF16], name="mla_kv_up")
    k_m = jnp.concatenate([k_nope.reshape(T, MLA_HEADS, dh),
                           jnp.broadcast_to(k_pe[:, None, :], (T, MLA_HEADS, dh))], -1)
    o_mla = _mla_attention(q_m.reshape(B, S, -1), k_m.reshape(B, S, -1),
                           v_m.reshape(B, S, -1), MLA_HEADS)

    d_outs, d_lses = [], []
    for window, dil in DIL_PATTERNS:
        o, lse = _dilated_pattern(dq.reshape(B, S, dw), dk.reshape(B, S, dw),
                                  dv.reshape(B, S, dw), window, dil, DIL_HEADS)
        d_outs.append(o.reshape(T, dw))
        d_lses.append(lse.reshape(T, LANE))

    mixed = _combine(o_cmp.reshape(T, -1), o_sel.reshape(T, -1), o_win.reshape(T, -1), gate,
                     o_mla.reshape(T, -1), d_outs, d_lses)
    h2 = _out_proj(mixed, p["w_out"].astype(BF16), h2, p["g_mix_post"], name="mix_out_proj")

    xw = XATTN_HEADS * dh
    (xq,) = _proj(h2, p["g_mem_pre"], p["w_xq"].astype(BF16),
                  [(("none",) * XATTN_HEADS, scale)], [BF16], name="xattn_q")
    M = mem.shape[1]
    xk, xv = _proj(mem.reshape(B * M, D), p["g_mem_kv"], p["w_xkv"].astype(BF16),
                   [(("none",) * XATTN_HEADS, 1.0), (("none",) * XATTN_HEADS, 1.0)],
                   [BF16, BF16], bm=min(512, B * M), name="xattn_kv")
    xo = _cross_attention(xq.reshape(B, S, xw), xk.reshape(B, M, xw), xv.reshape(B, M, xw),
                          XATTN_HEADS)
    h2 = _out_proj(xo.reshape(T, xw), p["w_xo"].astype(BF16), h2, p["g_mem_post"],
                   name="xattn_out_proj")

    up = _mlp_up(h2, p["g_mlp_pre"], p["w_up"].astype(BF16))
    h2 = _out_proj(up, p["w_down"].astype(BF16), h2, p["g_mlp_post"], name="mlp_down")
    return h2.reshape(B, S, D)


_LAYER_PARAMS = ("g_mix_pre", "w_in", "cmp_pos_emb", "w_cmp_k1", "w_cmp_k2", "w_cmp_v1", "w_cmp_v2",
                 "g_q_lora", "g_kv_lora", "w_uq", "w_ukv", "w_out", "g_mix_post", "g_mem_pre",
                 "g_mem_kv", "w_xq", "w_xkv", "w_xo", "g_mem_post", "g_mlp_pre", "w_up", "w_down",
                 "g_mlp_post")


def kernel(x, mem, positions, g_mix_pre, w_in, cmp_pos_emb, w_cmp_k1, w_cmp_k2, w_cmp_v1, w_cmp_v2, g_q_lora, g_kv_lora, w_uq, w_ukv, w_out, g_mix_post, g_mem_pre, g_mem_kv, w_xq, w_xkv, w_xo, g_mem_post, g_mlp_pre, w_up, w_down, g_mlp_post):
    stacked = dict(zip(_LAYER_PARAMS, (
        g_mix_pre, w_in, cmp_pos_emb, w_cmp_k1, w_cmp_k2, w_cmp_v1, w_cmp_v2, g_q_lora, g_kv_lora,
        w_uq, w_ukv, w_out, g_mix_post, g_mem_pre, g_mem_kv, w_xq, w_xkv, w_xo, g_mem_post,
        g_mlp_pre, w_up, w_down, g_mlp_post)))
    B, S, D = x.shape
    T = B * S
    assert S % PEN_CHUNK == 0
    cosf, sinf = _full_tables(positions)
    cosr, sinr = _small_tables(positions)
    tabs = tuple(t.reshape(T, LANE) for t in (cosf, sinf, cosr, sinr))
    n_chunk = S // CMP_STRIDE
    end = jnp.minimum(jnp.arange(n_chunk) * CMP_STRIDE + CMP_LEN - 1, S - 1)
    tabs_c = _full_tables(positions[:, end])
    blk = (jnp.arange(S) // SLC_BLOCK) % LANE
    onehot = jnp.broadcast_to((blk[:, None] == jnp.arange(LANE)[None, :]).astype(BF16)[None],
                              (B, S, LANE))
    h = x
    for layer in range(stacked["w_in"].shape[0]):
        p = {}
        for name, val in stacked.items():
            v = val[layer]
            p[name] = v[None, :] if name.startswith("g_") else v
        h = _layer(h, mem, tabs, tabs_c, onehot, p)
    return h
```

```python
import functools

import numpy as np
import jax
import jax.numpy as jnp
from jax import lax
from jax.experimental import pallas as pl
from jax.experimental.pallas import tpu as pltpu

F32 = jnp.float32
BF16 = jnp.bfloat16

LANE = 128
VMEM_LIMIT = 56 * 1024 * 1024

HEAD_DIM = 128
ROPE_THETA = 10000.0
NORM_EPS = 1e-6
NEG_INF = -1e30
BIG = 1e9
LOG2E = 1.4426950408889634
NSA_HEADS = 4
MLA_HEADS = 6
DIL_HEADS = 6
CMP_LEN = 32
CMP_STRIDE = 16
SLC_BLOCK = 64
SLC_TOPK = 16
NSA_WINDOW = 512
Q_LORA = 512
KV_LORA = 512
QK_NOPE = 128
QK_ROPE = 64
DIL_PATTERNS = ((128, 1), (512, 4), (2048, 16))
XATTN_HEADS = 4
PEN_CHUNK = LANE * SLC_BLOCK


def _cparams(n_grid):
    return pltpu.CompilerParams(
        dimension_semantics=("arbitrary",) * n_grid, vmem_limit_bytes=VMEM_LIMIT)


def _rms(x, g):
    return x * lax.rsqrt(jnp.mean(x * x, axis=-1, keepdims=True) + NORM_EPS) * g


def _dot_t(a, b):
    return lax.dot_general(a, b, (((1,), (1,)), ((), ())), preferred_element_type=F32)


def _proj_kernel(out_plan, has_norm, n_tab, has_extra, chunk, *refs):
    x_ref, g_ref, w_ref = refs[:3]
    tab_refs = refs[3:3 + n_tab]
    n_in = 3 + n_tab + int(has_extra)
    out_refs = refs[n_in:]
    x = x_ref[...].astype(F32)
    if has_norm:
        x = _rms(x, g_ref[...])
    xb = x.astype(BF16)
    n_cols = w_ref.shape[1]
    flat = []
    for oi, (modes, scale) in enumerate(out_plan):
        for k, mode in enumerate(modes):
            if mode == "extra":
                out_refs[oi][:, k * LANE:(k + 1) * LANE] = refs[n_in - 1][...]
            else:
                flat.append((oi, k * LANE, mode, scale))
    for c0 in range(0, n_cols, chunk):
        c1 = min(c0 + chunk, n_cols)
        acc = jnp.dot(xb, w_ref[:, c0:c1], preferred_element_type=F32)
        for s in range((c1 - c0) // LANE):
            oi, off, mode, scale = flat[c0 // LANE + s]
            a = acc[:, s * LANE:(s + 1) * LANE]
            if mode == "rope":
                a = a * tab_refs[0][...] + pltpu.roll(a, LANE // 2, 1) * tab_refs[1][...]
            elif mode == "rope_r":
                a = a * tab_refs[2][...] + pltpu.roll(a, LANE // 2, 1) * tab_refs[3][...]
            if scale != 1.0:
                a = a * scale
            out_refs[oi][:, off:off + LANE] = a.astype(out_refs[oi].dtype)


def _proj(x, g, w, out_plan, out_dtypes, tabs=(), extra=None, bm=512, chunk=512, name="proj"):
    T, K = x.shape
    N = w.shape[1]
    assert T % bm == 0 and N % LANE == 0
    assert sum(sum(md != "extra" for md in m) for m, _ in out_plan) * LANE == N
    has_norm = g is not None
    if g is None:
        g = jnp.ones((1, K), F32)
    extras = () if extra is None else (extra,)
    in_specs = [pl.BlockSpec((bm, K), lambda i: (i, 0)),
                pl.BlockSpec((1, K), lambda i: (0, 0)),
                pl.BlockSpec((K, N), lambda i: (0, 0))]
    in_specs += [pl.BlockSpec((bm, LANE), lambda i: (i, 0)) for _ in tabs + extras]
    out_shape = [jax.ShapeDtypeStruct((T, len(m) * LANE), dt)
                 for (m, _), dt in zip(out_plan, out_dtypes)]
    out_specs = [pl.BlockSpec((bm, len(m) * LANE), lambda i: (i, 0)) for m, _ in out_plan]
    return pl.pallas_call(
        functools.partial(_proj_kernel, out_plan, has_norm, len(tabs), extra is not None, chunk),
        out_shape=out_shape, grid=(T // bm,), in_specs=in_specs, out_specs=out_specs,
        compiler_params=_cparams(1), name=name,
    )(x, g, w, *tabs, *extras)


def _mlp_up_kernel(x_ref, g_ref, w_ref, o_ref, xn_ref):
    @pl.when(pl.program_id(1) == 0)
    def _():
        xn_ref[...] = _rms(x_ref[...], g_ref[...]).astype(BF16)
    a = jnp.dot(xn_ref[...], w_ref[...], preferred_element_type=F32)
    a = jnp.maximum(a, 0.0)
    o_ref[...] = (a * a).astype(o_ref.dtype)


def _mlp_up(x, g, w, bm=512, bn=1024):
    T, K = x.shape
    N = w.shape[1]
    return pl.pallas_call(
        _mlp_up_kernel,
        out_shape=jax.ShapeDtypeStruct((T, N), BF16),
        grid=(T // bm, N // bn),
        in_specs=[pl.BlockSpec((bm, K), lambda i, j: (i, 0)),
                  pl.BlockSpec((1, K), lambda i, j: (0, 0)),
                  pl.BlockSpec((K, bn), lambda i, j: (0, j))],
        out_specs=pl.BlockSpec((bm, bn), lambda i, j: (i, j)),
        scratch_shapes=[pltpu.VMEM((bm, K), BF16)],
        compiler_params=_cparams(2), name="mlp_up",
    )(x, g, w)


def _out_proj_kernel(n_k, a_ref, w_ref, h_ref, g_ref, o_ref, acc_ref):
    k = pl.program_id(1)
    part = jnp.dot(a_ref[...], w_ref[...], preferred_element_type=F32)

    def finish(y):
        o_ref[...] = h_ref[...] + _rms(y, g_ref[...])

    if n_k == 1:
        finish(part)
    else:
        @pl.when(k == 0)
        def _():
            acc_ref[...] = part

        @pl.when(jnp.logical_and(k > 0, k < n_k - 1))
        def _():
            acc_ref[...] += part

        @pl.when(k == n_k - 1)
        def _():
            finish(acc_ref[...] + part)


def _out_proj(a, w, h, g, bm=512, bk=2048, name="out_proj"):
    T, K = a.shape
    N = w.shape[1]
    bk = min(bk, K)
    n_k = K // bk
    return pl.pallas_call(
        functools.partial(_out_proj_kernel, n_k),
        out_shape=jax.ShapeDtypeStruct((T, N), F32),
        grid=(T // bm, n_k),
        in_specs=[pl.BlockSpec((bm, bk), lambda i, k: (i, k)),
                  pl.BlockSpec((bk, N), lambda i, k: (k, 0)),
                  pl.BlockSpec((bm, N), lambda i, k: (i, 0)),
                  pl.BlockSpec((1, N), lambda i, k: (0, 0))],
        out_specs=pl.BlockSpec((bm, N), lambda i, k: (i, 0)),
        scratch_shapes=[pltpu.VMEM((bm, N) if n_k > 1 else (8, LANE), F32)],
        compiler_params=_cparams(2), name=name,
    )(a, w, h, g)


def _online_step(carry, st, st_max, vt):
    m, l, acc = carry
    m_new = jnp.maximum(m, st_max)
    alpha = jnp.exp2(m - m_new)
    p = jnp.exp2(st - m_new)
    l = alpha * l + jnp.sum(p, axis=0, keepdims=True)
    acc = alpha * acc + jnp.dot(vt, p.astype(BF16), preferred_element_type=F32)
    return m_new, l, acc


def _online_init(cols, dv):
    return (jnp.full((1, cols), NEG_INF, F32), jnp.zeros((1, cols), F32),
            jnp.zeros((dv, cols), F32))


def _flash_transposed(n_full, scores, diag_mask, vt_ref, st_ref):
    def put(t, slot):
        st = scores(t)
        st_ref[slot] = st
        return jnp.max(st, axis=0, keepdims=True)

    def half(t, carry, slot):
        nxt = put(t + 1, 1 - slot)
        return _online_step(carry[:3], st_ref[slot], carry[3], vt_ref[t]) + (nxt,)

    def pair(u, carry):
        return half(2 * u + 1, half(2 * u, carry, 0), 1)

    carry = _online_init(st_ref.shape[2], vt_ref.shape[-2]) + (put(0, 0),)
    carry = lax.fori_loop(0, n_full // 2, pair, carry)
    odd = n_full & 1
    carry = lax.fori_loop(0, odd, lambda _, c: half(n_full - 1, c, 0), carry)
    st = diag_mask(st_ref[odd])
    _, l, acc = _online_step(carry[:3], st, jnp.max(st, axis=0, keepdims=True), vt_ref[n_full])
    return acc / l


def _mla_kernel(tq, tk, q_ref, k_ref, vt_ref, o_ref, st_ref):
    t0 = pl.program_id(2) * tq
    q = q_ref[...]
    last = t0 // tk

    def scores(t):
        start = pl.multiple_of(t * tk, tk)
        return _dot_t(k_ref[pl.ds(start, tk), :], q)

    def diag_mask(st):
        key = lax.broadcasted_iota(jnp.int32, (tk, tq), 0) + last * tk
        qry = lax.broadcasted_iota(jnp.int32, (tk, tq), 1) + t0
        return jnp.where(key <= qry, st, NEG_INF)

    o_ref[...] = _flash_transposed(last, scores, diag_mask, vt_ref, st_ref).T.astype(o_ref.dtype)


def _key_tiles_t(v, n_heads, tk):
    B, S, _ = v.shape
    return v.reshape(B, S // tk, tk, n_heads, HEAD_DIM).transpose(0, 3, 1, 4, 2)


def _mla_attention(q, k, v, n_heads, tq=512, tk=1024):
    B, S, _ = q.shape
    dq = q.shape[2] // n_heads
    nt = S // tk
    assert tk % tq == 0 and S % tk == 0
    return pl.pallas_call(
        functools.partial(_mla_kernel, tq, tk),
        out_shape=jax.ShapeDtypeStruct((B, S, n_heads * HEAD_DIM), BF16),
        grid=(B, n_heads, S // tq),
        in_specs=[pl.BlockSpec((None, tq, dq), lambda b, h, i: (b, i, h)),
                  pl.BlockSpec((None, S, dq), lambda b, h, i: (b, 0, h)),
                  pl.BlockSpec((None, None, nt, HEAD_DIM, tk), lambda b, h, i: (b, h, 0, 0, 0))],
        out_specs=pl.BlockSpec((None, tq, HEAD_DIM), lambda b, h, i: (b, i, h)),
        scratch_shapes=[pltpu.VMEM((2, tk, tq), F32)],
        compiler_params=_cparams(3), name="mla_attention",
    )(q, k, _key_tiles_t(v, n_heads, tk))


def _band_block(q, kwin, vwin, window, base):
    R = q.shape[0]
    C = window + LANE
    s = _dot_t(q, kwin)
    r = lax.broadcasted_iota(jnp.int32, (R, C), 0) & (LANE - 1)
    c = lax.broadcasted_iota(jnp.int32, (R, C), 1)
    mask = (c >= r) & (c <= r + window) & (c >= window - base)
    s = jnp.where(mask, s, NEG_INF)
    m = jnp.max(s, axis=-1, keepdims=True)
    p = jnp.exp2(s - m)
    l = jnp.sum(p, axis=-1, keepdims=True)
    o = jnp.dot(p.astype(BF16), vwin, preferred_element_type=F32) / l
    return o, m + jnp.log2(l)


def _stack_heads(x, n):
    return jnp.concatenate([x[:, h * LANE:(h + 1) * LANE] for h in range(n)], axis=0)


def _nsa_window_kernel(tq, window, q_ref, kp_ref, kc_ref, vp_ref, vc_ref, o_ref):
    i = pl.program_id(1)
    kwin = jnp.concatenate([kp_ref[...], kc_ref[...]], axis=0)
    vwin = jnp.concatenate([vp_ref[...], vc_ref[...]], axis=0)
    for j in range(tq // LANE):
        q = _stack_heads(q_ref[j * LANE:(j + 1) * LANE, :], NSA_HEADS)
        lo = j * LANE
        o, _ = _band_block(q, kwin[lo:lo + window + LANE], vwin[lo:lo + window + LANE],
                           window, i * tq + lo)
        for h in range(NSA_HEADS):
            o_ref[lo:lo + LANE, h * LANE:(h + 1) * LANE] = (
                o[h * LANE:(h + 1) * LANE].astype(o_ref.dtype))


def _nsa_window_attention(q, k, v, window):
    B, S, W = q.shape
    tq = window
    prev = lambda b, i: (b, jnp.maximum(i - 1, 0), 0)
    cur = lambda b, i: (b, i, 0)
    return pl.pallas_call(
        functools.partial(_nsa_window_kernel, tq, window),
        out_shape=jax.ShapeDtypeStruct((B, S, W), BF16),
        grid=(B, S // tq),
        in_specs=[pl.BlockSpec((None, tq, W), cur),
                  pl.BlockSpec((None, window, HEAD_DIM), prev),
                  pl.BlockSpec((None, tq, HEAD_DIM), cur),
                  pl.BlockSpec((None, window, HEAD_DIM), prev),
                  pl.BlockSpec((None, tq, HEAD_DIM), cur)],
        out_specs=pl.BlockSpec((None, tq, W), cur),
        compiler_params=_cparams(2), name="nsa_window_attention",
    )(q, k, k, v, v)


def _dilated_kernel(tq, window, n_heads, q_ref, kp_ref, kc_ref, vp_ref, vc_ref, o_ref, lse_ref):
    i = pl.program_id(2)
    lse_ref[...] = jnp.zeros(lse_ref.shape, F32)
    for h in range(n_heads):
        hs = slice(h * LANE, (h + 1) * LANE)
        kwin = jnp.concatenate([kp_ref[:, hs], kc_ref[:, hs]], axis=0)
        vwin = jnp.concatenate([vp_ref[:, hs], vc_ref[:, hs]], axis=0)
        for j in range(tq // LANE):
            lo = j * LANE
            o, lse = _band_block(q_ref[lo:lo + LANE, hs], kwin[lo:lo + window + LANE],
                                 vwin[lo:lo + window + LANE], window, i * tq + lo)
            o_ref[lo:lo + LANE, hs] = o.astype(o_ref.dtype)
            lse_ref[lo:lo + LANE, h:h + 1] = lse


def _dilated_pattern(q, k, v, window, dil, n_heads, tq=512):
    B, S, W = q.shape
    L = S // dil
    tq = min(tq, L)
    w = window // dil
    assert w == LANE and L % tq == 0
    view = lambda t: t.reshape(B, L, dil * W)
    sub = tq // w
    prev = lambda b, r, i: (b, jnp.maximum(i * sub - 1, 0), r)
    cur = lambda b, r, i: (b, i, r)
    o, lse = pl.pallas_call(
        functools.partial(_dilated_kernel, tq, w, n_heads),
        out_shape=[jax.ShapeDtypeStruct((B, L, dil * W), BF16),
                   jax.ShapeDtypeStruct((B, L, dil * LANE), F32)],
        grid=(B, dil, L // tq),
        in_specs=[pl.BlockSpec((None, tq, W), cur),
                  pl.BlockSpec((None, w, W), prev),
                  pl.BlockSpec((None, tq, W), cur),
                  pl.BlockSpec((None, w, W), prev),
                  pl.BlockSpec((None, tq, W), cur)],
        out_specs=[pl.BlockSpec((None, tq, W), cur),
                   pl.BlockSpec((None, tq, LANE), cur)],
        compiler_params=_cparams(3), name="dilated_attention_d%d" % dil,
    )(view(q), view(k), view(k), view(v), view(v))
    return o.reshape(B, S, W), lse.reshape(B, S, LANE)


def _gelu_tanh(x):
    return 0.5 * x * (1.0 + jnp.tanh(0.7978845608028654 * (x + 0.044715 * (x * x * x))))


def _compress_kernel(rope, t_ref, pe_ref, w1a_ref, w1b_ref, w2_ref, cos_ref, sin_ref, o_ref):
    t = t_ref[...].astype(F32)
    n = t.shape[0]
    first = jnp.dot((t + pe_ref[0:1, :]).astype(BF16), w1a_ref[...], preferred_element_type=F32)
    second = jnp.dot((t + pe_ref[1:2, :]).astype(BF16), w1b_ref[...], preferred_element_type=F32)
    hid = first + pltpu.roll(second, n - 1, 0)
    out = jnp.dot(_gelu_tanh(hid).astype(BF16), w2_ref[...], preferred_element_type=F32)
    if rope:
        out = out * cos_ref[...] + pltpu.roll(out, LANE // 2, 1) * sin_ref[...]
    o_ref[...] = out.astype(o_ref.dtype)


def _compress(t, pe2, w1a, w1b, w2, cos_c, sin_c, rope):
    B, n, K = t.shape
    whole = lambda b: (0, 0)
    per_b = lambda b: (b, 0, 0)
    return pl.pallas_call(
        functools.partial(_compress_kernel, rope),
        out_shape=jax.ShapeDtypeStruct((B, n, HEAD_DIM), BF16),
        grid=(B,),
        in_specs=[pl.BlockSpec((None, n, K), per_b),
                  pl.BlockSpec((2, K), whole),
                  pl.BlockSpec((K, HEAD_DIM), whole),
                  pl.BlockSpec((K, HEAD_DIM), whole),
                  pl.BlockSpec((HEAD_DIM, HEAD_DIM), whole),
                  pl.BlockSpec((None, n, HEAD_DIM), per_b),
                  pl.BlockSpec((None, n, HEAD_DIM), per_b)],
        out_specs=pl.BlockSpec((None, n, HEAD_DIM), per_b),
        compiler_params=_cparams(1), name="nsa_compress",
    )(t, pe2, w1a, w1b, w2, cos_c, sin_c)


def _nsa_cmp_kernel(tq, n_slc, n_sel, q_ref, kc_ref, vc_ref, o_ref, pen_ref):
    i = pl.program_id(1)
    n_cmp = kc_ref.shape[0]
    per_blk = SLC_BLOCK // CMP_STRIDE
    slc_shift = n_slc.bit_length() - 1
    q = _stack_heads(q_ref[...], NSA_HEADS)
    s = _dot_t(q, kc_ref[...])
    R = NSA_HEADS * tq
    row = lax.broadcasted_iota(jnp.int32, (R, n_cmp), 0) & (tq - 1)
    col = lax.broadcasted_iota(jnp.int32, (R, n_cmp), 1)
    j_of = col & (n_slc - 1)
    r_of = col >> slc_shift
    cmp_end = (per_blk * j_of + r_of) * CMP_STRIDE + (CMP_LEN - 1)
    cmask = cmp_end <= row + i * tq
    s = jnp.where(cmask, s, NEG_INF)
    m = jnp.max(s, axis=-1, keepdims=True)
    e = jnp.exp2(s - m)
    p = jnp.where(cmask, e / jnp.sum(e, axis=-1, keepdims=True), 0.0)
    o = jnp.dot(p.astype(BF16), vc_ref[...], preferred_element_type=F32)
    for h in range(NSA_HEADS):
        o_ref[:, h * LANE:(h + 1) * LANE] = o[h * tq:(h + 1) * tq].astype(o_ref.dtype)

    ph = p[0:tq]
    for h in range(1, NSA_HEADS):
        ph = ph + p[h * tq:(h + 1) * tq]
    groups = [ph[:, r * n_slc:(r + 1) * n_slc] for r in range(per_blk)]
    blk = lax.broadcasted_iota(jnp.int32, (tq, n_slc), 1)
    spill = jnp.where(blk == 0, 0.0, pltpu.roll(groups[per_blk - 1], 1, 1))
    imp = groups[0]
    for r in range(1, per_blk):
        imp = imp + groups[r]
    imp = imp + spill
    tpos = lax.broadcasted_iota(jnp.int32, (tq, n_slc), 0) + i * tq
    cur = tpos >> (SLC_BLOCK.bit_length() - 1)
    forced = (blk == 0) | (blk == cur) | (blk == cur - 1)
    valid = blk <= cur
    work = jnp.where(forced, BIG, jnp.where(valid, imp, -BIG))
    blk_f = blk.astype(F32)
    sel = jnp.zeros((tq, n_slc), F32)
    for _ in range(n_sel):
        mx = jnp.max(work, axis=-1, keepdims=True)
        first = jnp.min(jnp.where(work == mx, blk_f, float(n_slc)), axis=-1, keepdims=True)
        pick = blk_f == first
        sel = jnp.where(pick, 1.0, sel)
        work = jnp.where(pick, -jnp.inf, work)
    pen = jnp.where((sel > 0.0) & valid, 0.0, NEG_INF).astype(pen_ref.dtype)
    for c in range(n_slc // LANE):
        pen_ref[c] = pen[:, c * LANE:(c + 1) * LANE]


def _nsa_compressed(q, k_c, v_c, tq=128):
    B, S, W = q.shape
    n_cmp = k_c.shape[1]
    n_slc = S // SLC_BLOCK
    n_sel = min(SLC_TOPK, n_slc)
    assert n_slc % LANE == 0 and n_slc & (n_slc - 1) == 0
    assert n_cmp == 4 * n_slc and tq & (tq - 1) == 0
    n_pc = n_slc // LANE
    return pl.pallas_call(
        functools.partial(_nsa_cmp_kernel, tq, n_slc, n_sel),
        out_shape=[jax.ShapeDtypeStruct((B, S, W), BF16),
                   jax.ShapeDtypeStruct((B, n_pc, S, LANE), BF16)],
        grid=(B, S // tq),
        in_specs=[pl.BlockSpec((None, tq, W), lambda b, i: (b, i, 0)),
                  pl.BlockSpec((None, n_cmp, HEAD_DIM), lambda b, i: (b, 0, 0)),
                  pl.BlockSpec((None, n_cmp, HEAD_DIM), lambda b, i: (b, 0, 0))],
        out_specs=[pl.BlockSpec((None, tq, W), lambda b, i: (b, i, 0)),
                   pl.BlockSpec((None, n_pc, tq, LANE), lambda b, i: (b, 0, i, 0))],
        compiler_params=_cparams(2), name="nsa_compressed_topk",
    )(q, k_c, v_c)


def _nsa_sel_kernel(tq, tk, q_ref, pen_ref, k_ref, vt_ref, o_ref, qa_ref, st_ref):
    i = pl.program_id(1)
    t0 = i * tq
    R = NSA_HEADS * tq
    last = t0 // tk
    tiles_per_chunk = PEN_CHUNK // tk
    q = _stack_heads(q_ref[...], NSA_HEADS)
    for c in range(pen_ref.shape[0]):
        qa_ref[c, :, 0:LANE] = q
        qa_ref[c, :, LANE:2 * LANE] = jnp.concatenate([pen_ref[c]] * NSA_HEADS, axis=0)

    def scores(t):
        start = pl.multiple_of(t * tk, tk)
        return _dot_t(k_ref[pl.ds(start, tk), :], qa_ref[t // tiles_per_chunk])

    def diag_mask(st):
        key = lax.broadcasted_iota(jnp.int32, (tk, R), 0) + last * tk
        qry = (lax.broadcasted_iota(jnp.int32, (tk, R), 1) & (tq - 1)) + t0
        return jnp.where(key <= qry, st, NEG_INF)

    o = _flash_transposed(last, scores, diag_mask, vt_ref, st_ref)
    for h in range(NSA_HEADS):
        o_ref[:, h * LANE:(h + 1) * LANE] = o[:, h * tq:(h + 1) * tq].T.astype(o_ref.dtype)


def _nsa_selected(q, pen, k_aug, v, tq=128, tk=1024):
    B, S, W = q.shape
    n_pc = pen.shape[1]
    nt = S // tk
    assert tk % tq == 0 and PEN_CHUNK % tk == 0 and tq & (tq - 1) == 0
    vt = _key_tiles_t(v, 1, tk)[:, 0]
    return pl.pallas_call(
        functools.partial(_nsa_sel_kernel, tq, tk),
        out_shape=jax.ShapeDtypeStruct((B, S, W), BF16),
        grid=(B, S // tq),
        in_specs=[pl.BlockSpec((None, tq, W), lambda b, i: (b, i, 0)),
                  pl.BlockSpec((None, n_pc, tq, LANE), lambda b, i: (b, 0, i, 0)),
                  pl.BlockSpec((None, S, 2 * HEAD_DIM), lambda b, i: (b, 0, 0)),
                  pl.BlockSpec((None, nt, HEAD_DIM, tk), lambda b, i: (b, 0, 0, 0))],
        out_specs=pl.BlockSpec((None, tq, W), lambda b, i: (b, i, 0)),
        scratch_shapes=[pltpu.VMEM((n_pc, NSA_HEADS * tq, 2 * HEAD_DIM), BF16),
                        pltpu.VMEM((2, tk, NSA_HEADS * tq), F32)],
        compiler_params=_cparams(2), name="nsa_selected_attention",
    )(q, pen, k_aug, vt)


def _combine_kernel(oc_ref, os_ref, ow_ref, gate_ref, ob_ref,
                    d0_ref, d1_ref, d2_ref, l0_ref, l1_ref, l2_ref, o_ref):
    g = jax.nn.sigmoid(gate_ref[...])
    for h in range(NSA_HEADS):
        hs = slice(h * LANE, (h + 1) * LANE)
        a = (g[:, 3 * h:3 * h + 1] * oc_ref[:, hs].astype(F32)
             + g[:, 3 * h + 1:3 * h + 2] * os_ref[:, hs].astype(F32)
             + g[:, 3 * h + 2:3 * h + 3] * ow_ref[:, hs].astype(F32))
        o_ref[:, hs] = a.astype(o_ref.dtype)
    off = NSA_HEADS * LANE
    o_ref[:, off:off + MLA_HEADS * LANE] = ob_ref[...]
    off += MLA_HEADS * LANE
    l0, l1, l2 = l0_ref[...], l1_ref[...], l2_ref[...]
    mx = jnp.maximum(jnp.maximum(l0, l1), l2)
    e0, e1, e2 = jnp.exp2(l0 - mx), jnp.exp2(l1 - mx), jnp.exp2(l2 - mx)
    den = e0 + e1 + e2
    w0, w1, w2 = e0 / den, e1 / den, e2 / den
    for h in range(DIL_HEADS):
        hs = slice(h * LANE, (h + 1) * LANE)
        a = (w0[:, h:h + 1] * d0_ref[:, hs].astype(F32)
             + w1[:, h:h + 1] * d1_ref[:, hs].astype(F32)
             + w2[:, h:h + 1] * d2_ref[:, hs].astype(F32))
        o_ref[:, off + h * LANE:off + (h + 1) * LANE] = a.astype(o_ref.dtype)


def _combine(o_c, o_s, o_w, gate, o_b, d_outs, d_lses, bm=512):
    T = o_c.shape[0]
    width = (NSA_HEADS + MLA_HEADS + DIL_HEADS) * LANE
    args = (o_c, o_s, o_w, gate, o_b) + tuple(d_outs) + tuple(d_lses)
    return pl.pallas_call(
        _combine_kernel,
        out_shape=jax.ShapeDtypeStruct((T, width), BF16),
        grid=(T // bm,),
        in_specs=[pl.BlockSpec((bm, a.shape[1]), lambda i: (i, 0)) for a in args],
        out_specs=pl.BlockSpec((bm, width), lambda i: (i, 0)),
        compiler_params=_cparams(1), name="mixer_combine",
    )(*args)


def _xattn_kernel(n_heads, q_ref, k_ref, v_ref, o_ref):
    for h in range(n_heads):
        hs = slice(h * LANE, (h + 1) * LANE)
        s = _dot_t(q_ref[:, hs], k_ref[:, hs])
        m = jnp.max(s, axis=-1, keepdims=True)
        p = jnp.exp2(s - m)
        l = jnp.sum(p, axis=-1, keepdims=True)
        o = jnp.dot(p.astype(BF16), v_ref[:, hs], preferred_element_type=F32) / l
        o_ref[:, hs] = o.astype(o_ref.dtype)


def _cross_attention(q, k, v, n_heads, tq=512):
    B, S, W = q.shape
    M = k.shape[1]
    return pl.pallas_call(
        functools.partial(_xattn_kernel, n_heads),
        out_shape=jax.ShapeDtypeStruct((B, S, W), BF16),
        grid=(B, S // tq),
        in_specs=[pl.BlockSpec((None, tq, W), lambda b, i: (b, i, 0)),
                  pl.BlockSpec((None, M, W), lambda b, i: (b, 0, 0)),
                  pl.BlockSpec((None, M, W), lambda b, i: (b, 0, 0))],
        out_specs=pl.BlockSpec((None, tq, W), lambda b, i: (b, i, 0)),
        compiler_params=_cparams(2), name="cross_attention",
    )(q, k, v)


def _rope_tables(pos, dim):
    inv = ROPE_THETA ** (-jnp.arange(0, dim, 2, dtype=F32) / dim)
    ang = pos.astype(F32)[..., None] * inv
    return jnp.cos(ang), jnp.sin(ang)


def _full_tables(pos):
    c, s = _rope_tables(pos, HEAD_DIM)
    return jnp.concatenate([c, c], -1), jnp.concatenate([-s, s], -1)


def _spread_rope64(t):
    z = jnp.zeros(t.shape[:-1] + (QK_ROPE // 2,), t.dtype)
    return jnp.concatenate([t[..., :QK_ROPE // 2], z, t[..., QK_ROPE // 2:], z], -1)


def _small_tables(pos):
    c, s = _rope_tables(pos, QK_ROPE)
    return _spread_rope64(jnp.concatenate([c, c], -1)), _spread_rope64(jnp.concatenate([-s, s], -1))


def _layer(h, mem, tabs, tabs_c, onehot, p):
    B, S, D = h.shape
    T = B * S
    dh = HEAD_DIM
    scale = dh ** -0.5 * LOG2E
    h2 = h.reshape(T, D)

    w_in = p["w_in"]
    cuts = np.cumsum([NSA_HEADS * dh, 6 * dh, 3 * NSA_HEADS, Q_LORA, KV_LORA, QK_ROPE])
    w_q, w_kv, w_g, w_cq, w_ckv, w_kr, w_dil = jnp.split(w_in, cuts, axis=1)
    kv = [w_kv[:, k * dh:(k + 1) * dh] for k in range(6)]
    dw = DIL_HEADS * dh
    w_dq, w_dk, w_dv = w_dil[:, :dw], w_dil[:, dw:2 * dw], w_dil[:, 2 * dw:]
    w_rope = jnp.concatenate([w_q, kv[2], kv[4], w_dq, w_dk, _spread_rope64(w_kr)], 1).astype(BF16)
    w_gpad = jnp.pad(w_g, ((0, 0), (0, LANE - w_g.shape[1])))
    w_plain = jnp.concatenate([kv[0], kv[1], kv[3], kv[5], w_gpad, w_cq, w_ckv, w_dv], 1).astype(BF16)

    nsa_q, k_s, k_w, dq, dk, k_pe = _proj(
        h2, p["g_mix_pre"], w_rope,
        [(("rope",) * NSA_HEADS, scale), (("rope",), 1.0), (("rope",), 1.0),
         (("rope",) * DIL_HEADS, scale), (("rope",) * DIL_HEADS, 1.0), (("rope_r",), 1.0)],
        [BF16] * 6, tabs=tabs, name="in_proj_rope")
    k_cr, v_cr, v_s, v_w, gate, cq, ckv, dv = _proj(
        h2, p["g_mix_pre"], w_plain,
        [(("none",), 1.0)] * 4 + [(("none",), 1.0), (("none",) * 4, 1.0), (("none",) * 4, 1.0),
                                  (("none",) * DIL_HEADS, 1.0)],
        [BF16] * 4 + [F32, BF16, BF16, BF16], name="in_proj_plain")

    n_chunk = S // CMP_STRIDE
    half = CMP_LEN // 2
    pe2 = p["cmp_pos_emb"].reshape(2, half * dh)
    cos_c, sin_c = tabs_c

    def compress(t, w1, w2, rope):
        out = _compress(t.reshape(B, n_chunk, CMP_STRIDE * dh), pe2,
                        w1[:half * dh].astype(BF16), w1[half * dh:].astype(BF16),
                        w2.astype(BF16), cos_c, sin_c, rope)
        return out.reshape(B, n_chunk // 4, 4, dh).transpose(0, 2, 1, 3).reshape(B, n_chunk, dh)

    k_c = compress(k_cr, p["w_cmp_k1"], p["w_cmp_k2"], True)
    v_c = compress(v_cr, p["w_cmp_v1"], p["w_cmp_v2"], False)
    q3 = nsa_q.reshape(B, S, NSA_HEADS * dh)
    o_cmp, pen = _nsa_compressed(q3, k_c, v_c)
    k_aug = jnp.concatenate([k_s.reshape(B, S, dh), onehot], axis=-1)
    o_sel = _nsa_selected(q3, pen, k_aug, v_s.reshape(B, S, dh))
    o_win = _nsa_window_attention(q3, k_w.reshape(B, S, dh), v_w.reshape(B, S, dh), NSA_WINDOW)

    dqk = QK_NOPE + QK_ROPE
    w_uq = p["w_uq"].reshape(Q_LORA, MLA_HEADS, dqk)
    w_uq = jnp.concatenate([w_uq[..., :QK_NOPE], _spread_rope64(w_uq[..., QK_NOPE:])], -1)
    w_uq = w_uq.reshape(Q_LORA, MLA_HEADS * 2 * dh).astype(BF16)
    (q_m,) = _proj(cq, p["g_q_lora"], w_uq, [(("none", "rope_r") * MLA_HEADS, dqk ** -0.5 * LOG2E)],
                   [BF16], tabs=tabs, name="mla_q_up")
    w_ukv = p["w_ukv"].reshape(KV_LORA, MLA_HEADS, 2 * dh)
    w_ukv = jnp.concatenate([w_ukv[..., :dh].reshape(KV_LORA, -1),
                             w_ukv[..., dh:].reshape(KV_LORA, -1)], 1).astype(BF16)
    k_m, v_m = _proj(ckv, p["g_kv_lora"], w_ukv,
                     [(("none", "extra") * MLA_HEADS, 1.0), (("none",) * MLA_HEADS, 1.0)],
                     [BF16, BF16], extra=k_pe, name="mla_kv_up")
    o_mla = _mla_attention(q_m.reshape(B, S, -1), k_m.reshape(B, S, -1),
                           v_m.reshape(B, S, -1), MLA_HEADS)

    d_outs, d_lses = [], []
    for window, dil in DIL_PATTERNS:
        o, lse = _dilated_pattern(dq.reshape(B, S, dw), dk.reshape(B, S, dw),
                                  dv.reshape(B, S, dw), window, dil, DIL_HEADS)
        d_outs.append(o.reshape(T, dw))
        d_lses.append(lse.reshape(T, LANE))

    mixed = _combine(o_cmp.reshape(T, -1), o_sel.reshape(T, -1), o_win.reshape(T, -1), gate,
                     o_mla.reshape(T, -1), d_outs, d_lses)
    h2 = _out_proj(mixed, p["w_out"].astype(BF16), h2, p["g_mix_post"], name="mix_out_proj")

    xw = XATTN_HEADS * dh
    (xq,) = _proj(h2, p["g_mem_pre"], p["w_xq"].astype(BF16),
                  [(("none",) * XATTN_HEADS, scale)], [BF16], name="xattn_q")
    M = mem.shape[1]
    xk, xv = _proj(mem.reshape(B * M, D), p["g_mem_kv"], p["w_xkv"].astype(BF16),
                   [(("none",) * XATTN_HEADS, 1.0), (("none",) * XATTN_HEADS, 1.0)],
                   [BF16, BF16], bm=min(512, B * M), name="xattn_kv")
    xo = _cross_attention(xq.reshape(B, S, xw), xk.reshape(B, M, xw), xv.reshape(B, M, xw),
                          XATTN_HEADS)
    h2 = _out_proj(xo.reshape(T, xw), p["w_xo"].astype(BF16), h2, p["g_mem_post"],
                   name="xattn_out_proj")

    up = _mlp_up(h2, p["g_mlp_pre"], p["w_up"].astype(BF16))
    h2 = _out_proj(up, p["w_down"].astype(BF16), h2, p["g_mlp_post"], name="mlp_down")
    return h2.reshape(B, S, D)


_LAYER_PARAMS = ("g_mix_pre", "w_in", "cmp_pos_emb", "w_cmp_k1", "w_cmp_k2", "w_cmp_v1", "w_cmp_v2",
                 "g_q_lora", "g_kv_lora", "w_uq", "w_ukv", "w_out", "g_mix_post", "g_mem_pre",
                 "g_mem_kv", "w_xq", "w_xkv", "w_xo", "g_mem_post", "g_mlp_pre", "w_up", "w_down",
                 "g_mlp_post")


def kernel(x, mem, positions, g_mix_pre, w_in, cmp_pos_emb, w_cmp_k1, w_cmp_k2, w_cmp_v1, w_cmp_v2, g_q_lora, g_kv_lora, w_uq, w_ukv, w_out, g_mix_post, g_mem_pre, g_mem_kv, w_xq, w_xkv, w_xo, g_mem_post, g_mlp_pre, w_up, w_down, g_mlp_post):
    stacked = dict(zip(_LAYER_PARAMS, (
        g_mix_pre, w_in, cmp_pos_emb, w_cmp_k1, w_cmp_k2, w_cmp_v1, w_cmp_v2, g_q_lora, g_kv_lora,
        w_uq, w_ukv, w_out, g_mix_post, g_mem_pre, g_mem_kv, w_xq, w_xkv, w_xo, g_mem_post,
        g_mlp_pre, w_up, w_down, g_mlp_post)))
    B, S, D = x.shape
    T = B * S
    assert S % PEN_CHUNK == 0
    cosf, sinf = _full_tables(positions)
    cosr, sinr = _small_tables(positions)
    tabs = tuple(t.reshape(T, LANE) for t in (cosf, sinf, cosr, sinr))
    n_chunk = S // CMP_STRIDE
    end = jnp.minimum(jnp.arange(n_chunk) * CMP_STRIDE + CMP_LEN - 1, S - 1)
    tabs_c = _full_tables(positions[:, end])
    blk = (jnp.arange(S) // SLC_BLOCK) % LANE
    onehot = jnp.broadcast_to((blk[:, None] == jnp.arange(LANE)[None, :]).astype(BF16)[None],
                              (B, S, LANE))
    h = x
    for layer in range(stacked["w_in"].shape[0]):
        p = {}
        for name, val in stacked.items():
            v = val[layer]
            p[name] = v[None, :] if name.startswith("g_") else v
        h = _layer(h, mem, tabs, tabs_c, onehot, p)
    return h
```

```python
import functools

import numpy as np
import jax
import jax.numpy as jnp
from jax import lax
from jax.experimental import pallas as pl
from jax.experimental.pallas import tpu as pltpu

F32 = jnp.float32
BF16 = jnp.bfloat16

LANE = 128
VMEM_LIMIT = 56 * 1024 * 1024

HEAD_DIM = 128
ROPE_THETA = 10000.0
NORM_EPS = 1e-6
NEG_INF = -1e30
BIG = 1e9
LOG2E = 1.4426950408889634
NSA_HEADS = 4
MLA_HEADS = 6
DIL_HEADS = 6
CMP_LEN = 32
CMP_STRIDE = 16
SLC_BLOCK = 64
SLC_TOPK = 16
NSA_WINDOW = 512
Q_LORA = 512
KV_LORA = 512
QK_NOPE = 128
QK_ROPE = 64
DIL_PATTERNS = ((128, 1), (512, 4), (2048, 16))
XATTN_HEADS = 4
PEN_CHUNK = LANE * SLC_BLOCK


def _cparams(n_grid):
    return pltpu.CompilerParams(
        dimension_semantics=("arbitrary",) * n_grid, vmem_limit_bytes=VMEM_LIMIT)


def _rms(x, g):
    return x * lax.rsqrt(jnp.mean(x * x, axis=-1, keepdims=True) + NORM_EPS) * g


def _dot_t(a, b):
    return lax.dot_general(a, b, (((1,), (1,)), ((), ())), preferred_element_type=F32)


def _proj_kernel(out_plan, has_norm, n_tab, has_extra, chunk, *refs):
    x_ref, g_ref, w_ref = refs[:3]
    tab_refs = refs[3:3 + n_tab]
    n_in = 3 + n_tab + int(has_extra)
    out_refs = refs[n_in:]
    x = x_ref[...].astype(F32)
    if has_norm:
        x = _rms(x, g_ref[...])
    xb = x.astype(BF16)
    n_cols = w_ref.shape[1]
    flat = []
    for oi, (modes, scale) in enumerate(out_plan):
        for k, mode in enumerate(modes):
            if mode == "extra":
                out_refs[oi][:, k * LANE:(k + 1) * LANE] = refs[n_in - 1][...]
            else:
                flat.append((oi, k * LANE, mode, scale))
    for c0 in range(0, n_cols, chunk):
        c1 = min(c0 + chunk, n_cols)
        acc = jnp.dot(xb, w_ref[:, c0:c1], preferred_element_type=F32)
        for s in range((c1 - c0) // LANE):
            oi, off, mode, scale = flat[c0 // LANE + s]
            a = acc[:, s * LANE:(s + 1) * LANE]
            if mode == "rope":
                a = a * tab_refs[0][...] + pltpu.roll(a, LANE // 2, 1) * tab_refs[1][...]
            elif mode == "rope_r":
                a = a * tab_refs[2][...] + pltpu.roll(a, LANE // 2, 1) * tab_refs[3][...]
            if scale != 1.0:
                a = a * scale
            out_refs[oi][:, off:off + LANE] = a.astype(out_refs[oi].dtype)


def _proj(x, g, w, out_plan, out_dtypes, tabs=(), extra=None, bm=512, chunk=512, name="proj"):
    T, K = x.shape
    N = w.shape[1]
    assert T % bm == 0 and N % LANE == 0
    assert sum(sum(md != "extra" for md in m) for m, _ in out_plan) * LANE == N
    has_norm = g is not None
    if g is None:
        g = jnp.ones((1, K), F32)
    extras = () if extra is None else (extra,)
    in_specs = [pl.BlockSpec((bm, K), lambda i: (i, 0)),
                pl.BlockSpec((1, K), lambda i: (0, 0)),
                pl.BlockSpec((K, N), lambda i: (0, 0))]
    in_specs += [pl.BlockSpec((bm, LANE), lambda i: (i, 0)) for _ in tabs + extras]
    out_shape = [jax.ShapeDtypeStruct((T, len(m) * LANE), dt)
                 for (m, _), dt in zip(out_plan, out_dtypes)]
    out_specs = [pl.BlockSpec((bm, len(m) * LANE), lambda i: (i, 0)) for m, _ in out_plan]
    return pl.pallas_call(
        functools.partial(_proj_kernel, out_plan, has_norm, len(tabs), extra is not None, chunk),
        out_shape=out_shape, grid=(T // bm,), in_specs=in_specs, out_specs=out_specs,
        compiler_params=_cparams(1), name=name,
    )(x, g, w, *tabs, *extras)


def _mlp_up_kernel(x_ref, g_ref, w_ref, o_ref, xn_ref):
    @pl.when(pl.program_id(1) == 0)
    def _():
        xn_ref[...] = _rms(x_ref[...], g_ref[...]).astype(BF16)
    a = jnp.dot(xn_ref[...], w_ref[...], preferred_element_type=F32)
    a = jnp.maximum(a, 0.0)
    o_ref[...] = (a * a).astype(o_ref.dtype)


def _mlp_up(x, g, w, bm=1024, bn=1024):
    T, K = x.shape
    N = w.shape[1]
    return pl.pallas_call(
        _mlp_up_kernel,
        out_shape=jax.ShapeDtypeStruct((T, N), BF16),
        grid=(T // bm, N // bn),
        in_specs=[pl.BlockSpec((bm, K), lambda i, j: (i, 0)),
                  pl.BlockSpec((1, K), lambda i, j: (0, 0)),
                  pl.BlockSpec((K, bn), lambda i, j: (0, j))],
        out_specs=pl.BlockSpec((bm, bn), lambda i, j: (i, j)),
        scratch_shapes=[pltpu.VMEM((bm, K), BF16)],
        compiler_params=_cparams(2), name="mlp_up",
    )(x, g, w)


def _out_proj_kernel(n_k, a_ref, w_ref, h_ref, g_ref, o_ref, acc_ref):
    k = pl.program_id(1)
    part = jnp.dot(a_ref[...], w_ref[...], preferred_element_type=F32)

    def finish(y):
        o_ref[...] = h_ref[...] + _rms(y, g_ref[...])

    if n_k == 1:
        finish(part)
    else:
        @pl.when(k == 0)
        def _():
            acc_ref[...] = part

        @pl.when(jnp.logical_and(k > 0, k < n_k - 1))
        def _():
            acc_ref[...] += part

        @pl.when(k == n_k - 1)
        def _():
            finish(acc_ref[...] + part)


def _out_proj(a, w, h, g, bm=512, bk=2048, name="out_proj"):
    T, K = a.shape
    N = w.shape[1]
    bk = min(bk, K)
    n_k = K // bk
    return pl.pallas_call(
        functools.partial(_out_proj_kernel, n_k),
        out_shape=jax.ShapeDtypeStruct((T, N), F32),
        grid=(T // bm, n_k),
        in_specs=[pl.BlockSpec((bm, bk), lambda i, k: (i, k)),
                  pl.BlockSpec((bk, N), lambda i, k: (k, 0)),
                  pl.BlockSpec((bm, N), lambda i, k: (i, 0)),
                  pl.BlockSpec((1, N), lambda i, k: (0, 0))],
        out_specs=pl.BlockSpec((bm, N), lambda i, k: (i, 0)),
        scratch_shapes=[pltpu.VMEM((bm, N) if n_k > 1 else (8, LANE), F32)],
        compiler_params=_cparams(2), name=name,
    )(a, w, h, g)


def _online_step(carry, st, st_max, vt):
    m, l, acc = carry
    m_new = jnp.maximum(m, st_max)
    alpha = jnp.exp2(m - m_new)
    p = jnp.exp2(st - m_new)
    l = alpha * l + jnp.sum(p, axis=0, keepdims=True)
    acc = alpha * acc + jnp.dot(vt, p.astype(BF16), preferred_element_type=F32)
    return m_new, l, acc


def _online_init(cols, dv):
    return (jnp.full((1, cols), NEG_INF, F32), jnp.zeros((1, cols), F32),
            jnp.zeros((dv, cols), F32))


def _flash_transposed(n_full, scores, diag_mask, vt_ref, st_ref):
    def put(t, slot):
        st = scores(t)
        st_ref[slot] = st
        return jnp.max(st, axis=0, keepdims=True)

    def half(t, carry, slot):
        nxt = put(t + 1, 1 - slot)
        return _online_step(carry[:3], st_ref[slot], carry[3], vt_ref[t]) + (nxt,)

    def pair(u, carry):
        return half(2 * u + 1, half(2 * u, carry, 0), 1)

    carry = _online_init(st_ref.shape[2], vt_ref.shape[-2]) + (put(0, 0),)
    carry = lax.fori_loop(0, n_full // 2, pair, carry)
    odd = n_full & 1
    carry = lax.fori_loop(0, odd, lambda _, c: half(n_full - 1, c, 0), carry)
    st = diag_mask(st_ref[odd])
    _, l, acc = _online_step(carry[:3], st, jnp.max(st, axis=0, keepdims=True), vt_ref[n_full])
    return acc / l


def _mla_kernel(tq, tk, q_ref, k_ref, vt_ref, o_ref, st_ref):
    t0 = pl.program_id(2) * tq
    q = q_ref[...]
    last = t0 // tk

    def scores(t):
        start = pl.multiple_of(t * tk, tk)
        return _dot_t(k_ref[pl.ds(start, tk), :], q)

    def diag_mask(st):
        key = lax.broadcasted_iota(jnp.int32, (tk, tq), 0) + last * tk
        qry = lax.broadcasted_iota(jnp.int32, (tk, tq), 1) + t0
        return jnp.where(key <= qry, st, NEG_INF)

    o_ref[...] = _flash_transposed(last, scores, diag_mask, vt_ref, st_ref).T.astype(o_ref.dtype)


def _key_tiles_t(v, n_heads, tk):
    B, S, _ = v.shape
    return v.reshape(B, S // tk, tk, n_heads, HEAD_DIM).transpose(0, 3, 1, 4, 2)


def _mla_attention(q, k, v, n_heads, tq=1024, tk=1024):
    B, S, _ = q.shape
    dq = q.shape[2] // n_heads
    nt = S // tk
    assert tk % tq == 0 and S % tk == 0
    return pl.pallas_call(
        functools.partial(_mla_kernel, tq, tk),
        out_shape=jax.ShapeDtypeStruct((B, S, n_heads * HEAD_DIM), BF16),
        grid=(B, n_heads, S // tq),
        in_specs=[pl.BlockSpec((None, tq, dq), lambda b, h, i: (b, i, h)),
                  pl.BlockSpec((None, S, dq), lambda b, h, i: (b, 0, h)),
                  pl.BlockSpec((None, None, nt, HEAD_DIM, tk), lambda b, h, i: (b, h, 0, 0, 0))],
        out_specs=pl.BlockSpec((None, tq, HEAD_DIM), lambda b, h, i: (b, i, h)),
        scratch_shapes=[pltpu.VMEM((2, tk, tq), F32)],
        compiler_params=_cparams(3), name="mla_attention",
    )(q, k, _key_tiles_t(v, n_heads, tk))


def _band_block(q, kwin, vwin, window, base):
    R = q.shape[0]
    C = window + LANE
    s = _dot_t(q, kwin)
    r = lax.broadcasted_iota(jnp.int32, (R, C), 0) & (LANE - 1)
    c = lax.broadcasted_iota(jnp.int32, (R, C), 1)
    mask = (c >= r) & (c <= r + window) & (c >= window - base)
    s = jnp.where(mask, s, NEG_INF)
    m = jnp.max(s, axis=-1, keepdims=True)
    p = jnp.exp2(s - m)
    l = jnp.sum(p, axis=-1, keepdims=True)
    o = jnp.dot(p.astype(BF16), vwin, preferred_element_type=F32) / l
    return o, m + jnp.log2(l)


def _stack_heads(x, n):
    return jnp.concatenate([x[:, h * LANE:(h + 1) * LANE] for h in range(n)], axis=0)


def _nsa_window_kernel(tq, window, q_ref, kp_ref, kc_ref, vp_ref, vc_ref, o_ref):
    i = pl.program_id(1)
    kwin = jnp.concatenate([kp_ref[...], kc_ref[...]], axis=0)
    vwin = jnp.concatenate([vp_ref[...], vc_ref[...]], axis=0)
    for j in range(tq // LANE):
        q = _stack_heads(q_ref[j * LANE:(j + 1) * LANE, :], NSA_HEADS)
        lo = j * LANE
        o, _ = _band_block(q, kwin[lo:lo + window + LANE], vwin[lo:lo + window + LANE],
                           window, i * tq + lo)
        for h in range(NSA_HEADS):
            o_ref[lo:lo + LANE, h * LANE:(h + 1) * LANE] = (
                o[h * LANE:(h + 1) * LANE].astype(o_ref.dtype))


def _nsa_window_attention(q, k, v, window):
    B, S, W = q.shape
    tq = window
    prev = lambda b, i: (b, jnp.maximum(i - 1, 0), 0)
    cur = lambda b, i: (b, i, 0)
    return pl.pallas_call(
        functools.partial(_nsa_window_kernel, tq, window),
        out_shape=jax.ShapeDtypeStruct((B, S, W), BF16),
        grid=(B, S // tq),
        in_specs=[pl.BlockSpec((None, tq, W), cur),
                  pl.BlockSpec((None, window, HEAD_DIM), prev),
                  pl.BlockSpec((None, tq, HEAD_DIM), cur),
                  pl.BlockSpec((None, window, HEAD_DIM), prev),
                  pl.BlockSpec((None, tq, HEAD_DIM), cur)],
        out_specs=pl.BlockSpec((None, tq, W), cur),
        compiler_params=_cparams(2), name="nsa_window_attention",
    )(q, k, k, v, v)


def _dilated_kernel(tq, window, n_heads, q_ref, kp_ref, kc_ref, vp_ref, vc_ref, o_ref, lse_ref):
    i = pl.program_id(2)
    lse_ref[...] = jnp.zeros(lse_ref.shape, F32)
    for h in range(n_heads):
        hs = slice(h * LANE, (h + 1) * LANE)
        kwin = jnp.concatenate([kp_ref[:, hs], kc_ref[:, hs]], axis=0)
        vwin = jnp.concatenate([vp_ref[:, hs], vc_ref[:, hs]], axis=0)
        for j in range(tq // LANE):
            lo = j * LANE
            o, lse = _band_block(q_ref[lo:lo + LANE, hs], kwin[lo:lo + window + LANE],
                                 vwin[lo:lo + window + LANE], window, i * tq + lo)
            o_ref[lo:lo + LANE, hs] = o.astype(o_ref.dtype)
            lse_ref[lo:lo + LANE, h:h + 1] = lse


def _dilated_pattern(q, k, v, window, dil, n_heads, tq=512):
    B, S, W = q.shape
    L = S // dil
    tq = min(tq, L)
    w = window // dil
    assert w == LANE and L % tq == 0
    view = lambda t: t.reshape(B, L, dil * W)
    sub = tq // w
    prev = lambda b, r, i: (b, jnp.maximum(i * sub - 1, 0), r)
    cur = lambda b, r, i: (b, i, r)
    o, lse = pl.pallas_call(
        functools.partial(_dilated_kernel, tq, w, n_heads),
        out_shape=[jax.ShapeDtypeStruct((B, L, dil * W), BF16),
                   jax.ShapeDtypeStruct((B, L, dil * LANE), F32)],
        grid=(B, dil, L // tq),
        in_specs=[pl.BlockSpec((None, tq, W), cur),
                  pl.BlockSpec((None, w, W), prev),
                  pl.BlockSpec((None, tq, W), cur),
                  pl.BlockSpec((None, w, W), prev),
                  pl.BlockSpec((None, tq, W), cur)],
        out_specs=[pl.BlockSpec((None, tq, W), cur),
                   pl.BlockSpec((None, tq, LANE), cur)],
        compiler_params=_cparams(3), name="dilated_attention_d%d" % dil,
    )(view(q), view(k), view(k), view(v), view(v))
    return o.reshape(B, S, W), lse.reshape(B, S, LANE)


def _gelu_tanh(x):
    return 0.5 * x * (1.0 + jnp.tanh(0.7978845608028654 * (x + 0.044715 * (x * x * x))))


def _compress_kernel(rope, t_ref, pe_ref, w1a_ref, w1b_ref, w2_ref, cos_ref, sin_ref, o_ref):
    t = t_ref[...].astype(F32)
    n = t.shape[0]
    first = jnp.dot((t + pe_ref[0:1, :]).astype(BF16), w1a_ref[...], preferred_element_type=F32)
    second = jnp.dot((t + pe_ref[1:2, :]).astype(BF16), w1b_ref[...], preferred_element_type=F32)
    hid = first + pltpu.roll(second, n - 1, 0)
    out = jnp.dot(_gelu_tanh(hid).astype(BF16), w2_ref[...], preferred_element_type=F32)
    if rope:
        out = out * cos_ref[...] + pltpu.roll(out, LANE // 2, 1) * sin_ref[...]
    o_ref[...] = out.astype(o_ref.dtype)


def _compress(t, pe2, w1a, w1b, w2, cos_c, sin_c, rope):
    B, n, K = t.shape
    whole = lambda b: (0, 0)
    per_b = lambda b: (b, 0, 0)
    return pl.pallas_call(
        functools.partial(_compress_kernel, rope),
        out_shape=jax.ShapeDtypeStruct((B, n, HEAD_DIM), BF16),
        grid=(B,),
        in_specs=[pl.BlockSpec((None, n, K), per_b),
                  pl.BlockSpec((2, K), whole),
                  pl.BlockSpec((K, HEAD_DIM), whole),
                  pl.BlockSpec((K, HEAD_DIM), whole),
                  pl.BlockSpec((HEAD_DIM, HEAD_DIM), whole),
                  pl.BlockSpec((None, n, HEAD_DIM), per_b),
                  pl.BlockSpec((None, n, HEAD_DIM), per_b)],
        out_specs=pl.BlockSpec((None, n, HEAD_DIM), per_b),
        compiler_params=_cparams(1), name="nsa_compress",
    )(t, pe2, w1a, w1b, w2, cos_c, sin_c)


def _nsa_cmp_block(tq, n_slc, n_sel, t0, q, kc, vc):
    n_cmp = kc.shape[0]
    per_blk = SLC_BLOCK // CMP_STRIDE
    slc_shift = n_slc.bit_length() - 1
    q = _stack_heads(q, NSA_HEADS)
    s = _dot_t(q, kc)
    R = NSA_HEADS * tq
    row = lax.broadcasted_iota(jnp.int32, (R, n_cmp), 0) & (tq - 1)
    col = lax.broadcasted_iota(jnp.int32, (R, n_cmp), 1)
    j_of = col & (n_slc - 1)
    r_of = col >> slc_shift
    cmp_end = (per_blk * j_of + r_of) * CMP_STRIDE + (CMP_LEN - 1)
    cmask = cmp_end <= row + t0
    s = jnp.where(cmask, s, NEG_INF)
    m = jnp.max(s, axis=-1, keepdims=True)
    e = jnp.exp2(s - m)
    p = jnp.where(cmask, e / jnp.sum(e, axis=-1, keepdims=True), 0.0)
    o = jnp.dot(p.astype(BF16), vc, preferred_element_type=F32)

    ph = p[0:tq]
    for h in range(1, NSA_HEADS):
        ph = ph + p[h * tq:(h + 1) * tq]
    groups = [ph[:, r * n_slc:(r + 1) * n_slc] for r in range(per_blk)]
    blk = lax.broadcasted_iota(jnp.int32, (tq, n_slc), 1)
    spill = jnp.where(blk == 0, 0.0, pltpu.roll(groups[per_blk - 1], 1, 1))
    imp = groups[0]
    for r in range(1, per_blk):
        imp = imp + groups[r]
    imp = imp + spill
    tpos = lax.broadcasted_iota(jnp.int32, (tq, n_slc), 0) + t0
    cur = tpos >> (SLC_BLOCK.bit_length() - 1)
    forced = (blk == 0) | (blk == cur) | (blk == cur - 1)
    valid = blk <= cur
    work = jnp.where(forced, BIG, jnp.where(valid, imp, -BIG))
    blk_f = blk.astype(F32)
    sel = jnp.zeros((tq, n_slc), F32)
    for _ in range(n_sel):
        mx = jnp.max(work, axis=-1, keepdims=True)
        first = jnp.min(jnp.where(work == mx, blk_f, float(n_slc)), axis=-1, keepdims=True)
        pick = blk_f == first
        sel = jnp.where(pick, 1.0, sel)
        work = jnp.where(pick, -jnp.inf, work)
    return o, jnp.where((sel > 0.0) & valid, 0.0, NEG_INF)


def _nsa_cmp_kernel(tq, n_sub, n_slc, n_sel, q_ref, kc_ref, vc_ref, o_ref, pen_ref):
    i = pl.program_id(1)
    for u in range(n_sub):
        rows = slice(u * tq, (u + 1) * tq)
        o, pen = _nsa_cmp_block(tq, n_slc, n_sel, (i * n_sub + u) * tq, q_ref[rows, :],
                                kc_ref[...], vc_ref[...])
        for h in range(NSA_HEADS):
            o_ref[rows, h * LANE:(h + 1) * LANE] = o[h * tq:(h + 1) * tq].astype(o_ref.dtype)
        pen = pen.astype(pen_ref.dtype)
        for c in range(n_slc // LANE):
            pen_ref[c, rows, :] = pen[:, c * LANE:(c + 1) * LANE]


def _nsa_compressed(q, k_c, v_c, tq=128, n_sub=4):
    B, S, W = q.shape
    n_cmp = k_c.shape[1]
    n_slc = S // SLC_BLOCK
    n_sel = min(SLC_TOPK, n_slc)
    assert n_slc % LANE == 0 and n_slc & (n_slc - 1) == 0
    assert n_cmp == 4 * n_slc and tq & (tq - 1) == 0
    n_pc = n_slc // LANE
    bq = tq * n_sub
    return pl.pallas_call(
        functools.partial(_nsa_cmp_kernel, tq, n_sub, n_slc, n_sel),
        out_shape=[jax.ShapeDtypeStruct((B, S, W), BF16),
                   jax.ShapeDtypeStruct((B, n_pc, S, LANE), BF16)],
        grid=(B, S // bq),
        in_specs=[pl.BlockSpec((None, bq, W), lambda b, i: (b, i, 0)),
                  pl.BlockSpec((None, n_cmp, HEAD_DIM), lambda b, i: (b, 0, 0)),
                  pl.BlockSpec((None, n_cmp, HEAD_DIM), lambda b, i: (b, 0, 0))],
        out_specs=[pl.BlockSpec((None, bq, W), lambda b, i: (b, i, 0)),
                   pl.BlockSpec((None, n_pc, bq, LANE), lambda b, i: (b, 0, i, 0))],
        compiler_params=_cparams(2), name="nsa_compressed_topk",
    )(q, k_c, v_c)


def _nsa_sel_kernel(tq, tk, q_ref, pen_ref, k_ref, vt_ref, o_ref, qa_ref, st_ref):
    i = pl.program_id(1)
    t0 = i * tq
    R = NSA_HEADS * tq
    last = t0 // tk
    tiles_per_chunk = PEN_CHUNK // tk
    q = _stack_heads(q_ref[...], NSA_HEADS)
    for c in range(pen_ref.shape[0]):
        qa_ref[c, :, 0:LANE] = q
        qa_ref[c, :, LANE:2 * LANE] = jnp.concatenate([pen_ref[c]] * NSA_HEADS, axis=0)

    def scores(t):
        start = pl.multiple_of(t * tk, tk)
        return _dot_t(k_ref[pl.ds(start, tk), :], qa_ref[t // tiles_per_chunk])

    def diag_mask(st):
        key = lax.broadcasted_iota(jnp.int32, (tk, R), 0) + last * tk
        qry = (lax.broadcasted_iota(jnp.int32, (tk, R), 1) & (tq - 1)) + t0
        return jnp.where(key <= qry, st, NEG_INF)

    o = _flash_transposed(last, scores, diag_mask, vt_ref, st_ref)
    for h in range(NSA_HEADS):
        o_ref[:, h * LANE:(h + 1) * LANE] = o[:, h * tq:(h + 1) * tq].T.astype(o_ref.dtype)


def _nsa_selected(q, pen, k_aug, v, tq=256, tk=1024):
    B, S, W = q.shape
    n_pc = pen.shape[1]
    nt = S // tk
    assert tk % tq == 0 and PEN_CHUNK % tk == 0 and tq & (tq - 1) == 0
    vt = _key_tiles_t(v, 1, tk)[:, 0]
    return pl.pallas_call(
        functools.partial(_nsa_sel_kernel, tq, tk),
        out_shape=jax.ShapeDtypeStruct((B, S, W), BF16),
        grid=(B, S // tq),
        in_specs=[pl.BlockSpec((None, tq, W), lambda b, i: (b, i, 0)),
                  pl.BlockSpec((None, n_pc, tq, LANE), lambda b, i: (b, 0, i, 0)),
                  pl.BlockSpec((None, S, 2 * HEAD_DIM), lambda b, i: (b, 0, 0)),
                  pl.BlockSpec((None, nt, HEAD_DIM, tk), lambda b, i: (b, 0, 0, 0))],
        out_specs=pl.BlockSpec((None, tq, W), lambda b, i: (b, i, 0)),
        scratch_shapes=[pltpu.VMEM((n_pc, NSA_HEADS * tq, 2 * HEAD_DIM), BF16),
                        pltpu.VMEM((2, tk, NSA_HEADS * tq), F32)],
        compiler_params=_cparams(2), name="nsa_selected_attention",
    )(q, pen, k_aug, vt)


def _combine_kernel(oc_ref, os_ref, ow_ref, gate_ref, ob_ref,
                    d0_ref, d1_ref, d2_ref, l0_ref, l1_ref, l2_ref, o_ref):
    g = jax.nn.sigmoid(gate_ref[...])
    for h in range(NSA_HEADS):
        hs = slice(h * LANE, (h + 1) * LANE)
        a = (g[:, 3 * h:3 * h + 1] * oc_ref[:, hs].astype(F32)
             + g[:, 3 * h + 1:3 * h + 2] * os_ref[:, hs].astype(F32)
             + g[:, 3 * h + 2:3 * h + 3] * ow_ref[:, hs].astype(F32))
        o_ref[:, hs] = a.astype(o_ref.dtype)
    off = NSA_HEADS * LANE
    o_ref[:, off:off + MLA_HEADS * LANE] = ob_ref[...]
    off += MLA_HEADS * LANE
    l0, l1, l2 = l0_ref[...], l1_ref[...], l2_ref[...]
    mx = jnp.maximum(jnp.maximum(l0, l1), l2)
    e0, e1, e2 = jnp.exp2(l0 - mx), jnp.exp2(l1 - mx), jnp.exp2(l2 - mx)
    den = e0 + e1 + e2
    w0, w1, w2 = e0 / den, e1 / den, e2 / den
    for h in range(DIL_HEADS):
        hs = slice(h * LANE, (h + 1) * LANE)
        a = (w0[:, h:h + 1] * d0_ref[:, hs].astype(F32)
             + w1[:, h:h + 1] * d1_ref[:, hs].astype(F32)
             + w2[:, h:h + 1] * d2_ref[:, hs].astype(F32))
        o_ref[:, off + h * LANE:off + (h + 1) * LANE] = a.astype(o_ref.dtype)


def _combine(o_c, o_s, o_w, gate, o_b, d_outs, d_lses, bm=512):
    T = o_c.shape[0]
    width = (NSA_HEADS + MLA_HEADS + DIL_HEADS) * LANE
    args = (o_c, o_s, o_w, gate, o_b) + tuple(d_outs) + tuple(d_lses)
    return pl.pallas_call(
        _combine_kernel,
        out_shape=jax.ShapeDtypeStruct((T, width), BF16),
        grid=(T // bm,),
        in_specs=[pl.BlockSpec((bm, a.shape[1]), lambda i: (i, 0)) for a in args],
        out_specs=pl.BlockSpec((bm, width), lambda i: (i, 0)),
        compiler_params=_cparams(1), name="mixer_combine",
    )(*args)


def _xattn_kernel(n_heads, q_ref, k_ref, v_ref, o_ref):
    for h in range(n_heads):
        hs = slice(h * LANE, (h + 1) * LANE)
        s = _dot_t(q_ref[:, hs], k_ref[:, hs])
        m = jnp.max(s, axis=-1, keepdims=True)
        p = jnp.exp2(s - m)
        l = jnp.sum(p, axis=-1, keepdims=True)
        o = jnp.dot(p.astype(BF16), v_ref[:, hs], preferred_element_type=F32) / l
        o_ref[:, hs] = o.astype(o_ref.dtype)


def _cross_attention(q, k, v, n_heads, tq=512):
    B, S, W = q.shape
    M = k.shape[1]
    return pl.pallas_call(
        functools.partial(_xattn_kernel, n_heads),
        out_shape=jax.ShapeDtypeStruct((B, S, W), BF16),
        grid=(B, S // tq),
        in_specs=[pl.BlockSpec((None, tq, W), lambda b, i: (b, i, 0)),
                  pl.BlockSpec((None, M, W), lambda b, i: (b, 0, 0)),
                  pl.BlockSpec((None, M, W), lambda b, i: (b, 0, 0))],
        out_specs=pl.BlockSpec((None, tq, W), lambda b, i: (b, i, 0)),
        compiler_params=_cparams(2), name="cross_attention",
    )(q, k, v)


def _rope_tables(pos, dim):
    inv = ROPE_THETA ** (-jnp.arange(0, dim, 2, dtype=F32) / dim)
    ang = pos.astype(F32)[..., None] * inv
    return jnp.cos(ang), jnp.sin(ang)


def _full_tables(pos):
    c, s = _rope_tables(pos, HEAD_DIM)
    return jnp.concatenate([c, c], -1), jnp.concatenate([-s, s], -1)


def _spread_rope64(t):
    z = jnp.zeros(t.shape[:-1] + (QK_ROPE // 2,), t.dtype)
    return jnp.concatenate([t[..., :QK_ROPE // 2], z, t[..., QK_ROPE // 2:], z], -1)


def _small_tables(pos):
    c, s = _rope_tables(pos, QK_ROPE)
    return _spread_rope64(jnp.concatenate([c, c], -1)), _spread_rope64(jnp.concatenate([-s, s], -1))


def _layer(h, mem, tabs, tabs_c, onehot, p):
    B, S, D = h.shape
    T = B * S
    dh = HEAD_DIM
    scale = dh ** -0.5 * LOG2E
    h2 = h.reshape(T, D)

    w_in = p["w_in"]
    cuts = np.cumsum([NSA_HEADS * dh, 6 * dh, 3 * NSA_HEADS, Q_LORA, KV_LORA, QK_ROPE])
    w_q, w_kv, w_g, w_cq, w_ckv, w_kr, w_dil = jnp.split(w_in, cuts, axis=1)
    kv = [w_kv[:, k * dh:(k + 1) * dh] for k in range(6)]
    dw = DIL_HEADS * dh
    w_dq, w_dk, w_dv = w_dil[:, :dw], w_dil[:, dw:2 * dw], w_dil[:, 2 * dw:]
    w_rope = jnp.concatenate([w_q, kv[2], kv[4], w_dq, w_dk, _spread_rope64(w_kr)], 1).astype(BF16)
    w_gpad = jnp.pad(w_g, ((0, 0), (0, LANE - w_g.shape[1])))
    w_plain = jnp.concatenate([kv[0], kv[1], kv[3], kv[5], w_gpad, w_cq, w_ckv, w_dv], 1).astype(BF16)

    nsa_q, k_s, k_w, dq, dk, k_pe = _proj(
        h2, p["g_mix_pre"], w_rope,
        [(("rope",) * NSA_HEADS, scale), (("rope",), 1.0), (("rope",), 1.0),
         (("rope",) * DIL_HEADS, scale), (("rope",) * DIL_HEADS, 1.0), (("rope_r",), 1.0)],
        [BF16] * 6, tabs=tabs, name="in_proj_rope")
    k_cr, v_cr, v_s, v_w, gate, cq, ckv, dv = _proj(
        h2, p["g_mix_pre"], w_plain,
        [(("none",), 1.0)] * 4 + [(("none",), 1.0), (("none",) * 4, 1.0), (("none",) * 4, 1.0),
                                  (("none",) * DIL_HEADS, 1.0)],
        [BF16] * 4 + [F32, BF16, BF16, BF16], name="in_proj_plain")

    n_chunk = S // CMP_STRIDE
    half = CMP_LEN // 2
    pe2 = p["cmp_pos_emb"].reshape(2, half * dh)
    cos_c, sin_c = tabs_c

    def compress(t, w1, w2, rope):
        out = _compress(t.reshape(B, n_chunk, CMP_STRIDE * dh), pe2,
                        w1[:half * dh].astype(BF16), w1[half * dh:].astype(BF16),
                        w2.astype(BF16), cos_c, sin_c, rope)
        return out.reshape(B, n_chunk // 4, 4, dh).transpose(0, 2, 1, 3).reshape(B, n_chunk, dh)

    k_c = compress(k_cr, p["w_cmp_k1"], p["w_cmp_k2"], True)
    v_c = compress(v_cr, p["w_cmp_v1"], p["w_cmp_v2"], False)
    q3 = nsa_q.reshape(B, S, NSA_HEADS * dh)
    o_cmp, pen = _nsa_compressed(q3, k_c, v_c)
    k_aug = jnp.concatenate([k_s.reshape(B, S, dh), onehot], axis=-1)
    o_sel = _nsa_selected(q3, pen, k_aug, v_s.reshape(B, S, dh))
    o_win = _nsa_window_attention(q3, k_w.reshape(B, S, dh), v_w.reshape(B, S, dh), NSA_WINDOW)

    dqk = QK_NOPE + QK_ROPE
    w_uq = p["w_uq"].reshape(Q_LORA, MLA_HEADS, dqk)
    w_uq = jnp.concatenate([w_uq[..., :QK_NOPE], _spread_rope64(w_uq[..., QK_NOPE:])], -1)
    w_uq = w_uq.reshape(Q_LORA, MLA_HEADS * 2 * dh).astype(BF16)
    (q_m,) = _proj(cq, p["g_q_lora"], w_uq, [(("none", "rope_r") * MLA_HEADS, dqk ** -0.5 * LOG2E)],
                   [BF16], tabs=tabs, name="mla_q_up")
    w_ukv = p["w_ukv"].reshape(KV_LORA, MLA_HEADS, 2 * dh)
    w_ukv = jnp.concatenate([w_ukv[..., :dh].reshape(KV_LORA, -1),
                             w_ukv[..., dh:].reshape(KV_LORA, -1)], 1).astype(BF16)
    k_m, v_m = _proj(ckv, p["g_kv_lora"], w_ukv,
                     [(("none", "extra") * MLA_HEADS, 1.0), (("none",) * MLA_HEADS, 1.0)],
                     [BF16, BF16], extra=k_pe, name="mla_kv_up")
    o_mla = _mla_attention(q_m.reshape(B, S, -1), k_m.reshape(B, S, -1),
                           v_m.reshape(B, S, -1), MLA_HEADS)

    d_outs, d_lses = [], []
    for window, dil in DIL_PATTERNS:
        o, lse = _dilated_pattern(dq.reshape(B, S, dw), dk.reshape(B, S, dw),
                                  dv.reshape(B, S, dw), window, dil, DIL_HEADS)
        d_outs.append(o.reshape(T, dw))
        d_lses.append(lse.reshape(T, LANE))

    mixed = _combine(o_cmp.reshape(T, -1), o_sel.reshape(T, -1), o_win.reshape(T, -1), gate,
                     o_mla.reshape(T, -1), d_outs, d_lses)
    h2 = _out_proj(mixed, p["w_out"].astype(BF16), h2, p["g_mix_post"], name="mix_out_proj")

    xw = XATTN_HEADS * dh
    (xq,) = _proj(h2, p["g_mem_pre"], p["w_xq"].astype(BF16),
                  [(("none",) * XATTN_HEADS, scale)], [BF16], name="xattn_q")
    M = mem.shape[1]
    xk, xv = _proj(mem.reshape(B * M, D), p["g_mem_kv"], p["w_xkv"].astype(BF16),
                   [(("none",) * XATTN_HEADS, 1.0), (("none",) * XATTN_HEADS, 1.0)],
                   [BF16, BF16], bm=min(512, B * M), name="xattn_kv")
    xo = _cross_attention(xq.reshape(B, S, xw), xk.reshape(B, M, xw), xv.reshape(B, M, xw),
                          XATTN_HEADS)
    h2 = _out_proj(xo.reshape(T, xw), p["w_xo"].astype(BF16), h2, p["g_mem_post"],
                   name="xattn_out_proj")

    up = _mlp_up(h2, p["g_mlp_pre"], p["w_up"].astype(BF16))
    h2 = _out_proj(up, p["w_down"].astype(BF16), h2, p["g_mlp_post"], name="mlp_down")
    return h2.reshape(B, S, D)


_LAYER_PARAMS = ("g_mix_pre", "w_in", "cmp_pos_emb", "w_cmp_k1", "w_cmp_k2", "w_cmp_v1", "w_cmp_v2",
                 "g_q_lora", "g_kv_lora", "w_uq", "w_ukv", "w_out", "g_mix_post", "g_mem_pre",
                 "g_mem_kv", "w_xq", "w_xkv", "w_xo", "g_mem_post", "g_mlp_pre", "w_up", "w_down",
                 "g_mlp_post")


def kernel(x, mem, positions, g_mix_pre, w_in, cmp_pos_emb, w_cmp_k1, w_cmp_k2, w_cmp_v1, w_cmp_v2, g_q_lora, g_kv_lora, w_uq, w_ukv, w_out, g_mix_post, g_mem_pre, g_mem_kv, w_xq, w_xkv, w_xo, g_mem_post, g_mlp_pre, w_up, w_down, g_mlp_post):
    stacked = dict(zip(_LAYER_PARAMS, (
        g_mix_pre, w_in, cmp_pos_emb, w_cmp_k1, w_cmp_k2, w_cmp_v1, w_cmp_v2, g_q_lora, g_kv_lora,
        w_uq, w_ukv, w_out, g_mix_post, g_mem_pre, g_mem_kv, w_xq, w_xkv, w_xo, g_mem_post,
        g_mlp_pre, w_up, w_down, g_mlp_post)))
    B, S, D = x.shape
    T = B * S
    assert S % PEN_CHUNK == 0
    cosf, sinf = _full_tables(positions)
    cosr, sinr = _small_tables(positions)
    tabs = tuple(t.reshape(T, LANE) for t in (cosf, sinf, cosr, sinr))
    n_chunk = S // CMP_STRIDE
    end = jnp.minimum(jnp.arange(n_chunk) * CMP_STRIDE + CMP_LEN - 1, S - 1)
    tabs_c = _full_tables(positions[:, end])
    blk = (jnp.arange(S) // SLC_BLOCK) % LANE
    onehot = jnp.broadcast_to((blk[:, None] == jnp.arange(LANE)[None, :]).astype(BF16)[None],
                              (B, S, LANE))
    h = x
    for layer in range(stacked["w_in"].shape[0]):
        p = {}
        for name, val in stacked.items():
            v = val[layer]
            p[name] = v[None, :] if name.startswith("g_") else v
        h = _layer(h, mem, tabs, tabs_c, onehot, p)
    return h
```

```python
import functools

import numpy as np
import jax
import jax.numpy as jnp
from jax import lax
from jax.experimental import pallas as pl
from jax.experimental.pallas import tpu as pltpu

F32 = jnp.float32
BF16 = jnp.bfloat16

LANE = 128
VMEM_LIMIT = 56 * 1024 * 1024

HEAD_DIM = 128
ROPE_THETA = 10000.0
NORM_EPS = 1e-6
NEG_INF = -1e30
BIG = 1e9
LOG2E = 1.4426950408889634
NSA_HEADS = 4
MLA_HEADS = 6
DIL_HEADS = 6
CMP_LEN = 32
CMP_STRIDE = 16
SLC_BLOCK = 64
SLC_TOPK = 16
NSA_WINDOW = 512
Q_LORA = 512
KV_LORA = 512
QK_NOPE = 128
QK_ROPE = 64
DIL_PATTERNS = ((128, 1), (512, 4), (2048, 16))
XATTN_HEADS = 4
PEN_CHUNK = LANE * SLC_BLOCK


def _cparams(n_grid):
    return pltpu.CompilerParams(
        dimension_semantics=("arbitrary",) * n_grid, vmem_limit_bytes=VMEM_LIMIT)


def _rms(x, g):
    return x * lax.rsqrt(jnp.mean(x * x, axis=-1, keepdims=True) + NORM_EPS) * g


def _dot_t(a, b):
    return lax.dot_general(a, b, (((1,), (1,)), ((), ())), preferred_element_type=F32)


def _proj_kernel(out_plan, has_norm, n_tab, has_extra, chunk, *refs):
    x_ref, g_ref, w_ref = refs[:3]
    tab_refs = refs[3:3 + n_tab]
    n_in = 3 + n_tab + int(has_extra)
    out_refs = refs[n_in:]
    x = x_ref[...].astype(F32)
    if has_norm:
        x = _rms(x, g_ref[...])
    xb = x.astype(BF16)
    n_cols = w_ref.shape[1]
    flat = []
    for oi, (modes, scale) in enumerate(out_plan):
        for k, mode in enumerate(modes):
            if mode == "extra":
                out_refs[oi][:, k * LANE:(k + 1) * LANE] = refs[n_in - 1][...]
            else:
                flat.append((oi, k * LANE, mode, scale))
    for c0 in range(0, n_cols, chunk):
        c1 = min(c0 + chunk, n_cols)
        acc = jnp.dot(xb, w_ref[:, c0:c1], preferred_element_type=F32)
        for s in range((c1 - c0) // LANE):
            oi, off, mode, scale = flat[c0 // LANE + s]
            a = acc[:, s * LANE:(s + 1) * LANE]
            if mode == "rope":
                a = a * tab_refs[0][...] + pltpu.roll(a, LANE // 2, 1) * tab_refs[1][...]
            elif mode == "rope_r":
                a = a * tab_refs[2][...] + pltpu.roll(a, LANE // 2, 1) * tab_refs[3][...]
            if scale != 1.0:
                a = a * scale
            out_refs[oi][:, off:off + LANE] = a.astype(out_refs[oi].dtype)


def _proj(x, g, w, out_plan, out_dtypes, tabs=(), extra=None, bm=512, chunk=512, name="proj"):
    T, K = x.shape
    N = w.shape[1]
    assert T % bm == 0 and N % LANE == 0
    assert sum(sum(md != "extra" for md in m) for m, _ in out_plan) * LANE == N
    has_norm = g is not None
    if g is None:
        g = jnp.ones((1, K), F32)
    extras = () if extra is None else (extra,)
    in_specs = [pl.BlockSpec((bm, K), lambda i: (i, 0)),
                pl.BlockSpec((1, K), lambda i: (0, 0)),
                pl.BlockSpec((K, N), lambda i: (0, 0))]
    in_specs += [pl.BlockSpec((bm, LANE), lambda i: (i, 0)) for _ in tabs + extras]
    out_shape = [jax.ShapeDtypeStruct((T, len(m) * LANE), dt)
                 for (m, _), dt in zip(out_plan, out_dtypes)]
    out_specs = [pl.BlockSpec((bm, len(m) * LANE), lambda i: (i, 0)) for m, _ in out_plan]
    return pl.pallas_call(
        functools.partial(_proj_kernel, out_plan, has_norm, len(tabs), extra is not None, chunk),
        out_shape=out_shape, grid=(T // bm,), in_specs=in_specs, out_specs=out_specs,
        compiler_params=_cparams(1), name=name,
    )(x, g, w, *tabs, *extras)


def _mlp_up_kernel(x_ref, g_ref, w_ref, o_ref, xn_ref):
    @pl.when(pl.program_id(1) == 0)
    def _():
        xn_ref[...] = _rms(x_ref[...], g_ref[...]).astype(BF16)
    a = jnp.dot(xn_ref[...], w_ref[...], preferred_element_type=F32)
    a = jnp.maximum(a, 0.0)
    o_ref[...] = (a * a).astype(o_ref.dtype)


def _mlp_up(x, g, w, bm=1024, bn=1024):
    T, K = x.shape
    N = w.shape[1]
    return pl.pallas_call(
        _mlp_up_kernel,
        out_shape=jax.ShapeDtypeStruct((T, N), BF16),
        grid=(T // bm, N // bn),
        in_specs=[pl.BlockSpec((bm, K), lambda i, j: (i, 0)),
                  pl.BlockSpec((1, K), lambda i, j: (0, 0)),
                  pl.BlockSpec((K, bn), lambda i, j: (0, j))],
        out_specs=pl.BlockSpec((bm, bn), lambda i, j: (i, j)),
        scratch_shapes=[pltpu.VMEM((bm, K), BF16)],
        compiler_params=_cparams(2), name="mlp_up",
    )(x, g, w)


def _out_proj_kernel(n_k, a_ref, w_ref, h_ref, g_ref, o_ref, acc_ref):
    k = pl.program_id(1)
    part = jnp.dot(a_ref[...], w_ref[...], preferred_element_type=F32)

    def finish(y):
        o_ref[...] = h_ref[...] + _rms(y, g_ref[...])

    if n_k == 1:
        finish(part)
    else:
        @pl.when(k == 0)
        def _():
            acc_ref[...] = part

        @pl.when(jnp.logical_and(k > 0, k < n_k - 1))
        def _():
            acc_ref[...] += part

        @pl.when(k == n_k - 1)
        def _():
            finish(acc_ref[...] + part)


def _out_proj(a, w, h, g, bm=512, bk=2048, name="out_proj"):
    T, K = a.shape
    N = w.shape[1]
    bk = min(bk, K)
    n_k = K // bk
    return pl.pallas_call(
        functools.partial(_out_proj_kernel, n_k),
        out_shape=jax.ShapeDtypeStruct((T, N), F32),
        grid=(T // bm, n_k),
        in_specs=[pl.BlockSpec((bm, bk), lambda i, k: (i, k)),
                  pl.BlockSpec((bk, N), lambda i, k: (k, 0)),
                  pl.BlockSpec((bm, N), lambda i, k: (i, 0)),
                  pl.BlockSpec((1, N), lambda i, k: (0, 0))],
        out_specs=pl.BlockSpec((bm, N), lambda i, k: (i, 0)),
        scratch_shapes=[pltpu.VMEM((bm, N) if n_k > 1 else (8, LANE), F32)],
        compiler_params=_cparams(2), name=name,
    )(a, w, h, g)


def _online_step(carry, st, st_max, vt):
    m, l, acc = carry
    m_new = jnp.maximum(m, st_max)
    alpha = jnp.exp2(m - m_new)
    p = jnp.exp2(st - m_new)
    l = alpha * l + jnp.sum(p, axis=0, keepdims=True)
    acc = alpha * acc + jnp.dot(vt, p.astype(BF16), preferred_element_type=F32)
    return m_new, l, acc


def _online_init(cols, dv):
    return (jnp.full((1, cols), NEG_INF, F32), jnp.zeros((1, cols), F32),
            jnp.zeros((dv, cols), F32))


def _flash_transposed(n_full, scores, diag_mask, vt_ref, st_ref):
    def put(t, slot):
        st = scores(t)
        st_ref[slot] = st
        return jnp.max(st, axis=0, keepdims=True)

    def half(t, carry, slot):
        nxt = put(t + 1, 1 - slot)
        return _online_step(carry[:3], st_ref[slot], carry[3], vt_ref[t]) + (nxt,)

    def pair(u, carry):
        return half(2 * u + 1, half(2 * u, carry, 0), 1)

    carry = _online_init(st_ref.shape[2], vt_ref.shape[-2]) + (put(0, 0),)
    carry = lax.fori_loop(0, n_full // 2, pair, carry)
    odd = n_full & 1
    carry = lax.fori_loop(0, odd, lambda _, c: half(n_full - 1, c, 0), carry)
    st = diag_mask(st_ref[odd])
    _, l, acc = _online_step(carry[:3], st, jnp.max(st, axis=0, keepdims=True), vt_ref[n_full])
    return acc / l


def _mla_kernel(tq, tk, q_ref, k_ref, vt_ref, o_ref, st_ref):
    t0 = pl.program_id(2) * tq
    q = q_ref[...]
    last = t0 // tk

    def scores(t):
        start = pl.multiple_of(t * tk, tk)
        return _dot_t(k_ref[pl.ds(start, tk), :], q)

    def diag_mask(st):
        key = lax.broadcasted_iota(jnp.int32, (tk, tq), 0) + last * tk
        qry = lax.broadcasted_iota(jnp.int32, (tk, tq), 1) + t0
        return jnp.where(key <= qry, st, NEG_INF)

    o_ref[...] = _flash_transposed(last, scores, diag_mask, vt_ref, st_ref).T.astype(o_ref.dtype)


def _key_tiles_t(v, n_heads, tk):
    B, S, _ = v.shape
    return v.reshape(B, S // tk, tk, n_heads, HEAD_DIM).transpose(0, 3, 1, 4, 2)


def _mla_attention(q, k, v, n_heads, tq=1024, tk=1024):
    B, S, _ = q.shape
    dq = q.shape[2] // n_heads
    nt = S // tk
    assert tk % tq == 0 and S % tk == 0
    return pl.pallas_call(
        functools.partial(_mla_kernel, tq, tk),
        out_shape=jax.ShapeDtypeStruct((B, S, n_heads * HEAD_DIM), BF16),
        grid=(B, n_heads, S // tq),
        in_specs=[pl.BlockSpec((None, tq, dq), lambda b, h, i: (b, i, h)),
                  pl.BlockSpec((None, S, dq), lambda b, h, i: (b, 0, h)),
                  pl.BlockSpec((None, None, nt, HEAD_DIM, tk), lambda b, h, i: (b, h, 0, 0, 0))],
        out_specs=pl.BlockSpec((None, tq, HEAD_DIM), lambda b, h, i: (b, i, h)),
        scratch_shapes=[pltpu.VMEM((2, tk, tq), F32)],
        compiler_params=_cparams(3), name="mla_attention",
    )(q, k, _key_tiles_t(v, n_heads, tk))


def _band_block(q, kwin, vwin, window, base):
    R = q.shape[0]
    C = window + LANE
    s = _dot_t(q, kwin)
    r = lax.broadcasted_iota(jnp.int32, (R, C), 0) & (LANE - 1)
    c = lax.broadcasted_iota(jnp.int32, (R, C), 1)
    mask = (c >= r) & (c <= r + window) & (c >= window - base)
    s = jnp.where(mask, s, NEG_INF)
    m = jnp.max(s, axis=-1, keepdims=True)
    p = jnp.exp2(s - m)
    l = jnp.sum(p, axis=-1, keepdims=True)
    o = jnp.dot(p.astype(BF16), vwin, preferred_element_type=F32) / l
    return o, m + jnp.log2(l)


def _stack_heads(x, n):
    return jnp.concatenate([x[:, h * LANE:(h + 1) * LANE] for h in range(n)], axis=0)


def _nsa_window_kernel(tq, window, q_ref, kp_ref, kc_ref, vp_ref, vc_ref, o_ref):
    i = pl.program_id(1)
    kwin = jnp.concatenate([kp_ref[...], kc_ref[...]], axis=0)
    vwin = jnp.concatenate([vp_ref[...], vc_ref[...]], axis=0)
    for j in range(tq // LANE):
        q = _stack_heads(q_ref[j * LANE:(j + 1) * LANE, :], NSA_HEADS)
        lo = j * LANE
        o, _ = _band_block(q, kwin[lo:lo + window + LANE], vwin[lo:lo + window + LANE],
                           window, i * tq + lo)
        for h in range(NSA_HEADS):
            o_ref[lo:lo + LANE, h * LANE:(h + 1) * LANE] = (
                o[h * LANE:(h + 1) * LANE].astype(o_ref.dtype))


def _nsa_window_attention(q, k, v, window):
    B, S, W = q.shape
    tq = window
    prev = lambda b, i: (b, jnp.maximum(i - 1, 0), 0)
    cur = lambda b, i: (b, i, 0)
    return pl.pallas_call(
        functools.partial(_nsa_window_kernel, tq, window),
        out_shape=jax.ShapeDtypeStruct((B, S, W), BF16),
        grid=(B, S // tq),
        in_specs=[pl.BlockSpec((None, tq, W), cur),
                  pl.BlockSpec((None, window, HEAD_DIM), prev),
                  pl.BlockSpec((None, tq, HEAD_DIM), cur),
                  pl.BlockSpec((None, window, HEAD_DIM), prev),
                  pl.BlockSpec((None, tq, HEAD_DIM), cur)],
        out_specs=pl.BlockSpec((None, tq, W), cur),
        compiler_params=_cparams(2), name="nsa_window_attention",
    )(q, k, k, v, v)


def _dilated_kernel(nb, patterns, q_ref, kp_ref, kc_ref, vp_ref, vc_ref, o_ref,
                    qf_ref, kf_ref, vf_ref, of_ref, lf_ref):
    i = pl.program_id(2)
    qf_ref[...] = q_ref[...].astype(F32)
    kf_ref[0:nb, :] = kp_ref[...].astype(F32)
    kf_ref[nb:2 * nb, :] = kc_ref[...].astype(F32)
    vf_ref[0:nb, :] = vp_ref[...].astype(F32)
    vf_ref[nb:2 * nb, :] = vc_ref[...].astype(F32)
    for pi, (window, dil) in enumerate(patterns):
        w = window // dil
        per_class = nb // dil
        for r in range(dil):
            for j in range(per_class // LANE):
                q_lo = r + j * LANE * dil
                k_lo = nb + q_lo - w * dil
                rows_q = pl.ds(q_lo, LANE, stride=dil)
                rows_k = pl.ds(k_lo, w + LANE, stride=dil)
                o, lse = _band_block(qf_ref[rows_q, :].astype(BF16),
                                     kf_ref[rows_k, :].astype(BF16),
                                     vf_ref[rows_k, :].astype(BF16),
                                     w, i * per_class + j * LANE)
                of_ref[pi, rows_q, :] = o
                lf_ref[pi, rows_q, :] = jnp.broadcast_to(lse, (LANE, LANE))
    n_pat = len(patterns)
    mx = lf_ref[0]
    for pi in range(1, n_pat):
        mx = jnp.maximum(mx, lf_ref[pi])
    es = [jnp.exp2(lf_ref[pi] - mx) for pi in range(n_pat)]
    den = es[0]
    num = es[0] * of_ref[0]
    for pi in range(1, n_pat):
        den = den + es[pi]
        num = num + es[pi] * of_ref[pi]
    o_ref[...] = (num / den).astype(o_ref.dtype)


def _dilated_attention(q, k, v, patterns, n_heads, nb=2048):
    B, S, W = q.shape
    for window, dil in patterns:
        assert window % dil == 0 and window // dil == LANE
        assert window <= nb and nb % (dil * LANE) == 0
    assert S % nb == 0
    prev = lambda b, h, i: (b, jnp.maximum(i - 1, 0), h)
    cur = lambda b, h, i: (b, i, h)
    blk = lambda index_map: pl.BlockSpec((None, nb, LANE), index_map)
    return pl.pallas_call(
        functools.partial(_dilated_kernel, nb, patterns),
        out_shape=jax.ShapeDtypeStruct((B, S, W), BF16),
        grid=(B, n_heads, S // nb),
        in_specs=[blk(cur), blk(prev), blk(cur), blk(prev), blk(cur)],
        out_specs=blk(cur),
        scratch_shapes=[pltpu.VMEM((nb, LANE), F32),
                        pltpu.VMEM((2 * nb, LANE), F32),
                        pltpu.VMEM((2 * nb, LANE), F32),
                        pltpu.VMEM((len(patterns), nb, LANE), F32),
                        pltpu.VMEM((len(patterns), nb, LANE), F32)],
        compiler_params=_cparams(3), name="dilated_attention",
    )(q, k, k, v, v)


def _gelu_tanh(x):
    return 0.5 * x * (1.0 + jnp.tanh(0.7978845608028654 * (x + 0.044715 * (x * x * x))))


def _compress_kernel(rope, t_ref, pe_ref, w1a_ref, w1b_ref, w2_ref, cos_ref, sin_ref, o_ref):
    t = t_ref[...].astype(F32)
    n = t.shape[0]
    first = jnp.dot((t + pe_ref[0:1, :]).astype(BF16), w1a_ref[...], preferred_element_type=F32)
    second = jnp.dot((t + pe_ref[1:2, :]).astype(BF16), w1b_ref[...], preferred_element_type=F32)
    hid = first + pltpu.roll(second, n - 1, 0)
    out = jnp.dot(_gelu_tanh(hid).astype(BF16), w2_ref[...], preferred_element_type=F32)
    if rope:
        out = out * cos_ref[...] + pltpu.roll(out, LANE // 2, 1) * sin_ref[...]
    o_ref[...] = out.astype(o_ref.dtype)


def _compress(t, pe2, w1a, w1b, w2, cos_c, sin_c, rope):
    B, n, K = t.shape
    whole = lambda b: (0, 0)
    per_b = lambda b: (b, 0, 0)
    return pl.pallas_call(
        functools.partial(_compress_kernel, rope),
        out_shape=jax.ShapeDtypeStruct((B, n, HEAD_DIM), BF16),
        grid=(B,),
        in_specs=[pl.BlockSpec((None, n, K), per_b),
                  pl.BlockSpec((2, K), whole),
                  pl.BlockSpec((K, HEAD_DIM), whole),
                  pl.BlockSpec((K, HEAD_DIM), whole),
                  pl.BlockSpec((HEAD_DIM, HEAD_DIM), whole),
                  pl.BlockSpec((None, n, HEAD_DIM), per_b),
                  pl.BlockSpec((None, n, HEAD_DIM), per_b)],
        out_specs=pl.BlockSpec((None, n, HEAD_DIM), per_b),
        compiler_params=_cparams(1), name="nsa_compress",
    )(t, pe2, w1a, w1b, w2, cos_c, sin_c)


def _nsa_cmp_block(tq, n_slc, n_sel, t0, q, kc, vc):
    n_cmp = kc.shape[0]
    per_blk = SLC_BLOCK // CMP_STRIDE
    slc_shift = n_slc.bit_length() - 1
    q = _stack_heads(q, NSA_HEADS)
    s = _dot_t(q, kc)
    R = NSA_HEADS * tq
    row = lax.broadcasted_iota(jnp.int32, (R, n_cmp), 0) & (tq - 1)
    col = lax.broadcasted_iota(jnp.int32, (R, n_cmp), 1)
    j_of = col & (n_slc - 1)
    r_of = col >> slc_shift
    cmp_end = (per_blk * j_of + r_of) * CMP_STRIDE + (CMP_LEN - 1)
    cmask = cmp_end <= row + t0
    s = jnp.where(cmask, s, NEG_INF)
    m = jnp.max(s, axis=-1, keepdims=True)
    e = jnp.exp2(s - m)
    p = jnp.where(cmask, e / jnp.sum(e, axis=-1, keepdims=True), 0.0)
    o = jnp.dot(p.astype(BF16), vc, preferred_element_type=F32)

    ph = p[0:tq]
    for h in range(1, NSA_HEADS):
        ph = ph + p[h * tq:(h + 1) * tq]
    groups = [ph[:, r * n_slc:(r + 1) * n_slc] for r in range(per_blk)]
    blk = lax.broadcasted_iota(jnp.int32, (tq, n_slc), 1)
    spill = jnp.where(blk == 0, 0.0, pltpu.roll(groups[per_blk - 1], 1, 1))
    imp = groups[0]
    for r in range(1, per_blk):
        imp = imp + groups[r]
    imp = imp + spill
    tpos = lax.broadcasted_iota(jnp.int32, (tq, n_slc), 0) + t0
    cur = tpos >> (SLC_BLOCK.bit_length() - 1)
    forced = (blk == 0) | (blk == cur) | (blk == cur - 1)
    valid = blk <= cur
    work = jnp.where(forced, BIG, jnp.where(valid, imp, -BIG))
    blk_f = blk.astype(F32)
    sel = jnp.zeros((tq, n_slc), F32)
    for _ in range(n_sel):
        mx = jnp.max(work, axis=-1, keepdims=True)
        first = jnp.min(jnp.where(work == mx, blk_f, float(n_slc)), axis=-1, keepdims=True)
        pick = blk_f == first
        sel = jnp.where(pick, 1.0, sel)
        work = jnp.where(pick, -jnp.inf, work)
    return o, jnp.where((sel > 0.0) & valid, 0.0, NEG_INF)


def _nsa_cmp_kernel(tq, n_sub, n_slc, n_sel, q_ref, kc_ref, vc_ref, o_ref, pen_ref):
    i = pl.program_id(1)
    for u in range(n_sub):
        rows = slice(u * tq, (u + 1) * tq)
        o, pen = _nsa_cmp_block(tq, n_slc, n_sel, (i * n_sub + u) * tq, q_ref[rows, :],
                                kc_ref[...], vc_ref[...])
        for h in range(NSA_HEADS):
            o_ref[rows, h * LANE:(h + 1) * LANE] = o[h * tq:(h + 1) * tq].astype(o_ref.dtype)
        pen = pen.astype(pen_ref.dtype)
        for c in range(n_slc // LANE):
            pen_ref[c, rows, :] = pen[:, c * LANE:(c + 1) * LANE]


def _nsa_compressed(q, k_c, v_c, tq=128, n_sub=4):
    B, S, W = q.shape
    n_cmp = k_c.shape[1]
    n_slc = S // SLC_BLOCK
    n_sel = min(SLC_TOPK, n_slc)
    assert n_slc % LANE == 0 and n_slc & (n_slc - 1) == 0
    assert n_cmp == 4 * n_slc and tq & (tq - 1) == 0
    n_pc = n_slc // LANE
    bq = tq * n_sub
    return pl.pallas_call(
        functools.partial(_nsa_cmp_kernel, tq, n_sub, n_slc, n_sel),
        out_shape=[jax.ShapeDtypeStruct((B, S, W), BF16),
                   jax.ShapeDtypeStruct((B, n_pc, S, LANE), BF16)],
        grid=(B, S // bq),
        in_specs=[pl.BlockSpec((None, bq, W), lambda b, i: (b, i, 0)),
                  pl.BlockSpec((None, n_cmp, HEAD_DIM), lambda b, i: (b, 0, 0)),
                  pl.BlockSpec((None, n_cmp, HEAD_DIM), lambda b, i: (b, 0, 0))],
        out_specs=[pl.BlockSpec((None, bq, W), lambda b, i: (b, i, 0)),
                   pl.BlockSpec((None, n_pc, bq, LANE), lambda b, i: (b, 0, i, 0))],
        compiler_params=_cparams(2), name="nsa_compressed_topk",
    )(q, k_c, v_c)


def _nsa_sel_kernel(tq, tk, q_ref, pen_ref, k_ref, vt_ref, o_ref, qa_ref, st_ref):
    i = pl.program_id(1)
    t0 = i * tq
    R = NSA_HEADS * tq
    last = t0 // tk
    tiles_per_chunk = PEN_CHUNK // tk
    q = _stack_heads(q_ref[...], NSA_HEADS)
    for c in range(pen_ref.shape[0]):
        qa_ref[c, :, 0:LANE] = q
        qa_ref[c, :, LANE:2 * LANE] = jnp.concatenate([pen_ref[c]] * NSA_HEADS, axis=0)

    def scores(t):
        start = pl.multiple_of(t * tk, tk)
        return _dot_t(k_ref[pl.ds(start, tk), :], qa_ref[t // tiles_per_chunk])

    def diag_mask(st):
        key = lax.broadcasted_iota(jnp.int32, (tk, R), 0) + last * tk
        qry = (lax.broadcasted_iota(jnp.int32, (tk, R), 1) & (tq - 1)) + t0
        return jnp.where(key <= qry, st, NEG_INF)

    o = _flash_transposed(last, scores, diag_mask, vt_ref, st_ref)
    for h in range(NSA_HEADS):
        o_ref[:, h * LANE:(h + 1) * LANE] = o[:, h * tq:(h + 1) * tq].T.astype(o_ref.dtype)


def _nsa_selected(q, pen, k_aug, v, tq=256, tk=1024):
    B, S, W = q.shape
    n_pc = pen.shape[1]
    nt = S // tk
    assert tk % tq == 0 and PEN_CHUNK % tk == 0 and tq & (tq - 1) == 0
    vt = _key_tiles_t(v, 1, tk)[:, 0]
    return pl.pallas_call(
        functools.partial(_nsa_sel_kernel, tq, tk),
        out_shape=jax.ShapeDtypeStruct((B, S, W), BF16),
        grid=(B, S // tq),
        in_specs=[pl.BlockSpec((None, tq, W), lambda b, i: (b, i, 0)),
                  pl.BlockSpec((None, n_pc, tq, LANE), lambda b, i: (b, 0, i, 0)),
                  pl.BlockSpec((None, S, 2 * HEAD_DIM), lambda b, i: (b, 0, 0)),
                  pl.BlockSpec((None, nt, HEAD_DIM, tk), lambda b, i: (b, 0, 0, 0))],
        out_specs=pl.BlockSpec((None, tq, W), lambda b, i: (b, i, 0)),
        scratch_shapes=[pltpu.VMEM((n_pc, NSA_HEADS * tq, 2 * HEAD_DIM), BF16),
                        pltpu.VMEM((2, tk, NSA_HEADS * tq), F32)],
        compiler_params=_cparams(2), name="nsa_selected_attention",
    )(q, pen, k_aug, vt)


def _combine_kernel(oc_ref, os_ref, ow_ref, gate_ref, ob_ref, od_ref, o_ref):
    g = jax.nn.sigmoid(gate_ref[...])
    for h in range(NSA_HEADS):
        hs = slice(h * LANE, (h + 1) * LANE)
        a = (g[:, 3 * h:3 * h + 1] * oc_ref[:, hs].astype(F32)
             + g[:, 3 * h + 1:3 * h + 2] * os_ref[:, hs].astype(F32)
             + g[:, 3 * h + 2:3 * h + 3] * ow_ref[:, hs].astype(F32))
        o_ref[:, hs] = a.astype(o_ref.dtype)
    off = NSA_HEADS * LANE
    o_ref[:, off:off + MLA_HEADS * LANE] = ob_ref[...]
    off += MLA_HEADS * LANE
    o_ref[:, off:off + DIL_HEADS * LANE] = od_ref[...]


def _combine(o_c, o_s, o_w, gate, o_b, o_d, bm=512):
    T = o_c.shape[0]
    width = (NSA_HEADS + MLA_HEADS + DIL_HEADS) * LANE
    args = (o_c, o_s, o_w, gate, o_b, o_d)
    return pl.pallas_call(
        _combine_kernel,
        out_shape=jax.ShapeDtypeStruct((T, width), BF16),
        grid=(T // bm,),
        in_specs=[pl.BlockSpec((bm, a.shape[1]), lambda i: (i, 0)) for a in args],
        out_specs=pl.BlockSpec((bm, width), lambda i: (i, 0)),
        compiler_params=_cparams(1), name="mixer_combine",
    )(*args)


def _xattn_kernel(n_heads, q_ref, k_ref, v_ref, o_ref):
    for h in range(n_heads):
        hs = slice(h * LANE, (h + 1) * LANE)
        s = _dot_t(q_ref[:, hs], k_ref[:, hs])
        m = jnp.max(s, axis=-1, keepdims=True)
        p = jnp.exp2(s - m)
        l = jnp.sum(p, axis=-1, keepdims=True)
        o = jnp.dot(p.astype(BF16), v_ref[:, hs], preferred_element_type=F32) / l
        o_ref[:, hs] = o.astype(o_ref.dtype)


def _cross_attention(q, k, v, n_heads, tq=512):
    B, S, W = q.shape
    M = k.shape[1]
    return pl.pallas_call(
        functools.partial(_xattn_kernel, n_heads),
        out_shape=jax.ShapeDtypeStruct((B, S, W), BF16),
        grid=(B, S // tq),
        in_specs=[pl.BlockSpec((None, tq, W), lambda b, i: (b, i, 0)),
                  pl.BlockSpec((None, M, W), lambda b, i: (b, 0, 0)),
                  pl.BlockSpec((None, M, W), lambda b, i: (b, 0, 0))],
        out_specs=pl.BlockSpec((None, tq, W), lambda b, i: (b, i, 0)),
        compiler_params=_cparams(2), name="cross_attention",
    )(q, k, v)


def _rope_tables(pos, dim):
    inv = ROPE_THETA ** (-jnp.arange(0, dim, 2, dtype=F32) / dim)
    ang = pos.astype(F32)[..., None] * inv
    return jnp.cos(ang), jnp.sin(ang)


def _full_tables(pos):
    c, s = _rope_tables(pos, HEAD_DIM)
    return jnp.concatenate([c, c], -1), jnp.concatenate([-s, s], -1)


def _spread_rope64(t):
    z = jnp.zeros(t.shape[:-1] + (QK_ROPE // 2,), t.dtype)
    return jnp.concatenate([t[..., :QK_ROPE // 2], z, t[..., QK_ROPE // 2:], z], -1)


def _small_tables(pos):
    c, s = _rope_tables(pos, QK_ROPE)
    return _spread_rope64(jnp.concatenate([c, c], -1)), _spread_rope64(jnp.concatenate([-s, s], -1))


def _layer(h, mem, tabs, tabs_c, onehot, p):
    B, S, D = h.shape
    T = B * S
    dh = HEAD_DIM
    scale = dh ** -0.5 * LOG2E
    h2 = h.reshape(T, D)

    w_in = p["w_in"]
    cuts = np.cumsum([NSA_HEADS * dh, 6 * dh, 3 * NSA_HEADS, Q_LORA, KV_LORA, QK_ROPE])
    w_q, w_kv, w_g, w_cq, w_ckv, w_kr, w_dil = jnp.split(w_in, cuts, axis=1)
    kv = [w_kv[:, k * dh:(k + 1) * dh] for k in range(6)]
    dw = DIL_HEADS * dh
    w_dq, w_dk, w_dv = w_dil[:, :dw], w_dil[:, dw:2 * dw], w_dil[:, 2 * dw:]
    w_rope = jnp.concatenate([w_q, kv[2], kv[4], w_dq, w_dk, _spread_rope64(w_kr)], 1).astype(BF16)
    w_gpad = jnp.pad(w_g, ((0, 0), (0, LANE - w_g.shape[1])))
    w_plain = jnp.concatenate([kv[0], kv[1], kv[3], kv[5], w_gpad, w_cq, w_ckv, w_dv], 1).astype(BF16)

    nsa_q, k_s, k_w, dq, dk, k_pe = _proj(
        h2, p["g_mix_pre"], w_rope,
        [(("rope",) * NSA_HEADS, scale), (("rope",), 1.0), (("rope",), 1.0),
         (("rope",) * DIL_HEADS, scale), (("rope",) * DIL_HEADS, 1.0), (("rope_r",), 1.0)],
        [BF16] * 6, tabs=tabs, name="in_proj_rope")
    k_cr, v_cr, v_s, v_w, gate, cq, ckv, dv = _proj(
        h2, p["g_mix_pre"], w_plain,
        [(("none",), 1.0)] * 4 + [(("none",), 1.0), (("none",) * 4, 1.0), (("none",) * 4, 1.0),
                                  (("none",) * DIL_HEADS, 1.0)],
        [BF16] * 4 + [F32, BF16, BF16, BF16], name="in_proj_plain")

    n_chunk = S // CMP_STRIDE
    half = CMP_LEN // 2
    pe2 = p["cmp_pos_emb"].reshape(2, half * dh)
    cos_c, sin_c = tabs_c

    def compress(t, w1, w2, rope):
        out = _compress(t.reshape(B, n_chunk, CMP_STRIDE * dh), pe2,
                        w1[:half * dh].astype(BF16), w1[half * dh:].astype(BF16),
                        w2.astype(BF16), cos_c, sin_c, rope)
        return out.reshape(B, n_chunk // 4, 4, dh).transpose(0, 2, 1, 3).reshape(B, n_chunk, dh)

    k_c = compress(k_cr, p["w_cmp_k1"], p["w_cmp_k2"], True)
    v_c = compress(v_cr, p["w_cmp_v1"], p["w_cmp_v2"], False)
    q3 = nsa_q.reshape(B, S, NSA_HEADS * dh)
    o_cmp, pen = _nsa_compressed(q3, k_c, v_c)
    k_aug = jnp.concatenate([k_s.reshape(B, S, dh), onehot], axis=-1)
    o_sel = _nsa_selected(q3, pen, k_aug, v_s.reshape(B, S, dh))
    o_win = _nsa_window_attention(q3, k_w.reshape(B, S, dh), v_w.reshape(B, S, dh), NSA_WINDOW)

    dqk = QK_NOPE + QK_ROPE
    w_uq = p["w_uq"].reshape(Q_LORA, MLA_HEADS, dqk)
    w_uq = jnp.concatenate([w_uq[..., :QK_NOPE], _spread_rope64(w_uq[..., QK_NOPE:])], -1)
    w_uq = w_uq.reshape(Q_LORA, MLA_HEADS * 2 * dh).astype(BF16)
    (q_m,) = _proj(cq, p["g_q_lora"], w_uq, [(("none", "rope_r") * MLA_HEADS, dqk ** -0.5 * LOG2E)],
                   [BF16], tabs=tabs, name="mla_q_up")
    w_ukv = p["w_ukv"].reshape(KV_LORA, MLA_HEADS, 2 * dh)
    w_ukv = jnp.concatenate([w_ukv[..., :dh].reshape(KV_LORA, -1),
                             w_ukv[..., dh:].reshape(KV_LORA, -1)], 1).astype(BF16)
    k_m, v_m = _proj(ckv, p["g_kv_lora"], w_ukv,
                     [(("none", "extra") * MLA_HEADS, 1.0), (("none",) * MLA_HEADS, 1.0)],
                     [BF16, BF16], extra=k_pe, name="mla_kv_up")
    o_mla = _mla_attention(q_m.reshape(B, S, -1), k_m.reshape(B, S, -1),
                           v_m.reshape(B, S, -1), MLA_HEADS)

    o_dil = _dilated_attention(dq.reshape(B, S, dw), dk.reshape(B, S, dw), dv.reshape(B, S, dw),
                               DIL_PATTERNS, DIL_HEADS)

    mixed = _combine(o_cmp.reshape(T, -1), o_sel.reshape(T, -1), o_win.reshape(T, -1), gate,
                     o_mla.reshape(T, -1), o_dil.reshape(T, dw))
    h2 = _out_proj(mixed, p["w_out"].astype(BF16), h2, p["g_mix_post"], name="mix_out_proj")

    xw = XATTN_HEADS * dh
    (xq,) = _proj(h2, p["g_mem_pre"], p["w_xq"].astype(BF16),
                  [(("none",) * XATTN_HEADS, scale)], [BF16], name="xattn_q")
    M = mem.shape[1]
    xk, xv = _proj(mem.reshape(B * M, D), p["g_mem_kv"], p["w_xkv"].astype(BF16),
                   [(("none",) * XATTN_HEADS, 1.0), (("none",) * XATTN_HEADS, 1.0)],
                   [BF16, BF16], bm=min(512, B * M), name="xattn_kv")
    xo = _cross_attention(xq.reshape(B, S, xw), xk.reshape(B, M, xw), xv.reshape(B, M, xw),
                          XATTN_HEADS)
    h2 = _out_proj(xo.reshape(T, xw), p["w_xo"].astype(BF16), h2, p["g_mem_post"],
                   name="xattn_out_proj")

    up = _mlp_up(h2, p["g_mlp_pre"], p["w_up"].astype(BF16))
    h2 = _out_proj(up, p["w_down"].astype(BF16), h2, p["g_mlp_post"], name="mlp_down")
    return h2.reshape(B, S, D)


_LAYER_PARAMS = ("g_mix_pre", "w_in", "cmp_pos_emb", "w_cmp_k1", "w_cmp_k2", "w_cmp_v1", "w_cmp_v2",
                 "g_q_lora", "g_kv_lora", "w_uq", "w_ukv", "w_out", "g_mix_post", "g_mem_pre",
                 "g_mem_kv", "w_xq", "w_xkv", "w_xo", "g_mem_post", "g_mlp_pre", "w_up", "w_down",
                 "g_mlp_post")


def kernel(x, mem, positions, g_mix_pre, w_in, cmp_pos_emb, w_cmp_k1, w_cmp_k2, w_cmp_v1, w_cmp_v2, g_q_lora, g_kv_lora, w_uq, w_ukv, w_out, g_mix_post, g_mem_pre, g_mem_kv, w_xq, w_xkv, w_xo, g_mem_post, g_mlp_pre, w_up, w_down, g_mlp_post):
    stacked = dict(zip(_LAYER_PARAMS, (
        g_mix_pre, w_in, cmp_pos_emb, w_cmp_k1, w_cmp_k2, w_cmp_v1, w_cmp_v2, g_q_lora, g_kv_lora,
        w_uq, w_ukv, w_out, g_mix_post, g_mem_pre, g_mem_kv, w_xq, w_xkv, w_xo, g_mem_post,
        g_mlp_pre, w_up, w_down, g_mlp_post)))
    B, S, D = x.shape
    T = B * S
    assert S % PEN_CHUNK == 0
    cosf, sinf = _full_tables(positions)
    cosr, sinr = _small_tables(positions)
    tabs = tuple(t.reshape(T, LANE) for t in (cosf, sinf, cosr, sinr))
    n_chunk = S // CMP_STRIDE
    end = jnp.minimum(jnp.arange(n_chunk) * CMP_STRIDE + CMP_LEN - 1, S - 1)
    tabs_c = _full_tables(positions[:, end])
    blk = (jnp.arange(S) // SLC_BLOCK) % LANE
    onehot = jnp.broadcast_to((blk[:, None] == jnp.arange(LANE)[None, :]).astype(BF16)[None],
                              (B, S, LANE))
    h = x
    for layer in range(stacked["w_in"].shape[0]):
        p = {}
        for name, val in stacked.items():
            v = val[layer]
            p[name] = v[None, :] if name.startswith("g_") else v
        h = _layer(h, mem, tabs, tabs_c, onehot, p)
    return h
```

```python
import functools

import numpy as np
import jax
import jax.numpy as jnp
from jax import lax
from jax.experimental import pallas as pl
from jax.experimental.pallas import tpu as pltpu

F32 = jnp.float32
BF16 = jnp.bfloat16

LANE = 128
VMEM_LIMIT = 56 * 1024 * 1024

HEAD_DIM = 128
ROPE_THETA = 10000.0
NORM_EPS = 1e-6
NEG_INF = -1e30
BIG = 1e9
LOG2E = 1.4426950408889634
NSA_HEADS = 4
MLA_HEADS = 6
DIL_HEADS = 6
CMP_LEN = 32
CMP_STRIDE = 16
SLC_BLOCK = 64
SLC_TOPK = 16
NSA_WINDOW = 512
Q_LORA = 512
KV_LORA = 512
QK_NOPE = 128
QK_ROPE = 64
DIL_PATTERNS = ((128, 1), (512, 4), (2048, 16))
XATTN_HEADS = 4
PEN_CHUNK = LANE * SLC_BLOCK


def _cparams(n_grid):
    return pltpu.CompilerParams(
        dimension_semantics=("arbitrary",) * n_grid, vmem_limit_bytes=VMEM_LIMIT)


def _rms(x, g):
    return x * lax.rsqrt(jnp.mean(x * x, axis=-1, keepdims=True) + NORM_EPS) * g


def _dot_t(a, b):
    return lax.dot_general(a, b, (((1,), (1,)), ((), ())), preferred_element_type=F32)


def _proj_kernel(out_plan, has_norm, n_tab, has_extra, w_t, chunk, *refs):
    x_ref, g_ref, w_ref = refs[:3]
    tab_refs = refs[3:3 + n_tab]
    n_in = 3 + n_tab + int(has_extra)
    out_refs = refs[n_in:]
    x = x_ref[...].astype(F32)
    if has_norm:
        x = _rms(x, g_ref[...])
    xb = x.astype(BF16)
    n_cols = w_ref.shape[0 if w_t else 1]
    flat = []
    for oi, (modes, scale) in enumerate(out_plan):
        for k, mode in enumerate(modes):
            if mode == "extra":
                out_refs[oi][:, k * LANE:(k + 1) * LANE] = refs[n_in - 1][...]
            else:
                flat.append((oi, k * LANE, mode, scale))
    for c0 in range(0, n_cols, chunk):
        c1 = min(c0 + chunk, n_cols)
        if w_t:
            acc = _dot_t(xb, w_ref[c0:c1, :])
        else:
            acc = jnp.dot(xb, w_ref[:, c0:c1], preferred_element_type=F32)
        for s in range((c1 - c0) // LANE):
            oi, off, mode, scale = flat[c0 // LANE + s]
            a = acc[:, s * LANE:(s + 1) * LANE]
            if mode == "rope":
                a = a * tab_refs[0][...] + pltpu.roll(a, LANE // 2, 1) * tab_refs[1][...]
            elif mode == "rope_r":
                a = a * tab_refs[2][...] + pltpu.roll(a, LANE // 2, 1) * tab_refs[3][...]
            if scale != 1.0:
                a = a * scale
            out_refs[oi][:, off:off + LANE] = a.astype(out_refs[oi].dtype)


def _proj(x, g, w, out_plan, out_dtypes, tabs=(), extra=None, w_t=False, bm=512, chunk=512,
          name="proj"):
    T, K = x.shape
    N = w.shape[0 if w_t else 1]
    assert T % bm == 0 and N % LANE == 0
    assert sum(sum(md != "extra" for md in m) for m, _ in out_plan) * LANE == N
    has_norm = g is not None
    if g is None:
        g = jnp.ones((1, K), F32)
    extras = () if extra is None else (extra,)
    in_specs = [pl.BlockSpec((bm, K), lambda i: (i, 0)),
                pl.BlockSpec((1, K), lambda i: (0, 0)),
                pl.BlockSpec(w.shape, lambda i: (0, 0), pipeline_mode=pl.Buffered(1))]
    in_specs += [pl.BlockSpec((bm, LANE), lambda i: (i, 0)) for _ in tabs + extras]
    out_shape = [jax.ShapeDtypeStruct((T, len(m) * LANE), dt)
                 for (m, _), dt in zip(out_plan, out_dtypes)]
    out_specs = [pl.BlockSpec((bm, len(m) * LANE), lambda i: (i, 0)) for m, _ in out_plan]
    return pl.pallas_call(
        functools.partial(_proj_kernel, out_plan, has_norm, len(tabs), extra is not None, w_t,
                          chunk),
        out_shape=out_shape, grid=(T // bm,), in_specs=in_specs, out_specs=out_specs,
        compiler_params=_cparams(1), name=name,
    )(x, g, w, *tabs, *extras)


def _mlp_up_kernel(x_ref, g_ref, w_ref, o_ref, xn_ref):
    @pl.when(pl.program_id(1) == 0)
    def _():
        xn_ref[...] = _rms(x_ref[...], g_ref[...]).astype(BF16)
    a = jnp.dot(xn_ref[...], w_ref[...], preferred_element_type=F32)
    a = jnp.maximum(a, 0.0)
    o_ref[...] = (a * a).astype(o_ref.dtype)


def _mlp_up(x, g, w, bm=1024, bn=1024):
    T, K = x.shape
    N = w.shape[1]
    return pl.pallas_call(
        _mlp_up_kernel,
        out_shape=jax.ShapeDtypeStruct((T, N), BF16),
        grid=(T // bm, N // bn),
        in_specs=[pl.BlockSpec((bm, K), lambda i, j: (i, 0)),
                  pl.BlockSpec((1, K), lambda i, j: (0, 0)),
                  pl.BlockSpec((K, bn), lambda i, j: (0, j))],
        out_specs=pl.BlockSpec((bm, bn), lambda i, j: (i, j)),
        scratch_shapes=[pltpu.VMEM((bm, K), BF16)],
        compiler_params=_cparams(2), name="mlp_up",
    )(x, g, w)


def _out_proj_kernel(n_k, a_ref, w_ref, h_ref, g_ref, o_ref, acc_ref):
    k = pl.program_id(1)
    part = jnp.dot(a_ref[...], w_ref[...], preferred_element_type=F32)

    def finish(y):
        o_ref[...] = h_ref[...] + _rms(y, g_ref[...])

    if n_k == 1:
        finish(part)
    else:
        @pl.when(k == 0)
        def _():
            acc_ref[...] = part

        @pl.when(jnp.logical_and(k > 0, k < n_k - 1))
        def _():
            acc_ref[...] += part

        @pl.when(k == n_k - 1)
        def _():
            finish(acc_ref[...] + part)


def _out_proj(a, w, h, g, bm=512, bk=2048, name="out_proj"):
    T, K = a.shape
    N = w.shape[1]
    bk = min(bk, K)
    n_k = K // bk
    return pl.pallas_call(
        functools.partial(_out_proj_kernel, n_k),
        out_shape=jax.ShapeDtypeStruct((T, N), F32),
        grid=(T // bm, n_k),
        in_specs=[pl.BlockSpec((bm, bk), lambda i, k: (i, k)),
                  pl.BlockSpec((bk, N), lambda i, k: (k, 0)),
                  pl.BlockSpec((bm, N), lambda i, k: (i, 0)),
                  pl.BlockSpec((1, N), lambda i, k: (0, 0))],
        out_specs=pl.BlockSpec((bm, N), lambda i, k: (i, 0)),
        scratch_shapes=[pltpu.VMEM((bm, N) if n_k > 1 else (8, LANE), F32)],
        compiler_params=_cparams(2), name=name,
    )(a, w, h, g)


def _online_step(carry, st, st_max, vt):
    m, l, acc = carry
    m_new = jnp.maximum(m, st_max)
    alpha = jnp.exp2(m - m_new)
    p = jnp.exp2(st - m_new)
    l = alpha * l + jnp.sum(p, axis=0, keepdims=True)
    acc = alpha * acc + jnp.dot(vt, p.astype(BF16), preferred_element_type=F32)
    return m_new, l, acc


def _online_init(cols, dv):
    return (jnp.full((1, cols), NEG_INF, F32), jnp.zeros((1, cols), F32),
            jnp.zeros((dv, cols), F32))


def _flash_transposed(n_full, scores, diag_mask, vt_ref, st_ref):
    def put(t, slot):
        st = scores(t)
        st_ref[slot] = st
        return jnp.max(st, axis=0, keepdims=True)

    def half(t, carry, slot):
        nxt = put(t + 1, 1 - slot)
        return _online_step(carry[:3], st_ref[slot], carry[3], vt_ref[t]) + (nxt,)

    def pair(u, carry):
        return half(2 * u + 1, half(2 * u, carry, 0), 1)

    carry = _online_init(st_ref.shape[2], vt_ref.shape[-2]) + (put(0, 0),)
    carry = lax.fori_loop(0, n_full // 2, pair, carry)
    odd = n_full & 1
    carry = lax.fori_loop(0, odd, lambda _, c: half(n_full - 1, c, 0), carry)
    st = diag_mask(st_ref[odd])
    _, l, acc = _online_step(carry[:3], st, jnp.max(st, axis=0, keepdims=True), vt_ref[n_full])
    return acc / l


def _mla_kernel(tq, tk, q_ref, k_ref, vt_ref, o_ref, st_ref):
    t0 = pl.program_id(2) * tq
    q = q_ref[...]
    last = t0 // tk

    def scores(t):
        start = pl.multiple_of(t * tk, tk)
        return _dot_t(k_ref[pl.ds(start, tk), :], q)

    def diag_mask(st):
        key = lax.broadcasted_iota(jnp.int32, (tk, tq), 0) + last * tk
        qry = lax.broadcasted_iota(jnp.int32, (tk, tq), 1) + t0
        return jnp.where(key <= qry, st, NEG_INF)

    o_ref[...] = _flash_transposed(last, scores, diag_mask, vt_ref, st_ref).T.astype(o_ref.dtype)


def _key_tiles_t(v, n_heads, tk):
    B, S, _ = v.shape
    return v.reshape(B, S // tk, tk, n_heads, HEAD_DIM).transpose(0, 3, 1, 4, 2)


def _mla_attention(q, k, v, n_heads, tq=1024, tk=1024):
    B, S, _ = q.shape
    dq = q.shape[2] // n_heads
    nt = S // tk
    assert tk % tq == 0 and S % tk == 0
    return pl.pallas_call(
        functools.partial(_mla_kernel, tq, tk),
        out_shape=jax.ShapeDtypeStruct((B, S, n_heads * HEAD_DIM), BF16),
        grid=(B, n_heads, S // tq),
        in_specs=[pl.BlockSpec((None, tq, dq), lambda b, h, i: (b, i, h)),
                  pl.BlockSpec((None, S, dq), lambda b, h, i: (b, 0, h)),
                  pl.BlockSpec((None, None, nt, HEAD_DIM, tk), lambda b, h, i: (b, h, 0, 0, 0))],
        out_specs=pl.BlockSpec((None, tq, HEAD_DIM), lambda b, h, i: (b, i, h)),
        scratch_shapes=[pltpu.VMEM((2, tk, tq), F32)],
        compiler_params=_cparams(3), name="mla_attention",
    )(q, k, _key_tiles_t(v, n_heads, tk))


def _band_block(q, kwin, vwin, window, base):
    R = q.shape[0]
    C = window + LANE
    s = _dot_t(q, kwin)
    r = lax.broadcasted_iota(jnp.int32, (R, C), 0) & (LANE - 1)
    c = lax.broadcasted_iota(jnp.int32, (R, C), 1)
    mask = (c >= r) & (c <= r + window) & (c >= window - base)
    s = jnp.where(mask, s, NEG_INF)
    m = jnp.max(s, axis=-1, keepdims=True)
    p = jnp.exp2(s - m)
    l = jnp.sum(p, axis=-1, keepdims=True)
    o = jnp.dot(p.astype(BF16), vwin, preferred_element_type=F32) / l
    return o, m + jnp.log2(l)


def _stack_heads(x, n):
    return jnp.concatenate([x[:, h * LANE:(h + 1) * LANE] for h in range(n)], axis=0)


def _nsa_window_kernel(tq, window, q_ref, kp_ref, kc_ref, vp_ref, vc_ref, o_ref):
    i = pl.program_id(1)
    kwin = jnp.concatenate([kp_ref[...], kc_ref[...]], axis=0)
    vwin = jnp.concatenate([vp_ref[...], vc_ref[...]], axis=0)
    for j in range(tq // LANE):
        q = _stack_heads(q_ref[j * LANE:(j + 1) * LANE, :], NSA_HEADS)
        lo = j * LANE
        o, _ = _band_block(q, kwin[lo:lo + window + LANE], vwin[lo:lo + window + LANE],
                           window, i * tq + lo)
        for h in range(NSA_HEADS):
            o_ref[lo:lo + LANE, h * LANE:(h + 1) * LANE] = (
                o[h * LANE:(h + 1) * LANE].astype(o_ref.dtype))


def _nsa_window_attention(q, k, v, window):
    B, S, W = q.shape
    tq = window
    prev = lambda b, i: (b, jnp.maximum(i - 1, 0), 0)
    cur = lambda b, i: (b, i, 0)
    return pl.pallas_call(
        functools.partial(_nsa_window_kernel, tq, window),
        out_shape=jax.ShapeDtypeStruct((B, S, W), BF16),
        grid=(B, S // tq),
        in_specs=[pl.BlockSpec((None, tq, W), cur),
                  pl.BlockSpec((None, window, HEAD_DIM), prev),
                  pl.BlockSpec((None, tq, HEAD_DIM), cur),
                  pl.BlockSpec((None, window, HEAD_DIM), prev),
                  pl.BlockSpec((None, tq, HEAD_DIM), cur)],
        out_specs=pl.BlockSpec((None, tq, W), cur),
        compiler_params=_cparams(2), name="nsa_window_attention",
    )(q, k, k, v, v)


def _dilated_kernel(nb, patterns, q_ref, kp_ref, kc_ref, vp_ref, vc_ref, o_ref,
                    qf_ref, kf_ref, vf_ref, of_ref, lf_ref):
    i = pl.program_id(2)
    qf_ref[...] = q_ref[...].astype(F32)
    kf_ref[0:nb, :] = kp_ref[...].astype(F32)
    kf_ref[nb:2 * nb, :] = kc_ref[...].astype(F32)
    vf_ref[0:nb, :] = vp_ref[...].astype(F32)
    vf_ref[nb:2 * nb, :] = vc_ref[...].astype(F32)
    for pi, (window, dil) in enumerate(patterns):
        w = window // dil
        per_class = nb // dil
        for r in range(dil):
            for j in range(per_class // LANE):
                q_lo = r + j * LANE * dil
                k_lo = nb + q_lo - w * dil
                rows_q = pl.ds(q_lo, LANE, stride=dil)
                rows_k = pl.ds(k_lo, w + LANE, stride=dil)
                o, lse = _band_block(qf_ref[rows_q, :].astype(BF16),
                                     kf_ref[rows_k, :].astype(BF16),
                                     vf_ref[rows_k, :].astype(BF16),
                                     w, i * per_class + j * LANE)
                of_ref[pi, rows_q, :] = o
                lf_ref[pi, rows_q, :] = jnp.broadcast_to(lse, (LANE, LANE))
    n_pat = len(patterns)
    mx = lf_ref[0]
    for pi in range(1, n_pat):
        mx = jnp.maximum(mx, lf_ref[pi])
    es = [jnp.exp2(lf_ref[pi] - mx) for pi in range(n_pat)]
    den = es[0]
    num = es[0] * of_ref[0]
    for pi in range(1, n_pat):
        den = den + es[pi]
        num = num + es[pi] * of_ref[pi]
    o_ref[...] = (num / den).astype(o_ref.dtype)


def _dilated_attention(q, k, v, patterns, n_heads, nb=2048):
    B, S, W = q.shape
    for window, dil in patterns:
        assert window % dil == 0 and window // dil == LANE
        assert window <= nb and nb % (dil * LANE) == 0
    assert S % nb == 0
    prev = lambda b, h, i: (b, jnp.maximum(i - 1, 0), h)
    cur = lambda b, h, i: (b, i, h)
    blk = lambda index_map: pl.BlockSpec((None, nb, LANE), index_map)
    return pl.pallas_call(
        functools.partial(_dilated_kernel, nb, patterns),
        out_shape=jax.ShapeDtypeStruct((B, S, W), BF16),
        grid=(B, n_heads, S // nb),
        in_specs=[blk(cur), blk(prev), blk(cur), blk(prev), blk(cur)],
        out_specs=blk(cur),
        scratch_shapes=[pltpu.VMEM((nb, LANE), F32),
                        pltpu.VMEM((2 * nb, LANE), F32),
                        pltpu.VMEM((2 * nb, LANE), F32),
                        pltpu.VMEM((len(patterns), nb, LANE), F32),
                        pltpu.VMEM((len(patterns), nb, LANE), F32)],
        compiler_params=_cparams(3), name="dilated_attention",
    )(q, k, k, v, v)


def _gelu_tanh(x):
    return 0.5 * x * (1.0 + jnp.tanh(0.7978845608028654 * (x + 0.044715 * (x * x * x))))


def _compress_kernel(rope, t_ref, pe_ref, w1a_ref, w1b_ref, w2_ref, cos_ref, sin_ref, o_ref):
    t = t_ref[...].astype(F32)
    n = t.shape[0]
    first = jnp.dot((t + pe_ref[0:1, :]).astype(BF16), w1a_ref[...], preferred_element_type=F32)
    second = jnp.dot((t + pe_ref[1:2, :]).astype(BF16), w1b_ref[...], preferred_element_type=F32)
    hid = first + pltpu.roll(second, n - 1, 0)
    out = jnp.dot(_gelu_tanh(hid).astype(BF16), w2_ref[...], preferred_element_type=F32)
    if rope:
        out = out * cos_ref[...] + pltpu.roll(out, LANE // 2, 1) * sin_ref[...]
    o_ref[...] = out.astype(o_ref.dtype)


def _compress(t, pe2, w1a, w1b, w2, cos_c, sin_c, rope):
    B, n, K = t.shape
    whole = lambda b: (0, 0)
    per_b = lambda b: (b, 0, 0)
    return pl.pallas_call(
        functools.partial(_compress_kernel, rope),
        out_shape=jax.ShapeDtypeStruct((B, n, HEAD_DIM), BF16),
        grid=(B,),
        in_specs=[pl.BlockSpec((None, n, K), per_b),
                  pl.BlockSpec((2, K), whole),
                  pl.BlockSpec((K, HEAD_DIM), whole),
                  pl.BlockSpec((K, HEAD_DIM), whole),
                  pl.BlockSpec((HEAD_DIM, HEAD_DIM), whole),
                  pl.BlockSpec((None, n, HEAD_DIM), per_b),
                  pl.BlockSpec((None, n, HEAD_DIM), per_b)],
        out_specs=pl.BlockSpec((None, n, HEAD_DIM), per_b),
        compiler_params=_cparams(1), name="nsa_compress",
    )(t, pe2, w1a, w1b, w2, cos_c, sin_c)


def _nsa_cmp_block(tq, n_slc, n_sel, t0, q, kc, vc):
    n_cmp = kc.shape[0]
    per_blk = SLC_BLOCK // CMP_STRIDE
    slc_shift = n_slc.bit_length() - 1
    q = _stack_heads(q, NSA_HEADS)
    s = _dot_t(q, kc)
    R = NSA_HEADS * tq
    row = lax.broadcasted_iota(jnp.int32, (R, n_cmp), 0) & (tq - 1)
    col = lax.broadcasted_iota(jnp.int32, (R, n_cmp), 1)
    j_of = col & (n_slc - 1)
    r_of = col >> slc_shift
    cmp_end = (per_blk * j_of + r_of) * CMP_STRIDE + (CMP_LEN - 1)
    cmask = cmp_end <= row + t0
    s = jnp.where(cmask, s, NEG_INF)
    m = jnp.max(s, axis=-1, keepdims=True)
    e = jnp.exp2(s - m)
    p = jnp.where(cmask, e / jnp.sum(e, axis=-1, keepdims=True), 0.0)
    o = jnp.dot(p.astype(BF16), vc, preferred_element_type=F32)

    ph = p[0:tq]
    for h in range(1, NSA_HEADS):
        ph = ph + p[h * tq:(h + 1) * tq]
    groups = [ph[:, r * n_slc:(r + 1) * n_slc] for r in range(per_blk)]
    blk = lax.broadcasted_iota(jnp.int32, (tq, n_slc), 1)
    spill = jnp.where(blk == 0, 0.0, pltpu.roll(groups[per_blk - 1], 1, 1))
    imp = groups[0]
    for r in range(1, per_blk):
        imp = imp + groups[r]
    imp = imp + spill
    tpos = lax.broadcasted_iota(jnp.int32, (tq, n_slc), 0) + t0
    cur = tpos >> (SLC_BLOCK.bit_length() - 1)
    forced = (blk == 0) | (blk == cur) | (blk == cur - 1)
    valid = blk <= cur
    work = jnp.where(forced, BIG, jnp.where(valid, imp, -BIG))
    blk_f = blk.astype(F32)
    sel = jnp.zeros((tq, n_slc), F32)
    for _ in range(n_sel):
        mx = jnp.max(work, axis=-1, keepdims=True)
        first = jnp.min(jnp.where(work == mx, blk_f, float(n_slc)), axis=-1, keepdims=True)
        pick = blk_f == first
        sel = jnp.where(pick, 1.0, sel)
        work = jnp.where(pick, -jnp.inf, work)
    return o, jnp.where((sel > 0.0) & valid, 0.0, NEG_INF)


def _nsa_cmp_kernel(tq, n_sub, n_slc, n_sel, q_ref, kc_ref, vc_ref, o_ref, pen_ref):
    i = pl.program_id(1)
    for u in range(n_sub):
        rows = slice(u * tq, (u + 1) * tq)
        o, pen = _nsa_cmp_block(tq, n_slc, n_sel, (i * n_sub + u) * tq, q_ref[rows, :],
                                kc_ref[...], vc_ref[...])
        for h in range(NSA_HEADS):
            o_ref[rows, h * LANE:(h + 1) * LANE] = o[h * tq:(h + 1) * tq].astype(o_ref.dtype)
        pen = pen.astype(pen_ref.dtype)
        for c in range(n_slc // LANE):
            pen_ref[c, rows, :] = pen[:, c * LANE:(c + 1) * LANE]


def _nsa_compressed(q, k_c, v_c, tq=128, n_sub=4):
    B, S, W = q.shape
    n_cmp = k_c.shape[1]
    n_slc = S // SLC_BLOCK
    n_sel = min(SLC_TOPK, n_slc)
    assert n_slc % LANE == 0 and n_slc & (n_slc - 1) == 0
    assert n_cmp == 4 * n_slc and tq & (tq - 1) == 0
    n_pc = n_slc // LANE
    bq = tq * n_sub
    return pl.pallas_call(
        functools.partial(_nsa_cmp_kernel, tq, n_sub, n_slc, n_sel),
        out_shape=[jax.ShapeDtypeStruct((B, S, W), BF16),
                   jax.ShapeDtypeStruct((B, n_pc, S, LANE), BF16)],
        grid=(B, S // bq),
        in_specs=[pl.BlockSpec((None, bq, W), lambda b, i: (b, i, 0)),
                  pl.BlockSpec((None, n_cmp, HEAD_DIM), lambda b, i: (b, 0, 0)),
                  pl.BlockSpec((None, n_cmp, HEAD_DIM), lambda b, i: (b, 0, 0))],
        out_specs=[pl.BlockSpec((None, bq, W), lambda b, i: (b, i, 0)),
                   pl.BlockSpec((None, n_pc, bq, LANE), lambda b, i: (b, 0, i, 0))],
        compiler_params=_cparams(2), name="nsa_compressed_topk",
    )(q, k_c, v_c)


def _nsa_sel_kernel(tq, tk, q_ref, pen_ref, k_ref, vt_ref, o_ref, qa_ref, st_ref):
    i = pl.program_id(1)
    t0 = i * tq
    R = NSA_HEADS * tq
    last = t0 // tk
    tiles_per_chunk = PEN_CHUNK // tk
    q = _stack_heads(q_ref[...], NSA_HEADS)
    for c in range(pen_ref.shape[0]):
        qa_ref[c, :, 0:LANE] = q
        qa_ref[c, :, LANE:2 * LANE] = jnp.concatenate([pen_ref[c]] * NSA_HEADS, axis=0)

    def scores(t):
        start = pl.multiple_of(t * tk, tk)
        return _dot_t(k_ref[pl.ds(start, tk), :], qa_ref[t // tiles_per_chunk])

    def diag_mask(st):
        key = lax.broadcasted_iota(jnp.int32, (tk, R), 0) + last * tk
        qry = (lax.broadcasted_iota(jnp.int32, (tk, R), 1) & (tq - 1)) + t0
        return jnp.where(key <= qry, st, NEG_INF)

    o = _flash_transposed(last, scores, diag_mask, vt_ref, st_ref)
    for h in range(NSA_HEADS):
        o_ref[:, h * LANE:(h + 1) * LANE] = o[:, h * tq:(h + 1) * tq].T.astype(o_ref.dtype)


def _nsa_selected(q, pen, k_aug, v, tq=256, tk=1024):
    B, S, W = q.shape
    n_pc = pen.shape[1]
    nt = S // tk
    assert tk % tq == 0 and PEN_CHUNK % tk == 0 and tq & (tq - 1) == 0
    vt = _key_tiles_t(v, 1, tk)[:, 0]
    return pl.pallas_call(
        functools.partial(_nsa_sel_kernel, tq, tk),
        out_shape=jax.ShapeDtypeStruct((B, S, W), BF16),
        grid=(B, S // tq),
        in_specs=[pl.BlockSpec((None, tq, W), lambda b, i: (b, i, 0)),
                  pl.BlockSpec((None, n_pc, tq, LANE), lambda b, i: (b, 0, i, 0)),
                  pl.BlockSpec((None, S, 2 * HEAD_DIM), lambda b, i: (b, 0, 0)),
                  pl.BlockSpec((None, nt, HEAD_DIM, tk), lambda b, i: (b, 0, 0, 0))],
        out_specs=pl.BlockSpec((None, tq, W), lambda b, i: (b, i, 0)),
        scratch_shapes=[pltpu.VMEM((n_pc, NSA_HEADS * tq, 2 * HEAD_DIM), BF16),
                        pltpu.VMEM((2, tk, NSA_HEADS * tq), F32)],
        compiler_params=_cparams(2), name="nsa_selected_attention",
    )(q, pen, k_aug, vt)


def _mix_out_kernel(oc_ref, os_ref, ow_ref, gate_ref, ob_ref, od_ref, w_ref, h_ref, g_ref,
                    o_ref, a_ref):
    gate = jax.nn.sigmoid(gate_ref[...])
    for h in range(NSA_HEADS):
        hs = slice(h * LANE, (h + 1) * LANE)
        a = (gate[:, 3 * h:3 * h + 1] * oc_ref[:, hs].astype(F32)
             + gate[:, 3 * h + 1:3 * h + 2] * os_ref[:, hs].astype(F32)
             + gate[:, 3 * h + 2:3 * h + 3] * ow_ref[:, hs].astype(F32))
        a_ref[:, hs] = a.astype(a_ref.dtype)
    off = NSA_HEADS * LANE
    a_ref[:, off:off + MLA_HEADS * LANE] = ob_ref[...]
    off += MLA_HEADS * LANE
    a_ref[:, off:off + DIL_HEADS * LANE] = od_ref[...]
    y = jnp.dot(a_ref[...], w_ref[...], preferred_element_type=F32)
    o_ref[...] = h_ref[...] + _rms(y, g_ref[...])


def _mix_out(o_c, o_s, o_w, gate, o_b, o_d, w, h, g, bm=512):
    T, N = h.shape
    width = (NSA_HEADS + MLA_HEADS + DIL_HEADS) * LANE
    heads = (o_c, o_s, o_w, gate, o_b, o_d)
    row = lambda a: pl.BlockSpec((bm, a.shape[1]), lambda i: (i, 0))
    return pl.pallas_call(
        _mix_out_kernel,
        out_shape=jax.ShapeDtypeStruct((T, N), F32),
        grid=(T // bm,),
        in_specs=[row(a) for a in heads] + [
            pl.BlockSpec((width, N), lambda i: (0, 0), pipeline_mode=pl.Buffered(1)),
            row(h), pl.BlockSpec((1, N), lambda i: (0, 0))],
        out_specs=row(h),
        scratch_shapes=[pltpu.VMEM((bm, width), BF16)],
        compiler_params=_cparams(1), name="mix_out_proj",
    )(*heads, w, h, g)


def _xattn_kernel(n_heads, q_ref, k_ref, v_ref, o_ref):
    for h in range(n_heads):
        hs = slice(h * LANE, (h + 1) * LANE)
        s = _dot_t(q_ref[:, hs], k_ref[:, hs])
        m = jnp.max(s, axis=-1, keepdims=True)
        p = jnp.exp2(s - m)
        l = jnp.sum(p, axis=-1, keepdims=True)
        o = jnp.dot(p.astype(BF16), v_ref[:, hs], preferred_element_type=F32) / l
        o_ref[:, hs] = o.astype(o_ref.dtype)


def _cross_attention(q, k, v, n_heads, tq=512):
    B, S, W = q.shape
    M = k.shape[1]
    return pl.pallas_call(
        functools.partial(_xattn_kernel, n_heads),
        out_shape=jax.ShapeDtypeStruct((B, S, W), BF16),
        grid=(B, S // tq),
        in_specs=[pl.BlockSpec((None, tq, W), lambda b, i: (b, i, 0)),
                  pl.BlockSpec((None, M, W), lambda b, i: (b, 0, 0)),
                  pl.BlockSpec((None, M, W), lambda b, i: (b, 0, 0))],
        out_specs=pl.BlockSpec((None, tq, W), lambda b, i: (b, i, 0)),
        compiler_params=_cparams(2), name="cross_attention",
    )(q, k, v)


def _rope_tables(pos, dim):
    inv = ROPE_THETA ** (-jnp.arange(0, dim, 2, dtype=F32) / dim)
    ang = pos.astype(F32)[..., None] * inv
    return jnp.cos(ang), jnp.sin(ang)


def _full_tables(pos):
    c, s = _rope_tables(pos, HEAD_DIM)
    return jnp.concatenate([c, c], -1), jnp.concatenate([-s, s], -1)


def _spread_rope64(t):
    z = jnp.zeros(t.shape[:-1] + (QK_ROPE // 2,), t.dtype)
    return jnp.concatenate([t[..., :QK_ROPE // 2], z, t[..., QK_ROPE // 2:], z], -1)


def _small_tables(pos):
    c, s = _rope_tables(pos, QK_ROPE)
    return _spread_rope64(jnp.concatenate([c, c], -1)), _spread_rope64(jnp.concatenate([-s, s], -1))


def _layer(h, mem, tabs, tabs_c, onehot, p):
    B, S, D = h.shape
    T = B * S
    dh = HEAD_DIM
    scale = dh ** -0.5 * LOG2E
    h2 = h.reshape(T, D)

    w_in = p["w_in"].T
    cuts = np.cumsum([NSA_HEADS * dh, 6 * dh, 3 * NSA_HEADS, Q_LORA, KV_LORA, QK_ROPE])
    w_q, w_kv, w_g, w_cq, w_ckv, w_kr, w_dil = jnp.split(w_in, cuts, axis=0)
    kv = [w_kv[k * dh:(k + 1) * dh] for k in range(6)]
    dw = DIL_HEADS * dh
    w_dq, w_dk, w_dv = w_dil[:dw], w_dil[dw:2 * dw], w_dil[2 * dw:]
    w_gpad = jnp.pad(w_g, ((0, LANE - w_g.shape[0]), (0, 0)))
    w_all = jnp.concatenate([w_q, kv[2], kv[4], w_dq, w_dk, _spread_rope64(w_kr.T).T,
                             kv[0], kv[1], kv[3], kv[5], w_gpad, w_cq, w_ckv, w_dv], 0).astype(BF16)
    one = lambda mode, n=1, s=1.0: ((mode,) * n, s)
    nsa_q, k_s, k_w, dq, dk, k_pe, k_cr, v_cr, v_s, v_w, gate, cq, ckv, dv = _proj(
        h2, p["g_mix_pre"], w_all,
        [one("rope", NSA_HEADS, scale), one("rope"), one("rope"), one("rope", DIL_HEADS, scale),
         one("rope", DIL_HEADS), one("rope_r"), one("none"), one("none"), one("none"), one("none"),
         one("none"), one("none", Q_LORA // LANE), one("none", KV_LORA // LANE),
         one("none", DIL_HEADS)],
        [BF16] * 10 + [F32, BF16, BF16, BF16], tabs=tabs, w_t=True, name="in_proj")

    n_chunk = S // CMP_STRIDE
    half = CMP_LEN // 2
    pe2 = p["cmp_pos_emb"].reshape(2, half * dh)
    cos_c, sin_c = tabs_c

    def compress(t, w1, w2, rope):
        out = _compress(t.reshape(B, n_chunk, CMP_STRIDE * dh), pe2,
                        w1[:half * dh].astype(BF16), w1[half * dh:].astype(BF16),
                        w2.astype(BF16), cos_c, sin_c, rope)
        return out.reshape(B, n_chunk // 4, 4, dh).transpose(0, 2, 1, 3).reshape(B, n_chunk, dh)

    k_c = compress(k_cr, p["w_cmp_k1"], p["w_cmp_k2"], True)
    v_c = compress(v_cr, p["w_cmp_v1"], p["w_cmp_v2"], False)
    q3 = nsa_q.reshape(B, S, NSA_HEADS * dh)
    o_cmp, pen = _nsa_compressed(q3, k_c, v_c)
    k_aug = jnp.concatenate([k_s.reshape(B, S, dh), onehot], axis=-1)
    o_sel = _nsa_selected(q3, pen, k_aug, v_s.reshape(B, S, dh))
    o_win = _nsa_window_attention(q3, k_w.reshape(B, S, dh), v_w.reshape(B, S, dh), NSA_WINDOW)

    dqk = QK_NOPE + QK_ROPE
    w_uq = p["w_uq"].reshape(Q_LORA, MLA_HEADS, dqk)
    w_uq = jnp.concatenate([w_uq[..., :QK_NOPE], _spread_rope64(w_uq[..., QK_NOPE:])], -1)
    w_uq = w_uq.reshape(Q_LORA, MLA_HEADS * 2 * dh).astype(BF16)
    (q_m,) = _proj(cq, p["g_q_lora"], w_uq, [(("none", "rope_r") * MLA_HEADS, dqk ** -0.5 * LOG2E)],
                   [BF16], tabs=tabs, name="mla_q_up")
    w_ukv = p["w_ukv"].reshape(KV_LORA, MLA_HEADS, 2 * dh)
    w_ukv = jnp.concatenate([w_ukv[..., :dh].reshape(KV_LORA, -1),
                             w_ukv[..., dh:].reshape(KV_LORA, -1)], 1).astype(BF16)
    k_m, v_m = _proj(ckv, p["g_kv_lora"], w_ukv,
                     [(("none", "extra") * MLA_HEADS, 1.0), (("none",) * MLA_HEADS, 1.0)],
                     [BF16, BF16], extra=k_pe, name="mla_kv_up")
    o_mla = _mla_attention(q_m.reshape(B, S, -1), k_m.reshape(B, S, -1),
                           v_m.reshape(B, S, -1), MLA_HEADS)

    o_dil = _dilated_attention(dq.reshape(B, S, dw), dk.reshape(B, S, dw), dv.reshape(B, S, dw),
                               DIL_PATTERNS, DIL_HEADS)

    h2 = _mix_out(o_cmp.reshape(T, -1), o_sel.reshape(T, -1), o_win.reshape(T, -1), gate,
                  o_mla.reshape(T, -1), o_dil.reshape(T, dw), p["w_out"].astype(BF16), h2,
                  p["g_mix_post"])

    xw = XATTN_HEADS * dh
    (xq,) = _proj(h2, p["g_mem_pre"], p["w_xq"].astype(BF16),
                  [(("none",) * XATTN_HEADS, scale)], [BF16], name="xattn_q")
    M = mem.shape[1]
    xk, xv = _proj(mem.reshape(B * M, D), p["g_mem_kv"], p["w_xkv"].astype(BF16),
                   [(("none",) * XATTN_HEADS, 1.0), (("none",) * XATTN_HEADS, 1.0)],
                   [BF16, BF16], bm=min(512, B * M), name="xattn_kv")
    xo = _cross_attention(xq.reshape(B, S, xw), xk.reshape(B, M, xw), xv.reshape(B, M, xw),
                          XATTN_HEADS)
    h2 = _out_proj(xo.reshape(T, xw), p["w_xo"].astype(BF16), h2, p["g_mem_post"],
                   name="xattn_out_proj")

    up = _mlp_up(h2, p["g_mlp_pre"], p["w_up"].astype(BF16))
    h2 = _out_proj(up, p["w_down"].astype(BF16), h2, p["g_mlp_post"], name="mlp_down")
    return h2.reshape(B, S, D)


_LAYER_PARAMS = ("g_mix_pre", "w_in", "cmp_pos_emb", "w_cmp_k1", "w_cmp_k2", "w_cmp_v1", "w_cmp_v2",
                 "g_q_lora", "g_kv_lora", "w_uq", "w_ukv", "w_out", "g_mix_post", "g_mem_pre",
                 "g_mem_kv", "w_xq", "w_xkv", "w_xo", "g_mem_post", "g_mlp_pre", "w_up", "w_down",
                 "g_mlp_post")


def kernel(x, mem, positions, g_mix_pre, w_in, cmp_pos_emb, w_cmp_k1, w_cmp_k2, w_cmp_v1, w_cmp_v2, g_q_lora, g_kv_lora, w_uq, w_ukv, w_out, g_mix_post, g_mem_pre, g_mem_kv, w_xq, w_xkv, w_xo, g_mem_post, g_mlp_pre, w_up, w_down, g_mlp_post):
    stacked = dict(zip(_LAYER_PARAMS, (
        g_mix_pre, w_in, cmp_pos_emb, w_cmp_k1, w_cmp_k2, w_cmp_v1, w_cmp_v2, g_q_lora, g_kv_lora,
        w_uq, w_ukv, w_out, g_mix_post, g_mem_pre, g_mem_kv, w_xq, w_xkv, w_xo, g_mem_post,
        g_mlp_pre, w_up, w_down, g_mlp_post)))
    B, S, D = x.shape
    T = B * S
    assert S % PEN_CHUNK == 0
    cosf, sinf = _full_tables(positions)
    cosr, sinr = _small_tables(positions)
    tabs = tuple(t.reshape(T, LANE) for t in (cosf, sinf, cosr, sinr))
    n_chunk = S // CMP_STRIDE
    end = jnp.minimum(jnp.arange(n_chunk) * CMP_STRIDE + CMP_LEN - 1, S - 1)
    tabs_c = _full_tables(positions[:, end])
    blk = (jnp.arange(S) // SLC_BLOCK) % LANE
    onehot = jnp.broadcast_to((blk[:, None] == jnp.arange(LANE)[None, :]).astype(BF16)[None],
                              (B, S, LANE))
    h = x
    for layer in range(stacked["w_in"].shape[0]):
        p = {}
        for name, val in stacked.items():
            v = val[layer]
            p[name] = v[None, :] if name.startswith("g_") else v
        h = _layer(h, mem, tabs, tabs_c, onehot, p)
    return h
```

```python
import functools

import numpy as np
import jax
import jax.numpy as jnp
from jax import lax
from jax.experimental import pallas as pl
from jax.experimental.pallas import tpu as pltpu

F32 = jnp.float32
BF16 = jnp.bfloat16

LANE = 128
VMEM_LIMIT = 56 * 1024 * 1024

HEAD_DIM = 128
ROPE_THETA = 10000.0
NORM_EPS = 1e-6
NEG_INF = -1e30
BIG = 1e9
LOG2E = 1.4426950408889634
NSA_HEADS = 4
MLA_HEADS = 6
DIL_HEADS = 6
CMP_LEN = 32
CMP_STRIDE = 16
SLC_BLOCK = 64
SLC_TOPK = 16
NSA_WINDOW = 512
Q_LORA = 512
KV_LORA = 512
QK_NOPE = 128
QK_ROPE = 64
DIL_PATTERNS = ((128, 1), (512, 4), (2048, 16))
XATTN_HEADS = 4
PEN_CHUNK = LANE * SLC_BLOCK


def _cparams(n_grid):
    return pltpu.CompilerParams(
        dimension_semantics=("arbitrary",) * n_grid, vmem_limit_bytes=VMEM_LIMIT)


def _rms(x, g):
    return x * lax.rsqrt(jnp.mean(x * x, axis=-1, keepdims=True) + NORM_EPS) * g


def _dot_t(a, b):
    return lax.dot_general(a, b, (((1,), (1,)), ((), ())), preferred_element_type=F32)


def _proj_kernel(out_plan, has_norm, n_tab, has_extra, w_t, chunk, *refs):
    x_ref, g_ref, w_ref = refs[:3]
    tab_refs = refs[3:3 + n_tab]
    n_in = 3 + n_tab + int(has_extra)
    out_refs = refs[n_in:]
    x = x_ref[...].astype(F32)
    if has_norm:
        x = _rms(x, g_ref[...])
    xb = x.astype(BF16)
    n_cols = w_ref.shape[0 if w_t else 1]
    flat = []
    for oi, (modes, scale) in enumerate(out_plan):
        for k, mode in enumerate(modes):
            if mode == "extra":
                out_refs[oi][:, k * LANE:(k + 1) * LANE] = refs[n_in - 1][...]
            else:
                flat.append((oi, k * LANE, mode, scale))
    for c0 in range(0, n_cols, chunk):
        c1 = min(c0 + chunk, n_cols)
        if w_t:
            acc = _dot_t(xb, w_ref[c0:c1, :])
        else:
            acc = jnp.dot(xb, w_ref[:, c0:c1], preferred_element_type=F32)
        for s in range((c1 - c0) // LANE):
            oi, off, mode, scale = flat[c0 // LANE + s]
            a = acc[:, s * LANE:(s + 1) * LANE]
            if mode == "rope":
                a = a * tab_refs[0][...] + pltpu.roll(a, LANE // 2, 1) * tab_refs[1][...]
            elif mode == "rope_r":
                a = a * tab_refs[2][...] + pltpu.roll(a, LANE // 2, 1) * tab_refs[3][...]
            if scale != 1.0:
                a = a * scale
            out_refs[oi][:, off:off + LANE] = a.astype(out_refs[oi].dtype)


def _proj(x, g, w, out_plan, out_dtypes, tabs=(), extra=None, w_t=False, bm=512, chunk=512,
          name="proj"):
    T, K = x.shape
    N = w.shape[0 if w_t else 1]
    assert T % bm == 0 and N % LANE == 0
    assert sum(sum(md != "extra" for md in m) for m, _ in out_plan) * LANE == N
    has_norm = g is not None
    if g is None:
        g = jnp.ones((1, K), F32)
    extras = () if extra is None else (extra,)
    in_specs = [pl.BlockSpec((bm, K), lambda i: (i, 0)),
                pl.BlockSpec((1, K), lambda i: (0, 0)),
                pl.BlockSpec(w.shape, lambda i: (0, 0), pipeline_mode=pl.Buffered(1))]
    in_specs += [pl.BlockSpec((bm, LANE), lambda i: (i, 0)) for _ in tabs + extras]
    out_shape = [jax.ShapeDtypeStruct((T, len(m) * LANE), dt)
                 for (m, _), dt in zip(out_plan, out_dtypes)]
    out_specs = [pl.BlockSpec((bm, len(m) * LANE), lambda i: (i, 0)) for m, _ in out_plan]
    return pl.pallas_call(
        functools.partial(_proj_kernel, out_plan, has_norm, len(tabs), extra is not None, w_t,
                          chunk),
        out_shape=out_shape, grid=(T // bm,), in_specs=in_specs, out_specs=out_specs,
        compiler_params=_cparams(1), name=name,
    )(x, g, w, *tabs, *extras)


def _mlp_up_kernel(x_ref, g_ref, w_ref, o_ref, xn_ref):
    @pl.when(pl.program_id(1) == 0)
    def _():
        xn_ref[...] = _rms(x_ref[...], g_ref[...]).astype(BF16)
    a = jnp.dot(xn_ref[...], w_ref[...], preferred_element_type=F32)
    a = jnp.maximum(a, 0.0)
    o_ref[...] = (a * a).astype(o_ref.dtype)


def _mlp_up(x, g, w, bm=1024, bn=1024):
    T, K = x.shape
    N = w.shape[1]
    return pl.pallas_call(
        _mlp_up_kernel,
        out_shape=jax.ShapeDtypeStruct((T, N), BF16),
        grid=(T // bm, N // bn),
        in_specs=[pl.BlockSpec((bm, K), lambda i, j: (i, 0)),
                  pl.BlockSpec((1, K), lambda i, j: (0, 0)),
                  pl.BlockSpec((K, bn), lambda i, j: (0, j))],
        out_specs=pl.BlockSpec((bm, bn), lambda i, j: (i, j)),
        scratch_shapes=[pltpu.VMEM((bm, K), BF16)],
        compiler_params=_cparams(2), name="mlp_up",
    )(x, g, w)


def _out_proj_kernel(n_k, a_ref, w_ref, h_ref, g_ref, o_ref, acc_ref):
    k = pl.program_id(1)
    part = jnp.dot(a_ref[...], w_ref[...], preferred_element_type=F32)

    def finish(y):
        o_ref[...] = h_ref[...] + _rms(y, g_ref[...])

    if n_k == 1:
        finish(part)
    else:
        @pl.when(k == 0)
        def _():
            acc_ref[...] = part

        @pl.when(jnp.logical_and(k > 0, k < n_k - 1))
        def _():
            acc_ref[...] += part

        @pl.when(k == n_k - 1)
        def _():
            finish(acc_ref[...] + part)


def _out_proj(a, w, h, g, bm=512, bk=2048, name="out_proj"):
    T, K = a.shape
    N = w.shape[1]
    bk = min(bk, K)
    n_k = K // bk
    return pl.pallas_call(
        functools.partial(_out_proj_kernel, n_k),
        out_shape=jax.ShapeDtypeStruct((T, N), F32),
        grid=(T // bm, n_k),
        in_specs=[pl.BlockSpec((bm, bk), lambda i, k: (i, k)),
                  pl.BlockSpec((bk, N), lambda i, k: (k, 0)),
                  pl.BlockSpec((bm, N), lambda i, k: (i, 0)),
                  pl.BlockSpec((1, N), lambda i, k: (0, 0))],
        out_specs=pl.BlockSpec((bm, N), lambda i, k: (i, 0)),
        scratch_shapes=[pltpu.VMEM((bm, N) if n_k > 1 else (8, LANE), F32)],
        compiler_params=_cparams(2), name=name,
    )(a, w, h, g)


def _online_step(carry, st, st_max, vt):
    m, l, acc = carry
    m_new = jnp.maximum(m, st_max)
    alpha = jnp.exp2(m - m_new)
    p = jnp.exp2(st - m_new)
    l = alpha * l + jnp.sum(p, axis=0, keepdims=True)
    acc = alpha * acc + jnp.dot(vt, p.astype(BF16), preferred_element_type=F32)
    return m_new, l, acc


def _online_init(cols, dv):
    return (jnp.full((1, cols), NEG_INF, F32), jnp.zeros((1, cols), F32),
            jnp.zeros((dv, cols), F32))


def _flash_transposed(n_full, n_groups, scores, diag_mask, values, st_ref):
    groups = range(n_groups)

    def put(t, slot):
        maxes = []
        for g in groups:
            st = scores(g, t)
            st_ref[g, slot] = st
            maxes.append(jnp.max(st, axis=0, keepdims=True))
        return tuple(maxes)

    def half(t, carry, slot):
        stats, st_max = carry
        nxt = put(t + 1, 1 - slot)
        stats = tuple(_online_step(stats[g], st_ref[g, slot], st_max[g], values(g, t))
                      for g in groups)
        return stats, nxt

    def pair(u, carry):
        return half(2 * u + 1, half(2 * u, carry, 0), 1)

    init = tuple(_online_init(st_ref.shape[3], HEAD_DIM) for g in groups)
    carry = lax.fori_loop(0, n_full // 2, pair, (init, put(0, 0)))
    odd = n_full & 1
    stats, _ = lax.fori_loop(0, odd, lambda _, c: half(n_full - 1, c, 0), carry)
    outs = []
    for g in groups:
        st = diag_mask(g, st_ref[g, odd])
        _, l, acc = _online_step(stats[g], st, jnp.max(st, axis=0, keepdims=True),
                                 values(g, n_full))
        outs.append(acc / l)
    return outs


def _mla_kernel(tq, tk, hg, q_ref, k_ref, vt_ref, o_ref, st_ref):
    t0 = pl.program_id(2) * tq
    dq = q_ref.shape[1] // hg
    last = t0 // tk

    def scores(g, t):
        start = pl.multiple_of(t * tk, tk)
        cols = slice(g * dq, (g + 1) * dq)
        return _dot_t(k_ref[pl.ds(start, tk), cols], q_ref[:, cols])

    def diag_mask(g, st):
        key = lax.broadcasted_iota(jnp.int32, (tk, tq), 0) + last * tk
        qry = lax.broadcasted_iota(jnp.int32, (tk, tq), 1) + t0
        return jnp.where(key <= qry, st, NEG_INF)

    outs = _flash_transposed(last, hg, scores, diag_mask, lambda g, t: vt_ref[g, t], st_ref)
    for g in range(hg):
        o_ref[:, g * HEAD_DIM:(g + 1) * HEAD_DIM] = outs[g].T.astype(o_ref.dtype)


def _key_tiles_t(v, n_heads, tk):
    B, S, _ = v.shape
    return v.reshape(B, S // tk, tk, n_heads, HEAD_DIM).transpose(0, 3, 1, 4, 2)


def _mla_attention(q, k, v, n_heads, tq=1024, tk=1024, hg=2):
    B, S, _ = q.shape
    dq = q.shape[2] // n_heads
    nt = S // tk
    assert tk % tq == 0 and S % tk == 0 and n_heads % hg == 0
    resident = pl.Buffered(1)
    return pl.pallas_call(
        functools.partial(_mla_kernel, tq, tk, hg),
        out_shape=jax.ShapeDtypeStruct((B, S, n_heads * HEAD_DIM), BF16),
        grid=(B, n_heads // hg, S // tq),
        in_specs=[pl.BlockSpec((None, tq, hg * dq), lambda b, h, i: (b, i, h)),
                  pl.BlockSpec((None, S, hg * dq), lambda b, h, i: (b, 0, h),
                               pipeline_mode=resident),
                  pl.BlockSpec((None, hg, nt, HEAD_DIM, tk), lambda b, h, i: (b, h, 0, 0, 0),
                               pipeline_mode=resident)],
        out_specs=pl.BlockSpec((None, tq, hg * HEAD_DIM), lambda b, h, i: (b, i, h)),
        scratch_shapes=[pltpu.VMEM((hg, 2, tk, tq), F32)],
        compiler_params=_cparams(3), name="mla_attention",
    )(q, k, _key_tiles_t(v, n_heads, tk))


def _band_block(q, kwin, vwin, window, base):
    R = q.shape[0]
    C = window + LANE
    s = _dot_t(q, kwin)
    r = lax.broadcasted_iota(jnp.int32, (R, C), 0) & (LANE - 1)
    c = lax.broadcasted_iota(jnp.int32, (R, C), 1)
    mask = (c >= r) & (c <= r + window) & (c >= window - base)
    s = jnp.where(mask, s, NEG_INF)
    m = jnp.max(s, axis=-1, keepdims=True)
    p = jnp.exp2(s - m)
    l = jnp.sum(p, axis=-1, keepdims=True)
    o = jnp.dot(p.astype(BF16), vwin, preferred_element_type=F32) / l
    return o, m + jnp.log2(l)


def _stack_heads(x, n):
    return jnp.concatenate([x[:, h * LANE:(h + 1) * LANE] for h in range(n)], axis=0)


def _nsa_window_kernel(tq, window, q_ref, kp_ref, kc_ref, vp_ref, vc_ref, o_ref):
    i = pl.program_id(1)
    kwin = jnp.concatenate([kp_ref[...], kc_ref[...]], axis=0)
    vwin = jnp.concatenate([vp_ref[...], vc_ref[...]], axis=0)
    for j in range(tq // LANE):
        q = _stack_heads(q_ref[j * LANE:(j + 1) * LANE, :], NSA_HEADS)
        lo = j * LANE
        o, _ = _band_block(q, kwin[lo:lo + window + LANE], vwin[lo:lo + window + LANE],
                           window, i * tq + lo)
        for h in range(NSA_HEADS):
            o_ref[lo:lo + LANE, h * LANE:(h + 1) * LANE] = (
                o[h * LANE:(h + 1) * LANE].astype(o_ref.dtype))


def _nsa_window_attention(q, k, v, window):
    B, S, W = q.shape
    tq = window
    prev = lambda b, i: (b, jnp.maximum(i - 1, 0), 0)
    cur = lambda b, i: (b, i, 0)
    return pl.pallas_call(
        functools.partial(_nsa_window_kernel, tq, window),
        out_shape=jax.ShapeDtypeStruct((B, S, W), BF16),
        grid=(B, S // tq),
        in_specs=[pl.BlockSpec((None, tq, W), cur),
                  pl.BlockSpec((None, window, HEAD_DIM), prev),
                  pl.BlockSpec((None, tq, HEAD_DIM), cur),
                  pl.BlockSpec((None, window, HEAD_DIM), prev),
                  pl.BlockSpec((None, tq, HEAD_DIM), cur)],
        out_specs=pl.BlockSpec((None, tq, W), cur),
        compiler_params=_cparams(2), name="nsa_window_attention",
    )(q, k, k, v, v)


def _dilated_kernel(nb, patterns, q_ref, kp_ref, kc_ref, vp_ref, vc_ref, o_ref,
                    qf_ref, kf_ref, vf_ref, of_ref, lf_ref):
    i = pl.program_id(2)
    qf_ref[...] = q_ref[...].astype(F32)
    kf_ref[0:nb, :] = kp_ref[...].astype(F32)
    kf_ref[nb:2 * nb, :] = kc_ref[...].astype(F32)
    vf_ref[0:nb, :] = vp_ref[...].astype(F32)
    vf_ref[nb:2 * nb, :] = vc_ref[...].astype(F32)
    for pi, (window, dil) in enumerate(patterns):
        w = window // dil
        per_class = nb // dil
        for r in range(dil):
            for j in range(per_class // LANE):
                q_lo = r + j * LANE * dil
                k_lo = nb + q_lo - w * dil
                rows_q = pl.ds(q_lo, LANE, stride=dil)
                rows_k = pl.ds(k_lo, w + LANE, stride=dil)
                o, lse = _band_block(qf_ref[rows_q, :].astype(BF16),
                                     kf_ref[rows_k, :].astype(BF16),
                                     vf_ref[rows_k, :].astype(BF16),
                                     w, i * per_class + j * LANE)
                of_ref[pi, rows_q, :] = o
                lf_ref[pi, rows_q, :] = jnp.broadcast_to(lse, (LANE, LANE))
    n_pat = len(patterns)
    mx = lf_ref[0]
    for pi in range(1, n_pat):
        mx = jnp.maximum(mx, lf_ref[pi])
    es = [jnp.exp2(lf_ref[pi] - mx) for pi in range(n_pat)]
    den = es[0]
    num = es[0] * of_ref[0]
    for pi in range(1, n_pat):
        den = den + es[pi]
        num = num + es[pi] * of_ref[pi]
    o_ref[...] = (num / den).astype(o_ref.dtype)


def _dilated_attention(q, k, v, patterns, n_heads, nb=2048):
    B, S, W = q.shape
    for window, dil in patterns:
        assert window % dil == 0 and window // dil == LANE
        assert window <= nb and nb % (dil * LANE) == 0
    assert S % nb == 0
    prev = lambda b, h, i: (b, jnp.maximum(i - 1, 0), h)
    cur = lambda b, h, i: (b, i, h)
    blk = lambda index_map: pl.BlockSpec((None, nb, LANE), index_map)
    return pl.pallas_call(
        functools.partial(_dilated_kernel, nb, patterns),
        out_shape=jax.ShapeDtypeStruct((B, S, W), BF16),
        grid=(B, n_heads, S // nb),
        in_specs=[blk(cur), blk(prev), blk(cur), blk(prev), blk(cur)],
        out_specs=blk(cur),
        scratch_shapes=[pltpu.VMEM((nb, LANE), F32),
                        pltpu.VMEM((2 * nb, LANE), F32),
                        pltpu.VMEM((2 * nb, LANE), F32),
                        pltpu.VMEM((len(patterns), nb, LANE), F32),
                        pltpu.VMEM((len(patterns), nb, LANE), F32)],
        compiler_params=_cparams(3), name="dilated_attention",
    )(q, k, k, v, v)


def _gelu_tanh(x):
    return 0.5 * x * (1.0 + jnp.tanh(0.7978845608028654 * (x + 0.044715 * (x * x * x))))


def _compress_kernel(rope, t_ref, pe_ref, w1a_ref, w1b_ref, w2_ref, cos_ref, sin_ref, o_ref):
    t = t_ref[...].astype(F32)
    n = t.shape[0]
    first = jnp.dot((t + pe_ref[0:1, :]).astype(BF16), w1a_ref[...], preferred_element_type=F32)
    second = jnp.dot((t + pe_ref[1:2, :]).astype(BF16), w1b_ref[...], preferred_element_type=F32)
    hid = first + pltpu.roll(second, n - 1, 0)
    out = jnp.dot(_gelu_tanh(hid).astype(BF16), w2_ref[...], preferred_element_type=F32)
    if rope:
        out = out * cos_ref[...] + pltpu.roll(out, LANE // 2, 1) * sin_ref[...]
    o_ref[...] = out.astype(o_ref.dtype)


def _compress(t, pe2, w1a, w1b, w2, cos_c, sin_c, rope):
    B, n, K = t.shape
    whole = lambda b: (0, 0)
    per_b = lambda b: (b, 0, 0)
    return pl.pallas_call(
        functools.partial(_compress_kernel, rope),
        out_shape=jax.ShapeDtypeStruct((B, n, HEAD_DIM), BF16),
        grid=(B,),
        in_specs=[pl.BlockSpec((None, n, K), per_b),
                  pl.BlockSpec((2, K), whole),
                  pl.BlockSpec((K, HEAD_DIM), whole),
                  pl.BlockSpec((K, HEAD_DIM), whole),
                  pl.BlockSpec((HEAD_DIM, HEAD_DIM), whole),
                  pl.BlockSpec((None, n, HEAD_DIM), per_b),
                  pl.BlockSpec((None, n, HEAD_DIM), per_b)],
        out_specs=pl.BlockSpec((None, n, HEAD_DIM), per_b),
        compiler_params=_cparams(1), name="nsa_compress",
    )(t, pe2, w1a, w1b, w2, cos_c, sin_c)


def _nsa_cmp_block(tq, n_slc, n_sel, t0, q, kc, vc):
    n_cmp = kc.shape[0]
    per_blk = SLC_BLOCK // CMP_STRIDE
    slc_shift = n_slc.bit_length() - 1
    q = _stack_heads(q, NSA_HEADS)
    s = _dot_t(q, kc)
    R = NSA_HEADS * tq
    row = lax.broadcasted_iota(jnp.int32, (R, n_cmp), 0) & (tq - 1)
    col = lax.broadcasted_iota(jnp.int32, (R, n_cmp), 1)
    j_of = col & (n_slc - 1)
    r_of = col >> slc_shift
    cmp_end = (per_blk * j_of + r_of) * CMP_STRIDE + (CMP_LEN - 1)
    cmask = cmp_end <= row + t0
    s = jnp.where(cmask, s, NEG_INF)
    m = jnp.max(s, axis=-1, keepdims=True)
    e = jnp.exp2(s - m)
    p = jnp.where(cmask, e / jnp.sum(e, axis=-1, keepdims=True), 0.0)
    o = jnp.dot(p.astype(BF16), vc, preferred_element_type=F32)

    ph = p[0:tq]
    for h in range(1, NSA_HEADS):
        ph = ph + p[h * tq:(h + 1) * tq]
    groups = [ph[:, r * n_slc:(r + 1) * n_slc] for r in range(per_blk)]
    blk = lax.broadcasted_iota(jnp.int32, (tq, n_slc), 1)
    spill = jnp.where(blk == 0, 0.0, pltpu.roll(groups[per_blk - 1], 1, 1))
    imp = groups[0]
    for r in range(1, per_blk):
        imp = imp + groups[r]
    imp = imp + spill
    tpos = lax.broadcasted_iota(jnp.int32, (tq, n_slc), 0) + t0
    cur = tpos >> (SLC_BLOCK.bit_length() - 1)
    forced = (blk == 0) | (blk == cur) | (blk == cur - 1)
    valid = blk <= cur
    work = jnp.where(forced, BIG, jnp.where(valid, imp, -BIG))
    blk_f = blk.astype(F32)
    sel = jnp.zeros((tq, n_slc), F32)
    for _ in range(n_sel):
        mx = jnp.max(work, axis=-1, keepdims=True)
        first = jnp.min(jnp.where(work == mx, blk_f, float(n_slc)), axis=-1, keepdims=True)
        pick = blk_f == first
        sel = jnp.where(pick, 1.0, sel)
        work = jnp.where(pick, -jnp.inf, work)
    return o, jnp.where((sel > 0.0) & valid, 0.0, NEG_INF)


def _nsa_cmp_kernel(tq, n_sub, n_slc, n_sel, q_ref, kc_ref, vc_ref, o_ref, pen_ref):
    i = pl.program_id(1)
    for u in range(n_sub):
        rows = slice(u * tq, (u + 1) * tq)
        o, pen = _nsa_cmp_block(tq, n_slc, n_sel, (i * n_sub + u) * tq, q_ref[rows, :],
                                kc_ref[...], vc_ref[...])
        for h in range(NSA_HEADS):
            o_ref[rows, h * LANE:(h + 1) * LANE] = o[h * tq:(h + 1) * tq].astype(o_ref.dtype)
        pen = pen.astype(pen_ref.dtype)
        for c in range(n_slc // LANE):
            pen_ref[c, rows, :] = pen[:, c * LANE:(c + 1) * LANE]


def _nsa_compressed(q, k_c, v_c, tq=128, n_sub=4):
    B, S, W = q.shape
    n_cmp = k_c.shape[1]
    n_slc = S // SLC_BLOCK
    n_sel = min(SLC_TOPK, n_slc)
    assert n_slc % LANE == 0 and n_slc & (n_slc - 1) == 0
    assert n_cmp == 4 * n_slc and tq & (tq - 1) == 0
    n_pc = n_slc // LANE
    bq = tq * n_sub
    return pl.pallas_call(
        functools.partial(_nsa_cmp_kernel, tq, n_sub, n_slc, n_sel),
        out_shape=[jax.ShapeDtypeStruct((B, S, W), BF16),
                   jax.ShapeDtypeStruct((B, n_pc, S, LANE), BF16)],
        grid=(B, S // bq),
        in_specs=[pl.BlockSpec((None, bq, W), lambda b, i: (b, i, 0)),
                  pl.BlockSpec((None, n_cmp, HEAD_DIM), lambda b, i: (b, 0, 0)),
                  pl.BlockSpec((None, n_cmp, HEAD_DIM), lambda b, i: (b, 0, 0))],
        out_specs=[pl.BlockSpec((None, bq, W), lambda b, i: (b, i, 0)),
                   pl.BlockSpec((None, n_pc, bq, LANE), lambda b, i: (b, 0, i, 0))],
        compiler_params=_cparams(2), name="nsa_compressed_topk",
    )(q, k_c, v_c)


def _nsa_sel_kernel(tq, tk, n_sub, q_ref, pen_ref, k_ref, vt_ref, o_ref, qa_ref, st_ref):
    t0 = pl.program_id(1) * (tq * n_sub)
    R = NSA_HEADS * tq
    last = t0 // tk
    tiles_per_chunk = PEN_CHUNK // tk
    for u in range(n_sub):
        rows = slice(u * tq, (u + 1) * tq)
        q = _stack_heads(q_ref[rows, :], NSA_HEADS)
        for c in range(pen_ref.shape[0]):
            qa_ref[u, c, :, 0:LANE] = q
            qa_ref[u, c, :, LANE:2 * LANE] = jnp.concatenate([pen_ref[c, rows, :]] * NSA_HEADS,
                                                             axis=0)

    def scores(u, t):
        start = pl.multiple_of(t * tk, tk)
        return _dot_t(k_ref[pl.ds(start, tk), :], qa_ref[u, t // tiles_per_chunk])

    def diag_mask(u, st):
        key = lax.broadcasted_iota(jnp.int32, (tk, R), 0) + last * tk
        qry = (lax.broadcasted_iota(jnp.int32, (tk, R), 1) & (tq - 1)) + (t0 + u * tq)
        return jnp.where(key <= qry, st, NEG_INF)

    outs = _flash_transposed(last, n_sub, scores, diag_mask, lambda u, t: vt_ref[t], st_ref)
    for u in range(n_sub):
        for h in range(NSA_HEADS):
            o_ref[u * tq:(u + 1) * tq, h * LANE:(h + 1) * LANE] = (
                outs[u][:, h * tq:(h + 1) * tq].T.astype(o_ref.dtype))


def _nsa_selected(q, pen, k_aug, v, tq=128, n_sub=2, tk=1024):
    B, S, W = q.shape
    n_pc = pen.shape[1]
    nt = S // tk
    bq = tq * n_sub
    assert tk % bq == 0 and PEN_CHUNK % tk == 0 and tq & (tq - 1) == 0
    vt = _key_tiles_t(v, 1, tk)[:, 0]
    return pl.pallas_call(
        functools.partial(_nsa_sel_kernel, tq, tk, n_sub),
        out_shape=jax.ShapeDtypeStruct((B, S, W), BF16),
        grid=(B, S // bq),
        in_specs=[pl.BlockSpec((None, bq, W), lambda b, i: (b, i, 0)),
                  pl.BlockSpec((None, n_pc, bq, LANE), lambda b, i: (b, 0, i, 0)),
                  pl.BlockSpec((None, S, 2 * HEAD_DIM), lambda b, i: (b, 0, 0)),
                  pl.BlockSpec((None, nt, HEAD_DIM, tk), lambda b, i: (b, 0, 0, 0))],
        out_specs=pl.BlockSpec((None, bq, W), lambda b, i: (b, i, 0)),
        scratch_shapes=[pltpu.VMEM((n_sub, n_pc, NSA_HEADS * tq, 2 * HEAD_DIM), BF16),
                        pltpu.VMEM((n_sub, 2, tk, NSA_HEADS * tq), F32)],
        compiler_params=_cparams(2), name="nsa_selected_attention",
    )(q, pen, k_aug, vt)


def _mix_out_kernel(oc_ref, os_ref, ow_ref, gate_ref, ob_ref, od_ref, w_ref, h_ref, g_ref,
                    o_ref, a_ref):
    gate = jax.nn.sigmoid(gate_ref[...])
    for h in range(NSA_HEADS):
        hs = slice(h * LANE, (h + 1) * LANE)
        a = (gate[:, 3 * h:3 * h + 1] * oc_ref[:, hs].astype(F32)
             + gate[:, 3 * h + 1:3 * h + 2] * os_ref[:, hs].astype(F32)
             + gate[:, 3 * h + 2:3 * h + 3] * ow_ref[:, hs].astype(F32))
        a_ref[:, hs] = a.astype(a_ref.dtype)
    off = NSA_HEADS * LANE
    a_ref[:, off:off + MLA_HEADS * LANE] = ob_ref[...]
    off += MLA_HEADS * LANE
    a_ref[:, off:off + DIL_HEADS * LANE] = od_ref[...]
    y = jnp.dot(a_ref[...], w_ref[...], preferred_element_type=F32)
    o_ref[...] = h_ref[...] + _rms(y, g_ref[...])


def _mix_out(o_c, o_s, o_w, gate, o_b, o_d, w, h, g, bm=512):
    T, N = h.shape
    width = (NSA_HEADS + MLA_HEADS + DIL_HEADS) * LANE
    heads = (o_c, o_s, o_w, gate, o_b, o_d)
    row = lambda a: pl.BlockSpec((bm, a.shape[1]), lambda i: (i, 0))
    return pl.pallas_call(
        _mix_out_kernel,
        out_shape=jax.ShapeDtypeStruct((T, N), F32),
        grid=(T // bm,),
        in_specs=[row(a) for a in heads] + [
            pl.BlockSpec((width, N), lambda i: (0, 0), pipeline_mode=pl.Buffered(1)),
            row(h), pl.BlockSpec((1, N), lambda i: (0, 0))],
        out_specs=row(h),
        scratch_shapes=[pltpu.VMEM((bm, width), BF16)],
        compiler_params=_cparams(1), name="mix_out_proj",
    )(*heads, w, h, g)


def _xattn_kernel(n_heads, q_ref, k_ref, v_ref, o_ref):
    for h in range(n_heads):
        hs = slice(h * LANE, (h + 1) * LANE)
        s = _dot_t(q_ref[:, hs], k_ref[:, hs])
        m = jnp.max(s, axis=-1, keepdims=True)
        p = jnp.exp2(s - m)
        l = jnp.sum(p, axis=-1, keepdims=True)
        o = jnp.dot(p.astype(BF16), v_ref[:, hs], preferred_element_type=F32) / l
        o_ref[:, hs] = o.astype(o_ref.dtype)


def _cross_attention(q, k, v, n_heads, tq=512):
    B, S, W = q.shape
    M = k.shape[1]
    return pl.pallas_call(
        functools.partial(_xattn_kernel, n_heads),
        out_shape=jax.ShapeDtypeStruct((B, S, W), BF16),
        grid=(B, S // tq),
        in_specs=[pl.BlockSpec((None, tq, W), lambda b, i: (b, i, 0)),
                  pl.BlockSpec((None, M, W), lambda b, i: (b, 0, 0)),
                  pl.BlockSpec((None, M, W), lambda b, i: (b, 0, 0))],
        out_specs=pl.BlockSpec((None, tq, W), lambda b, i: (b, i, 0)),
        compiler_params=_cparams(2), name="cross_attention",
    )(q, k, v)


def _rope_tables(pos, dim):
    inv = ROPE_THETA ** (-jnp.arange(0, dim, 2, dtype=F32) / dim)
    ang = pos.astype(F32)[..., None] * inv
    return jnp.cos(ang), jnp.sin(ang)


def _full_tables(pos):
    c, s = _rope_tables(pos, HEAD_DIM)
    return jnp.concatenate([c, c], -1), jnp.concatenate([-s, s], -1)


def _spread_rope64(t):
    z = jnp.zeros(t.shape[:-1] + (QK_ROPE // 2,), t.dtype)
    return jnp.concatenate([t[..., :QK_ROPE // 2], z, t[..., QK_ROPE // 2:], z], -1)


def _small_tables(pos):
    c, s = _rope_tables(pos, QK_ROPE)
    return _spread_rope64(jnp.concatenate([c, c], -1)), _spread_rope64(jnp.concatenate([-s, s], -1))


def _layer(h, mem, tabs, tabs_c, onehot, p):
    B, S, D = h.shape
    T = B * S
    dh = HEAD_DIM
    scale = dh ** -0.5 * LOG2E
    h2 = h.reshape(T, D)

    w_in = p["w_in"].T
    cuts = np.cumsum([NSA_HEADS * dh, 6 * dh, 3 * NSA_HEADS, Q_LORA, KV_LORA, QK_ROPE])
    w_q, w_kv, w_g, w_cq, w_ckv, w_kr, w_dil = jnp.split(w_in, cuts, axis=0)
    kv = [w_kv[k * dh:(k + 1) * dh] for k in range(6)]
    dw = DIL_HEADS * dh
    w_dq, w_dk, w_dv = w_dil[:dw], w_dil[dw:2 * dw], w_dil[2 * dw:]
    w_gpad = jnp.pad(w_g, ((0, LANE - w_g.shape[0]), (0, 0)))
    w_all = jnp.concatenate([w_q, kv[2], kv[4], w_dq, w_dk, _spread_rope64(w_kr.T).T,
                             kv[0], kv[1], kv[3], kv[5], w_gpad, w_cq, w_ckv, w_dv], 0).astype(BF16)
    one = lambda mode, n=1, s=1.0: ((mode,) * n, s)
    nsa_q, k_s, k_w, dq, dk, k_pe, k_cr, v_cr, v_s, v_w, gate, cq, ckv, dv = _proj(
        h2, p["g_mix_pre"], w_all,
        [one("rope", NSA_HEADS, scale), one("rope"), one("rope"), one("rope", DIL_HEADS, scale),
         one("rope", DIL_HEADS), one("rope_r"), one("none"), one("none"), one("none"), one("none"),
         one("none"), one("none", Q_LORA // LANE), one("none", KV_LORA // LANE),
         one("none", DIL_HEADS)],
        [BF16] * 10 + [F32, BF16, BF16, BF16], tabs=tabs, w_t=True, name="in_proj")

    n_chunk = S // CMP_STRIDE
    half = CMP_LEN // 2
    pe2 = p["cmp_pos_emb"].reshape(2, half * dh)
    cos_c, sin_c = tabs_c

    def compress(t, w1, w2, rope):
        out = _compress(t.reshape(B, n_chunk, CMP_STRIDE * dh), pe2,
                        w1[:half * dh].astype(BF16), w1[half * dh:].astype(BF16),
                        w2.astype(BF16), cos_c, sin_c, rope)
        return out.reshape(B, n_chunk // 4, 4, dh).transpose(0, 2, 1, 3).reshape(B, n_chunk, dh)

    k_c = compress(k_cr, p["w_cmp_k1"], p["w_cmp_k2"], True)
    v_c = compress(v_cr, p["w_cmp_v1"], p["w_cmp_v2"], False)
    q3 = nsa_q.reshape(B, S, NSA_HEADS * dh)
    o_cmp, pen = _nsa_compressed(q3, k_c, v_c)
    k_aug = jnp.concatenate([k_s.reshape(B, S, dh), onehot], axis=-1)
    o_sel = _nsa_selected(q3, pen, k_aug, v_s.reshape(B, S, dh))
    o_win = _nsa_window_attention(q3, k_w.reshape(B, S, dh), v_w.reshape(B, S, dh), NSA_WINDOW)

    dqk = QK_NOPE + QK_ROPE
    w_uq = p["w_uq"].reshape(Q_LORA, MLA_HEADS, dqk)
    w_uq = jnp.concatenate([w_uq[..., :QK_NOPE], _spread_rope64(w_uq[..., QK_NOPE:])], -1)
    w_uq = w_uq.reshape(Q_LORA, MLA_HEADS * 2 * dh).astype(BF16)
    (q_m,) = _proj(cq, p["g_q_lora"], w_uq, [(("none", "rope_r") * MLA_HEADS, dqk ** -0.5 * LOG2E)],
                   [BF16], tabs=tabs, name="mla_q_up")
    w_ukv = p["w_ukv"].reshape(KV_LORA, MLA_HEADS, 2 * dh)
    w_ukv = jnp.concatenate([w_ukv[..., :dh].reshape(KV_LORA, -1),
                             w_ukv[..., dh:].reshape(KV_LORA, -1)], 1).astype(BF16)
    k_m, v_m = _proj(ckv, p["g_kv_lora"], w_ukv,
                     [(("none", "extra") * MLA_HEADS, 1.0), (("none",) * MLA_HEADS, 1.0)],
                     [BF16, BF16], extra=k_pe, name="mla_kv_up")
    o_mla = _mla_attention(q_m.reshape(B, S, -1), k_m.reshape(B, S, -1),
                           v_m.reshape(B, S, -1), MLA_HEADS)

    o_dil = _dilated_attention(dq.reshape(B, S, dw), dk.reshape(B, S, dw), dv.reshape(B, S, dw),
                               DIL_PATTERNS, DIL_HEADS)

    h2 = _mix_out(o_cmp.reshape(T, -1), o_sel.reshape(T, -1), o_win.reshape(T, -1), gate,
                  o_mla.reshape(T, -1), o_dil.reshape(T, dw), p["w_out"].astype(BF16), h2,
                  p["g_mix_post"])

    xw = XATTN_HEADS * dh
    (xq,) = _proj(h2, p["g_mem_pre"], p["w_xq"].astype(BF16),
                  [(("none",) * XATTN_HEADS, scale)], [BF16], name="xattn_q")
    M = mem.shape[1]
    xk, xv = _proj(mem.reshape(B * M, D), p["g_mem_kv"], p["w_xkv"].astype(BF16),
                   [(("none",) * XATTN_HEADS, 1.0), (("none",) * XATTN_HEADS, 1.0)],
                   [BF16, BF16], bm=min(512, B * M), name="xattn_kv")
    xo = _cross_attention(xq.reshape(B, S, xw), xk.reshape(B, M, xw), xv.reshape(B, M, xw),
                          XATTN_HEADS)
    h2 = _out_proj(xo.reshape(T, xw), p["w_xo"].astype(BF16), h2, p["g_mem_post"],
                   name="xattn_out_proj")

    up = _mlp_up(h2, p["g_mlp_pre"], p["w_up"].astype(BF16))
    h2 = _out_proj(up, p["w_down"].astype(BF16), h2, p["g_mlp_post"], name="mlp_down")
    return h2.reshape(B, S, D)


_LAYER_PARAMS = ("g_mix_pre", "w_in", "cmp_pos_emb", "w_cmp_k1", "w_cmp_k2", "w_cmp_v1", "w_cmp_v2",
                 "g_q_lora", "g_kv_lora", "w_uq", "w_ukv", "w_out", "g_mix_post", "g_mem_pre",
                 "g_mem_kv", "w_xq", "w_xkv", "w_xo", "g_mem_post", "g_mlp_pre", "w_up", "w_down",
                 "g_mlp_post")


def kernel(x, mem, positions, g_mix_pre, w_in, cmp_pos_emb, w_cmp_k1, w_cmp_k2, w_cmp_v1, w_cmp_v2, g_q_lora, g_kv_lora, w_uq, w_ukv, w_out, g_mix_post, g_mem_pre, g_mem_kv, w_xq, w_xkv, w_xo, g_mem_post, g_mlp_pre, w_up, w_down, g_mlp_post):
    stacked = dict(zip(_LAYER_PARAMS, (
        g_mix_pre, w_in, cmp_pos_emb, w_cmp_k1, w_cmp_k2, w_cmp_v1, w_cmp_v2, g_q_lora, g_kv_lora,
        w_uq, w_ukv, w_out, g_mix_post, g_mem_pre, g_mem_kv, w_xq, w_xkv, w_xo, g_mem_post,
        g_mlp_pre, w_up, w_down, g_mlp_post)))
    B, S, D = x.shape
    T = B * S
    assert S % PEN_CHUNK == 0
    cosf, sinf = _full_tables(positions)
    cosr, sinr = _small_tables(positions)
    tabs = tuple(t.reshape(T, LANE) for t in (cosf, sinf, cosr, sinr))
    n_chunk = S // CMP_STRIDE
    end = jnp.minimum(jnp.arange(n_chunk) * CMP_STRIDE + CMP_LEN - 1, S - 1)
    tabs_c = _full_tables(positions[:, end])
    blk = (jnp.arange(S) // SLC_BLOCK) % LANE
    onehot = jnp.broadcast_to((blk[:, None] == jnp.arange(LANE)[None, :]).astype(BF16)[None],
                              (B, S, LANE))
    h = x
    for layer in range(stacked["w_in"].shape[0]):
        p = {}
        for name, val in stacked.items():
            v = val[layer]
            p[name] = v[None, :] if name.startswith("g_") else v
        h = _layer(h, mem, tabs, tabs_c, onehot, p)
    return h
```

```python
import functools

import numpy as np
import jax
import jax.numpy as jnp
from jax import lax
from jax.experimental import pallas as pl
from jax.experimental.pallas import tpu as pltpu

F32 = jnp.float32
BF16 = jnp.bfloat16

LANE = 128
VMEM_LIMIT = 56 * 1024 * 1024

HEAD_DIM = 128
ROPE_THETA = 10000.0
NORM_EPS = 1e-6
NEG_INF = -1e30
BIG = 1e9
LOG2E = 1.4426950408889634
NSA_HEADS = 4
MLA_HEADS = 6
DIL_HEADS = 6
CMP_LEN = 32
CMP_STRIDE = 16
SLC_BLOCK = 64
SLC_TOPK = 16
NSA_WINDOW = 512
Q_LORA = 512
KV_LORA = 512
QK_NOPE = 128
QK_ROPE = 64
DIL_PATTERNS = ((128, 1), (512, 4), (2048, 16))
XATTN_HEADS = 4
PEN_CHUNK = LANE * SLC_BLOCK


def _cparams(n_grid):
    return pltpu.CompilerParams(
        dimension_semantics=("arbitrary",) * n_grid, vmem_limit_bytes=VMEM_LIMIT)


def _rms(x, g):
    return x * lax.rsqrt(jnp.mean(x * x, axis=-1, keepdims=True) + NORM_EPS) * g


def _dot_t(a, b):
    return lax.dot_general(a, b, (((1,), (1,)), ((), ())), preferred_element_type=F32)


def _proj_kernel(out_plan, has_norm, n_tab, has_extra, w_t, chunk, *refs):
    x_ref, g_ref, w_ref = refs[:3]
    tab_refs = refs[3:3 + n_tab]
    n_in = 3 + n_tab + int(has_extra)
    out_refs = refs[n_in:]
    x = x_ref[...].astype(F32)
    if has_norm:
        x = _rms(x, g_ref[...])
    xb = x.astype(BF16)
    n_cols = w_ref.shape[0 if w_t else 1]
    flat = []
    for oi, (modes, scale) in enumerate(out_plan):
        for k, mode in enumerate(modes):
            if mode == "extra":
                out_refs[oi][:, k * LANE:(k + 1) * LANE] = refs[n_in - 1][...]
            else:
                flat.append((oi, k * LANE, mode, scale))
    for c0 in range(0, n_cols, chunk):
        c1 = min(c0 + chunk, n_cols)
        if w_t:
            acc = _dot_t(xb, w_ref[c0:c1, :])
        else:
            acc = jnp.dot(xb, w_ref[:, c0:c1], preferred_element_type=F32)
        for s in range((c1 - c0) // LANE):
            oi, off, mode, scale = flat[c0 // LANE + s]
            a = acc[:, s * LANE:(s + 1) * LANE]
            if mode == "rope":
                a = a * tab_refs[0][...] + pltpu.roll(a, LANE // 2, 1) * tab_refs[1][...]
            elif mode == "rope_r":
                a = a * tab_refs[2][...] + pltpu.roll(a, LANE // 2, 1) * tab_refs[3][...]
            if scale != 1.0:
                a = a * scale
            out_refs[oi][:, off:off + LANE] = a.astype(out_refs[oi].dtype)


def _proj(x, g, w, out_plan, out_dtypes, tabs=(), extra=None, w_t=False, bm=512, chunk=512,
          name="proj"):
    T, K = x.shape
    N = w.shape[0 if w_t else 1]
    assert T % bm == 0 and N % LANE == 0
    assert sum(sum(md != "extra" for md in m) for m, _ in out_plan) * LANE == N
    has_norm = g is not None
    if g is None:
        g = jnp.ones((1, K), F32)
    extras = () if extra is None else (extra,)
    in_specs = [pl.BlockSpec((bm, K), lambda i: (i, 0)),
                pl.BlockSpec((1, K), lambda i: (0, 0)),
                pl.BlockSpec(w.shape, lambda i: (0, 0), pipeline_mode=pl.Buffered(1))]
    in_specs += [pl.BlockSpec((bm, LANE), lambda i: (i, 0)) for _ in tabs + extras]
    out_shape = [jax.ShapeDtypeStruct((T, len(m) * LANE), dt)
                 for (m, _), dt in zip(out_plan, out_dtypes)]
    out_specs = [pl.BlockSpec((bm, len(m) * LANE), lambda i: (i, 0)) for m, _ in out_plan]
    return pl.pallas_call(
        functools.partial(_proj_kernel, out_plan, has_norm, len(tabs), extra is not None, w_t,
                          chunk),
        out_shape=out_shape, grid=(T // bm,), in_specs=in_specs, out_specs=out_specs,
        compiler_params=_cparams(1), name=name,
    )(x, g, w, *tabs, *extras)


def _mlp_up_kernel(x_ref, g_ref, w_ref, o_ref, xn_ref):
    @pl.when(pl.program_id(1) == 0)
    def _():
        xn_ref[...] = _rms(x_ref[...], g_ref[...]).astype(BF16)
    a = jnp.dot(xn_ref[...], w_ref[...], preferred_element_type=F32)
    a = jnp.maximum(a, 0.0)
    o_ref[...] = (a * a).astype(o_ref.dtype)


def _mlp_up(x, g, w, layer, bm=1024, bn=1024):
    T, K = x.shape
    N = w.shape[2]
    return pl.pallas_call(
        _mlp_up_kernel,
        out_shape=jax.ShapeDtypeStruct((T, N), BF16),
        grid=(T // bm, N // bn),
        in_specs=[pl.BlockSpec((bm, K), lambda i, j: (i, 0)),
                  pl.BlockSpec((1, K), lambda i, j: (0, 0)),
                  pl.BlockSpec((None, K, bn), lambda i, j: (layer, 0, j))],
        out_specs=pl.BlockSpec((bm, bn), lambda i, j: (i, j)),
        scratch_shapes=[pltpu.VMEM((bm, K), BF16)],
        compiler_params=_cparams(2), name="mlp_up",
    )(x, g, w)


def _out_proj_kernel(n_k, a_ref, w_ref, h_ref, g_ref, o_ref):
    k = pl.program_id(1)
    part = jnp.dot(a_ref[...], w_ref[...], preferred_element_type=F32)

    def finish(y):
        o_ref[...] = h_ref[...] + _rms(y, g_ref[...])

    if n_k == 1:
        finish(part)
    else:
        @pl.when(k == 0)
        def _():
            o_ref[...] = part

        @pl.when(jnp.logical_and(k > 0, k < n_k - 1))
        def _():
            o_ref[...] += part

        @pl.when(k == n_k - 1)
        def _():
            finish(o_ref[...] + part)


def _out_proj(a, w, h, g, layer, bm=512, bk=2048, name="out_proj"):
    T, K = a.shape
    N = w.shape[2]
    bk = min(bk, K)
    n_k = K // bk
    return pl.pallas_call(
        functools.partial(_out_proj_kernel, n_k),
        out_shape=jax.ShapeDtypeStruct((T, N), F32),
        grid=(T // bm, n_k),
        in_specs=[pl.BlockSpec((bm, bk), lambda i, k: (i, k)),
                  pl.BlockSpec((None, bk, N), lambda i, k: (layer, k, 0)),
                  pl.BlockSpec((bm, N), lambda i, k: (i, 0)),
                  pl.BlockSpec((1, N), lambda i, k: (0, 0))],
        out_specs=pl.BlockSpec((bm, N), lambda i, k: (i, 0)),
        compiler_params=_cparams(2), name=name,
    )(a, w, h, g)


def _online_step(carry, st, st_max, v):
    m, l, acc = carry
    m_new = jnp.maximum(m, st_max)
    alpha = jnp.exp2(m - m_new)
    p = jnp.exp2(st - m_new)
    l = alpha * l + jnp.sum(p, axis=0, keepdims=True)
    pv = lax.dot_general(v, p.astype(BF16), (((0,), (0,)), ((), ())), preferred_element_type=F32)
    return m_new, l, alpha * acc + pv


def _online_init(cols, dv):
    return (jnp.full((1, cols), NEG_INF, F32), jnp.zeros((1, cols), F32),
            jnp.zeros((dv, cols), F32))


def _flash_transposed(n_full, n_groups, scores, diag_mask, values, st_ref):
    groups = range(n_groups)

    def put(t, slot):
        maxes = []
        for g in groups:
            st = scores(g, t)
            st_ref[g, slot] = st
            maxes.append(jnp.max(st, axis=0, keepdims=True))
        return tuple(maxes)

    def half(t, carry, slot):
        stats, st_max = carry
        nxt = put(t + 1, 1 - slot)
        stats = tuple(_online_step(stats[g], st_ref[g, slot], st_max[g], values(g, t))
                      for g in groups)
        return stats, nxt

    def pair(u, carry):
        return half(2 * u + 1, half(2 * u, carry, 0), 1)

    init = tuple(_online_init(st_ref.shape[3], HEAD_DIM) for g in groups)
    carry = lax.fori_loop(0, n_full // 2, pair, (init, put(0, 0)))
    odd = n_full & 1
    stats, _ = lax.fori_loop(0, odd, lambda _, c: half(n_full - 1, c, 0), carry)
    outs = []
    for g in groups:
        st = diag_mask(g, st_ref[g, odd])
        _, l, acc = _online_step(stats[g], st, jnp.max(st, axis=0, keepdims=True),
                                 values(g, n_full))
        outs.append(acc / l)
    return outs


def _mla_kernel(tq, tk, hg, q_ref, k_ref, v_ref, o_ref, st_ref):
    t0 = pl.program_id(2) * tq
    dq = q_ref.shape[1] // hg
    last = t0 // tk

    def scores(g, t):
        start = pl.multiple_of(t * tk, tk)
        cols = slice(g * dq, (g + 1) * dq)
        return _dot_t(k_ref[pl.ds(start, tk), cols], q_ref[:, cols])

    def values(g, t):
        start = pl.multiple_of(t * tk, tk)
        return v_ref[pl.ds(start, tk), g * HEAD_DIM:(g + 1) * HEAD_DIM]

    def diag_mask(g, st):
        key = lax.broadcasted_iota(jnp.int32, (tk, tq), 0) + last * tk
        qry = lax.broadcasted_iota(jnp.int32, (tk, tq), 1) + t0
        return jnp.where(key <= qry, st, NEG_INF)

    outs = _flash_transposed(last, hg, scores, diag_mask, values, st_ref)
    for g in range(hg):
        o_ref[:, g * HEAD_DIM:(g + 1) * HEAD_DIM] = outs[g].T.astype(o_ref.dtype)


def _mla_attention(q, k, v, n_heads, tq=1024, tk=1024, hg=2):
    B, S, _ = q.shape
    dq = q.shape[2] // n_heads
    assert tk % tq == 0 and S % tk == 0 and n_heads % hg == 0
    resident = pl.Buffered(1)
    return pl.pallas_call(
        functools.partial(_mla_kernel, tq, tk, hg),
        out_shape=jax.ShapeDtypeStruct((B, S, n_heads * HEAD_DIM), BF16),
        grid=(B, n_heads // hg, S // tq),
        in_specs=[pl.BlockSpec((None, tq, hg * dq), lambda b, h, i: (b, i, h)),
                  pl.BlockSpec((None, S, hg * dq), lambda b, h, i: (b, 0, h),
                               pipeline_mode=resident),
                  pl.BlockSpec((None, S, hg * HEAD_DIM), lambda b, h, i: (b, 0, h),
                               pipeline_mode=resident)],
        out_specs=pl.BlockSpec((None, tq, hg * HEAD_DIM), lambda b, h, i: (b, i, h)),
        scratch_shapes=[pltpu.VMEM((hg, 2, tk, tq), F32)],
        compiler_params=_cparams(3), name="mla_attention",
    )(q, k, v)


def _band_block(q, kwin, vwin, window, base):
    R = q.shape[0]
    C = window + LANE
    s = _dot_t(q, kwin)
    r = lax.broadcasted_iota(jnp.int32, (R, C), 0) & (LANE - 1)
    c = lax.broadcasted_iota(jnp.int32, (R, C), 1)
    mask = (c >= r) & (c <= r + window) & (c >= window - base)
    s = jnp.where(mask, s, NEG_INF)
    m = jnp.max(s, axis=-1, keepdims=True)
    p = jnp.exp2(s - m)
    l = jnp.sum(p, axis=-1, keepdims=True)
    o = jnp.dot(p.astype(BF16), vwin, preferred_element_type=F32) / l
    return o, m + jnp.log2(l)


def _stack_heads(x, n):
    return jnp.concatenate([x[:, h * LANE:(h + 1) * LANE] for h in range(n)], axis=0)


def _nsa_window_kernel(tq, window, q_ref, kp_ref, kc_ref, vp_ref, vc_ref, o_ref):
    i = pl.program_id(1)
    kwin = jnp.concatenate([kp_ref[...], kc_ref[...]], axis=0)
    vwin = jnp.concatenate([vp_ref[...], vc_ref[...]], axis=0)
    for j in range(tq // LANE):
        q = _stack_heads(q_ref[j * LANE:(j + 1) * LANE, :], NSA_HEADS)
        lo = j * LANE
        o, _ = _band_block(q, kwin[lo:lo + window + LANE], vwin[lo:lo + window + LANE],
                           window, i * tq + lo)
        for h in range(NSA_HEADS):
            o_ref[lo:lo + LANE, h * LANE:(h + 1) * LANE] = (
                o[h * LANE:(h + 1) * LANE].astype(o_ref.dtype))


def _nsa_window_attention(q, k, v, window):
    B, S, W = q.shape
    tq = window
    prev = lambda b, i: (b, jnp.maximum(i - 1, 0), 0)
    cur = lambda b, i: (b, i, 0)
    return pl.pallas_call(
        functools.partial(_nsa_window_kernel, tq, window),
        out_shape=jax.ShapeDtypeStruct((B, S, W), BF16),
        grid=(B, S // tq),
        in_specs=[pl.BlockSpec((None, tq, W), cur),
                  pl.BlockSpec((None, window, HEAD_DIM), prev),
                  pl.BlockSpec((None, tq, HEAD_DIM), cur),
                  pl.BlockSpec((None, window, HEAD_DIM), prev),
                  pl.BlockSpec((None, tq, HEAD_DIM), cur)],
        out_specs=pl.BlockSpec((None, tq, W), cur),
        compiler_params=_cparams(2), name="nsa_window_attention",
    )(q, k, k, v, v)


def _dilated_kernel(nb, patterns, q_ref, kp_ref, kc_ref, vp_ref, vc_ref, o_ref,
                    qf_ref, kf_ref, vf_ref, of_ref, lf_ref):
    i = pl.program_id(2)
    qf_ref[...] = q_ref[...].astype(F32)
    kf_ref[0:nb, :] = kp_ref[...].astype(F32)
    kf_ref[nb:2 * nb, :] = kc_ref[...].astype(F32)
    vf_ref[0:nb, :] = vp_ref[...].astype(F32)
    vf_ref[nb:2 * nb, :] = vc_ref[...].astype(F32)
    for pi, (window, dil) in enumerate(patterns):
        w = window // dil
        per_class = nb // dil
        for r in range(dil):
            for j in range(per_class // LANE):
                q_lo = r + j * LANE * dil
                k_lo = nb + q_lo - w * dil
                rows_q = pl.ds(q_lo, LANE, stride=dil)
                rows_k = pl.ds(k_lo, w + LANE, stride=dil)
                o, lse = _band_block(qf_ref[rows_q, :].astype(BF16),
                                     kf_ref[rows_k, :].astype(BF16),
                                     vf_ref[rows_k, :].astype(BF16),
                                     w, i * per_class + j * LANE)
                of_ref[pi, rows_q, :] = o
                lf_ref[pi, rows_q, :] = jnp.broadcast_to(lse, (LANE, LANE))
    n_pat = len(patterns)
    mx = lf_ref[0]
    for pi in range(1, n_pat):
        mx = jnp.maximum(mx, lf_ref[pi])
    es = [jnp.exp2(lf_ref[pi] - mx) for pi in range(n_pat)]
    den = es[0]
    num = es[0] * of_ref[0]
    for pi in range(1, n_pat):
        den = den + es[pi]
        num = num + es[pi] * of_ref[pi]
    o_ref[...] = (num / den).astype(o_ref.dtype)


def _dilated_attention(q, k, v, patterns, n_heads, nb=2048):
    B, S, W = q.shape
    for window, dil in patterns:
        assert window % dil == 0 and window // dil == LANE
        assert window <= nb and nb % (dil * LANE) == 0
    assert S % nb == 0
    prev = lambda b, h, i: (b, jnp.maximum(i - 1, 0), h)
    cur = lambda b, h, i: (b, i, h)
    blk = lambda index_map: pl.BlockSpec((None, nb, LANE), index_map)
    return pl.pallas_call(
        functools.partial(_dilated_kernel, nb, patterns),
        out_shape=jax.ShapeDtypeStruct((B, S, W), BF16),
        grid=(B, n_heads, S // nb),
        in_specs=[blk(cur), blk(prev), blk(cur), blk(prev), blk(cur)],
        out_specs=blk(cur),
        scratch_shapes=[pltpu.VMEM((nb, LANE), F32),
                        pltpu.VMEM((2 * nb, LANE), F32),
                        pltpu.VMEM((2 * nb, LANE), F32),
                        pltpu.VMEM((len(patterns), nb, LANE), F32),
                        pltpu.VMEM((len(patterns), nb, LANE), F32)],
        compiler_params=_cparams(3), name="dilated_attention",
    )(q, k, k, v, v)


def _gelu_tanh(x):
    return 0.5 * x * (1.0 + jnp.tanh(0.7978845608028654 * (x + 0.044715 * (x * x * x))))


def _compress_kernel(rope, t_ref, pe_ref, w1a_ref, w1b_ref, w2_ref, cos_ref, sin_ref, o_ref):
    t = t_ref[...].astype(F32)
    n = t.shape[0]
    first = jnp.dot((t + pe_ref[0:1, :]).astype(BF16), w1a_ref[...], preferred_element_type=F32)
    second = jnp.dot((t + pe_ref[1:2, :]).astype(BF16), w1b_ref[...], preferred_element_type=F32)
    hid = first + pltpu.roll(second, n - 1, 0)
    out = jnp.dot(_gelu_tanh(hid).astype(BF16), w2_ref[...], preferred_element_type=F32)
    if rope:
        out = out * cos_ref[...] + pltpu.roll(out, LANE // 2, 1) * sin_ref[...]
    o_ref[...] = out.astype(o_ref.dtype)


def _compress(t, pe2, w1a, w1b, w2, cos_c, sin_c, rope):
    B, n, K = t.shape
    whole = lambda b: (0, 0)
    per_b = lambda b: (b, 0, 0)
    return pl.pallas_call(
        functools.partial(_compress_kernel, rope),
        out_shape=jax.ShapeDtypeStruct((B, n, HEAD_DIM), BF16),
        grid=(B,),
        in_specs=[pl.BlockSpec((None, n, K), per_b),
                  pl.BlockSpec((2, K), whole),
                  pl.BlockSpec((K, HEAD_DIM), whole),
                  pl.BlockSpec((K, HEAD_DIM), whole),
                  pl.BlockSpec((HEAD_DIM, HEAD_DIM), whole),
                  pl.BlockSpec((None, n, HEAD_DIM), per_b),
                  pl.BlockSpec((None, n, HEAD_DIM), per_b)],
        out_specs=pl.BlockSpec((None, n, HEAD_DIM), per_b),
        compiler_params=_cparams(1), name="nsa_compress",
    )(t, pe2, w1a, w1b, w2, cos_c, sin_c)


def _nsa_cmp_block(tq, n_slc, n_sel, t0, q, kc, vc):
    n_cmp = kc.shape[0]
    per_blk = SLC_BLOCK // CMP_STRIDE
    slc_shift = n_slc.bit_length() - 1
    q = _stack_heads(q, NSA_HEADS)
    s = _dot_t(q, kc)
    R = NSA_HEADS * tq
    row = lax.broadcasted_iota(jnp.int32, (R, n_cmp), 0) & (tq - 1)
    col = lax.broadcasted_iota(jnp.int32, (R, n_cmp), 1)
    j_of = col & (n_slc - 1)
    r_of = col >> slc_shift
    cmp_end = (per_blk * j_of + r_of) * CMP_STRIDE + (CMP_LEN - 1)
    cmask = cmp_end <= row + t0
    s = jnp.where(cmask, s, NEG_INF)
    m = jnp.max(s, axis=-1, keepdims=True)
    e = jnp.exp2(s - m)
    p = jnp.where(cmask, e / jnp.sum(e, axis=-1, keepdims=True), 0.0)
    o = jnp.dot(p.astype(BF16), vc, preferred_element_type=F32)

    ph = p[0:tq]
    for h in range(1, NSA_HEADS):
        ph = ph + p[h * tq:(h + 1) * tq]
    groups = [ph[:, r * n_slc:(r + 1) * n_slc] for r in range(per_blk)]
    blk = lax.broadcasted_iota(jnp.int32, (tq, n_slc), 1)
    spill = jnp.where(blk == 0, 0.0, pltpu.roll(groups[per_blk - 1], 1, 1))
    imp = groups[0]
    for r in range(1, per_blk):
        imp = imp + groups[r]
    imp = imp + spill
    tpos = lax.broadcasted_iota(jnp.int32, (tq, n_slc), 0) + t0
    cur = tpos >> (SLC_BLOCK.bit_length() - 1)
    forced = (blk == 0) | (blk == cur) | (blk == cur - 1)
    valid = blk <= cur
    work = jnp.where(forced, BIG, jnp.where(valid, imp, -BIG))
    blk_f = blk.astype(F32)
    sel = jnp.zeros((tq, n_slc), F32)
    for _ in range(n_sel):
        mx = jnp.max(work, axis=-1, keepdims=True)
        first = jnp.min(jnp.where(work == mx, blk_f, float(n_slc)), axis=-1, keepdims=True)
        pick = blk_f == first
        sel = jnp.where(pick, 1.0, sel)
        work = jnp.where(pick, -jnp.inf, work)
    return o, jnp.where((sel > 0.0) & valid, 0.0, NEG_INF)


def _nsa_cmp_kernel(tq, n_sub, n_slc, n_sel, q_ref, kc_ref, vc_ref, o_ref, pen_ref):
    i = pl.program_id(1)
    for u in range(n_sub):
        rows = slice(u * tq, (u + 1) * tq)
        o, pen = _nsa_cmp_block(tq, n_slc, n_sel, (i * n_sub + u) * tq, q_ref[rows, :],
                                kc_ref[...], vc_ref[...])
        for h in range(NSA_HEADS):
            o_ref[rows, h * LANE:(h + 1) * LANE] = o[h * tq:(h + 1) * tq].astype(o_ref.dtype)
        pen = pen.astype(pen_ref.dtype)
        for c in range(n_slc // LANE):
            pen_ref[c, rows, :] = pen[:, c * LANE:(c + 1) * LANE]


def _nsa_compressed(q, k_c, v_c, tq=128, n_sub=4):
    B, S, W = q.shape
    n_cmp = k_c.shape[1]
    n_slc = S // SLC_BLOCK
    n_sel = min(SLC_TOPK, n_slc)
    assert n_slc % LANE == 0 and n_slc & (n_slc - 1) == 0
    assert n_cmp == 4 * n_slc and tq & (tq - 1) == 0
    n_pc = n_slc // LANE
    bq = tq * n_sub
    return pl.pallas_call(
        functools.partial(_nsa_cmp_kernel, tq, n_sub, n_slc, n_sel),
        out_shape=[jax.ShapeDtypeStruct((B, S, W), BF16),
                   jax.ShapeDtypeStruct((B, n_pc, S, LANE), BF16)],
        grid=(B, S // bq),
        in_specs=[pl.BlockSpec((None, bq, W), lambda b, i: (b, i, 0)),
                  pl.BlockSpec((None, n_cmp, HEAD_DIM), lambda b, i: (b, 0, 0)),
                  pl.BlockSpec((None, n_cmp, HEAD_DIM), lambda b, i: (b, 0, 0))],
        out_specs=[pl.BlockSpec((None, bq, W), lambda b, i: (b, i, 0)),
                   pl.BlockSpec((None, n_pc, bq, LANE), lambda b, i: (b, 0, i, 0))],
        compiler_params=_cparams(2), name="nsa_compressed_topk",
    )(q, k_c, v_c)


def _nsa_sel_kernel(tq, tk, n_sub, q_ref, pen_ref, k_ref, v_ref, o_ref, qa_ref, st_ref):
    t0 = pl.program_id(1) * (tq * n_sub)
    R = NSA_HEADS * tq
    last = t0 // tk
    tiles_per_chunk = PEN_CHUNK // tk
    for u in range(n_sub):
        rows = slice(u * tq, (u + 1) * tq)
        q = _stack_heads(q_ref[rows, :], NSA_HEADS)
        for c in range(pen_ref.shape[0]):
            qa_ref[u, c, :, 0:LANE] = q
            qa_ref[u, c, :, LANE:2 * LANE] = jnp.concatenate([pen_ref[c, rows, :]] * NSA_HEADS,
                                                             axis=0)

    def scores(u, t):
        start = pl.multiple_of(t * tk, tk)
        return _dot_t(k_ref[pl.ds(start, tk), :], qa_ref[u, t // tiles_per_chunk])

    def diag_mask(u, st):
        key = lax.broadcasted_iota(jnp.int32, (tk, R), 0) + last * tk
        qry = (lax.broadcasted_iota(jnp.int32, (tk, R), 1) & (tq - 1)) + (t0 + u * tq)
        return jnp.where(key <= qry, st, NEG_INF)

    def values(u, t):
        return v_ref[pl.ds(pl.multiple_of(t * tk, tk), tk), :]

    outs = _flash_transposed(last, n_sub, scores, diag_mask, values, st_ref)
    for u in range(n_sub):
        for h in range(NSA_HEADS):
            o_ref[u * tq:(u + 1) * tq, h * LANE:(h + 1) * LANE] = (
                outs[u][:, h * tq:(h + 1) * tq].T.astype(o_ref.dtype))


def _nsa_selected(q, pen, k_aug, v, tq=128, n_sub=2, tk=1024):
    B, S, W = q.shape
    n_pc = pen.shape[1]
    bq = tq * n_sub
    assert tk % bq == 0 and PEN_CHUNK % tk == 0 and tq & (tq - 1) == 0
    return pl.pallas_call(
        functools.partial(_nsa_sel_kernel, tq, tk, n_sub),
        out_shape=jax.ShapeDtypeStruct((B, S, W), BF16),
        grid=(B, S // bq),
        in_specs=[pl.BlockSpec((None, bq, W), lambda b, i: (b, i, 0)),
                  pl.BlockSpec((None, n_pc, bq, LANE), lambda b, i: (b, 0, i, 0)),
                  pl.BlockSpec((None, S, 2 * HEAD_DIM), lambda b, i: (b, 0, 0)),
                  pl.BlockSpec((None, S, HEAD_DIM), lambda b, i: (b, 0, 0))],
        out_specs=pl.BlockSpec((None, bq, W), lambda b, i: (b, i, 0)),
        scratch_shapes=[pltpu.VMEM((n_sub, n_pc, NSA_HEADS * tq, 2 * HEAD_DIM), BF16),
                        pltpu.VMEM((n_sub, 2, tk, NSA_HEADS * tq), F32)],
        compiler_params=_cparams(2), name="nsa_selected_attention",
    )(q, pen, k_aug, v)


def _mix_out_kernel(oc_ref, os_ref, ow_ref, gate_ref, ob_ref, od_ref, w_ref, h_ref, g_ref,
                    o_ref, a_ref):
    gate = jax.nn.sigmoid(gate_ref[...])
    for h in range(NSA_HEADS):
        hs = slice(h * LANE, (h + 1) * LANE)
        a = (gate[:, 3 * h:3 * h + 1] * oc_ref[:, hs].astype(F32)
             + gate[:, 3 * h + 1:3 * h + 2] * os_ref[:, hs].astype(F32)
             + gate[:, 3 * h + 2:3 * h + 3] * ow_ref[:, hs].astype(F32))
        a_ref[:, hs] = a.astype(a_ref.dtype)
    off = NSA_HEADS * LANE
    a_ref[:, off:off + MLA_HEADS * LANE] = ob_ref[...]
    off += MLA_HEADS * LANE
    a_ref[:, off:off + DIL_HEADS * LANE] = od_ref[...]
    y = jnp.dot(a_ref[...], w_ref[...], preferred_element_type=F32)
    o_ref[...] = h_ref[...] + _rms(y, g_ref[...])


def _mix_out(o_c, o_s, o_w, gate, o_b, o_d, w, h, g, bm=512):
    T, N = h.shape
    width = (NSA_HEADS + MLA_HEADS + DIL_HEADS) * LANE
    heads = (o_c, o_s, o_w, gate, o_b, o_d)
    row = lambda a: pl.BlockSpec((bm, a.shape[1]), lambda i: (i, 0))
    return pl.pallas_call(
        _mix_out_kernel,
        out_shape=jax.ShapeDtypeStruct((T, N), F32),
        grid=(T // bm,),
        in_specs=[row(a) for a in heads] + [
            pl.BlockSpec((width, N), lambda i: (0, 0), pipeline_mode=pl.Buffered(1)),
            row(h), pl.BlockSpec((1, N), lambda i: (0, 0))],
        out_specs=row(h),
        scratch_shapes=[pltpu.VMEM((bm, width), BF16)],
        compiler_params=_cparams(1), name="mix_out_proj",
    )(*heads, w, h, g)


def _xattn_kernel(n_heads, scale, h_ref, gq_ref, wq_ref, k_ref, v_ref, wo_ref, go_ref, o_ref,
                  a_ref):
    h = h_ref[...]
    x = _rms(h, gq_ref[...]).astype(BF16)
    q = (jnp.dot(x, wq_ref[...], preferred_element_type=F32) * scale).astype(BF16)
    for hd in range(n_heads):
        hs = slice(hd * LANE, (hd + 1) * LANE)
        s = _dot_t(q[:, hs], k_ref[:, hs])
        m = jnp.max(s, axis=-1, keepdims=True)
        p = jnp.exp2(s - m)
        l = jnp.sum(p, axis=-1, keepdims=True)
        o = jnp.dot(p.astype(BF16), v_ref[:, hs], preferred_element_type=F32) / l
        a_ref[:, hs] = o.astype(a_ref.dtype)
    y = jnp.dot(a_ref[...], wo_ref[...], preferred_element_type=F32)
    o_ref[...] = h + _rms(y, go_ref[...])


def _cross_attention(h, gq, wq, k, v, wo, go, n_heads, scale, bm=512):
    B, S, D = h.shape
    M, W = k.shape[1:]
    const = lambda shape: pl.BlockSpec(shape, lambda b, i: (0, 0), pipeline_mode=pl.Buffered(1))
    return pl.pallas_call(
        functools.partial(_xattn_kernel, n_heads, scale),
        out_shape=jax.ShapeDtypeStruct((B, S, D), F32),
        grid=(B, S // bm),
        in_specs=[pl.BlockSpec((None, bm, D), lambda b, i: (b, i, 0)),
                  const((1, D)), const((D, W)),
                  pl.BlockSpec((None, M, W), lambda b, i: (b, 0, 0)),
                  pl.BlockSpec((None, M, W), lambda b, i: (b, 0, 0)),
                  const((W, D)), const((1, D))],
        out_specs=pl.BlockSpec((None, bm, D), lambda b, i: (b, i, 0)),
        scratch_shapes=[pltpu.VMEM((bm, W), BF16)],
        compiler_params=_cparams(2), name="cross_attention",
    )(h, gq, wq, k, v, wo, go)


def _rope_tables(pos, dim):
    inv = ROPE_THETA ** (-jnp.arange(0, dim, 2, dtype=F32) / dim)
    ang = pos.astype(F32)[..., None] * inv
    return jnp.cos(ang), jnp.sin(ang)


def _full_tables(pos):
    c, s = _rope_tables(pos, HEAD_DIM)
    return jnp.concatenate([c, c], -1), jnp.concatenate([-s, s], -1)


def _spread_rope64(t):
    z = jnp.zeros(t.shape[:-1] + (QK_ROPE // 2,), t.dtype)
    return jnp.concatenate([t[..., :QK_ROPE // 2], z, t[..., QK_ROPE // 2:], z], -1)


def _small_tables(pos):
    c, s = _rope_tables(pos, QK_ROPE)
    return _spread_rope64(jnp.concatenate([c, c], -1)), _spread_rope64(jnp.concatenate([-s, s], -1))


def _layer(h, mem, tabs, tabs_c, onehot, p):
    B, S, D = h.shape
    T = B * S
    dh = HEAD_DIM
    scale = dh ** -0.5 * LOG2E
    h2 = h.reshape(T, D)

    w_in = p["w_in"].T
    cuts = np.cumsum([NSA_HEADS * dh, 6 * dh, 3 * NSA_HEADS, Q_LORA, KV_LORA, QK_ROPE])
    w_q, w_kv, w_g, w_cq, w_ckv, w_kr, w_dil = jnp.split(w_in, cuts, axis=0)
    kv = [w_kv[k * dh:(k + 1) * dh] for k in range(6)]
    dw = DIL_HEADS * dh
    w_dq, w_dk, w_dv = w_dil[:dw], w_dil[dw:2 * dw], w_dil[2 * dw:]
    w_gpad = jnp.pad(w_g, ((0, LANE - w_g.shape[0]), (0, 0)))
    w_all = jnp.concatenate([w_q, kv[2], kv[4], w_dq, w_dk, _spread_rope64(w_kr.T).T,
                             kv[0], kv[1], kv[3], kv[5], w_gpad, w_cq, w_ckv, w_dv], 0).astype(BF16)
    one = lambda mode, n=1, s=1.0: ((mode,) * n, s)
    nsa_q, k_s, k_w, dq, dk, k_pe, k_cr, v_cr, v_s, v_w, gate, cq, ckv, dv = _proj(
        h2, p["g_mix_pre"], w_all,
        [one("rope", NSA_HEADS, scale), one("rope"), one("rope"), one("rope", DIL_HEADS, scale),
         one("rope", DIL_HEADS), one("rope_r"), one("none"), one("none"), one("none"), one("none"),
         one("none"), one("none", Q_LORA // LANE), one("none", KV_LORA // LANE),
         one("none", DIL_HEADS)],
        [BF16] * 10 + [F32, BF16, BF16, BF16], tabs=tabs, w_t=True, name="in_proj")

    n_chunk = S // CMP_STRIDE
    half = CMP_LEN // 2
    pe2 = p["cmp_pos_emb"].reshape(2, half * dh)
    cos_c, sin_c = tabs_c

    def compress(t, w1, w2, rope):
        out = _compress(t.reshape(B, n_chunk, CMP_STRIDE * dh), pe2,
                        w1[:half * dh].astype(BF16), w1[half * dh:].astype(BF16),
                        w2.astype(BF16), cos_c, sin_c, rope)
        return out.reshape(B, n_chunk // 4, 4, dh).transpose(0, 2, 1, 3).reshape(B, n_chunk, dh)

    k_c = compress(k_cr, p["w_cmp_k1"], p["w_cmp_k2"], True)
    v_c = compress(v_cr, p["w_cmp_v1"], p["w_cmp_v2"], False)
    q3 = nsa_q.reshape(B, S, NSA_HEADS * dh)
    o_cmp, pen = _nsa_compressed(q3, k_c, v_c)
    k_aug = jnp.concatenate([k_s.reshape(B, S, dh), onehot], axis=-1)
    o_sel = _nsa_selected(q3, pen, k_aug, v_s.reshape(B, S, dh))
    o_win = _nsa_window_attention(q3, k_w.reshape(B, S, dh), v_w.reshape(B, S, dh), NSA_WINDOW)

    dqk = QK_NOPE + QK_ROPE
    w_uq = p["w_uq"].reshape(Q_LORA, MLA_HEADS, dqk)
    w_uq = jnp.concatenate([w_uq[..., :QK_NOPE], _spread_rope64(w_uq[..., QK_NOPE:])], -1)
    w_uq = w_uq.reshape(Q_LORA, MLA_HEADS * 2 * dh).astype(BF16)
    (q_m,) = _proj(cq, p["g_q_lora"], w_uq, [(("none", "rope_r") * MLA_HEADS, dqk ** -0.5 * LOG2E)],
                   [BF16], tabs=tabs, name="mla_q_up")
    w_ukv = p["w_ukv"].reshape(KV_LORA, MLA_HEADS, 2 * dh)
    w_ukv = jnp.concatenate([w_ukv[..., :dh].reshape(KV_LORA, -1),
                             w_ukv[..., dh:].reshape(KV_LORA, -1)], 1).astype(BF16)
    k_m, v_m = _proj(ckv, p["g_kv_lora"], w_ukv,
                     [(("none", "extra") * MLA_HEADS, 1.0), (("none",) * MLA_HEADS, 1.0)],
                     [BF16, BF16], extra=k_pe, name="mla_kv_up")
    o_mla = _mla_attention(q_m.reshape(B, S, -1), k_m.reshape(B, S, -1),
                           v_m.reshape(B, S, -1), MLA_HEADS)

    o_dil = _dilated_attention(dq.reshape(B, S, dw), dk.reshape(B, S, dw), dv.reshape(B, S, dw),
                               DIL_PATTERNS, DIL_HEADS)

    h2 = _mix_out(o_cmp.reshape(T, -1), o_sel.reshape(T, -1), o_win.reshape(T, -1), gate,
                  o_mla.reshape(T, -1), o_dil.reshape(T, dw), p["w_out"].astype(BF16), h2,
                  p["g_mix_post"])

    xw = XATTN_HEADS * dh
    M = mem.shape[1]
    xk, xv = _proj(mem.reshape(B * M, D), p["g_mem_kv"], p["w_xkv"].astype(BF16),
                   [(("none",) * XATTN_HEADS, 1.0), (("none",) * XATTN_HEADS, 1.0)],
                   [BF16, BF16], bm=min(512, B * M), name="xattn_kv")
    h2 = _cross_attention(h2.reshape(B, S, D), p["g_mem_pre"], p["w_xq"].astype(BF16),
                          xk.reshape(B, M, xw), xv.reshape(B, M, xw), p["w_xo"].astype(BF16),
                          p["g_mem_post"], XATTN_HEADS, scale).reshape(T, D)

    up = _mlp_up(h2, p["g_mlp_pre"], p["w_up_all"], p["layer"])
    h2 = _out_proj(up, p["w_down_all"], h2, p["g_mlp_post"], layer=p["layer"], name="mlp_down")
    return h2.reshape(B, S, D)


_LAYER_PARAMS = ("g_mix_pre", "w_in", "cmp_pos_emb", "w_cmp_k1", "w_cmp_k2", "w_cmp_v1", "w_cmp_v2",
                 "g_q_lora", "g_kv_lora", "w_uq", "w_ukv", "w_out", "g_mix_post", "g_mem_pre",
                 "g_mem_kv", "w_xq", "w_xkv", "w_xo", "g_mem_post", "g_mlp_pre", "w_up", "w_down",
                 "g_mlp_post")


def kernel(x, mem, positions, g_mix_pre, w_in, cmp_pos_emb, w_cmp_k1, w_cmp_k2, w_cmp_v1, w_cmp_v2, g_q_lora, g_kv_lora, w_uq, w_ukv, w_out, g_mix_post, g_mem_pre, g_mem_kv, w_xq, w_xkv, w_xo, g_mem_post, g_mlp_pre, w_up, w_down, g_mlp_post):
    stacked = dict(zip(_LAYER_PARAMS, (
        g_mix_pre, w_in, cmp_pos_emb, w_cmp_k1, w_cmp_k2, w_cmp_v1, w_cmp_v2, g_q_lora, g_kv_lora,
        w_uq, w_ukv, w_out, g_mix_post, g_mem_pre, g_mem_kv, w_xq, w_xkv, w_xo, g_mem_post,
        g_mlp_pre, w_up, w_down, g_mlp_post)))
    B, S, D = x.shape
    T = B * S
    assert S % PEN_CHUNK == 0
    cosf, sinf = _full_tables(positions)
    cosr, sinr = _small_tables(positions)
    tabs = tuple(t.reshape(T, LANE) for t in (cosf, sinf, cosr, sinr))
    n_chunk = S // CMP_STRIDE
    end = jnp.minimum(jnp.arange(n_chunk) * CMP_STRIDE + CMP_LEN - 1, S - 1)
    tabs_c = _full_tables(positions[:, end])
    blk = (jnp.arange(S) // SLC_BLOCK) % LANE
    onehot = jnp.broadcast_to((blk[:, None] == jnp.arange(LANE)[None, :]).astype(BF16)[None],
                              (B, S, LANE))
    h = x
    w_up_all, w_down_all = w_up.astype(BF16), w_down.astype(BF16)
    for layer in range(stacked["w_in"].shape[0]):
        p = {"layer": layer, "w_up_all": w_up_all, "w_down_all": w_down_all}
        for name, val in stacked.items():
            if name not in ("w_up", "w_down"):
                v = val[layer]
                p[name] = v[None, :] if name.startswith("g_") else v
        h = _layer(h, mem, tabs, tabs_c, onehot, p)
    return h
```

```python
import functools

import numpy as np
import jax
import jax.numpy as jnp
from jax import lax
from jax.experimental import pallas as pl
from jax.experimental.pallas import tpu as pltpu

F32 = jnp.float32
BF16 = jnp.bfloat16

LANE = 128
VMEM_LIMIT = 56 * 1024 * 1024

HEAD_DIM = 128
ROPE_THETA = 10000.0
NORM_EPS = 1e-6
NEG_INF = -1e30
BIG = 1e9
LOG2E = 1.4426950408889634
NSA_HEADS = 4
MLA_HEADS = 6
DIL_HEADS = 6
CMP_LEN = 32
CMP_STRIDE = 16
SLC_BLOCK = 64
SLC_TOPK = 16
NSA_WINDOW = 512
Q_LORA = 512
KV_LORA = 512
QK_NOPE = 128
QK_ROPE = 64
DIL_PATTERNS = ((128, 1), (512, 4), (2048, 16))
XATTN_HEADS = 4
PEN_CHUNK = LANE * SLC_BLOCK


def _cparams(n_grid):
    return pltpu.CompilerParams(
        dimension_semantics=("arbitrary",) * n_grid, vmem_limit_bytes=VMEM_LIMIT)


def _rms(x, g):
    return x * lax.rsqrt(jnp.mean(x * x, axis=-1, keepdims=True) + NORM_EPS) * g


def _dot_t(a, b):
    return lax.dot_general(a, b, (((1,), (1,)), ((), ())), preferred_element_type=F32)


def _proj_kernel(out_plan, has_norm, n_tab, has_extra, w_t, chunk, *refs):
    x_ref, g_ref, w_ref = refs[:3]
    tab_refs = refs[3:3 + n_tab]
    n_in = 3 + n_tab + int(has_extra)
    out_refs = refs[n_in:]
    x = x_ref[...].astype(F32)
    if has_norm:
        x = _rms(x, g_ref[...])
    xb = x.astype(BF16)
    n_cols = w_ref.shape[0 if w_t else 1]
    flat = []
    for oi, (modes, scale) in enumerate(out_plan):
        for k, mode in enumerate(modes):
            if mode == "extra":
                out_refs[oi][:, k * LANE:(k + 1) * LANE] = refs[n_in - 1][...]
            else:
                flat.append((oi, k * LANE, mode, scale))
    for c0 in range(0, n_cols, chunk):
        c1 = min(c0 + chunk, n_cols)
        if w_t:
            acc = _dot_t(xb, w_ref[c0:c1, :])
        else:
            acc = jnp.dot(xb, w_ref[:, c0:c1], preferred_element_type=F32)
        for s in range((c1 - c0) // LANE):
            oi, off, mode, scale = flat[c0 // LANE + s]
            a = acc[:, s * LANE:(s + 1) * LANE]
            if mode == "rope":
                a = a * tab_refs[0][...] + pltpu.roll(a, LANE // 2, 1) * tab_refs[1][...]
            elif mode == "rope_r":
                a = a * tab_refs[2][...] + pltpu.roll(a, LANE // 2, 1) * tab_refs[3][...]
            if scale != 1.0:
                a = a * scale
            out_refs[oi][:, off:off + LANE] = a.astype(out_refs[oi].dtype)


def _proj(x, g, w, out_plan, out_dtypes, tabs=(), extra=None, w_t=False, bm=512, chunk=512,
          name="proj"):
    T, K = x.shape
    N = w.shape[0 if w_t else 1]
    assert T % bm == 0 and N % LANE == 0
    assert sum(sum(md != "extra" for md in m) for m, _ in out_plan) * LANE == N
    has_norm = g is not None
    if g is None:
        g = jnp.ones((1, K), F32)
    extras = () if extra is None else (extra,)
    in_specs = [pl.BlockSpec((bm, K), lambda i: (i, 0)),
                pl.BlockSpec((1, K), lambda i: (0, 0)),
                pl.BlockSpec(w.shape, lambda i: (0, 0), pipeline_mode=pl.Buffered(1))]
    in_specs += [pl.BlockSpec((bm, LANE), lambda i: (i, 0)) for _ in tabs + extras]
    out_shape = [jax.ShapeDtypeStruct((T, len(m) * LANE), dt)
                 for (m, _), dt in zip(out_plan, out_dtypes)]
    out_specs = [pl.BlockSpec((bm, len(m) * LANE), lambda i: (i, 0)) for m, _ in out_plan]
    return pl.pallas_call(
        functools.partial(_proj_kernel, out_plan, has_norm, len(tabs), extra is not None, w_t,
                          chunk),
        out_shape=out_shape, grid=(T // bm,), in_specs=in_specs, out_specs=out_specs,
        compiler_params=_cparams(1), name=name,
    )(x, g, w, *tabs, *extras)


def _mlp_up_kernel(x_ref, g_ref, w_ref, o_ref, xn_ref):
    @pl.when(pl.program_id(1) == 0)
    def _():
        xn_ref[...] = _rms(x_ref[...], g_ref[...]).astype(BF16)
    a = jnp.dot(xn_ref[...], w_ref[...], preferred_element_type=F32)
    a = jnp.maximum(a, 0.0)
    o_ref[...] = (a * a).astype(o_ref.dtype)


def _mlp_up(x, g, w, layer, bm=1024, bn=2048):
    T, K = x.shape
    N = w.shape[2]
    return pl.pallas_call(
        _mlp_up_kernel,
        out_shape=jax.ShapeDtypeStruct((T, N), BF16),
        grid=(T // bm, N // bn),
        in_specs=[pl.BlockSpec((bm, K), lambda i, j: (i, 0)),
                  pl.BlockSpec((1, K), lambda i, j: (0, 0)),
                  pl.BlockSpec((None, K, bn), lambda i, j: (layer, 0, j))],
        out_specs=pl.BlockSpec((bm, bn), lambda i, j: (i, j)),
        scratch_shapes=[pltpu.VMEM((bm, K), BF16)],
        compiler_params=_cparams(2), name="mlp_up",
    )(x, g, w)


def _out_proj_kernel(n_k, a_ref, w_ref, h_ref, g_ref, o_ref):
    k = pl.program_id(1)
    part = jnp.dot(a_ref[...], w_ref[...], preferred_element_type=F32)

    def finish(y):
        o_ref[...] = h_ref[...] + _rms(y, g_ref[...])

    if n_k == 1:
        finish(part)
    else:
        @pl.when(k == 0)
        def _():
            o_ref[...] = part

        @pl.when(jnp.logical_and(k > 0, k < n_k - 1))
        def _():
            o_ref[...] += part

        @pl.when(k == n_k - 1)
        def _():
            finish(o_ref[...] + part)


def _out_proj(a, w, h, g, layer, bm=512, bk=2048, name="out_proj"):
    T, K = a.shape
    N = w.shape[2]
    bk = min(bk, K)
    n_k = K // bk
    return pl.pallas_call(
        functools.partial(_out_proj_kernel, n_k),
        out_shape=jax.ShapeDtypeStruct((T, N), F32),
        grid=(T // bm, n_k),
        in_specs=[pl.BlockSpec((bm, bk), lambda i, k: (i, k)),
                  pl.BlockSpec((None, bk, N), lambda i, k: (layer, k, 0)),
                  pl.BlockSpec((bm, N), lambda i, k: (i, 0)),
                  pl.BlockSpec((1, N), lambda i, k: (0, 0))],
        out_specs=pl.BlockSpec((bm, N), lambda i, k: (i, 0)),
        compiler_params=_cparams(2), name=name,
    )(a, w, h, g)


def _online_step(carry, st, st_max, v):
    m, l, acc = carry
    m_new = jnp.maximum(m, st_max)
    alpha = jnp.exp2(m - m_new)
    p = jnp.exp2(st - m_new)
    l = alpha * l + jnp.sum(p, axis=0, keepdims=True)
    pv = lax.dot_general(v, p.astype(BF16), (((0,), (0,)), ((), ())), preferred_element_type=F32)
    return m_new, l, alpha * acc + pv


def _online_init(cols, dv):
    return (jnp.full((1, cols), NEG_INF, F32), jnp.zeros((1, cols), F32),
            jnp.zeros((dv, cols), F32))


def _flash_transposed(n_full, n_groups, scores, diag_mask, values, st_ref):
    groups = range(n_groups)

    def put(t, slot):
        maxes = []
        for g in groups:
            st = scores(g, t)
            st_ref[g, slot] = st
            maxes.append(jnp.max(st, axis=0, keepdims=True))
        return tuple(maxes)

    def half(t, carry, slot):
        stats, st_max = carry
        nxt = put(t + 1, 1 - slot)
        stats = tuple(_online_step(stats[g], st_ref[g, slot], st_max[g], values(g, t))
                      for g in groups)
        return stats, nxt

    def pair(u, carry):
        return half(2 * u + 1, half(2 * u, carry, 0), 1)

    init = tuple(_online_init(st_ref.shape[3], HEAD_DIM) for g in groups)
    carry = lax.fori_loop(0, n_full // 2, pair, (init, put(0, 0)))
    odd = n_full & 1
    stats, _ = lax.fori_loop(0, odd, lambda _, c: half(n_full - 1, c, 0), carry)
    outs = []
    for g in groups:
        st = diag_mask(g, st_ref[g, odd])
        _, l, acc = _online_step(stats[g], st, jnp.max(st, axis=0, keepdims=True),
                                 values(g, n_full))
        outs.append(acc / l)
    return outs


def _mla_kernel(tq, tk, hg, q_ref, k_ref, v_ref, o_ref, st_ref):
    t0 = pl.program_id(2) * tq
    dq = q_ref.shape[1] // hg
    last = t0 // tk

    def scores(g, t):
        start = pl.multiple_of(t * tk, tk)
        cols = slice(g * dq, (g + 1) * dq)
        return _dot_t(k_ref[pl.ds(start, tk), cols], q_ref[:, cols])

    def values(g, t):
        start = pl.multiple_of(t * tk, tk)
        return v_ref[pl.ds(start, tk), g * HEAD_DIM:(g + 1) * HEAD_DIM]

    def diag_mask(g, st):
        key = lax.broadcasted_iota(jnp.int32, (tk, tq), 0) + last * tk
        qry = lax.broadcasted_iota(jnp.int32, (tk, tq), 1) + t0
        return jnp.where(key <= qry, st, NEG_INF)

    outs = _flash_transposed(last, hg, scores, diag_mask, values, st_ref)
    for g in range(hg):
        o_ref[:, g * HEAD_DIM:(g + 1) * HEAD_DIM] = outs[g].T.astype(o_ref.dtype)


def _mla_attention(q, k, v, n_heads, tq=1024, tk=1024, hg=2):
    B, S, _ = q.shape
    dq = q.shape[2] // n_heads
    assert tk % tq == 0 and S % tk == 0 and n_heads % hg == 0
    resident = pl.Buffered(1)
    return pl.pallas_call(
        functools.partial(_mla_kernel, tq, tk, hg),
        out_shape=jax.ShapeDtypeStruct((B, S, n_heads * HEAD_DIM), BF16),
        grid=(B, n_heads // hg, S // tq),
        in_specs=[pl.BlockSpec((None, tq, hg * dq), lambda b, h, i: (b, i, h)),
                  pl.BlockSpec((None, S, hg * dq), lambda b, h, i: (b, 0, h),
                               pipeline_mode=resident),
                  pl.BlockSpec((None, S, hg * HEAD_DIM), lambda b, h, i: (b, 0, h),
                               pipeline_mode=resident)],
        out_specs=pl.BlockSpec((None, tq, hg * HEAD_DIM), lambda b, h, i: (b, i, h)),
        scratch_shapes=[pltpu.VMEM((hg, 2, tk, tq), F32)],
        compiler_params=_cparams(3), name="mla_attention",
    )(q, k, v)


def _band_block(q, kwin, vwin, window, base):
    R = q.shape[0]
    C = window + LANE
    s = _dot_t(q, kwin)
    r = lax.broadcasted_iota(jnp.int32, (R, C), 0) & (LANE - 1)
    c = lax.broadcasted_iota(jnp.int32, (R, C), 1)
    mask = (c >= r) & (c <= r + window) & (c >= window - base)
    s = jnp.where(mask, s, NEG_INF)
    m = jnp.max(s, axis=-1, keepdims=True)
    p = jnp.exp2(s - m)
    l = jnp.sum(p, axis=-1, keepdims=True)
    o = jnp.dot(p.astype(BF16), vwin, preferred_element_type=F32) / l
    return o, m + jnp.log2(l)


def _stack_heads(x, n):
    return jnp.concatenate([x[:, h * LANE:(h + 1) * LANE] for h in range(n)], axis=0)


def _nsa_window_kernel(tq, window, q_ref, kp_ref, kc_ref, vp_ref, vc_ref, o_ref):
    i = pl.program_id(1)
    kwin = jnp.concatenate([kp_ref[...], kc_ref[...]], axis=0)
    vwin = jnp.concatenate([vp_ref[...], vc_ref[...]], axis=0)
    for j in range(tq // LANE):
        q = _stack_heads(q_ref[j * LANE:(j + 1) * LANE, :], NSA_HEADS)
        lo = j * LANE
        o, _ = _band_block(q, kwin[lo:lo + window + LANE], vwin[lo:lo + window + LANE],
                           window, i * tq + lo)
        for h in range(NSA_HEADS):
            o_ref[lo:lo + LANE, h * LANE:(h + 1) * LANE] = (
                o[h * LANE:(h + 1) * LANE].astype(o_ref.dtype))


def _nsa_window_attention(q, k, v, window):
    B, S, W = q.shape
    tq = window
    prev = lambda b, i: (b, jnp.maximum(i - 1, 0), 0)
    cur = lambda b, i: (b, i, 0)
    return pl.pallas_call(
        functools.partial(_nsa_window_kernel, tq, window),
        out_shape=jax.ShapeDtypeStruct((B, S, W), BF16),
        grid=(B, S // tq),
        in_specs=[pl.BlockSpec((None, tq, W), cur),
                  pl.BlockSpec((None, window, HEAD_DIM), prev),
                  pl.BlockSpec((None, tq, HEAD_DIM), cur),
                  pl.BlockSpec((None, window, HEAD_DIM), prev),
                  pl.BlockSpec((None, tq, HEAD_DIM), cur)],
        out_specs=pl.BlockSpec((None, tq, W), cur),
        compiler_params=_cparams(2), name="nsa_window_attention",
    )(q, k, k, v, v)


def _dilated_kernel(nb, patterns, q_ref, kp_ref, kc_ref, vp_ref, vc_ref, o_ref,
                    qf_ref, kf_ref, vf_ref, of_ref, lf_ref):
    i = pl.program_id(2)
    qf_ref[...] = q_ref[...].astype(F32)
    kf_ref[0:nb, :] = kp_ref[...].astype(F32)
    kf_ref[nb:2 * nb, :] = kc_ref[...].astype(F32)
    vf_ref[0:nb, :] = vp_ref[...].astype(F32)
    vf_ref[nb:2 * nb, :] = vc_ref[...].astype(F32)
    for pi, (window, dil) in enumerate(patterns):
        w = window // dil
        per_class = nb // dil
        for r in range(dil):
            for j in range(per_class // LANE):
                q_lo = r + j * LANE * dil
                k_lo = nb + q_lo - w * dil
                rows_q = pl.ds(q_lo, LANE, stride=dil)
                rows_k = pl.ds(k_lo, w + LANE, stride=dil)
                o, lse = _band_block(qf_ref[rows_q, :].astype(BF16),
                                     kf_ref[rows_k, :].astype(BF16),
                                     vf_ref[rows_k, :].astype(BF16),
                                     w, i * per_class + j * LANE)
                of_ref[pi, rows_q, :] = o
                lf_ref[pi, rows_q, :] = jnp.broadcast_to(lse, (LANE, LANE))
    n_pat = len(patterns)
    mx = lf_ref[0]
    for pi in range(1, n_pat):
        mx = jnp.maximum(mx, lf_ref[pi])
    es = [jnp.exp2(lf_ref[pi] - mx) for pi in range(n_pat)]
    den = es[0]
    num = es[0] * of_ref[0]
    for pi in range(1, n_pat):
        den = den + es[pi]
        num = num + es[pi] * of_ref[pi]
    o_ref[...] = (num / den).astype(o_ref.dtype)


def _dilated_attention(q, k, v, patterns, n_heads, nb=2048):
    B, S, W = q.shape
    for window, dil in patterns:
        assert window % dil == 0 and window // dil == LANE
        assert window <= nb and nb % (dil * LANE) == 0
    assert S % nb == 0
    prev = lambda b, h, i: (b, jnp.maximum(i - 1, 0), h)
    cur = lambda b, h, i: (b, i, h)
    blk = lambda index_map: pl.BlockSpec((None, nb, LANE), index_map)
    return pl.pallas_call(
        functools.partial(_dilated_kernel, nb, patterns),
        out_shape=jax.ShapeDtypeStruct((B, S, W), BF16),
        grid=(B, n_heads, S // nb),
        in_specs=[blk(cur), blk(prev), blk(cur), blk(prev), blk(cur)],
        out_specs=blk(cur),
        scratch_shapes=[pltpu.VMEM((nb, LANE), F32),
                        pltpu.VMEM((2 * nb, LANE), F32),
                        pltpu.VMEM((2 * nb, LANE), F32),
                        pltpu.VMEM((len(patterns), nb, LANE), F32),
                        pltpu.VMEM((len(patterns), nb, LANE), F32)],
        compiler_params=_cparams(3), name="dilated_attention",
    )(q, k, k, v, v)


def _gelu_tanh(x):
    return 0.5 * x * (1.0 + jnp.tanh(0.7978845608028654 * (x + 0.044715 * (x * x * x))))


def _compress_kernel(rope, t_ref, pe_ref, w1a_ref, w1b_ref, w2_ref, cos_ref, sin_ref, o_ref):
    t = t_ref[...].astype(F32)
    n = t.shape[0]
    first = jnp.dot((t + pe_ref[0:1, :]).astype(BF16), w1a_ref[...], preferred_element_type=F32)
    second = jnp.dot((t + pe_ref[1:2, :]).astype(BF16), w1b_ref[...], preferred_element_type=F32)
    hid = first + pltpu.roll(second, n - 1, 0)
    out = jnp.dot(_gelu_tanh(hid).astype(BF16), w2_ref[...], preferred_element_type=F32)
    if rope:
        out = out * cos_ref[...] + pltpu.roll(out, LANE // 2, 1) * sin_ref[...]
    o_ref[...] = out.astype(o_ref.dtype)


def _compress(t, pe2, w1a, w1b, w2, cos_c, sin_c, rope):
    B, n, K = t.shape
    whole = lambda b: (0, 0)
    per_b = lambda b: (b, 0, 0)
    return pl.pallas_call(
        functools.partial(_compress_kernel, rope),
        out_shape=jax.ShapeDtypeStruct((B, n, HEAD_DIM), BF16),
        grid=(B,),
        in_specs=[pl.BlockSpec((None, n, K), per_b),
                  pl.BlockSpec((2, K), whole),
                  pl.BlockSpec((K, HEAD_DIM), whole),
                  pl.BlockSpec((K, HEAD_DIM), whole),
                  pl.BlockSpec((HEAD_DIM, HEAD_DIM), whole),
                  pl.BlockSpec((None, n, HEAD_DIM), per_b),
                  pl.BlockSpec((None, n, HEAD_DIM), per_b)],
        out_specs=pl.BlockSpec((None, n, HEAD_DIM), per_b),
        compiler_params=_cparams(1), name="nsa_compress",
    )(t, pe2, w1a, w1b, w2, cos_c, sin_c)


def _nsa_cmp_block(tq, n_slc, n_sel, t0, q, kc, vc):
    n_cmp = kc.shape[0]
    per_blk = SLC_BLOCK // CMP_STRIDE
    slc_shift = n_slc.bit_length() - 1
    q = _stack_heads(q, NSA_HEADS)
    s = _dot_t(q, kc)
    R = NSA_HEADS * tq
    row = lax.broadcasted_iota(jnp.int32, (R, n_cmp), 0) & (tq - 1)
    col = lax.broadcasted_iota(jnp.int32, (R, n_cmp), 1)
    j_of = col & (n_slc - 1)
    r_of = col >> slc_shift
    cmp_end = (per_blk * j_of + r_of) * CMP_STRIDE + (CMP_LEN - 1)
    cmask = cmp_end <= row + t0
    s = jnp.where(cmask, s, NEG_INF)
    m = jnp.max(s, axis=-1, keepdims=True)
    e = jnp.exp2(s - m)
    p = jnp.where(cmask, e / jnp.sum(e, axis=-1, keepdims=True), 0.0)
    o = jnp.dot(p.astype(BF16), vc, preferred_element_type=F32)

    ph = p[0:tq]
    for h in range(1, NSA_HEADS):
        ph = ph + p[h * tq:(h + 1) * tq]
    groups = [ph[:, r * n_slc:(r + 1) * n_slc] for r in range(per_blk)]
    blk = lax.broadcasted_iota(jnp.int32, (tq, n_slc), 1)
    spill = jnp.where(blk == 0, 0.0, pltpu.roll(groups[per_blk - 1], 1, 1))
    imp = groups[0]
    for r in range(1, per_blk):
        imp = imp + groups[r]
    imp = imp + spill
    tpos = lax.broadcasted_iota(jnp.int32, (tq, n_slc), 0) + t0
    cur = tpos >> (SLC_BLOCK.bit_length() - 1)
    forced = (blk == 0) | (blk == cur) | (blk == cur - 1)
    valid = blk <= cur
    work = jnp.where(forced, BIG, jnp.where(valid, imp, -BIG))
    blk_f = blk.astype(F32)
    sel = jnp.zeros((tq, n_slc), F32)
    for _ in range(n_sel):
        mx = jnp.max(work, axis=-1, keepdims=True)
        first = jnp.min(jnp.where(work == mx, blk_f, float(n_slc)), axis=-1, keepdims=True)
        pick = blk_f == first
        sel = jnp.where(pick, 1.0, sel)
        work = jnp.where(pick, -jnp.inf, work)
    return o, jnp.where((sel > 0.0) & valid, 0.0, NEG_INF)


def _nsa_cmp_kernel(tq, n_sub, n_slc, n_sel, q_ref, kc_ref, vc_ref, o_ref, pen_ref):
    i = pl.program_id(1)
    per_blk = SLC_BLOCK // CMP_STRIDE

    def run(n_j):
        take = lambda ref: jnp.concatenate(
            [ref[r * n_slc:r * n_slc + n_j, :] for r in range(per_blk)], axis=0)
        kc, vc = (kc_ref[...], vc_ref[...]) if n_j == n_slc else (take(kc_ref), take(vc_ref))
        for u in range(n_sub):
            rows = slice(u * tq, (u + 1) * tq)
            o, pen = _nsa_cmp_block(tq, n_j, min(n_sel, n_j), (i * n_sub + u) * tq,
                                    q_ref[rows, :], kc, vc)
            for h in range(NSA_HEADS):
                o_ref[rows, h * LANE:(h + 1) * LANE] = o[h * tq:(h + 1) * tq].astype(o_ref.dtype)
            pen = pen.astype(pen_ref.dtype)
            for c in range(n_slc // LANE):
                if c < n_j // LANE:
                    pen_ref[c, rows, :] = pen[:, c * LANE:(c + 1) * LANE]
                else:
                    pen_ref[c, rows, :] = jnp.full((tq, LANE), NEG_INF, pen_ref.dtype)

    half = n_slc // 2
    if half % LANE == 0:
        early = (i + 1) * (tq * n_sub) <= half * SLC_BLOCK
        pl.when(early)(lambda: run(half))
        pl.when(jnp.logical_not(early))(lambda: run(n_slc))
    else:
        run(n_slc)


def _nsa_compressed(q, k_c, v_c, tq=128, n_sub=4):
    B, S, W = q.shape
    n_cmp = k_c.shape[1]
    n_slc = S // SLC_BLOCK
    n_sel = min(SLC_TOPK, n_slc)
    assert n_slc % LANE == 0 and n_slc & (n_slc - 1) == 0
    assert n_cmp == 4 * n_slc and tq & (tq - 1) == 0
    n_pc = n_slc // LANE
    bq = tq * n_sub
    return pl.pallas_call(
        functools.partial(_nsa_cmp_kernel, tq, n_sub, n_slc, n_sel),
        out_shape=[jax.ShapeDtypeStruct((B, S, W), BF16),
                   jax.ShapeDtypeStruct((B, n_pc, S, LANE), BF16)],
        grid=(B, S // bq),
        in_specs=[pl.BlockSpec((None, bq, W), lambda b, i: (b, i, 0)),
                  pl.BlockSpec((None, n_cmp, HEAD_DIM), lambda b, i: (b, 0, 0)),
                  pl.BlockSpec((None, n_cmp, HEAD_DIM), lambda b, i: (b, 0, 0))],
        out_specs=[pl.BlockSpec((None, bq, W), lambda b, i: (b, i, 0)),
                   pl.BlockSpec((None, n_pc, bq, LANE), lambda b, i: (b, 0, i, 0))],
        compiler_params=_cparams(2), name="nsa_compressed_topk",
    )(q, k_c, v_c)


def _nsa_sel_kernel(tq, tk, n_sub, q_ref, pen_ref, k_ref, v_ref, o_ref, qa_ref, st_ref):
    t0 = pl.program_id(1) * (tq * n_sub)
    R = NSA_HEADS * tq
    last = t0 // tk
    tiles_per_chunk = PEN_CHUNK // tk
    for u in range(n_sub):
        rows = slice(u * tq, (u + 1) * tq)
        q = _stack_heads(q_ref[rows, :], NSA_HEADS)
        for c in range(pen_ref.shape[0]):
            qa_ref[u, c, :, 0:LANE] = q
            qa_ref[u, c, :, LANE:2 * LANE] = jnp.concatenate([pen_ref[c, rows, :]] * NSA_HEADS,
                                                             axis=0)

    def scores(u, t):
        start = pl.multiple_of(t * tk, tk)
        return _dot_t(k_ref[pl.ds(start, tk), :], qa_ref[u, t // tiles_per_chunk])

    def diag_mask(u, st):
        key = lax.broadcasted_iota(jnp.int32, (tk, R), 0) + last * tk
        qry = (lax.broadcasted_iota(jnp.int32, (tk, R), 1) & (tq - 1)) + (t0 + u * tq)
        return jnp.where(key <= qry, st, NEG_INF)

    def values(u, t):
        return v_ref[pl.ds(pl.multiple_of(t * tk, tk), tk), :]

    outs = _flash_transposed(last, n_sub, scores, diag_mask, values, st_ref)
    for u in range(n_sub):
        for h in range(NSA_HEADS):
            o_ref[u * tq:(u + 1) * tq, h * LANE:(h + 1) * LANE] = (
                outs[u][:, h * tq:(h + 1) * tq].T.astype(o_ref.dtype))


def _nsa_selected(q, pen, k_aug, v, tq=128, n_sub=2, tk=1024):
    B, S, W = q.shape
    n_pc = pen.shape[1]
    bq = tq * n_sub
    assert tk % bq == 0 and PEN_CHUNK % tk == 0 and tq & (tq - 1) == 0
    return pl.pallas_call(
        functools.partial(_nsa_sel_kernel, tq, tk, n_sub),
        out_shape=jax.ShapeDtypeStruct((B, S, W), BF16),
        grid=(B, S // bq),
        in_specs=[pl.BlockSpec((None, bq, W), lambda b, i: (b, i, 0)),
                  pl.BlockSpec((None, n_pc, bq, LANE), lambda b, i: (b, 0, i, 0)),
                  pl.BlockSpec((None, S, 2 * HEAD_DIM), lambda b, i: (b, 0, 0)),
                  pl.BlockSpec((None, S, HEAD_DIM), lambda b, i: (b, 0, 0))],
        out_specs=pl.BlockSpec((None, bq, W), lambda b, i: (b, i, 0)),
        scratch_shapes=[pltpu.VMEM((n_sub, n_pc, NSA_HEADS * tq, 2 * HEAD_DIM), BF16),
                        pltpu.VMEM((n_sub, 2, tk, NSA_HEADS * tq), F32)],
        compiler_params=_cparams(2), name="nsa_selected_attention",
    )(q, pen, k_aug, v)


def _mix_out_kernel(oc_ref, os_ref, ow_ref, gate_ref, ob_ref, od_ref, w_ref, h_ref, g_ref,
                    o_ref, a_ref):
    half = a_ref.shape[0] // 2
    for part in range(2):
        rows = slice(part * half, (part + 1) * half)
        gate = jax.nn.sigmoid(gate_ref[rows, :])
        for h in range(NSA_HEADS):
            hs = slice(h * LANE, (h + 1) * LANE)
            a = (gate[:, 3 * h:3 * h + 1] * oc_ref[rows, hs].astype(F32)
                 + gate[:, 3 * h + 1:3 * h + 2] * os_ref[rows, hs].astype(F32)
                 + gate[:, 3 * h + 2:3 * h + 3] * ow_ref[rows, hs].astype(F32))
            a_ref[rows, hs] = a.astype(a_ref.dtype)
        off = NSA_HEADS * LANE
        a_ref[rows, off:off + MLA_HEADS * LANE] = ob_ref[rows, :]
        off += MLA_HEADS * LANE
        a_ref[rows, off:off + DIL_HEADS * LANE] = od_ref[rows, :]
        y = jnp.dot(a_ref[rows, :], w_ref[...], preferred_element_type=F32)
        o_ref[rows, :] = h_ref[rows, :] + _rms(y, g_ref[...])


def _mix_out(o_c, o_s, o_w, gate, o_b, o_d, w, h, g, bm=512):
    T, N = h.shape
    width = (NSA_HEADS + MLA_HEADS + DIL_HEADS) * LANE
    heads = (o_c, o_s, o_w, gate, o_b, o_d)
    row = lambda a: pl.BlockSpec((bm, a.shape[1]), lambda i: (i, 0))
    return pl.pallas_call(
        _mix_out_kernel,
        out_shape=jax.ShapeDtypeStruct((T, N), F32),
        grid=(T // bm,),
        in_specs=[row(a) for a in heads] + [
            pl.BlockSpec((width, N), lambda i: (0, 0), pipeline_mode=pl.Buffered(1)),
            row(h), pl.BlockSpec((1, N), lambda i: (0, 0))],
        out_specs=row(h),
        scratch_shapes=[pltpu.VMEM((bm, width), BF16)],
        compiler_params=_cparams(1), name="mix_out_proj",
    )(*heads, w, h, g)


def _xattn_kernel(n_heads, scale, h_ref, gq_ref, wq_ref, k_ref, v_ref, wo_ref, go_ref, o_ref,
                  a_ref):
    h = h_ref[...]
    x = _rms(h, gq_ref[...]).astype(BF16)
    q = (jnp.dot(x, wq_ref[...], preferred_element_type=F32) * scale).astype(BF16)
    for hd in range(n_heads):
        hs = slice(hd * LANE, (hd + 1) * LANE)
        s = _dot_t(q[:, hs], k_ref[:, hs])
        m = jnp.max(s, axis=-1, keepdims=True)
        p = jnp.exp2(s - m)
        l = jnp.sum(p, axis=-1, keepdims=True)
        o = jnp.dot(p.astype(BF16), v_ref[:, hs], preferred_element_type=F32) / l
        a_ref[:, hs] = o.astype(a_ref.dtype)
    y = jnp.dot(a_ref[...], wo_ref[...], preferred_element_type=F32)
    o_ref[...] = h + _rms(y, go_ref[...])


def _cross_attention(h, gq, wq, k, v, wo, go, n_heads, scale, bm=512):
    B, S, D = h.shape
    M, W = k.shape[1:]
    const = lambda shape: pl.BlockSpec(shape, lambda b, i: (0, 0), pipeline_mode=pl.Buffered(1))
    return pl.pallas_call(
        functools.partial(_xattn_kernel, n_heads, scale),
        out_shape=jax.ShapeDtypeStruct((B, S, D), F32),
        grid=(B, S // bm),
        in_specs=[pl.BlockSpec((None, bm, D), lambda b, i: (b, i, 0)),
                  const((1, D)), const((D, W)),
                  pl.BlockSpec((None, M, W), lambda b, i: (b, 0, 0)),
                  pl.BlockSpec((None, M, W), lambda b, i: (b, 0, 0)),
                  const((W, D)), const((1, D))],
        out_specs=pl.BlockSpec((None, bm, D), lambda b, i: (b, i, 0)),
        scratch_shapes=[pltpu.VMEM((bm, W), BF16)],
        compiler_params=_cparams(2), name="cross_attention",
    )(h, gq, wq, k, v, wo, go)


def _rope_tables(pos, dim):
    inv = ROPE_THETA ** (-jnp.arange(0, dim, 2, dtype=F32) / dim)
    ang = pos.astype(F32)[..., None] * inv
    return jnp.cos(ang), jnp.sin(ang)


def _full_tables(pos):
    c, s = _rope_tables(pos, HEAD_DIM)
    return jnp.concatenate([c, c], -1), jnp.concatenate([-s, s], -1)


def _spread_rope64(t):
    z = jnp.zeros(t.shape[:-1] + (QK_ROPE // 2,), t.dtype)
    return jnp.concatenate([t[..., :QK_ROPE // 2], z, t[..., QK_ROPE // 2:], z], -1)


def _small_tables(pos):
    c, s = _rope_tables(pos, QK_ROPE)
    return _spread_rope64(jnp.concatenate([c, c], -1)), _spread_rope64(jnp.concatenate([-s, s], -1))


def _layer(h, mem, tabs, tabs_c, onehot, p):
    B, S, D = h.shape
    T = B * S
    dh = HEAD_DIM
    scale = dh ** -0.5 * LOG2E
    h2 = h.reshape(T, D)

    w_in = p["w_in"].T
    cuts = np.cumsum([NSA_HEADS * dh, 6 * dh, 3 * NSA_HEADS, Q_LORA, KV_LORA, QK_ROPE])
    w_q, w_kv, w_g, w_cq, w_ckv, w_kr, w_dil = jnp.split(w_in, cuts, axis=0)
    kv = [w_kv[k * dh:(k + 1) * dh] for k in range(6)]
    dw = DIL_HEADS * dh
    w_dq, w_dk, w_dv = w_dil[:dw], w_dil[dw:2 * dw], w_dil[2 * dw:]
    w_gpad = jnp.pad(w_g, ((0, LANE - w_g.shape[0]), (0, 0)))
    w_all = jnp.concatenate([w_q, kv[2], kv[4], w_dq, w_dk, _spread_rope64(w_kr.T).T,
                             kv[0], kv[1], kv[3], kv[5], w_gpad, w_cq, w_ckv, w_dv], 0).astype(BF16)
    one = lambda mode, n=1, s=1.0: ((mode,) * n, s)
    nsa_q, k_s, k_w, dq, dk, k_pe, k_cr, v_cr, v_s, v_w, gate, cq, ckv, dv = _proj(
        h2, p["g_mix_pre"], w_all,
        [one("rope", NSA_HEADS, scale), one("rope"), one("rope"), one("rope", DIL_HEADS, scale),
         one("rope", DIL_HEADS), one("rope_r"), one("none"), one("none"), one("none"), one("none"),
         one("none"), one("none", Q_LORA // LANE), one("none", KV_LORA // LANE),
         one("none", DIL_HEADS)],
        [BF16] * 10 + [F32, BF16, BF16, BF16], tabs=tabs, w_t=True, name="in_proj")

    n_chunk = S // CMP_STRIDE
    half = CMP_LEN // 2
    pe2 = p["cmp_pos_emb"].reshape(2, half * dh)
    cos_c, sin_c = tabs_c

    def compress(t, w1, w2, rope):
        out = _compress(t.reshape(B, n_chunk, CMP_STRIDE * dh), pe2,
                        w1[:half * dh].astype(BF16), w1[half * dh:].astype(BF16),
                        w2.astype(BF16), cos_c, sin_c, rope)
        return out.reshape(B, n_chunk // 4, 4, dh).transpose(0, 2, 1, 3).reshape(B, n_chunk, dh)

    k_c = compress(k_cr, p["w_cmp_k1"], p["w_cmp_k2"], True)
    v_c = compress(v_cr, p["w_cmp_v1"], p["w_cmp_v2"], False)
    q3 = nsa_q.reshape(B, S, NSA_HEADS * dh)
    o_cmp, pen = _nsa_compressed(q3, k_c, v_c)
    k_aug = jnp.concatenate([k_s.reshape(B, S, dh), onehot], axis=-1)
    o_sel = _nsa_selected(q3, pen, k_aug, v_s.reshape(B, S, dh))
    o_win = _nsa_window_attention(q3, k_w.reshape(B, S, dh), v_w.reshape(B, S, dh), NSA_WINDOW)

    dqk = QK_NOPE + QK_ROPE
    w_uq = p["w_uq"].reshape(Q_LORA, MLA_HEADS, dqk)
    w_uq = jnp.concatenate([w_uq[..., :QK_NOPE], _spread_rope64(w_uq[..., QK_NOPE:])], -1)
    w_uq = w_uq.reshape(Q_LORA, MLA_HEADS * 2 * dh).astype(BF16)
    (q_m,) = _proj(cq, p["g_q_lora"], w_uq, [(("none", "rope_r") * MLA_HEADS, dqk ** -0.5 * LOG2E)],
                   [BF16], tabs=tabs, bm=1024, name="mla_q_up")
    w_ukv = p["w_ukv"].reshape(KV_LORA, MLA_HEADS, 2 * dh)
    w_ukv = jnp.concatenate([w_ukv[..., :dh].reshape(KV_LORA, -1),
                             w_ukv[..., dh:].reshape(KV_LORA, -1)], 1).astype(BF16)
    k_m, v_m = _proj(ckv, p["g_kv_lora"], w_ukv,
                     [(("none", "extra") * MLA_HEADS, 1.0), (("none",) * MLA_HEADS, 1.0)],
                     [BF16, BF16], extra=k_pe, bm=1024, name="mla_kv_up")
    o_mla = _mla_attention(q_m.reshape(B, S, -1), k_m.reshape(B, S, -1),
                           v_m.reshape(B, S, -1), MLA_HEADS)

    o_dil = _dilated_attention(dq.reshape(B, S, dw), dk.reshape(B, S, dw), dv.reshape(B, S, dw),
                               DIL_PATTERNS, DIL_HEADS)

    h2 = _mix_out(o_cmp.reshape(T, -1), o_sel.reshape(T, -1), o_win.reshape(T, -1), gate,
                  o_mla.reshape(T, -1), o_dil.reshape(T, dw), p["w_out"].astype(BF16), h2,
                  p["g_mix_post"])

    xw = XATTN_HEADS * dh
    M = mem.shape[1]
    xk, xv = _proj(mem.reshape(B * M, D), p["g_mem_kv"], p["w_xkv"].astype(BF16),
                   [(("none",) * XATTN_HEADS, 1.0), (("none",) * XATTN_HEADS, 1.0)],
                   [BF16, BF16], bm=min(512, B * M), name="xattn_kv")
    h2 = _cross_attention(h2.reshape(B, S, D), p["g_mem_pre"], p["w_xq"].astype(BF16),
                          xk.reshape(B, M, xw), xv.reshape(B, M, xw), p["w_xo"].astype(BF16),
                          p["g_mem_post"], XATTN_HEADS, scale).reshape(T, D)

    up = _mlp_up(h2, p["g_mlp_pre"], p["w_up_all"], p["layer"])
    h2 = _out_proj(up, p["w_down_all"], h2, p["g_mlp_post"], layer=p["layer"], name="mlp_down")
    return h2.reshape(B, S, D)


_LAYER_PARAMS = ("g_mix_pre", "w_in", "cmp_pos_emb", "w_cmp_k1", "w_cmp_k2", "w_cmp_v1", "w_cmp_v2",
                 "g_q_lora", "g_kv_lora", "w_uq", "w_ukv", "w_out", "g_mix_post", "g_mem_pre",
                 "g_mem_kv", "w_xq", "w_xkv", "w_xo", "g_mem_post", "g_mlp_pre", "w_up", "w_down",
                 "g_mlp_post")


def kernel(x, mem, positions, g_mix_pre, w_in, cmp_pos_emb, w_cmp_k1, w_cmp_k2, w_cmp_v1, w_cmp_v2, g_q_lora, g_kv_lora, w_uq, w_ukv, w_out, g_mix_post, g_mem_pre, g_mem_kv, w_xq, w_xkv, w_xo, g_mem_post, g_mlp_pre, w_up, w_down, g_mlp_post):
    stacked = dict(zip(_LAYER_PARAMS, (
        g_mix_pre, w_in, cmp_pos_emb, w_cmp_k1, w_cmp_k2, w_cmp_v1, w_cmp_v2, g_q_lora, g_kv_lora,
        w_uq, w_ukv, w_out, g_mix_post, g_mem_pre, g_mem_kv, w_xq, w_xkv, w_xo, g_mem_post,
        g_mlp_pre, w_up, w_down, g_mlp_post)))
    B, S, D = x.shape
    T = B * S
    assert S % PEN_CHUNK == 0
    cosf, sinf = _full_tables(positions)
    cosr, sinr = _small_tables(positions)
    tabs = tuple(t.reshape(T, LANE) for t in (cosf, sinf, cosr, sinr))
    n_chunk = S // CMP_STRIDE
    end = jnp.minimum(jnp.arange(n_chunk) * CMP_STRIDE + CMP_LEN - 1, S - 1)
    tabs_c = _full_tables(positions[:, end])
    blk = (jnp.arange(S) // SLC_BLOCK) % LANE
    onehot = jnp.broadcast_to((blk[:, None] == jnp.arange(LANE)[None, :]).astype(BF16)[None],
                              (B, S, LANE))
    h = x
    w_up_all, w_down_all = w_up.astype(BF16), w_down.astype(BF16)
    for layer in range(stacked["w_in"].shape[0]):
        p = {"layer": layer, "w_up_all": w_up_all, "w_down_all": w_down_all}
        for name, val in stacked.items():
            if name not in ("w_up", "w_down"):
                v = val[layer]
                p[name] = v[None, :] if name.startswith("g_") else v
        h = _layer(h, mem, tabs, tabs_c, onehot, p)
    return h
```

```python
import functools

import numpy as np
import jax
import jax.numpy as jnp
from jax import lax
from jax.experimental import pallas as pl
from jax.experimental.pallas import tpu as pltpu

F32 = jnp.float32
BF16 = jnp.bfloat16

LANE = 128
VMEM_LIMIT = 56 * 1024 * 1024

HEAD_DIM = 128
ROPE_THETA = 10000.0
NORM_EPS = 1e-6
NEG_INF = -1e30
BIG = 1e9
LOG2E = 1.4426950408889634
NSA_HEADS = 4
MLA_HEADS = 6
DIL_HEADS = 6
CMP_LEN = 32
CMP_STRIDE = 16
SLC_BLOCK = 64
SLC_TOPK = 16
NSA_WINDOW = 512
Q_LORA = 512
KV_LORA = 512
QK_NOPE = 128
QK_ROPE = 64
DIL_PATTERNS = ((128, 1), (512, 4), (2048, 16))
XATTN_HEADS = 4
PEN_CHUNK = LANE * SLC_BLOCK


def _cparams(n_grid):
    return pltpu.CompilerParams(
        dimension_semantics=("arbitrary",) * n_grid, vmem_limit_bytes=VMEM_LIMIT)


def _rms(x, g):
    return x * lax.rsqrt(jnp.mean(x * x, axis=-1, keepdims=True) + NORM_EPS) * g


def _dot_t(a, b):
    return lax.dot_general(a, b, (((1,), (1,)), ((), ())), preferred_element_type=F32)


def _proj_kernel(out_plan, has_norm, n_tab, has_extra, w_t, chunk, *refs):
    x_ref, g_ref, w_ref = refs[:3]
    tab_refs = refs[3:3 + n_tab]
    n_in = 3 + n_tab + int(has_extra)
    out_refs = refs[n_in:]
    x = x_ref[...].astype(F32)
    if has_norm:
        x = _rms(x, g_ref[...])
    xb = x.astype(BF16)
    n_cols = w_ref.shape[0 if w_t else 1]
    flat = []
    for oi, (modes, scale) in enumerate(out_plan):
        for k, mode in enumerate(modes):
            if mode == "extra":
                out_refs[oi][:, k * LANE:(k + 1) * LANE] = refs[n_in - 1][...]
            else:
                flat.append((oi, k * LANE, mode, scale))
    for c0 in range(0, n_cols, chunk):
        c1 = min(c0 + chunk, n_cols)
        if w_t:
            acc = _dot_t(xb, w_ref[c0:c1, :])
        else:
            acc = jnp.dot(xb, w_ref[:, c0:c1], preferred_element_type=F32)
        for s in range((c1 - c0) // LANE):
            oi, off, mode, scale = flat[c0 // LANE + s]
            a = acc[:, s * LANE:(s + 1) * LANE]
            if mode == "rope":
                a = a * tab_refs[0][...] + pltpu.roll(a, LANE // 2, 1) * tab_refs[1][...]
            elif mode == "rope_r":
                a = a * tab_refs[2][...] + pltpu.roll(a, LANE // 2, 1) * tab_refs[3][...]
            if scale != 1.0:
                a = a * scale
            out_refs[oi][:, off:off + LANE] = a.astype(out_refs[oi].dtype)


def _proj(x, g, w, out_plan, out_dtypes, tabs=(), extra=None, w_t=False, bm=512, chunk=512,
          name="proj"):
    T, K = x.shape
    N = w.shape[0 if w_t else 1]
    assert T % bm == 0 and N % LANE == 0
    assert sum(sum(md != "extra" for md in m) for m, _ in out_plan) * LANE == N
    has_norm = g is not None
    if g is None:
        g = jnp.ones((1, K), F32)
    extras = () if extra is None else (extra,)
    in_specs = [pl.BlockSpec((bm, K), lambda i: (i, 0)),
                pl.BlockSpec((1, K), lambda i: (0, 0)),
                pl.BlockSpec(w.shape, lambda i: (0, 0), pipeline_mode=pl.Buffered(1))]
    in_specs += [pl.BlockSpec((bm, LANE), lambda i: (i, 0)) for _ in tabs + extras]
    out_shape = [jax.ShapeDtypeStruct((T, len(m) * LANE), dt)
                 for (m, _), dt in zip(out_plan, out_dtypes)]
    out_specs = [pl.BlockSpec((bm, len(m) * LANE), lambda i: (i, 0)) for m, _ in out_plan]
    return pl.pallas_call(
        functools.partial(_proj_kernel, out_plan, has_norm, len(tabs), extra is not None, w_t,
                          chunk),
        out_shape=out_shape, grid=(T // bm,), in_specs=in_specs, out_specs=out_specs,
        compiler_params=_cparams(1), name=name,
    )(x, g, w, *tabs, *extras)


def _mlp_up_kernel(x_ref, g_ref, w_ref, o_ref, xn_ref):
    @pl.when(pl.program_id(1) == 0)
    def _():
        xn_ref[...] = _rms(x_ref[...], g_ref[...]).astype(BF16)
    a = jnp.dot(xn_ref[...], w_ref[...], preferred_element_type=F32)
    a = jnp.maximum(a, 0.0)
    o_ref[...] = (a * a).astype(o_ref.dtype)


def _mlp_up(x, g, w, layer, bm=1024, bn=2048):
    T, K = x.shape
    N = w.shape[2]
    return pl.pallas_call(
        _mlp_up_kernel,
        out_shape=jax.ShapeDtypeStruct((T, N), BF16),
        grid=(T // bm, N // bn),
        in_specs=[pl.BlockSpec((bm, K), lambda i, j: (i, 0)),
                  pl.BlockSpec((1, K), lambda i, j: (0, 0)),
                  pl.BlockSpec((None, K, bn), lambda i, j: (layer, 0, j))],
        out_specs=pl.BlockSpec((bm, bn), lambda i, j: (i, j)),
        scratch_shapes=[pltpu.VMEM((bm, K), BF16)],
        compiler_params=_cparams(2), name="mlp_up",
    )(x, g, w)


def _out_proj_kernel(n_k, a_ref, w_ref, h_ref, g_ref, o_ref):
    k = pl.program_id(1)

    def part(rows=slice(None)):
        return jnp.dot(a_ref[rows, :], w_ref[...], preferred_element_type=F32)

    def finish(first):
        half = o_ref.shape[0] // 2
        for r in range(2):
            rows = slice(r * half, (r + 1) * half)
            y = part(rows) if first else o_ref[rows, :] + part(rows)
            o_ref[rows, :] = h_ref[rows, :] + _rms(y, g_ref[...])

    if n_k == 1:
        finish(True)
    else:
        @pl.when(k == 0)
        def _():
            o_ref[...] = part()

        @pl.when(jnp.logical_and(k > 0, k < n_k - 1))
        def _():
            o_ref[...] += part()

        @pl.when(k == n_k - 1)
        def _():
            finish(False)


def _out_proj(a, w, h, g, layer, bm=512, bk=2048, name="out_proj"):
    T, K = a.shape
    N = w.shape[2]
    bk = min(bk, K)
    n_k = K // bk
    return pl.pallas_call(
        functools.partial(_out_proj_kernel, n_k),
        out_shape=jax.ShapeDtypeStruct((T, N), F32),
        grid=(T // bm, n_k),
        in_specs=[pl.BlockSpec((bm, bk), lambda i, k: (i, k)),
                  pl.BlockSpec((None, bk, N), lambda i, k: (layer, k, 0)),
                  pl.BlockSpec((bm, N), lambda i, k: (i, 0)),
                  pl.BlockSpec((1, N), lambda i, k: (0, 0))],
        out_specs=pl.BlockSpec((bm, N), lambda i, k: (i, 0)),
        compiler_params=_cparams(2), name=name,
    )(a, w, h, g)


def _online_step(carry, st, st_max, v):
    m, l, acc = carry
    m_new = jnp.maximum(m, st_max)
    alpha = jnp.exp2(m - m_new)
    p = jnp.exp2(st - m_new)
    l = alpha * l + jnp.sum(p, axis=0, keepdims=True)
    pv = lax.dot_general(v, p.astype(BF16), (((0,), (0,)), ((), ())), preferred_element_type=F32)
    return m_new, l, alpha * acc + pv


def _online_init(cols, dv):
    return (jnp.full((1, cols), NEG_INF, F32), jnp.zeros((1, cols), F32),
            jnp.zeros((dv, cols), F32))


def _flash_transposed(n_full, n_groups, scores, diag_mask, values, st_ref):
    groups = range(n_groups)

    def put(t, slot):
        maxes = []
        for g in groups:
            st = scores(g, t)
            st_ref[g, slot] = st
            maxes.append(jnp.max(st, axis=0, keepdims=True))
        return tuple(maxes)

    def half(t, carry, slot):
        stats, st_max = carry
        nxt = put(t + 1, 1 - slot)
        stats = tuple(_online_step(stats[g], st_ref[g, slot], st_max[g], values(g, t))
                      for g in groups)
        return stats, nxt

    def pair(u, carry):
        return half(2 * u + 1, half(2 * u, carry, 0), 1)

    init = tuple(_online_init(st_ref.shape[3], HEAD_DIM) for g in groups)
    carry = lax.fori_loop(0, n_full // 2, pair, (init, put(0, 0)))
    odd = n_full & 1
    stats, _ = lax.fori_loop(0, odd, lambda _, c: half(n_full - 1, c, 0), carry)
    outs = []
    for g in groups:
        st = diag_mask(g, st_ref[g, odd])
        _, l, acc = _online_step(stats[g], st, jnp.max(st, axis=0, keepdims=True),
                                 values(g, n_full))
        outs.append(acc / l)
    return outs


def _mla_kernel(tq, tk, hg, q_ref, k_ref, v_ref, o_ref, st_ref):
    t0 = pl.program_id(2) * tq
    dq = q_ref.shape[1] // hg
    last = t0 // tk

    def scores(g, t):
        start = pl.multiple_of(t * tk, tk)
        cols = slice(g * dq, (g + 1) * dq)
        return _dot_t(k_ref[pl.ds(start, tk), cols], q_ref[:, cols])

    def values(g, t):
        start = pl.multiple_of(t * tk, tk)
        return v_ref[pl.ds(start, tk), g * HEAD_DIM:(g + 1) * HEAD_DIM]

    def diag_mask(g, st):
        key = lax.broadcasted_iota(jnp.int32, (tk, tq), 0) + last * tk
        qry = lax.broadcasted_iota(jnp.int32, (tk, tq), 1) + t0
        return jnp.where(key <= qry, st, NEG_INF)

    outs = _flash_transposed(last, hg, scores, diag_mask, values, st_ref)
    for g in range(hg):
        o_ref[:, g * HEAD_DIM:(g + 1) * HEAD_DIM] = outs[g].T.astype(o_ref.dtype)


def _mla_attention(q, k, v, n_heads, tq=1024, tk=1024, hg=2):
    B, S, _ = q.shape
    dq = q.shape[2] // n_heads
    assert tk % tq == 0 and S % tk == 0 and n_heads % hg == 0
    resident = pl.Buffered(1)
    return pl.pallas_call(
        functools.partial(_mla_kernel, tq, tk, hg),
        out_shape=jax.ShapeDtypeStruct((B, S, n_heads * HEAD_DIM), BF16),
        grid=(B, n_heads // hg, S // tq),
        in_specs=[pl.BlockSpec((None, tq, hg * dq), lambda b, h, i: (b, i, h)),
                  pl.BlockSpec((None, S, hg * dq), lambda b, h, i: (b, 0, h),
                               pipeline_mode=resident),
                  pl.BlockSpec((None, S, hg * HEAD_DIM), lambda b, h, i: (b, 0, h),
                               pipeline_mode=resident)],
        out_specs=pl.BlockSpec((None, tq, hg * HEAD_DIM), lambda b, h, i: (b, i, h)),
        scratch_shapes=[pltpu.VMEM((hg, 2, tk, tq), F32)],
        compiler_params=_cparams(3), name="mla_attention",
    )(q, k, v)


def _band_block(q, kwin, vwin, window, base):
    R = q.shape[0]
    C = window + LANE
    s = _dot_t(q, kwin)
    r = lax.broadcasted_iota(jnp.int32, (R, C), 0) & (LANE - 1)
    c = lax.broadcasted_iota(jnp.int32, (R, C), 1)
    mask = (c >= r) & (c <= r + window) & (c >= window - base)
    s = jnp.where(mask, s, NEG_INF)
    m = jnp.max(s, axis=-1, keepdims=True)
    p = jnp.exp2(s - m)
    l = jnp.sum(p, axis=-1, keepdims=True)
    o = jnp.dot(p.astype(BF16), vwin, preferred_element_type=F32) / l
    return o, m + jnp.log2(l)


def _stack_heads(x, n):
    return jnp.concatenate([x[:, h * LANE:(h + 1) * LANE] for h in range(n)], axis=0)


def _nsa_window_kernel(tq, window, q_ref, kp_ref, kc_ref, vp_ref, vc_ref, o_ref):
    i = pl.program_id(1)
    kwin = jnp.concatenate([kp_ref[...], kc_ref[...]], axis=0)
    vwin = jnp.concatenate([vp_ref[...], vc_ref[...]], axis=0)
    for j in range(tq // LANE):
        q = _stack_heads(q_ref[j * LANE:(j + 1) * LANE, :], NSA_HEADS)
        lo = j * LANE
        o, _ = _band_block(q, kwin[lo:lo + window + LANE], vwin[lo:lo + window + LANE],
                           window, i * tq + lo)
        for h in range(NSA_HEADS):
            o_ref[lo:lo + LANE, h * LANE:(h + 1) * LANE] = (
                o[h * LANE:(h + 1) * LANE].astype(o_ref.dtype))


def _nsa_window_attention(q, k, v, window):
    B, S, W = q.shape
    tq = window
    prev = lambda b, i: (b, jnp.maximum(i - 1, 0), 0)
    cur = lambda b, i: (b, i, 0)
    return pl.pallas_call(
        functools.partial(_nsa_window_kernel, tq, window),
        out_shape=jax.ShapeDtypeStruct((B, S, W), BF16),
        grid=(B, S // tq),
        in_specs=[pl.BlockSpec((None, tq, W), cur),
                  pl.BlockSpec((None, window, HEAD_DIM), prev),
                  pl.BlockSpec((None, tq, HEAD_DIM), cur),
                  pl.BlockSpec((None, window, HEAD_DIM), prev),
                  pl.BlockSpec((None, tq, HEAD_DIM), cur)],
        out_specs=pl.BlockSpec((None, tq, W), cur),
        compiler_params=_cparams(2), name="nsa_window_attention",
    )(q, k, k, v, v)


def _dilated_kernel(nb, patterns, q_ref, kp_ref, kc_ref, vp_ref, vc_ref, o_ref,
                    qf_ref, kf_ref, vf_ref, of_ref, lf_ref):
    i = pl.program_id(2)
    qf_ref[...] = q_ref[...].astype(F32)
    kf_ref[0:nb, :] = kp_ref[...].astype(F32)
    kf_ref[nb:2 * nb, :] = kc_ref[...].astype(F32)
    vf_ref[0:nb, :] = vp_ref[...].astype(F32)
    vf_ref[nb:2 * nb, :] = vc_ref[...].astype(F32)
    for pi, (window, dil) in enumerate(patterns):
        w = window // dil
        per_class = nb // dil
        for r in range(dil):
            for j in range(per_class // LANE):
                q_lo = r + j * LANE * dil
                k_lo = nb + q_lo - w * dil
                rows_q = pl.ds(q_lo, LANE, stride=dil)
                rows_k = pl.ds(k_lo, w + LANE, stride=dil)
                o, lse = _band_block(qf_ref[rows_q, :].astype(BF16),
                                     kf_ref[rows_k, :].astype(BF16),
                                     vf_ref[rows_k, :].astype(BF16),
                                     w, i * per_class + j * LANE)
                of_ref[pi, rows_q, :] = o
                lf_ref[pi, rows_q, :] = jnp.broadcast_to(lse, (LANE, LANE))
    n_pat = len(patterns)
    mx = lf_ref[0]
    for pi in range(1, n_pat):
        mx = jnp.maximum(mx, lf_ref[pi])
    es = [jnp.exp2(lf_ref[pi] - mx) for pi in range(n_pat)]
    den = es[0]
    num = es[0] * of_ref[0]
    for pi in range(1, n_pat):
        den = den + es[pi]
        num = num + es[pi] * of_ref[pi]
    o_ref[...] = (num / den).astype(o_ref.dtype)


def _dilated_attention(q, k, v, patterns, n_heads, nb=2048):
    B, S, W = q.shape
    for window, dil in patterns:
        assert window % dil == 0 and window // dil == LANE
        assert window <= nb and nb % (dil * LANE) == 0
    assert S % nb == 0
    prev = lambda b, h, i: (b, jnp.maximum(i - 1, 0), h)
    cur = lambda b, h, i: (b, i, h)
    blk = lambda index_map: pl.BlockSpec((None, nb, LANE), index_map)
    return pl.pallas_call(
        functools.partial(_dilated_kernel, nb, patterns),
        out_shape=jax.ShapeDtypeStruct((B, S, W), BF16),
        grid=(B, n_heads, S // nb),
        in_specs=[blk(cur), blk(prev), blk(cur), blk(prev), blk(cur)],
        out_specs=blk(cur),
        scratch_shapes=[pltpu.VMEM((nb, LANE), F32),
                        pltpu.VMEM((2 * nb, LANE), F32),
                        pltpu.VMEM((2 * nb, LANE), F32),
                        pltpu.VMEM((len(patterns), nb, LANE), F32),
                        pltpu.VMEM((len(patterns), nb, LANE), F32)],
        compiler_params=_cparams(3), name="dilated_attention",
    )(q, k, k, v, v)


def _gelu_tanh(x):
    return 0.5 * x * (1.0 + jnp.tanh(0.7978845608028654 * (x + 0.044715 * (x * x * x))))


def _compress_kernel(rope, t_ref, pe_ref, w1a_ref, w1b_ref, w2_ref, cos_ref, sin_ref, o_ref):
    t = t_ref[...].astype(F32)
    n = t.shape[0]
    first = jnp.dot((t + pe_ref[0:1, :]).astype(BF16), w1a_ref[...], preferred_element_type=F32)
    second = jnp.dot((t + pe_ref[1:2, :]).astype(BF16), w1b_ref[...], preferred_element_type=F32)
    hid = first + pltpu.roll(second, n - 1, 0)
    out = jnp.dot(_gelu_tanh(hid).astype(BF16), w2_ref[...], preferred_element_type=F32)
    if rope:
        out = out * cos_ref[...] + pltpu.roll(out, LANE // 2, 1) * sin_ref[...]
    o_ref[...] = out.astype(o_ref.dtype)


def _compress(t, pe2, w1a, w1b, w2, cos_c, sin_c, rope):
    B, n, K = t.shape
    whole = lambda b: (0, 0)
    per_b = lambda b: (b, 0, 0)
    return pl.pallas_call(
        functools.partial(_compress_kernel, rope),
        out_shape=jax.ShapeDtypeStruct((B, n, HEAD_DIM), BF16),
        grid=(B,),
        in_specs=[pl.BlockSpec((None, n, K), per_b),
                  pl.BlockSpec((2, K), whole),
                  pl.BlockSpec((K, HEAD_DIM), whole),
                  pl.BlockSpec((K, HEAD_DIM), whole),
                  pl.BlockSpec((HEAD_DIM, HEAD_DIM), whole),
                  pl.BlockSpec((None, n, HEAD_DIM), per_b),
                  pl.BlockSpec((None, n, HEAD_DIM), per_b)],
        out_specs=pl.BlockSpec((None, n, HEAD_DIM), per_b),
        compiler_params=_cparams(1), name="nsa_compress",
    )(t, pe2, w1a, w1b, w2, cos_c, sin_c)


def _nsa_cmp_block(tq, n_slc, n_sel, t0, q, kc, vc):
    n_cmp = kc.shape[0]
    per_blk = SLC_BLOCK // CMP_STRIDE
    slc_shift = n_slc.bit_length() - 1
    q = _stack_heads(q, NSA_HEADS)
    s = _dot_t(q, kc)
    R = NSA_HEADS * tq
    row = lax.broadcasted_iota(jnp.int32, (R, n_cmp), 0) & (tq - 1)
    col = lax.broadcasted_iota(jnp.int32, (R, n_cmp), 1)
    j_of = col & (n_slc - 1)
    r_of = col >> slc_shift
    cmp_end = (per_blk * j_of + r_of) * CMP_STRIDE + (CMP_LEN - 1)
    cmask = cmp_end <= row + t0
    s = jnp.where(cmask, s, NEG_INF)
    m = jnp.max(s, axis=-1, keepdims=True)
    e = jnp.exp2(s - m)
    p = jnp.where(cmask, e / jnp.sum(e, axis=-1, keepdims=True), 0.0)
    o = jnp.dot(p.astype(BF16), vc, preferred_element_type=F32)

    ph = p[0:tq]
    for h in range(1, NSA_HEADS):
        ph = ph + p[h * tq:(h + 1) * tq]
    groups = [ph[:, r * n_slc:(r + 1) * n_slc] for r in range(per_blk)]
    blk = lax.broadcasted_iota(jnp.int32, (tq, n_slc), 1)
    spill = jnp.where(blk == 0, 0.0, pltpu.roll(groups[per_blk - 1], 1, 1))
    imp = groups[0]
    for r in range(1, per_blk):
        imp = imp + groups[r]
    imp = imp + spill
    tpos = lax.broadcasted_iota(jnp.int32, (tq, n_slc), 0) + t0
    cur = tpos >> (SLC_BLOCK.bit_length() - 1)
    forced = (blk == 0) | (blk == cur) | (blk == cur - 1)
    valid = blk <= cur
    work = jnp.where(forced, BIG, jnp.where(valid, imp, -BIG))
    blk_f = blk.astype(F32)
    sel = jnp.zeros((tq, n_slc), F32)
    for _ in range(n_sel):
        mx = jnp.max(work, axis=-1, keepdims=True)
        first = jnp.min(jnp.where(work == mx, blk_f, float(n_slc)), axis=-1, keepdims=True)
        pick = blk_f == first
        sel = jnp.where(pick, 1.0, sel)
        work = jnp.where(pick, -jnp.inf, work)
    return o, jnp.where((sel > 0.0) & valid, 0.0, NEG_INF)


def _nsa_cmp_kernel(tq, n_sub, n_slc, n_sel, q_ref, kc_ref, vc_ref, o_ref, pen_ref):
    i = pl.program_id(1)
    per_blk = SLC_BLOCK // CMP_STRIDE

    def run(n_j):
        take = lambda ref: jnp.concatenate(
            [ref[r * n_slc:r * n_slc + n_j, :] for r in range(per_blk)], axis=0)
        kc, vc = (kc_ref[...], vc_ref[...]) if n_j == n_slc else (take(kc_ref), take(vc_ref))
        for u in range(n_sub):
            rows = slice(u * tq, (u + 1) * tq)
            o, pen = _nsa_cmp_block(tq, n_j, min(n_sel, n_j), (i * n_sub + u) * tq,
                                    q_ref[rows, :], kc, vc)
            for h in range(NSA_HEADS):
                o_ref[rows, h * LANE:(h + 1) * LANE] = o[h * tq:(h + 1) * tq].astype(o_ref.dtype)
            pen = pen.astype(pen_ref.dtype)
            for c in range(n_slc // LANE):
                if c < n_j // LANE:
                    pen_ref[c, rows, :] = pen[:, c * LANE:(c + 1) * LANE]
                else:
                    pen_ref[c, rows, :] = jnp.full((tq, LANE), NEG_INF, pen_ref.dtype)

    half = n_slc // 2
    if half % LANE == 0:
        early = (i + 1) * (tq * n_sub) <= half * SLC_BLOCK
        pl.when(early)(lambda: run(half))
        pl.when(jnp.logical_not(early))(lambda: run(n_slc))
    else:
        run(n_slc)


def _nsa_compressed(q, k_c, v_c, tq=128, n_sub=4):
    B, S, W = q.shape
    n_cmp = k_c.shape[1]
    n_slc = S // SLC_BLOCK
    n_sel = min(SLC_TOPK, n_slc)
    assert n_slc % LANE == 0 and n_slc & (n_slc - 1) == 0
    assert n_cmp == 4 * n_slc and tq & (tq - 1) == 0
    n_pc = n_slc // LANE
    bq = tq * n_sub
    return pl.pallas_call(
        functools.partial(_nsa_cmp_kernel, tq, n_sub, n_slc, n_sel),
        out_shape=[jax.ShapeDtypeStruct((B, S, W), BF16),
                   jax.ShapeDtypeStruct((B, n_pc, S, LANE), BF16)],
        grid=(B, S // bq),
        in_specs=[pl.BlockSpec((None, bq, W), lambda b, i: (b, i, 0)),
                  pl.BlockSpec((None, n_cmp, HEAD_DIM), lambda b, i: (b, 0, 0)),
                  pl.BlockSpec((None, n_cmp, HEAD_DIM), lambda b, i: (b, 0, 0))],
        out_specs=[pl.BlockSpec((None, bq, W), lambda b, i: (b, i, 0)),
                   pl.BlockSpec((None, n_pc, bq, LANE), lambda b, i: (b, 0, i, 0))],
        compiler_params=_cparams(2), name="nsa_compressed_topk",
    )(q, k_c, v_c)


def _nsa_sel_kernel(tq, tk, n_sub, q_ref, pen_ref, k_ref, v_ref, o_ref, qa_ref, st_ref):
    t0 = pl.program_id(1) * (tq * n_sub)
    R = NSA_HEADS * tq
    last = t0 // tk
    tiles_per_chunk = PEN_CHUNK // tk
    for u in range(n_sub):
        rows = slice(u * tq, (u + 1) * tq)
        q = _stack_heads(q_ref[rows, :], NSA_HEADS)
        for c in range(pen_ref.shape[0]):
            qa_ref[u, c, :, 0:LANE] = q
            qa_ref[u, c, :, LANE:2 * LANE] = jnp.concatenate([pen_ref[c, rows, :]] * NSA_HEADS,
                                                             axis=0)

    def scores(u, t):
        start = pl.multiple_of(t * tk, tk)
        return _dot_t(k_ref[pl.ds(start, tk), :], qa_ref[u, t // tiles_per_chunk])

    def diag_mask(u, st):
        key = lax.broadcasted_iota(jnp.int32, (tk, R), 0) + last * tk
        qry = (lax.broadcasted_iota(jnp.int32, (tk, R), 1) & (tq - 1)) + (t0 + u * tq)
        return jnp.where(key <= qry, st, NEG_INF)

    def values(u, t):
        return v_ref[pl.ds(pl.multiple_of(t * tk, tk), tk), :]

    outs = _flash_transposed(last, n_sub, scores, diag_mask, values, st_ref)
    for u in range(n_sub):
        for h in range(NSA_HEADS):
            o_ref[u * tq:(u + 1) * tq, h * LANE:(h + 1) * LANE] = (
                outs[u][:, h * tq:(h + 1) * tq].T.astype(o_ref.dtype))


def _nsa_selected(q, pen, k_aug, v, tq=128, n_sub=2, tk=1024):
    B, S, W = q.shape
    n_pc = pen.shape[1]
    bq = tq * n_sub
    assert tk % bq == 0 and PEN_CHUNK % tk == 0 and tq & (tq - 1) == 0
    return pl.pallas_call(
        functools.partial(_nsa_sel_kernel, tq, tk, n_sub),
        out_shape=jax.ShapeDtypeStruct((B, S, W), BF16),
        grid=(B, S // bq),
        in_specs=[pl.BlockSpec((None, bq, W), lambda b, i: (b, i, 0)),
                  pl.BlockSpec((None, n_pc, bq, LANE), lambda b, i: (b, 0, i, 0)),
                  pl.BlockSpec((None, S, 2 * HEAD_DIM), lambda b, i: (b, 0, 0)),
                  pl.BlockSpec((None, S, HEAD_DIM), lambda b, i: (b, 0, 0))],
        out_specs=pl.BlockSpec((None, bq, W), lambda b, i: (b, i, 0)),
        scratch_shapes=[pltpu.VMEM((n_sub, n_pc, NSA_HEADS * tq, 2 * HEAD_DIM), BF16),
                        pltpu.VMEM((n_sub, 2, tk, NSA_HEADS * tq), F32)],
        compiler_params=_cparams(2), name="nsa_selected_attention",
    )(q, pen, k_aug, v)


def _mix_out_kernel(oc_ref, os_ref, ow_ref, gate_ref, ob_ref, od_ref, w_ref, h_ref, g_ref,
                    o_ref, a_ref):
    half = a_ref.shape[0] // 2
    for part in range(2):
        rows = slice(part * half, (part + 1) * half)
        gate = jax.nn.sigmoid(gate_ref[rows, :])
        for h in range(NSA_HEADS):
            hs = slice(h * LANE, (h + 1) * LANE)
            a = (gate[:, 3 * h:3 * h + 1] * oc_ref[rows, hs].astype(F32)
                 + gate[:, 3 * h + 1:3 * h + 2] * os_ref[rows, hs].astype(F32)
                 + gate[:, 3 * h + 2:3 * h + 3] * ow_ref[rows, hs].astype(F32))
            a_ref[rows, hs] = a.astype(a_ref.dtype)
        off = NSA_HEADS * LANE
        a_ref[rows, off:off + MLA_HEADS * LANE] = ob_ref[rows, :]
        off += MLA_HEADS * LANE
        a_ref[rows, off:off + DIL_HEADS * LANE] = od_ref[rows, :]
        y = jnp.dot(a_ref[rows, :], w_ref[...], preferred_element_type=F32)
        o_ref[rows, :] = h_ref[rows, :] + _rms(y, g_ref[...])


def _mix_out(o_c, o_s, o_w, gate, o_b, o_d, w, h, g, bm=512):
    T, N = h.shape
    width = (NSA_HEADS + MLA_HEADS + DIL_HEADS) * LANE
    heads = (o_c, o_s, o_w, gate, o_b, o_d)
    row = lambda a: pl.BlockSpec((bm, a.shape[1]), lambda i: (i, 0))
    return pl.pallas_call(
        _mix_out_kernel,
        out_shape=jax.ShapeDtypeStruct((T, N), F32),
        grid=(T // bm,),
        in_specs=[row(a) for a in heads] + [
            pl.BlockSpec((width, N), lambda i: (0, 0), pipeline_mode=pl.Buffered(1)),
            row(h), pl.BlockSpec((1, N), lambda i: (0, 0))],
        out_specs=row(h),
        scratch_shapes=[pltpu.VMEM((bm, width), BF16)],
        compiler_params=_cparams(1), name="mix_out_proj",
    )(*heads, w, h, g)


def _xattn_kernel(n_heads, scale, h_ref, gq_ref, wq_ref, k_ref, v_ref, wo_ref, go_ref, o_ref,
                  a_ref):
    h = h_ref[...]
    x = _rms(h, gq_ref[...]).astype(BF16)
    q = (jnp.dot(x, wq_ref[...], preferred_element_type=F32) * scale).astype(BF16)
    for hd in range(n_heads):
        hs = slice(hd * LANE, (hd + 1) * LANE)
        s = _dot_t(q[:, hs], k_ref[:, hs])
        m = jnp.max(s, axis=-1, keepdims=True)
        p = jnp.exp2(s - m)
        l = jnp.sum(p, axis=-1, keepdims=True)
        o = jnp.dot(p.astype(BF16), v_ref[:, hs], preferred_element_type=F32) / l
        a_ref[:, hs] = o.astype(a_ref.dtype)
    y = jnp.dot(a_ref[...], wo_ref[...], preferred_element_type=F32)
    o_ref[...] = h + _rms(y, go_ref[...])


def _cross_attention(h, gq, wq, k, v, wo, go, n_heads, scale, bm=512):
    B, S, D = h.shape
    M, W = k.shape[1:]
    const = lambda shape: pl.BlockSpec(shape, lambda b, i: (0, 0), pipeline_mode=pl.Buffered(1))
    return pl.pallas_call(
        functools.partial(_xattn_kernel, n_heads, scale),
        out_shape=jax.ShapeDtypeStruct((B, S, D), F32),
        grid=(B, S // bm),
        in_specs=[pl.BlockSpec((None, bm, D), lambda b, i: (b, i, 0)),
                  const((1, D)), const((D, W)),
                  pl.BlockSpec((None, M, W), lambda b, i: (b, 0, 0)),
                  pl.BlockSpec((None, M, W), lambda b, i: (b, 0, 0)),
                  const((W, D)), const((1, D))],
        out_specs=pl.BlockSpec((None, bm, D), lambda b, i: (b, i, 0)),
        scratch_shapes=[pltpu.VMEM((bm, W), BF16)],
        compiler_params=_cparams(2), name="cross_attention",
    )(h, gq, wq, k, v, wo, go)


def _rope_tables(pos, dim):
    inv = ROPE_THETA ** (-jnp.arange(0, dim, 2, dtype=F32) / dim)
    ang = pos.astype(F32)[..., None] * inv
    return jnp.cos(ang), jnp.sin(ang)


def _full_tables(pos):
    c, s = _rope_tables(pos, HEAD_DIM)
    return jnp.concatenate([c, c], -1), jnp.concatenate([-s, s], -1)


def _spread_rope64(t):
    z = jnp.zeros(t.shape[:-1] + (QK_ROPE // 2,), t.dtype)
    return jnp.concatenate([t[..., :QK_ROPE // 2], z, t[..., QK_ROPE // 2:], z], -1)


def _small_tables(pos):
    c, s = _rope_tables(pos, QK_ROPE)
    return _spread_rope64(jnp.concatenate([c, c], -1)), _spread_rope64(jnp.concatenate([-s, s], -1))


def _layer(h, mem, tabs, tabs_c, onehot, p):
    B, S, D = h.shape
    T = B * S
    dh = HEAD_DIM
    scale = dh ** -0.5 * LOG2E
    h2 = h.reshape(T, D)

    w_in = p["w_in"].T
    cuts = np.cumsum([NSA_HEADS * dh, 6 * dh, 3 * NSA_HEADS, Q_LORA, KV_LORA, QK_ROPE])
    w_q, w_kv, w_g, w_cq, w_ckv, w_kr, w_dil = jnp.split(w_in, cuts, axis=0)
    kv = [w_kv[k * dh:(k + 1) * dh] for k in range(6)]
    dw = DIL_HEADS * dh
    w_dq, w_dk, w_dv = w_dil[:dw], w_dil[dw:2 * dw], w_dil[2 * dw:]
    w_gpad = jnp.pad(w_g, ((0, LANE - w_g.shape[0]), (0, 0)))
    w_all = jnp.concatenate([w_q, kv[2], kv[4], w_dq, w_dk, _spread_rope64(w_kr.T).T,
                             kv[0], kv[1], kv[3], kv[5], w_gpad, w_cq, w_ckv, w_dv], 0).astype(BF16)
    one = lambda mode, n=1, s=1.0: ((mode,) * n, s)
    nsa_q, k_s, k_w, dq, dk, k_pe, k_cr, v_cr, v_s, v_w, gate, cq, ckv, dv = _proj(
        h2, p["g_mix_pre"], w_all,
        [one("rope", NSA_HEADS, scale), one("rope"), one("rope"), one("rope", DIL_HEADS, scale),
         one("rope", DIL_HEADS), one("rope_r"), one("none"), one("none"), one("none"), one("none"),
         one("none"), one("none", Q_LORA // LANE), one("none", KV_LORA // LANE),
         one("none", DIL_HEADS)],
        [BF16] * 10 + [F32, BF16, BF16, BF16], tabs=tabs, w_t=True, name="in_proj")

    n_chunk = S // CMP_STRIDE
    half = CMP_LEN // 2
    pe2 = p["cmp_pos_emb"].reshape(2, half * dh)
    cos_c, sin_c = tabs_c

    def compress(t, w1, w2, rope):
        out = _compress(t.reshape(B, n_chunk, CMP_STRIDE * dh), pe2,
                        w1[:half * dh].astype(BF16), w1[half * dh:].astype(BF16),
                        w2.astype(BF16), cos_c, sin_c, rope)
        return out.reshape(B, n_chunk // 4, 4, dh).transpose(0, 2, 1, 3).reshape(B, n_chunk, dh)

    k_c = compress(k_cr, p["w_cmp_k1"], p["w_cmp_k2"], True)
    v_c = compress(v_cr, p["w_cmp_v1"], p["w_cmp_v2"], False)
    q3 = nsa_q.reshape(B, S, NSA_HEADS * dh)
    o_cmp, pen = _nsa_compressed(q3, k_c, v_c)
    k_aug = jnp.concatenate([k_s.reshape(B, S, dh), onehot], axis=-1)
    o_sel = _nsa_selected(q3, pen, k_aug, v_s.reshape(B, S, dh))
    o_win = _nsa_window_attention(q3, k_w.reshape(B, S, dh), v_w.reshape(B, S, dh), NSA_WINDOW)

    dqk = QK_NOPE + QK_ROPE
    w_uq = p["w_uq"].reshape(Q_LORA, MLA_HEADS, dqk)
    w_uq = jnp.concatenate([w_uq[..., :QK_NOPE], _spread_rope64(w_uq[..., QK_NOPE:])], -1)
    w_uq = w_uq.reshape(Q_LORA, MLA_HEADS * 2 * dh).astype(BF16)
    (q_m,) = _proj(cq, p["g_q_lora"], w_uq, [(("none", "rope_r") * MLA_HEADS, dqk ** -0.5 * LOG2E)],
                   [BF16], tabs=tabs, bm=1024, name="mla_q_up")
    w_ukv = p["w_ukv"].reshape(KV_LORA, MLA_HEADS, 2 * dh)
    w_ukv = jnp.concatenate([w_ukv[..., :dh].reshape(KV_LORA, -1),
                             w_ukv[..., dh:].reshape(KV_LORA, -1)], 1).astype(BF16)
    k_m, v_m = _proj(ckv, p["g_kv_lora"], w_ukv,
                     [(("none", "extra") * MLA_HEADS, 1.0), (("none",) * MLA_HEADS, 1.0)],
                     [BF16, BF16], extra=k_pe, bm=1024, name="mla_kv_up")
    o_mla = _mla_attention(q_m.reshape(B, S, -1), k_m.reshape(B, S, -1),
                           v_m.reshape(B, S, -1), MLA_HEADS)

    o_dil = _dilated_attention(dq.reshape(B, S, dw), dk.reshape(B, S, dw), dv.reshape(B, S, dw),
                               DIL_PATTERNS, DIL_HEADS)

    h2 = _mix_out(o_cmp.reshape(T, -1), o_sel.reshape(T, -1), o_win.reshape(T, -1), gate,
                  o_mla.reshape(T, -1), o_dil.reshape(T, dw), p["w_out"].astype(BF16), h2,
                  p["g_mix_post"])

    xw = XATTN_HEADS * dh
    M = mem.shape[1]
    xk, xv = _proj(mem.reshape(B * M, D), p["g_mem_kv"], p["w_xkv"].astype(BF16),
                   [(("none",) * XATTN_HEADS, 1.0), (("none",) * XATTN_HEADS, 1.0)],
                   [BF16, BF16], bm=min(512, B * M), name="xattn_kv")
    h2 = _cross_attention(h2.reshape(B, S, D), p["g_mem_pre"], p["w_xq"].astype(BF16),
                          xk.reshape(B, M, xw), xv.reshape(B, M, xw), p["w_xo"].astype(BF16),
                          p["g_mem_post"], XATTN_HEADS, scale).reshape(T, D)

    up = _mlp_up(h2, p["g_mlp_pre"], p["w_up_all"], p["layer"])
    h2 = _out_proj(up, p["w_down_all"], h2, p["g_mlp_post"], layer=p["layer"], bm=1024, bk=1024,
                   name="mlp_down")
    return h2.reshape(B, S, D)


_LAYER_PARAMS = ("g_mix_pre", "w_in", "cmp_pos_emb", "w_cmp_k1", "w_cmp_k2", "w_cmp_v1", "w_cmp_v2",
                 "g_q_lora", "g_kv_lora", "w_uq", "w_ukv", "w_out", "g_mix_post", "g_mem_pre",
                 "g_mem_kv", "w_xq", "w_xkv", "w_xo", "g_mem_post", "g_mlp_pre", "w_up", "w_down",
                 "g_mlp_post")


def kernel(x, mem, positions, g_mix_pre, w_in, cmp_pos_emb, w_cmp_k1, w_cmp_k2, w_cmp_v1, w_cmp_v2, g_q_lora, g_kv_lora, w_uq, w_ukv, w_out, g_mix_post, g_mem_pre, g_mem_kv, w_xq, w_xkv, w_xo, g_mem_post, g_mlp_pre, w_up, w_down, g_mlp_post):
    stacked = dict(zip(_LAYER_PARAMS, (
        g_mix_pre, w_in, cmp_pos_emb, w_cmp_k1, w_cmp_k2, w_cmp_v1, w_cmp_v2, g_q_lora, g_kv_lora,
        w_uq, w_ukv, w_out, g_mix_post, g_mem_pre, g_mem_kv, w_xq, w_xkv, w_xo, g_mem_post,
        g_mlp_pre, w_up, w_down, g_mlp_post)))
    B, S, D = x.shape
    T = B * S
    assert S % PEN_CHUNK == 0
    cosf, sinf = _full_tables(positions)
    cosr, sinr = _small_tables(positions)
    tabs = tuple(t.reshape(T, LANE) for t in (cosf, sinf, cosr, sinr))
    n_chunk = S // CMP_STRIDE
    end = jnp.minimum(jnp.arange(n_chunk) * CMP_STRIDE + CMP_LEN - 1, S - 1)
    tabs_c = _full_tables(positions[:, end])
    blk = (jnp.arange(S) // SLC_BLOCK) % LANE
    onehot = jnp.broadcast_to((blk[:, None] == jnp.arange(LANE)[None, :]).astype(BF16)[None],
                              (B, S, LANE))
    h = x
    w_up_all, w_down_all = w_up.astype(BF16), w_down.astype(BF16)
    for layer in range(stacked["w_in"].shape[0]):
        p = {"layer": layer, "w_up_all": w_up_all, "w_down_all": w_down_all}
        for name, val in stacked.items():
            if name not in ("w_up", "w_down"):
                v = val[layer]
                p[name] = v[None, :] if name.startswith("g_") else v
        h = _layer(h, mem, tabs, tabs_c, onehot, p)
    return h
```

```python
import functools

import numpy as np
import jax
import jax.numpy as jnp
from jax import lax
from jax.experimental import pallas as pl
from jax.experimental.pallas import tpu as pltpu

F32 = jnp.float32
BF16 = jnp.bfloat16

LANE = 128
VMEM_LIMIT = 56 * 1024 * 1024

HEAD_DIM = 128
ROPE_THETA = 10000.0
NORM_EPS = 1e-6
NEG_INF = -1e30
BIG = 1e9
LOG2E = 1.4426950408889634
NSA_HEADS = 4
MLA_HEADS = 6
DIL_HEADS = 6
CMP_LEN = 32
CMP_STRIDE = 16
SLC_BLOCK = 64
SLC_TOPK = 16
NSA_WINDOW = 512
Q_LORA = 512
KV_LORA = 512
QK_NOPE = 128
QK_ROPE = 64
DIL_PATTERNS = ((128, 1), (512, 4), (2048, 16))
XATTN_HEADS = 4
PEN_CHUNK = LANE * SLC_BLOCK


def _cparams(n_grid):
    return pltpu.CompilerParams(
        dimension_semantics=("arbitrary",) * n_grid, vmem_limit_bytes=VMEM_LIMIT)


def _rms(x, g):
    return x * lax.rsqrt(jnp.mean(x * x, axis=-1, keepdims=True) + NORM_EPS) * g


def _dot_t(a, b):
    return lax.dot_general(a, b, (((1,), (1,)), ((), ())), preferred_element_type=F32)


def _proj_kernel(out_plan, has_norm, n_tab, has_extra, w_t, chunk, *refs):
    x_ref, g_ref, w_ref = refs[:3]
    tab_refs = refs[3:3 + n_tab]
    n_in = 3 + n_tab + int(has_extra)
    out_refs = refs[n_in:]
    x = x_ref[...].astype(F32)
    if has_norm:
        x = _rms(x, g_ref[...])
    xb = x.astype(BF16)
    n_cols = w_ref.shape[0 if w_t else 1]
    flat = []
    for oi, (modes, scale) in enumerate(out_plan):
        for k, mode in enumerate(modes):
            if mode == "extra":
                out_refs[oi][:, k * LANE:(k + 1) * LANE] = refs[n_in - 1][...]
            else:
                flat.append((oi, k * LANE, mode, scale))
    for c0 in range(0, n_cols, chunk):
        c1 = min(c0 + chunk, n_cols)
        if w_t:
            acc = _dot_t(xb, w_ref[c0:c1, :])
        else:
            acc = jnp.dot(xb, w_ref[:, c0:c1], preferred_element_type=F32)
        for s in range((c1 - c0) // LANE):
            oi, off, mode, scale = flat[c0 // LANE + s]
            a = acc[:, s * LANE:(s + 1) * LANE]
            if mode == "rope":
                a = a * tab_refs[0][...] + pltpu.roll(a, LANE // 2, 1) * tab_refs[1][...]
            elif mode == "rope_r":
                a = a * tab_refs[2][...] + pltpu.roll(a, LANE // 2, 1) * tab_refs[3][...]
            if scale != 1.0:
                a = a * scale
            out_refs[oi][:, off:off + LANE] = a.astype(out_refs[oi].dtype)


def _proj(x, g, w, out_plan, out_dtypes, tabs=(), extra=None, w_t=False, bm=512, chunk=512,
          name="proj"):
    T, K = x.shape
    N = w.shape[0 if w_t else 1]
    assert T % bm == 0 and N % LANE == 0
    assert sum(sum(md != "extra" for md in m) for m, _ in out_plan) * LANE == N
    has_norm = g is not None
    if g is None:
        g = jnp.ones((1, K), F32)
    extras = () if extra is None else (extra,)
    in_specs = [pl.BlockSpec((bm, K), lambda i: (i, 0)),
                pl.BlockSpec((1, K), lambda i: (0, 0)),
                pl.BlockSpec(w.shape, lambda i: (0, 0), pipeline_mode=pl.Buffered(1))]
    in_specs += [pl.BlockSpec((bm, LANE), lambda i: (i, 0)) for _ in tabs + extras]
    out_shape = [jax.ShapeDtypeStruct((T, len(m) * LANE), dt)
                 for (m, _), dt in zip(out_plan, out_dtypes)]
    out_specs = [pl.BlockSpec((bm, len(m) * LANE), lambda i: (i, 0)) for m, _ in out_plan]
    return pl.pallas_call(
        functools.partial(_proj_kernel, out_plan, has_norm, len(tabs), extra is not None, w_t,
                          chunk),
        out_shape=out_shape, grid=(T // bm,), in_specs=in_specs, out_specs=out_specs,
        compiler_params=_cparams(1), name=name,
    )(x, g, w, *tabs, *extras)


def _mlp_up_kernel(x_ref, g_ref, w_ref, o_ref, xn_ref):
    @pl.when(pl.program_id(1) == 0)
    def _():
        xn_ref[...] = _rms(x_ref[...], g_ref[...]).astype(BF16)
    a = jnp.dot(xn_ref[...], w_ref[...], preferred_element_type=F32)
    a = jnp.maximum(a, 0.0)
    o_ref[...] = (a * a).astype(o_ref.dtype)


def _mlp_up(x, g, w, layer, bm=1024, bn=2048):
    T, K = x.shape
    N = w.shape[2]
    return pl.pallas_call(
        _mlp_up_kernel,
        out_shape=jax.ShapeDtypeStruct((T, N), BF16),
        grid=(T // bm, N // bn),
        in_specs=[pl.BlockSpec((bm, K), lambda i, j: (i, 0)),
                  pl.BlockSpec((1, K), lambda i, j: (0, 0)),
                  pl.BlockSpec((None, K, bn), lambda i, j: (layer, 0, j))],
        out_specs=pl.BlockSpec((bm, bn), lambda i, j: (i, j)),
        scratch_shapes=[pltpu.VMEM((bm, K), BF16)],
        compiler_params=_cparams(2), name="mlp_up",
    )(x, g, w)


def _out_proj_kernel(n_k, a_ref, w_ref, h_ref, g_ref, o_ref):
    k = pl.program_id(1)

    def part(rows=slice(None)):
        return jnp.dot(a_ref[rows, :], w_ref[...], preferred_element_type=F32)

    def finish(first):
        half = o_ref.shape[0] // 2
        for r in range(2):
            rows = slice(r * half, (r + 1) * half)
            y = part(rows) if first else o_ref[rows, :] + part(rows)
            o_ref[rows, :] = h_ref[rows, :] + _rms(y, g_ref[...])

    if n_k == 1:
        finish(True)
    else:
        @pl.when(k == 0)
        def _():
            o_ref[...] = part()

        @pl.when(jnp.logical_and(k > 0, k < n_k - 1))
        def _():
            o_ref[...] += part()

        @pl.when(k == n_k - 1)
        def _():
            finish(False)


def _out_proj(a, w, h, g, layer, bm=512, bk=2048, name="out_proj"):
    T, K = a.shape
    N = w.shape[2]
    bk = min(bk, K)
    n_k = K // bk
    return pl.pallas_call(
        functools.partial(_out_proj_kernel, n_k),
        out_shape=jax.ShapeDtypeStruct((T, N), F32),
        grid=(T // bm, n_k),
        in_specs=[pl.BlockSpec((bm, bk), lambda i, k: (i, k)),
                  pl.BlockSpec((None, bk, N), lambda i, k: (layer, k, 0)),
                  pl.BlockSpec((bm, N), lambda i, k: (i, 0)),
                  pl.BlockSpec((1, N), lambda i, k: (0, 0))],
        out_specs=pl.BlockSpec((bm, N), lambda i, k: (i, 0)),
        compiler_params=_cparams(2), name=name,
    )(a, w, h, g)


def _online_step(carry, st, st_max, v):
    m, l, acc = carry
    m_new = jnp.maximum(m, st_max)
    alpha = jnp.exp2(m - m_new)
    p = jnp.exp2(st - m_new)
    l = alpha * l + jnp.sum(p, axis=0, keepdims=True)
    pv = lax.dot_general(v, p.astype(BF16), (((0,), (0,)), ((), ())), preferred_element_type=F32)
    return m_new, l, alpha * acc + pv


def _online_init(cols, dv):
    return (jnp.full((1, cols), NEG_INF, F32), jnp.zeros((1, cols), F32),
            jnp.zeros((dv, cols), F32))


def _flash_transposed(n_full, n_groups, scores, diag_mask, values, st_ref):
    groups = range(n_groups)

    def put(t, slot, masked=False):
        maxes = []
        for g in groups:
            st = scores(g, t)
            if masked:
                st = diag_mask(g, st)
            st_ref[g, slot] = st
            maxes.append(jnp.max(st, axis=0, keepdims=True))
        return tuple(maxes)

    def tile_at(step):
        return jnp.where(step == 0, n_full, step - 1)

    def softmax_step(step, stats, st_max, slot):
        t = tile_at(step)
        return tuple(_online_step(stats[g], st_ref[g, slot], st_max[g], values(g, t))
                     for g in groups)

    def half(step, carry, slot):
        stats, st_max = carry
        nxt = put(step, 1 - slot)
        return softmax_step(step, stats, st_max, slot), nxt

    def pair(u, carry):
        return half(2 * u + 1, half(2 * u, carry, 0), 1)

    init = tuple(_online_init(st_ref.shape[3], HEAD_DIM) for g in groups)
    carry = lax.fori_loop(0, n_full // 2, pair, (init, put(n_full, 0, masked=True)))
    odd = n_full & 1
    stats, st_max = lax.fori_loop(0, odd, lambda _, c: half(n_full - 1, c, 0), carry)
    stats = softmax_step(n_full, stats, st_max, odd)
    return [acc / l for _, l, acc in stats]


def _mla_kernel(tq, tk, hg, q_ref, k_ref, v_ref, o_ref, st_ref):
    t0 = pl.program_id(2) * tq
    dq = q_ref.shape[1] // hg
    last = t0 // tk

    def scores(g, t):
        start = pl.multiple_of(t * tk, tk)
        cols = slice(g * dq, (g + 1) * dq)
        return _dot_t(k_ref[pl.ds(start, tk), cols], q_ref[:, cols])

    def values(g, t):
        start = pl.multiple_of(t * tk, tk)
        return v_ref[pl.ds(start, tk), g * HEAD_DIM:(g + 1) * HEAD_DIM]

    def diag_mask(g, st):
        key = lax.broadcasted_iota(jnp.int32, (tk, tq), 0) + last * tk
        qry = lax.broadcasted_iota(jnp.int32, (tk, tq), 1) + t0
        return jnp.where(key <= qry, st, NEG_INF)

    outs = _flash_transposed(last, hg, scores, diag_mask, values, st_ref)
    for g in range(hg):
        o_ref[:, g * HEAD_DIM:(g + 1) * HEAD_DIM] = outs[g].T.astype(o_ref.dtype)


def _mla_attention(q, k, v, n_heads, tq=1024, tk=1024, hg=2):
    B, S, _ = q.shape
    dq = q.shape[2] // n_heads
    assert tk % tq == 0 and S % tk == 0 and n_heads % hg == 0
    resident = pl.Buffered(1)
    return pl.pallas_call(
        functools.partial(_mla_kernel, tq, tk, hg),
        out_shape=jax.ShapeDtypeStruct((B, S, n_heads * HEAD_DIM), BF16),
        grid=(B, n_heads // hg, S // tq),
        in_specs=[pl.BlockSpec((None, tq, hg * dq), lambda b, h, i: (b, i, h)),
                  pl.BlockSpec((None, S, hg * dq), lambda b, h, i: (b, 0, h),
                               pipeline_mode=resident),
                  pl.BlockSpec((None, S, hg * HEAD_DIM), lambda b, h, i: (b, 0, h),
                               pipeline_mode=resident)],
        out_specs=pl.BlockSpec((None, tq, hg * HEAD_DIM), lambda b, h, i: (b, i, h)),
        scratch_shapes=[pltpu.VMEM((hg, 2, tk, tq), F32)],
        compiler_params=_cparams(3), name="mla_attention",
    )(q, k, v)


def _band_block(q, kwin, vwin, window, base):
    R = q.shape[0]
    C = window + LANE
    s = _dot_t(q, kwin)
    r = lax.broadcasted_iota(jnp.int32, (R, C), 0) & (LANE - 1)
    c = lax.broadcasted_iota(jnp.int32, (R, C), 1)
    mask = (c >= r) & (c <= r + window) & (c >= window - base)
    s = jnp.where(mask, s, NEG_INF)
    m = jnp.max(s, axis=-1, keepdims=True)
    p = jnp.exp2(s - m)
    l = jnp.sum(p, axis=-1, keepdims=True)
    o = jnp.dot(p.astype(BF16), vwin, preferred_element_type=F32) / l
    return o, m + jnp.log2(l)


def _stack_heads(x, n):
    return jnp.concatenate([x[:, h * LANE:(h + 1) * LANE] for h in range(n)], axis=0)


def _nsa_window_kernel(tq, window, q_ref, kp_ref, kc_ref, vp_ref, vc_ref, o_ref):
    i = pl.program_id(1)
    kwin = jnp.concatenate([kp_ref[...], kc_ref[...]], axis=0)
    vwin = jnp.concatenate([vp_ref[...], vc_ref[...]], axis=0)
    for j in range(tq // LANE):
        q = _stack_heads(q_ref[j * LANE:(j + 1) * LANE, :], NSA_HEADS)
        lo = j * LANE
        o, _ = _band_block(q, kwin[lo:lo + window + LANE], vwin[lo:lo + window + LANE],
                           window, i * tq + lo)
        for h in range(NSA_HEADS):
            o_ref[lo:lo + LANE, h * LANE:(h + 1) * LANE] = (
                o[h * LANE:(h + 1) * LANE].astype(o_ref.dtype))


def _nsa_window_attention(q, k, v, window):
    B, S, W = q.shape
    tq = window
    prev = lambda b, i: (b, jnp.maximum(i - 1, 0), 0)
    cur = lambda b, i: (b, i, 0)
    return pl.pallas_call(
        functools.partial(_nsa_window_kernel, tq, window),
        out_shape=jax.ShapeDtypeStruct((B, S, W), BF16),
        grid=(B, S // tq),
        in_specs=[pl.BlockSpec((None, tq, W), cur),
                  pl.BlockSpec((None, window, HEAD_DIM), prev),
                  pl.BlockSpec((None, tq, HEAD_DIM), cur),
                  pl.BlockSpec((None, window, HEAD_DIM), prev),
                  pl.BlockSpec((None, tq, HEAD_DIM), cur)],
        out_specs=pl.BlockSpec((None, tq, W), cur),
        compiler_params=_cparams(2), name="nsa_window_attention",
    )(q, k, k, v, v)


def _dilated_kernel(nb, patterns, q_ref, kp_ref, kc_ref, vp_ref, vc_ref, o_ref,
                    qf_ref, kf_ref, vf_ref, of_ref, lf_ref):
    i = pl.program_id(2)
    qf_ref[...] = q_ref[...].astype(F32)
    kf_ref[0:nb, :] = kp_ref[...].astype(F32)
    kf_ref[nb:2 * nb, :] = kc_ref[...].astype(F32)
    vf_ref[0:nb, :] = vp_ref[...].astype(F32)
    vf_ref[nb:2 * nb, :] = vc_ref[...].astype(F32)
    for pi, (window, dil) in enumerate(patterns):
        w = window // dil
        per_class = nb // dil
        for r in range(dil):
            for j in range(per_class // LANE):
                q_lo = r + j * LANE * dil
                k_lo = nb + q_lo - w * dil
                rows_q = pl.ds(q_lo, LANE, stride=dil)
                rows_k = pl.ds(k_lo, w + LANE, stride=dil)
                o, lse = _band_block(qf_ref[rows_q, :].astype(BF16),
                                     kf_ref[rows_k, :].astype(BF16),
                                     vf_ref[rows_k, :].astype(BF16),
                                     w, i * per_class + j * LANE)
                of_ref[pi, rows_q, :] = o
                lf_ref[pi, rows_q, :] = jnp.broadcast_to(lse, (LANE, LANE))
    n_pat = len(patterns)
    mx = lf_ref[0]
    for pi in range(1, n_pat):
        mx = jnp.maximum(mx, lf_ref[pi])
    es = [jnp.exp2(lf_ref[pi] - mx) for pi in range(n_pat)]
    den = es[0]
    num = es[0] * of_ref[0]
    for pi in range(1, n_pat):
        den = den + es[pi]
        num = num + es[pi] * of_ref[pi]
    o_ref[...] = (num / den).astype(o_ref.dtype)


def _dilated_attention(q, k, v, patterns, n_heads, nb=2048):
    B, S, W = q.shape
    for window, dil in patterns:
        assert window % dil == 0 and window // dil == LANE
        assert window <= nb and nb % (dil * LANE) == 0
    assert S % nb == 0
    prev = lambda b, h, i: (b, jnp.maximum(i - 1, 0), h)
    cur = lambda b, h, i: (b, i, h)
    blk = lambda index_map: pl.BlockSpec((None, nb, LANE), index_map)
    return pl.pallas_call(
        functools.partial(_dilated_kernel, nb, patterns),
        out_shape=jax.ShapeDtypeStruct((B, S, W), BF16),
        grid=(B, n_heads, S // nb),
        in_specs=[blk(cur), blk(prev), blk(cur), blk(prev), blk(cur)],
        out_specs=blk(cur),
        scratch_shapes=[pltpu.VMEM((nb, LANE), F32),
                        pltpu.VMEM((2 * nb, LANE), F32),
                        pltpu.VMEM((2 * nb, LANE), F32),
                        pltpu.VMEM((len(patterns), nb, LANE), F32),
                        pltpu.VMEM((len(patterns), nb, LANE), F32)],
        compiler_params=_cparams(3), name="dilated_attention",
    )(q, k, k, v, v)


def _gelu_tanh(x):
    return 0.5 * x * (1.0 + jnp.tanh(0.7978845608028654 * (x + 0.044715 * (x * x * x))))


def _compress_kernel(rope, t_ref, pe_ref, w1a_ref, w1b_ref, w2_ref, cos_ref, sin_ref, o_ref):
    t = t_ref[...].astype(F32)
    n = t.shape[0]
    first = jnp.dot((t + pe_ref[0:1, :]).astype(BF16), w1a_ref[...], preferred_element_type=F32)
    second = jnp.dot((t + pe_ref[1:2, :]).astype(BF16), w1b_ref[...], preferred_element_type=F32)
    hid = first + pltpu.roll(second, n - 1, 0)
    out = jnp.dot(_gelu_tanh(hid).astype(BF16), w2_ref[...], preferred_element_type=F32)
    if rope:
        out = out * cos_ref[...] + pltpu.roll(out, LANE // 2, 1) * sin_ref[...]
    o_ref[...] = out.astype(o_ref.dtype)


def _compress(t, pe2, w1a, w1b, w2, cos_c, sin_c, rope):
    B, n, K = t.shape
    whole = lambda b: (0, 0)
    per_b = lambda b: (b, 0, 0)
    return pl.pallas_call(
        functools.partial(_compress_kernel, rope),
        out_shape=jax.ShapeDtypeStruct((B, n, HEAD_DIM), BF16),
        grid=(B,),
        in_specs=[pl.BlockSpec((None, n, K), per_b),
                  pl.BlockSpec((2, K), whole),
                  pl.BlockSpec((K, HEAD_DIM), whole),
                  pl.BlockSpec((K, HEAD_DIM), whole),
                  pl.BlockSpec((HEAD_DIM, HEAD_DIM), whole),
                  pl.BlockSpec((None, n, HEAD_DIM), per_b),
                  pl.BlockSpec((None, n, HEAD_DIM), per_b)],
        out_specs=pl.BlockSpec((None, n, HEAD_DIM), per_b),
        compiler_params=_cparams(1), name="nsa_compress",
    )(t, pe2, w1a, w1b, w2, cos_c, sin_c)


def _nsa_cmp_block(tq, n_slc, n_sel, t0, q, kc, vc):
    n_cmp = kc.shape[0]
    per_blk = SLC_BLOCK // CMP_STRIDE
    slc_shift = n_slc.bit_length() - 1
    q = _stack_heads(q, NSA_HEADS)
    s = _dot_t(q, kc)
    R = NSA_HEADS * tq
    row = lax.broadcasted_iota(jnp.int32, (R, n_cmp), 0) & (tq - 1)
    col = lax.broadcasted_iota(jnp.int32, (R, n_cmp), 1)
    j_of = col & (n_slc - 1)
    r_of = col >> slc_shift
    cmp_end = (per_blk * j_of + r_of) * CMP_STRIDE + (CMP_LEN - 1)
    cmask = cmp_end <= row + t0
    s = jnp.where(cmask, s, NEG_INF)
    m = jnp.max(s, axis=-1, keepdims=True)
    e = jnp.exp2(s - m)
    p = jnp.where(cmask, e / jnp.sum(e, axis=-1, keepdims=True), 0.0)
    o = jnp.dot(p.astype(BF16), vc, preferred_element_type=F32)

    ph = p[0:tq]
    for h in range(1, NSA_HEADS):
        ph = ph + p[h * tq:(h + 1) * tq]
    groups = [ph[:, r * n_slc:(r + 1) * n_slc] for r in range(per_blk)]
    blk = lax.broadcasted_iota(jnp.int32, (tq, n_slc), 1)
    spill = jnp.where(blk == 0, 0.0, pltpu.roll(groups[per_blk - 1], 1, 1))
    imp = groups[0]
    for r in range(1, per_blk):
        imp = imp + groups[r]
    imp = imp + spill
    tpos = lax.broadcasted_iota(jnp.int32, (tq, n_slc), 0) + t0
    cur = tpos >> (SLC_BLOCK.bit_length() - 1)
    forced = (blk == 0) | (blk == cur) | (blk == cur - 1)
    valid = blk <= cur
    work = jnp.where(forced, BIG, jnp.where(valid, imp, -BIG))
    blk_f = blk.astype(F32)
    sel = jnp.zeros((tq, n_slc), F32)
    for _ in range(n_sel):
        mx = jnp.max(work, axis=-1, keepdims=True)
        first = jnp.min(jnp.where(work == mx, blk_f, float(n_slc)), axis=-1, keepdims=True)
        pick = blk_f == first
        sel = jnp.where(pick, 1.0, sel)
        work = jnp.where(pick, -jnp.inf, work)
    return o, jnp.where((sel > 0.0) & valid, 0.0, NEG_INF)


def _nsa_cmp_kernel(tq, n_sub, n_slc, n_sel, q_ref, kc_ref, vc_ref, o_ref, pen_ref):
    i = pl.program_id(1)
    per_blk = SLC_BLOCK // CMP_STRIDE

    def run(n_j):
        take = lambda ref: jnp.concatenate(
            [ref[r * n_slc:r * n_slc + n_j, :] for r in range(per_blk)], axis=0)
        kc, vc = (kc_ref[...], vc_ref[...]) if n_j == n_slc else (take(kc_ref), take(vc_ref))
        for u in range(n_sub):
            rows = slice(u * tq, (u + 1) * tq)
            o, pen = _nsa_cmp_block(tq, n_j, min(n_sel, n_j), (i * n_sub + u) * tq,
                                    q_ref[rows, :], kc, vc)
            for h in range(NSA_HEADS):
                o_ref[rows, h * LANE:(h + 1) * LANE] = o[h * tq:(h + 1) * tq].astype(o_ref.dtype)
            pen = pen.astype(pen_ref.dtype)
            for c in range(n_slc // LANE):
                if c < n_j // LANE:
                    pen_ref[c, rows, :] = pen[:, c * LANE:(c + 1) * LANE]
                else:
                    pen_ref[c, rows, :] = jnp.full((tq, LANE), NEG_INF, pen_ref.dtype)

    half = n_slc // 2
    if half % LANE == 0:
        early = (i + 1) * (tq * n_sub) <= half * SLC_BLOCK
        pl.when(early)(lambda: run(half))
        pl.when(jnp.logical_not(early))(lambda: run(n_slc))
    else:
        run(n_slc)


def _nsa_compressed(q, k_c, v_c, tq=128, n_sub=4):
    B, S, W = q.shape
    n_cmp = k_c.shape[1]
    n_slc = S // SLC_BLOCK
    n_sel = min(SLC_TOPK, n_slc)
    assert n_slc % LANE == 0 and n_slc & (n_slc - 1) == 0
    assert n_cmp == 4 * n_slc and tq & (tq - 1) == 0
    n_pc = n_slc // LANE
    bq = tq * n_sub
    return pl.pallas_call(
        functools.partial(_nsa_cmp_kernel, tq, n_sub, n_slc, n_sel),
        out_shape=[jax.ShapeDtypeStruct((B, S, W), BF16),
                   jax.ShapeDtypeStruct((B, n_pc, S, LANE), BF16)],
        grid=(B, S // bq),
        in_specs=[pl.BlockSpec((None, bq, W), lambda b, i: (b, i, 0)),
                  pl.BlockSpec((None, n_cmp, HEAD_DIM), lambda b, i: (b, 0, 0)),
                  pl.BlockSpec((None, n_cmp, HEAD_DIM), lambda b, i: (b, 0, 0))],
        out_specs=[pl.BlockSpec((None, bq, W), lambda b, i: (b, i, 0)),
                   pl.BlockSpec((None, n_pc, bq, LANE), lambda b, i: (b, 0, i, 0))],
        compiler_params=_cparams(2), name="nsa_compressed_topk",
    )(q, k_c, v_c)


def _nsa_sel_kernel(tq, tk, n_sub, q_ref, pen_ref, k_ref, v_ref, o_ref, qa_ref, st_ref):
    t0 = pl.program_id(1) * (tq * n_sub)
    R = NSA_HEADS * tq
    last = t0 // tk
    tiles_per_chunk = PEN_CHUNK // tk
    for u in range(n_sub):
        rows = slice(u * tq, (u + 1) * tq)
        q = _stack_heads(q_ref[rows, :], NSA_HEADS)
        for c in range(pen_ref.shape[0]):
            qa_ref[u, c, :, 0:LANE] = q
            qa_ref[u, c, :, LANE:2 * LANE] = jnp.concatenate([pen_ref[c, rows, :]] * NSA_HEADS,
                                                             axis=0)

    def scores(u, t):
        start = pl.multiple_of(t * tk, tk)
        return _dot_t(k_ref[pl.ds(start, tk), :], qa_ref[u, t // tiles_per_chunk])

    def diag_mask(u, st):
        key = lax.broadcasted_iota(jnp.int32, (tk, R), 0) + last * tk
        qry = (lax.broadcasted_iota(jnp.int32, (tk, R), 1) & (tq - 1)) + (t0 + u * tq)
        return jnp.where(key <= qry, st, NEG_INF)

    def values(u, t):
        return v_ref[pl.ds(pl.multiple_of(t * tk, tk), tk), :]

    outs = _flash_transposed(last, n_sub, scores, diag_mask, values, st_ref)
    for u in range(n_sub):
        for h in range(NSA_HEADS):
            o_ref[u * tq:(u + 1) * tq, h * LANE:(h + 1) * LANE] = (
                outs[u][:, h * tq:(h + 1) * tq].T.astype(o_ref.dtype))


def _nsa_selected(q, pen, k_aug, v, tq=128, n_sub=2, tk=1024):
    B, S, W = q.shape
    n_pc = pen.shape[1]
    bq = tq * n_sub
    assert tk % bq == 0 and PEN_CHUNK % tk == 0 and tq & (tq - 1) == 0
    return pl.pallas_call(
        functools.partial(_nsa_sel_kernel, tq, tk, n_sub),
        out_shape=jax.ShapeDtypeStruct((B, S, W), BF16),
        grid=(B, S // bq),
        in_specs=[pl.BlockSpec((None, bq, W), lambda b, i: (b, i, 0)),
                  pl.BlockSpec((None, n_pc, bq, LANE), lambda b, i: (b, 0, i, 0)),
                  pl.BlockSpec((None, S, 2 * HEAD_DIM), lambda b, i: (b, 0, 0)),
                  pl.BlockSpec((None, S, HEAD_DIM), lambda b, i: (b, 0, 0))],
        out_specs=pl.BlockSpec((None, bq, W), lambda b, i: (b, i, 0)),
        scratch_shapes=[pltpu.VMEM((n_sub, n_pc, NSA_HEADS * tq, 2 * HEAD_DIM), BF16),
                        pltpu.VMEM((n_sub, 2, tk, NSA_HEADS * tq), F32)],
        compiler_params=_cparams(2), name="nsa_selected_attention",
    )(q, pen, k_aug, v)


def _mix_out_kernel(oc_ref, os_ref, ow_ref, gate_ref, ob_ref, od_ref, w_ref, h_ref, g_ref,
                    o_ref, a_ref):
    half = a_ref.shape[0] // 2
    for part in range(2):
        rows = slice(part * half, (part + 1) * half)
        gate = jax.nn.sigmoid(gate_ref[rows, :])
        for h in range(NSA_HEADS):
            hs = slice(h * LANE, (h + 1) * LANE)
            a = (gate[:, 3 * h:3 * h + 1] * oc_ref[rows, hs].astype(F32)
                 + gate[:, 3 * h + 1:3 * h + 2] * os_ref[rows, hs].astype(F32)
                 + gate[:, 3 * h + 2:3 * h + 3] * ow_ref[rows, hs].astype(F32))
            a_ref[rows, hs] = a.astype(a_ref.dtype)
        off = NSA_HEADS * LANE
        a_ref[rows, off:off + MLA_HEADS * LANE] = ob_ref[rows, :]
        off += MLA_HEADS * LANE
        a_ref[rows, off:off + DIL_HEADS * LANE] = od_ref[rows, :]
        y = jnp.dot(a_ref[rows, :], w_ref[...], preferred_element_type=F32)
        o_ref[rows, :] = h_ref[rows, :] + _rms(y, g_ref[...])


def _mix_out(o_c, o_s, o_w, gate, o_b, o_d, w, h, g, bm=512):
    T, N = h.shape
    width = (NSA_HEADS + MLA_HEADS + DIL_HEADS) * LANE
    heads = (o_c, o_s, o_w, gate, o_b, o_d)
    row = lambda a: pl.BlockSpec((bm, a.shape[1]), lambda i: (i, 0))
    return pl.pallas_call(
        _mix_out_kernel,
        out_shape=jax.ShapeDtypeStruct((T, N), F32),
        grid=(T // bm,),
        in_specs=[row(a) for a in heads] + [
            pl.BlockSpec((width, N), lambda i: (0, 0), pipeline_mode=pl.Buffered(1)),
            row(h), pl.BlockSpec((1, N), lambda i: (0, 0))],
        out_specs=row(h),
        scratch_shapes=[pltpu.VMEM((bm, width), BF16)],
        compiler_params=_cparams(1), name="mix_out_proj",
    )(*heads, w, h, g)


def _xattn_kernel(n_heads, scale, h_ref, gq_ref, wq_ref, k_ref, v_ref, wo_ref, go_ref, o_ref,
                  a_ref):
    h = h_ref[...]
    x = _rms(h, gq_ref[...]).astype(BF16)
    q = (jnp.dot(x, wq_ref[...], preferred_element_type=F32) * scale).astype(BF16)
    for hd in range(n_heads):
        hs = slice(hd * LANE, (hd + 1) * LANE)
        s = _dot_t(q[:, hs], k_ref[:, hs])
        m = jnp.max(s, axis=-1, keepdims=True)
        p = jnp.exp2(s - m)
        l = jnp.sum(p, axis=-1, keepdims=True)
        o = jnp.dot(p.astype(BF16), v_ref[:, hs], preferred_element_type=F32) / l
        a_ref[:, hs] = o.astype(a_ref.dtype)
    y = jnp.dot(a_ref[...], wo_ref[...], preferred_element_type=F32)
    o_ref[...] = h + _rms(y, go_ref[...])


def _cross_attention(h, gq, wq, k, v, wo, go, n_heads, scale, bm=512):
    B, S, D = h.shape
    M, W = k.shape[1:]
    const = lambda shape: pl.BlockSpec(shape, lambda b, i: (0, 0), pipeline_mode=pl.Buffered(1))
    return pl.pallas_call(
        functools.partial(_xattn_kernel, n_heads, scale),
        out_shape=jax.ShapeDtypeStruct((B, S, D), F32),
        grid=(B, S // bm),
        in_specs=[pl.BlockSpec((None, bm, D), lambda b, i: (b, i, 0)),
                  const((1, D)), const((D, W)),
                  pl.BlockSpec((None, M, W), lambda b, i: (b, 0, 0)),
                  pl.BlockSpec((None, M, W), lambda b, i: (b, 0, 0)),
                  const((W, D)), const((1, D))],
        out_specs=pl.BlockSpec((None, bm, D), lambda b, i: (b, i, 0)),
        scratch_shapes=[pltpu.VMEM((bm, W), BF16)],
        compiler_params=_cparams(2), name="cross_attention",
    )(h, gq, wq, k, v, wo, go)


def _rope_tables(pos, dim):
    inv = ROPE_THETA ** (-jnp.arange(0, dim, 2, dtype=F32) / dim)
    ang = pos.astype(F32)[..., None] * inv
    return jnp.cos(ang), jnp.sin(ang)


def _full_tables(c, s):
    return jnp.concatenate([c, c], -1), jnp.concatenate([-s, s], -1)


def _spread_rope64(t):
    z = jnp.zeros(t.shape[:-1] + (QK_ROPE // 2,), t.dtype)
    return jnp.concatenate([t[..., :QK_ROPE // 2], z, t[..., QK_ROPE // 2:], z], -1)


def _small_tables(c, s):
    step = HEAD_DIM // QK_ROPE
    c, s = c[..., ::step], s[..., ::step]
    return _spread_rope64(jnp.concatenate([c, c], -1)), _spread_rope64(jnp.concatenate([-s, s], -1))


def _layer(h, mem, tabs, tabs_c, onehot, p):
    B, S, D = h.shape
    T = B * S
    dh = HEAD_DIM
    scale = dh ** -0.5 * LOG2E
    h2 = h.reshape(T, D)

    w_in = p["w_in"].T
    cuts = np.cumsum([NSA_HEADS * dh, 6 * dh, 3 * NSA_HEADS, Q_LORA, KV_LORA, QK_ROPE])
    w_q, w_kv, w_g, w_cq, w_ckv, w_kr, w_dil = jnp.split(w_in, cuts, axis=0)
    kv = [w_kv[k * dh:(k + 1) * dh] for k in range(6)]
    dw = DIL_HEADS * dh
    w_dq, w_dk, w_dv = w_dil[:dw], w_dil[dw:2 * dw], w_dil[2 * dw:]
    w_gpad = jnp.pad(w_g, ((0, LANE - w_g.shape[0]), (0, 0)))
    w_all = jnp.concatenate([w_q, kv[2], kv[4], w_dq, w_dk, _spread_rope64(w_kr.T).T,
                             kv[0], kv[1], kv[3], kv[5], w_gpad, w_cq, w_ckv, w_dv], 0).astype(BF16)
    one = lambda mode, n=1, s=1.0: ((mode,) * n, s)
    nsa_q, k_s, k_w, dq, dk, k_pe, k_cr, v_cr, v_s, v_w, gate, cq, ckv, dv = _proj(
        h2, p["g_mix_pre"], w_all,
        [one("rope", NSA_HEADS, scale), one("rope"), one("rope"), one("rope", DIL_HEADS, scale),
         one("rope", DIL_HEADS), one("rope_r"), one("none"), one("none"), one("none"), one("none"),
         one("none"), one("none", Q_LORA // LANE), one("none", KV_LORA // LANE),
         one("none", DIL_HEADS)],
        [BF16] * 10 + [F32, BF16, BF16, BF16], tabs=tabs, w_t=True, name="in_proj")

    n_chunk = S // CMP_STRIDE
    half = CMP_LEN // 2
    pe2 = p["cmp_pos_emb"].reshape(2, half * dh)
    cos_c, sin_c = tabs_c

    def compress(t, w1, w2, rope):
        out = _compress(t.reshape(B, n_chunk, CMP_STRIDE * dh), pe2,
                        w1[:half * dh].astype(BF16), w1[half * dh:].astype(BF16),
                        w2.astype(BF16), cos_c, sin_c, rope)
        return out.reshape(B, n_chunk // 4, 4, dh).transpose(0, 2, 1, 3).reshape(B, n_chunk, dh)

    k_c = compress(k_cr, p["w_cmp_k1"], p["w_cmp_k2"], True)
    v_c = compress(v_cr, p["w_cmp_v1"], p["w_cmp_v2"], False)
    q3 = nsa_q.reshape(B, S, NSA_HEADS * dh)
    o_cmp, pen = _nsa_compressed(q3, k_c, v_c)
    k_aug = jnp.concatenate([k_s.reshape(B, S, dh), onehot], axis=-1)
    o_sel = _nsa_selected(q3, pen, k_aug, v_s.reshape(B, S, dh))
    o_win = _nsa_window_attention(q3, k_w.reshape(B, S, dh), v_w.reshape(B, S, dh), NSA_WINDOW)

    dqk = QK_NOPE + QK_ROPE
    w_uq = p["w_uq"].reshape(Q_LORA, MLA_HEADS, dqk)
    w_uq = jnp.concatenate([w_uq[..., :QK_NOPE], _spread_rope64(w_uq[..., QK_NOPE:])], -1)
    w_uq = w_uq.reshape(Q_LORA, MLA_HEADS * 2 * dh).astype(BF16)
    (q_m,) = _proj(cq, p["g_q_lora"], w_uq, [(("none", "rope_r") * MLA_HEADS, dqk ** -0.5 * LOG2E)],
                   [BF16], tabs=tabs, bm=1024, name="mla_q_up")
    w_ukv = p["w_ukv"].reshape(KV_LORA, MLA_HEADS, 2 * dh)
    w_ukv = jnp.concatenate([w_ukv[..., :dh].reshape(KV_LORA, -1),
                             w_ukv[..., dh:].reshape(KV_LORA, -1)], 1).astype(BF16)
    k_m, v_m = _proj(ckv, p["g_kv_lora"], w_ukv,
                     [(("none", "extra") * MLA_HEADS, 1.0), (("none",) * MLA_HEADS, 1.0)],
                     [BF16, BF16], extra=k_pe, bm=1024, name="mla_kv_up")
    o_mla = _mla_attention(q_m.reshape(B, S, -1), k_m.reshape(B, S, -1),
                           v_m.reshape(B, S, -1), MLA_HEADS)

    o_dil = _dilated_attention(dq.reshape(B, S, dw), dk.reshape(B, S, dw), dv.reshape(B, S, dw),
                               DIL_PATTERNS, DIL_HEADS)

    h2 = _mix_out(o_cmp.reshape(T, -1), o_sel.reshape(T, -1), o_win.reshape(T, -1), gate,
                  o_mla.reshape(T, -1), o_dil.reshape(T, dw), p["w_out"].astype(BF16), h2,
                  p["g_mix_post"])

    xw = XATTN_HEADS * dh
    M = mem.shape[1]
    xk, xv = _proj(mem.reshape(B * M, D), p["g_mem_kv"], p["w_xkv"].astype(BF16),
                   [(("none",) * XATTN_HEADS, 1.0), (("none",) * XATTN_HEADS, 1.0)],
                   [BF16, BF16], bm=min(512, B * M), name="xattn_kv")
    h2 = _cross_attention(h2.reshape(B, S, D), p["g_mem_pre"], p["w_xq"].astype(BF16),
                          xk.reshape(B, M, xw), xv.reshape(B, M, xw), p["w_xo"].astype(BF16),
                          p["g_mem_post"], XATTN_HEADS, scale).reshape(T, D)

    up = _mlp_up(h2, p["g_mlp_pre"], p["w_up_all"], p["layer"])
    h2 = _out_proj(up, p["w_down_all"], h2, p["g_mlp_post"], layer=p["layer"], bm=1024, bk=1024,
                   name="mlp_down")
    return h2.reshape(B, S, D)


_LAYER_PARAMS = ("g_mix_pre", "w_in", "cmp_pos_emb", "w_cmp_k1", "w_cmp_k2", "w_cmp_v1", "w_cmp_v2",
                 "g_q_lora", "g_kv_lora", "w_uq", "w_ukv", "w_out", "g_mix_post", "g_mem_pre",
                 "g_mem_kv", "w_xq", "w_xkv", "w_xo", "g_mem_post", "g_mlp_pre", "w_up", "w_down",
                 "g_mlp_post")


def kernel(x, mem, positions, g_mix_pre, w_in, cmp_pos_emb, w_cmp_k1, w_cmp_k2, w_cmp_v1, w_cmp_v2, g_q_lora, g_kv_lora, w_uq, w_ukv, w_out, g_mix_post, g_mem_pre, g_mem_kv, w_xq, w_xkv, w_xo, g_mem_post, g_mlp_pre, w_up, w_down, g_mlp_post):
    stacked = dict(zip(_LAYER_PARAMS, (
        g_mix_pre, w_in, cmp_pos_emb, w_cmp_k1, w_cmp_k2, w_cmp_v1, w_cmp_v2, g_q_lora, g_kv_lora,
        w_uq, w_ukv, w_out, g_mix_post, g_mem_pre, g_mem_kv, w_xq, w_xkv, w_xo, g_mem_post,
        g_mlp_pre, w_up, w_down, g_mlp_post)))
    B, S, D = x.shape
    T = B * S
    assert S % PEN_CHUNK == 0
    cos, sin = _rope_tables(positions, HEAD_DIM)
    cosf, sinf = _full_tables(cos, sin)
    cosr, sinr = _small_tables(cos, sin)
    tabs = tuple(t.reshape(T, LANE) for t in (cosf, sinf, cosr, sinr))
    n_chunk = S // CMP_STRIDE
    end = np.minimum(np.arange(n_chunk) * CMP_STRIDE + CMP_LEN - 1, S - 1)
    tabs_c = (cosf[:, end], sinf[:, end])
    blk = (jnp.arange(S) // SLC_BLOCK) % LANE
    onehot = jnp.broadcast_to((blk[:, None] == jnp.arange(LANE)[None, :]).astype(BF16)[None],
                              (B, S, LANE))
    h = x
    w_up_all, w_down_all = w_up.astype(BF16), w_down.astype(BF16)
    for layer in range(stacked["w_in"].shape[0]):
        p = {"layer": layer, "w_up_all": w_up_all, "w_down_all": w_down_all}
        for name, val in stacked.items():
            if name not in ("w_up", "w_down"):
                v = val[layer]
                p[name] = v[None, :] if name.startswith("g_") else v
        h = _layer(h, mem, tabs, tabs_c, onehot, p)
    return h
```

```python
import functools

import numpy as np
import jax
import jax.numpy as jnp
from jax import lax
from jax.experimental import pallas as pl
from jax.experimental.pallas import tpu as pltpu

F32 = jnp.float32
BF16 = jnp.bfloat16

LANE = 128
VMEM_LIMIT = 56 * 1024 * 1024

HEAD_DIM = 128
ROPE_THETA = 10000.0
NORM_EPS = 1e-6
NEG_INF = -1e30
BIG = 1e9
LOG2E = 1.4426950408889634
NSA_HEADS = 4
MLA_HEADS = 6
DIL_HEADS = 6
CMP_LEN = 32
CMP_STRIDE = 16
SLC_BLOCK = 64
SLC_TOPK = 16
NSA_WINDOW = 512
Q_LORA = 512
KV_LORA = 512
QK_NOPE = 128
QK_ROPE = 64
DIL_PATTERNS = ((128, 1), (512, 4), (2048, 16))
XATTN_HEADS = 4
PEN_CHUNK = LANE * SLC_BLOCK


def _cparams(n_grid):
    return pltpu.CompilerParams(
        dimension_semantics=("arbitrary",) * n_grid, vmem_limit_bytes=VMEM_LIMIT)


def _rms(x, g):
    return x * lax.rsqrt(jnp.mean(x * x, axis=-1, keepdims=True) + NORM_EPS) * g


def _dot_t(a, b):
    return lax.dot_general(a, b, (((1,), (1,)), ((), ())), preferred_element_type=F32)


def _proj_kernel(out_plan, has_norm, n_tab, has_extra, w_t, chunk, *refs):
    x_ref, g_ref, w_ref = refs[:3]
    tab_refs = refs[3:3 + n_tab]
    n_in = 3 + n_tab + int(has_extra)
    out_refs = refs[n_in:]
    x = x_ref[...].astype(F32)
    if has_norm:
        x = _rms(x, g_ref[...])
    xb = x.astype(BF16)
    n_cols = w_ref.shape[0 if w_t else 1]
    flat = []
    for oi, (modes, scale) in enumerate(out_plan):
        for k, mode in enumerate(modes):
            if mode == "extra":
                out_refs[oi][:, k * LANE:(k + 1) * LANE] = refs[n_in - 1][...]
            else:
                flat.append((oi, k * LANE, mode, scale))
    for c0 in range(0, n_cols, chunk):
        c1 = min(c0 + chunk, n_cols)
        if w_t:
            acc = _dot_t(xb, w_ref[c0:c1, :])
        else:
            acc = jnp.dot(xb, w_ref[:, c0:c1], preferred_element_type=F32)
        for s in range((c1 - c0) // LANE):
            oi, off, mode, scale = flat[c0 // LANE + s]
            a = acc[:, s * LANE:(s + 1) * LANE]
            if mode == "rope":
                a = a * tab_refs[0][...] + pltpu.roll(a, LANE // 2, 1) * tab_refs[1][...]
            elif mode == "rope_r":
                a = a * tab_refs[2][...] + pltpu.roll(a, LANE // 2, 1) * tab_refs[3][...]
            if scale != 1.0:
                a = a * scale
            out_refs[oi][:, off:off + LANE] = a.astype(out_refs[oi].dtype)


def _proj(x, g, w, out_plan, out_dtypes, tabs=(), extra=None, w_t=False, bm=512, chunk=512,
          name="proj"):
    T, K = x.shape
    N = w.shape[0 if w_t else 1]
    assert T % bm == 0 and N % LANE == 0
    assert sum(sum(md != "extra" for md in m) for m, _ in out_plan) * LANE == N
    has_norm = g is not None
    if g is None:
        g = jnp.ones((1, K), F32)
    extras = () if extra is None else (extra,)
    in_specs = [pl.BlockSpec((bm, K), lambda i: (i, 0)),
                pl.BlockSpec((1, K), lambda i: (0, 0)),
                pl.BlockSpec(w.shape, lambda i: (0, 0), pipeline_mode=pl.Buffered(1))]
    in_specs += [pl.BlockSpec((bm, LANE), lambda i: (i, 0)) for _ in tabs + extras]
    out_shape = [jax.ShapeDtypeStruct((T, len(m) * LANE), dt)
                 for (m, _), dt in zip(out_plan, out_dtypes)]
    out_specs = [pl.BlockSpec((bm, len(m) * LANE), lambda i: (i, 0)) for m, _ in out_plan]
    return pl.pallas_call(
        functools.partial(_proj_kernel, out_plan, has_norm, len(tabs), extra is not None, w_t,
                          chunk),
        out_shape=out_shape, grid=(T // bm,), in_specs=in_specs, out_specs=out_specs,
        compiler_params=_cparams(1), name=name,
    )(x, g, w, *tabs, *extras)


def _mlp_up_kernel(x_ref, g_ref, w_ref, o_ref, xn_ref):
    @pl.when(pl.program_id(1) == 0)
    def _():
        xn_ref[...] = _rms(x_ref[...], g_ref[...]).astype(BF16)
    a = jnp.dot(xn_ref[...], w_ref[...], preferred_element_type=F32)
    a = jnp.maximum(a, 0.0)
    o_ref[...] = (a * a).astype(o_ref.dtype)


def _mlp_up(x, g, w, layer, bm=1024, bn=2048):
    T, K = x.shape
    N = w.shape[2]
    return pl.pallas_call(
        _mlp_up_kernel,
        out_shape=jax.ShapeDtypeStruct((T, N), BF16),
        grid=(T // bm, N // bn),
        in_specs=[pl.BlockSpec((bm, K), lambda i, j: (i, 0)),
                  pl.BlockSpec((1, K), lambda i, j: (0, 0)),
                  pl.BlockSpec((None, K, bn), lambda i, j: (layer, 0, j))],
        out_specs=pl.BlockSpec((bm, bn), lambda i, j: (i, j)),
        scratch_shapes=[pltpu.VMEM((bm, K), BF16)],
        compiler_params=_cparams(2), name="mlp_up",
    )(x, g, w)


def _out_proj_kernel(n_k, a_ref, w_ref, h_ref, g_ref, o_ref):
    k = pl.program_id(1)

    def part(rows=slice(None)):
        return jnp.dot(a_ref[rows, :], w_ref[...], preferred_element_type=F32)

    def finish(first):
        half = o_ref.shape[0] // 2
        for r in range(2):
            rows = slice(r * half, (r + 1) * half)
            y = part(rows) if first else o_ref[rows, :] + part(rows)
            o_ref[rows, :] = h_ref[rows, :] + _rms(y, g_ref[...])

    if n_k == 1:
        finish(True)
    else:
        @pl.when(k == 0)
        def _():
            o_ref[...] = part()

        @pl.when(jnp.logical_and(k > 0, k < n_k - 1))
        def _():
            o_ref[...] += part()

        @pl.when(k == n_k - 1)
        def _():
            finish(False)


def _out_proj(a, w, h, g, layer, bm=512, bk=2048, name="out_proj"):
    T, K = a.shape
    N = w.shape[2]
    bk = min(bk, K)
    n_k = K // bk
    return pl.pallas_call(
        functools.partial(_out_proj_kernel, n_k),
        out_shape=jax.ShapeDtypeStruct((T, N), F32),
        grid=(T // bm, n_k),
        in_specs=[pl.BlockSpec((bm, bk), lambda i, k: (i, k)),
                  pl.BlockSpec((None, bk, N), lambda i, k: (layer, k, 0)),
                  pl.BlockSpec((bm, N), lambda i, k: (i, 0)),
                  pl.BlockSpec((1, N), lambda i, k: (0, 0))],
        out_specs=pl.BlockSpec((bm, N), lambda i, k: (i, 0)),
        compiler_params=_cparams(2), name=name,
    )(a, w, h, g)


def _online_step(carry, st, st_max, v):
    m, l, acc = carry
    m_new = jnp.maximum(m, st_max)
    alpha = jnp.exp2(m - m_new)
    p = jnp.exp2(st - m_new)
    l = alpha * l + jnp.sum(p, axis=0, keepdims=True)
    pv = lax.dot_general(v, p.astype(BF16), (((0,), (0,)), ((), ())), preferred_element_type=F32)
    return m_new, l, alpha * acc + pv


def _online_init(cols, dv):
    return (jnp.full((1, cols), NEG_INF, F32), jnp.zeros((1, cols), F32),
            jnp.zeros((dv, cols), F32))


def _flash_transposed(n_full, n_groups, scores, diag_mask, values, st_ref):
    groups = range(n_groups)

    def put(t, slot, masked=False):
        maxes = []
        for g in groups:
            st = scores(g, t)
            if masked:
                st = diag_mask(g, st)
            st_ref[g, slot] = st
            maxes.append(jnp.max(st, axis=0, keepdims=True))
        return tuple(maxes)

    def tile_at(step):
        return jnp.where(step == 0, n_full, step - 1)

    def softmax_step(step, stats, st_max, slot):
        t = tile_at(step)
        return tuple(_online_step(stats[g], st_ref[g, slot], st_max[g], values(g, t))
                     for g in groups)

    def half(step, carry, slot):
        stats, st_max = carry
        nxt = put(step, 1 - slot)
        return softmax_step(step, stats, st_max, slot), nxt

    def pair(u, carry):
        return half(2 * u + 1, half(2 * u, carry, 0), 1)

    init = tuple(_online_init(st_ref.shape[3], HEAD_DIM) for g in groups)
    carry = lax.fori_loop(0, n_full // 2, pair, (init, put(n_full, 0, masked=True)))
    odd = n_full & 1
    stats, st_max = lax.fori_loop(0, odd, lambda _, c: half(n_full - 1, c, 0), carry)
    stats = softmax_step(n_full, stats, st_max, odd)
    return [acc / l for _, l, acc in stats]


def _mla_kernel(tq, tk, hg, q_ref, k_ref, v_ref, o_ref, st_ref):
    t0 = pl.program_id(2) * tq
    dq = q_ref.shape[1] // hg
    last = t0 // tk

    def scores(g, t):
        start = pl.multiple_of(t * tk, tk)
        cols = slice(g * dq, (g + 1) * dq)
        return _dot_t(k_ref[pl.ds(start, tk), cols], q_ref[:, cols])

    def values(g, t):
        start = pl.multiple_of(t * tk, tk)
        return v_ref[pl.ds(start, tk), g * HEAD_DIM:(g + 1) * HEAD_DIM]

    def diag_mask(g, st):
        key = lax.broadcasted_iota(jnp.int32, (tk, tq), 0) + last * tk
        qry = lax.broadcasted_iota(jnp.int32, (tk, tq), 1) + t0
        return jnp.where(key <= qry, st, NEG_INF)

    outs = _flash_transposed(last, hg, scores, diag_mask, values, st_ref)
    for g in range(hg):
        o_ref[:, g * HEAD_DIM:(g + 1) * HEAD_DIM] = outs[g].T.astype(o_ref.dtype)


def _mla_attention(q, k, v, n_heads, tq=1024, tk=1024, hg=2):
    B, S, _ = q.shape
    dq = q.shape[2] // n_heads
    assert tk % tq == 0 and S % tk == 0 and n_heads % hg == 0
    resident = pl.Buffered(1)
    return pl.pallas_call(
        functools.partial(_mla_kernel, tq, tk, hg),
        out_shape=jax.ShapeDtypeStruct((B, S, n_heads * HEAD_DIM), BF16),
        grid=(B, n_heads // hg, S // tq),
        in_specs=[pl.BlockSpec((None, tq, hg * dq), lambda b, h, i: (b, i, h)),
                  pl.BlockSpec((None, S, hg * dq), lambda b, h, i: (b, 0, h),
                               pipeline_mode=resident),
                  pl.BlockSpec((None, S, hg * HEAD_DIM), lambda b, h, i: (b, 0, h),
                               pipeline_mode=resident)],
        out_specs=pl.BlockSpec((None, tq, hg * HEAD_DIM), lambda b, h, i: (b, i, h)),
        scratch_shapes=[pltpu.VMEM((hg, 2, tk, tq), F32)],
        compiler_params=_cparams(3), name="mla_attention",
    )(q, k, v)


def _band_bias(R, window, base=None):
    C = window + LANE
    r = lax.broadcasted_iota(jnp.int32, (R, C), 0) & (LANE - 1)
    c = lax.broadcasted_iota(jnp.int32, (R, C), 1)
    mask = (c >= r) & (c <= r + window)
    if base is not None:
        mask = mask & (c >= window - base)
    return jnp.where(mask, 0.0, NEG_INF)


def _band_block(q, kwin, vwin, bias):
    s = _dot_t(q, kwin) + bias
    m = jnp.max(s, axis=-1, keepdims=True)
    p = jnp.exp2(s - m)
    l = jnp.sum(p, axis=-1, keepdims=True)
    o = jnp.dot(p.astype(BF16), vwin, preferred_element_type=F32) / l
    return o, m + jnp.log2(l)


def _stack_heads(x, n):
    return jnp.concatenate([x[:, h * LANE:(h + 1) * LANE] for h in range(n)], axis=0)


def _nsa_window_kernel(tq, window, q_ref, kp_ref, kc_ref, vp_ref, vc_ref, o_ref):
    i = pl.program_id(1)
    kwin = jnp.concatenate([kp_ref[...], kc_ref[...]], axis=0)
    vwin = jnp.concatenate([vp_ref[...], vc_ref[...]], axis=0)
    for j in range(tq // LANE):
        q = _stack_heads(q_ref[j * LANE:(j + 1) * LANE, :], NSA_HEADS)
        lo = j * LANE
        o, _ = _band_block(q, kwin[lo:lo + window + LANE], vwin[lo:lo + window + LANE],
                           _band_bias(q.shape[0], window, i * tq + lo))
        for h in range(NSA_HEADS):
            o_ref[lo:lo + LANE, h * LANE:(h + 1) * LANE] = (
                o[h * LANE:(h + 1) * LANE].astype(o_ref.dtype))


def _nsa_window_attention(q, k, v, window):
    B, S, W = q.shape
    tq = window
    prev = lambda b, i: (b, jnp.maximum(i - 1, 0), 0)
    cur = lambda b, i: (b, i, 0)
    return pl.pallas_call(
        functools.partial(_nsa_window_kernel, tq, window),
        out_shape=jax.ShapeDtypeStruct((B, S, W), BF16),
        grid=(B, S // tq),
        in_specs=[pl.BlockSpec((None, tq, W), cur),
                  pl.BlockSpec((None, window, HEAD_DIM), prev),
                  pl.BlockSpec((None, tq, HEAD_DIM), cur),
                  pl.BlockSpec((None, window, HEAD_DIM), prev),
                  pl.BlockSpec((None, tq, HEAD_DIM), cur)],
        out_specs=pl.BlockSpec((None, tq, W), cur),
        compiler_params=_cparams(2), name="nsa_window_attention",
    )(q, k, k, v, v)


def _dilated_kernel(nb, patterns, q_ref, kp_ref, kc_ref, vp_ref, vc_ref, o_ref,
                    qf_ref, kf_ref, vf_ref, of_ref, lf_ref):
    i = pl.program_id(2)
    qf_ref[...] = q_ref[...].astype(F32)
    kf_ref[0:nb, :] = kp_ref[...].astype(F32)
    kf_ref[nb:2 * nb, :] = kc_ref[...].astype(F32)
    vf_ref[0:nb, :] = vp_ref[...].astype(F32)
    vf_ref[nb:2 * nb, :] = vc_ref[...].astype(F32)
    for pi, (window, dil) in enumerate(patterns):
        w = window // dil
        per_class = nb // dil
        inner = _band_bias(LANE, w)
        first = _band_bias(LANE, w, i * per_class)
        for r in range(dil):
            for j in range(per_class // LANE):
                q_lo = r + j * LANE * dil
                k_lo = nb + q_lo - w * dil
                rows_q = pl.ds(q_lo, LANE, stride=dil)
                rows_k = pl.ds(k_lo, w + LANE, stride=dil)
                o, lse = _band_block(qf_ref[rows_q, :].astype(BF16),
                                     kf_ref[rows_k, :].astype(BF16),
                                     vf_ref[rows_k, :].astype(BF16),
                                     first if j == 0 else inner)
                of_ref[pi, rows_q, :] = o
                lf_ref[pi, rows_q, :] = jnp.broadcast_to(lse, (LANE, LANE))
    n_pat = len(patterns)
    mx = lf_ref[0]
    for pi in range(1, n_pat):
        mx = jnp.maximum(mx, lf_ref[pi])
    es = [jnp.exp2(lf_ref[pi] - mx) for pi in range(n_pat)]
    den = es[0]
    num = es[0] * of_ref[0]
    for pi in range(1, n_pat):
        den = den + es[pi]
        num = num + es[pi] * of_ref[pi]
    o_ref[...] = (num / den).astype(o_ref.dtype)


def _dilated_attention(q, k, v, patterns, n_heads, nb=2048):
    B, S, W = q.shape
    for window, dil in patterns:
        assert window % dil == 0 and window // dil == LANE
        assert window <= nb and nb % (dil * LANE) == 0
    assert S % nb == 0
    prev = lambda b, h, i: (b, jnp.maximum(i - 1, 0), h)
    cur = lambda b, h, i: (b, i, h)
    blk = lambda index_map: pl.BlockSpec((None, nb, LANE), index_map)
    return pl.pallas_call(
        functools.partial(_dilated_kernel, nb, patterns),
        out_shape=jax.ShapeDtypeStruct((B, S, W), BF16),
        grid=(B, n_heads, S // nb),
        in_specs=[blk(cur), blk(prev), blk(cur), blk(prev), blk(cur)],
        out_specs=blk(cur),
        scratch_shapes=[pltpu.VMEM((nb, LANE), F32),
                        pltpu.VMEM((2 * nb, LANE), F32),
                        pltpu.VMEM((2 * nb, LANE), F32),
                        pltpu.VMEM((len(patterns), nb, LANE), F32),
                        pltpu.VMEM((len(patterns), nb, LANE), F32)],
        compiler_params=_cparams(3), name="dilated_attention",
    )(q, k, k, v, v)


def _gelu_tanh(x):
    return 0.5 * x * (1.0 + jnp.tanh(0.7978845608028654 * (x + 0.044715 * (x * x * x))))


def _compress_kernel(rope, t_ref, pe_ref, w1a_ref, w1b_ref, w2_ref, cos_ref, sin_ref, o_ref):
    t = t_ref[...].astype(F32)
    n = t.shape[0]
    first = jnp.dot((t + pe_ref[0:1, :]).astype(BF16), w1a_ref[...], preferred_element_type=F32)
    second = jnp.dot((t + pe_ref[1:2, :]).astype(BF16), w1b_ref[...], preferred_element_type=F32)
    hid = first + pltpu.roll(second, n - 1, 0)
    out = jnp.dot(_gelu_tanh(hid).astype(BF16), w2_ref[...], preferred_element_type=F32)
    if rope:
        out = out * cos_ref[...] + pltpu.roll(out, LANE // 2, 1) * sin_ref[...]
    o_ref[...] = out.astype(o_ref.dtype)


def _compress(t, pe2, w1a, w1b, w2, cos_c, sin_c, rope):
    B, n, K = t.shape
    whole = lambda b: (0, 0)
    per_b = lambda b: (b, 0, 0)
    return pl.pallas_call(
        functools.partial(_compress_kernel, rope),
        out_shape=jax.ShapeDtypeStruct((B, n, HEAD_DIM), BF16),
        grid=(B,),
        in_specs=[pl.BlockSpec((None, n, K), per_b),
                  pl.BlockSpec((2, K), whole),
                  pl.BlockSpec((K, HEAD_DIM), whole),
                  pl.BlockSpec((K, HEAD_DIM), whole),
                  pl.BlockSpec((HEAD_DIM, HEAD_DIM), whole),
                  pl.BlockSpec((None, n, HEAD_DIM), per_b),
                  pl.BlockSpec((None, n, HEAD_DIM), per_b)],
        out_specs=pl.BlockSpec((None, n, HEAD_DIM), per_b),
        compiler_params=_cparams(1), name="nsa_compress",
    )(t, pe2, w1a, w1b, w2, cos_c, sin_c)


def _nsa_cmp_block(tq, n_slc, n_sel, t0, q, kc, vc):
    n_cmp = kc.shape[0]
    per_blk = SLC_BLOCK // CMP_STRIDE
    slc_shift = n_slc.bit_length() - 1
    q = _stack_heads(q, NSA_HEADS)
    s = _dot_t(q, kc)
    R = NSA_HEADS * tq
    row = lax.broadcasted_iota(jnp.int32, (R, n_cmp), 0) & (tq - 1)
    col = lax.broadcasted_iota(jnp.int32, (R, n_cmp), 1)
    j_of = col & (n_slc - 1)
    r_of = col >> slc_shift
    cmp_end = (per_blk * j_of + r_of) * CMP_STRIDE + (CMP_LEN - 1)
    cmask = cmp_end <= row + t0
    s = jnp.where(cmask, s, NEG_INF)
    m = jnp.max(s, axis=-1, keepdims=True)
    e = jnp.exp2(s - m)
    p = jnp.where(cmask, e / jnp.sum(e, axis=-1, keepdims=True), 0.0)
    o = jnp.dot(p.astype(BF16), vc, preferred_element_type=F32)

    ph = p[0:tq]
    for h in range(1, NSA_HEADS):
        ph = ph + p[h * tq:(h + 1) * tq]
    groups = [ph[:, r * n_slc:(r + 1) * n_slc] for r in range(per_blk)]
    blk = lax.broadcasted_iota(jnp.int32, (tq, n_slc), 1)
    spill = jnp.where(blk == 0, 0.0, pltpu.roll(groups[per_blk - 1], 1, 1))
    imp = groups[0]
    for r in range(1, per_blk):
        imp = imp + groups[r]
    imp = imp + spill
    tpos = lax.broadcasted_iota(jnp.int32, (tq, n_slc), 0) + t0
    cur = tpos >> (SLC_BLOCK.bit_length() - 1)
    forced = (blk == 0) | (blk == cur) | (blk == cur - 1)
    valid = blk <= cur
    work = jnp.where(forced, BIG, jnp.where(valid, imp, -BIG))
    blk_f = blk.astype(F32)
    sel = jnp.zeros((tq, n_slc), F32)
    for _ in range(n_sel):
        mx = jnp.max(work, axis=-1, keepdims=True)
        first = jnp.min(jnp.where(work == mx, blk_f, float(n_slc)), axis=-1, keepdims=True)
        pick = blk_f == first
        sel = jnp.where(pick, 1.0, sel)
        work = jnp.where(pick, -jnp.inf, work)
    return o, jnp.where((sel > 0.0) & valid, 0.0, NEG_INF)


def _nsa_cmp_kernel(tq, n_sub, n_slc, n_sel, q_ref, kc_ref, vc_ref, o_ref, pen_ref):
    i = pl.program_id(1)
    per_blk = SLC_BLOCK // CMP_STRIDE

    def run(n_j):
        take = lambda ref: jnp.concatenate(
            [ref[r * n_slc:r * n_slc + n_j, :] for r in range(per_blk)], axis=0)
        kc, vc = (kc_ref[...], vc_ref[...]) if n_j == n_slc else (take(kc_ref), take(vc_ref))
        for u in range(n_sub):
            rows = slice(u * tq, (u + 1) * tq)
            o, pen = _nsa_cmp_block(tq, n_j, min(n_sel, n_j), (i * n_sub + u) * tq,
                                    q_ref[rows, :], kc, vc)
            for h in range(NSA_HEADS):
                o_ref[rows, h * LANE:(h + 1) * LANE] = o[h * tq:(h + 1) * tq].astype(o_ref.dtype)
            pen = pen.astype(pen_ref.dtype)
            for c in range(n_slc // LANE):
                if c < n_j // LANE:
                    pen_ref[c, rows, :] = pen[:, c * LANE:(c + 1) * LANE]
                else:
                    pen_ref[c, rows, :] = jnp.full((tq, LANE), NEG_INF, pen_ref.dtype)

    half = n_slc // 2
    if half % LANE == 0:
        early = (i + 1) * (tq * n_sub) <= half * SLC_BLOCK
        pl.when(early)(lambda: run(half))
        pl.when(jnp.logical_not(early))(lambda: run(n_slc))
    else:
        run(n_slc)


def _nsa_compressed(q, k_c, v_c, tq=128, n_sub=4):
    B, S, W = q.shape
    n_cmp = k_c.shape[1]
    n_slc = S // SLC_BLOCK
    n_sel = min(SLC_TOPK, n_slc)
    assert n_slc % LANE == 0 and n_slc & (n_slc - 1) == 0
    assert n_cmp == 4 * n_slc and tq & (tq - 1) == 0
    n_pc = n_slc // LANE
    bq = tq * n_sub
    return pl.pallas_call(
        functools.partial(_nsa_cmp_kernel, tq, n_sub, n_slc, n_sel),
        out_shape=[jax.ShapeDtypeStruct((B, S, W), BF16),
                   jax.ShapeDtypeStruct((B, n_pc, S, LANE), BF16)],
        grid=(B, S // bq),
        in_specs=[pl.BlockSpec((None, bq, W), lambda b, i: (b, i, 0)),
                  pl.BlockSpec((None, n_cmp, HEAD_DIM), lambda b, i: (b, 0, 0)),
                  pl.BlockSpec((None, n_cmp, HEAD_DIM), lambda b, i: (b, 0, 0))],
        out_specs=[pl.BlockSpec((None, bq, W), lambda b, i: (b, i, 0)),
                   pl.BlockSpec((None, n_pc, bq, LANE), lambda b, i: (b, 0, i, 0))],
        compiler_params=_cparams(2), name="nsa_compressed_topk",
    )(q, k_c, v_c)


def _nsa_sel_kernel(tq, tk, n_sub, q_ref, pen_ref, k_ref, v_ref, o_ref, qa_ref, st_ref):
    t0 = pl.program_id(1) * (tq * n_sub)
    R = NSA_HEADS * tq
    last = t0 // tk
    tiles_per_chunk = PEN_CHUNK // tk
    for u in range(n_sub):
        rows = slice(u * tq, (u + 1) * tq)
        q = _stack_heads(q_ref[rows, :], NSA_HEADS)
        for c in range(pen_ref.shape[0]):
            qa_ref[u, c, :, 0:LANE] = q
            qa_ref[u, c, :, LANE:2 * LANE] = jnp.concatenate([pen_ref[c, rows, :]] * NSA_HEADS,
                                                             axis=0)

    def scores(u, t):
        start = pl.multiple_of(t * tk, tk)
        return _dot_t(k_ref[pl.ds(start, tk), :], qa_ref[u, t // tiles_per_chunk])

    def diag_mask(u, st):
        key = lax.broadcasted_iota(jnp.int32, (tk, R), 0) + last * tk
        qry = (lax.broadcasted_iota(jnp.int32, (tk, R), 1) & (tq - 1)) + (t0 + u * tq)
        return jnp.where(key <= qry, st, NEG_INF)

    def values(u, t):
        return v_ref[pl.ds(pl.multiple_of(t * tk, tk), tk), :]

    outs = _flash_transposed(last, n_sub, scores, diag_mask, values, st_ref)
    for u in range(n_sub):
        for h in range(NSA_HEADS):
            o_ref[u * tq:(u + 1) * tq, h * LANE:(h + 1) * LANE] = (
                outs[u][:, h * tq:(h + 1) * tq].T.astype(o_ref.dtype))


def _nsa_selected(q, pen, k_aug, v, tq=128, n_sub=2, tk=1024):
    B, S, W = q.shape
    n_pc = pen.shape[1]
    bq = tq * n_sub
    assert tk % bq == 0 and PEN_CHUNK % tk == 0 and tq & (tq - 1) == 0
    return pl.pallas_call(
        functools.partial(_nsa_sel_kernel, tq, tk, n_sub),
        out_shape=jax.ShapeDtypeStruct((B, S, W), BF16),
        grid=(B, S // bq),
        in_specs=[pl.BlockSpec((None, bq, W), lambda b, i: (b, i, 0)),
                  pl.BlockSpec((None, n_pc, bq, LANE), lambda b, i: (b, 0, i, 0)),
                  pl.BlockSpec((None, S, 2 * HEAD_DIM), lambda b, i: (b, 0, 0)),
                  pl.BlockSpec((None, S, HEAD_DIM), lambda b, i: (b, 0, 0))],
        out_specs=pl.BlockSpec((None, bq, W), lambda b, i: (b, i, 0)),
        scratch_shapes=[pltpu.VMEM((n_sub, n_pc, NSA_HEADS * tq, 2 * HEAD_DIM), BF16),
                        pltpu.VMEM((n_sub, 2, tk, NSA_HEADS * tq), F32)],
        compiler_params=_cparams(2), name="nsa_selected_attention",
    )(q, pen, k_aug, v)


def _mix_out_kernel(oc_ref, os_ref, ow_ref, gate_ref, ob_ref, od_ref, w_ref, h_ref, g_ref,
                    o_ref, a_ref):
    half = a_ref.shape[0] // 2
    for part in range(2):
        rows = slice(part * half, (part + 1) * half)
        gate = jax.nn.sigmoid(gate_ref[rows, :])
        for h in range(NSA_HEADS):
            hs = slice(h * LANE, (h + 1) * LANE)
            a = (gate[:, 3 * h:3 * h + 1] * oc_ref[rows, hs].astype(F32)
                 + gate[:, 3 * h + 1:3 * h + 2] * os_ref[rows, hs].astype(F32)
                 + gate[:, 3 * h + 2:3 * h + 3] * ow_ref[rows, hs].astype(F32))
            a_ref[rows, hs] = a.astype(a_ref.dtype)
        off = NSA_HEADS * LANE
        a_ref[rows, off:off + MLA_HEADS * LANE] = ob_ref[rows, :]
        off += MLA_HEADS * LANE
        a_ref[rows, off:off + DIL_HEADS * LANE] = od_ref[rows, :]
        y = jnp.dot(a_ref[rows, :], w_ref[...], preferred_element_type=F32)
        o_ref[rows, :] = h_ref[rows, :] + _rms(y, g_ref[...])


def _mix_out(o_c, o_s, o_w, gate, o_b, o_d, w, h, g, bm=512):
    T, N = h.shape
    width = (NSA_HEADS + MLA_HEADS + DIL_HEADS) * LANE
    heads = (o_c, o_s, o_w, gate, o_b, o_d)
    row = lambda a: pl.BlockSpec((bm, a.shape[1]), lambda i: (i, 0))
    return pl.pallas_call(
        _mix_out_kernel,
        out_shape=jax.ShapeDtypeStruct((T, N), F32),
        grid=(T // bm,),
        in_specs=[row(a) for a in heads] + [
            pl.BlockSpec((width, N), lambda i: (0, 0), pipeline_mode=pl.Buffered(1)),
            row(h), pl.BlockSpec((1, N), lambda i: (0, 0))],
        out_specs=row(h),
        scratch_shapes=[pltpu.VMEM((bm, width), BF16)],
        compiler_params=_cparams(1), name="mix_out_proj",
    )(*heads, w, h, g)


def _xattn_kernel(n_heads, scale, h_ref, gq_ref, wq_ref, k_ref, v_ref, wo_ref, go_ref, o_ref,
                  a_ref):
    h = h_ref[...]
    x = _rms(h, gq_ref[...]).astype(BF16)
    q = (jnp.dot(x, wq_ref[...], preferred_element_type=F32) * scale).astype(BF16)
    for hd in range(n_heads):
        hs = slice(hd * LANE, (hd + 1) * LANE)
        s = _dot_t(q[:, hs], k_ref[:, hs])
        m = jnp.max(s, axis=-1, keepdims=True)
        p = jnp.exp2(s - m)
        l = jnp.sum(p, axis=-1, keepdims=True)
        o = jnp.dot(p.astype(BF16), v_ref[:, hs], preferred_element_type=F32) / l
        a_ref[:, hs] = o.astype(a_ref.dtype)
    y = jnp.dot(a_ref[...], wo_ref[...], preferred_element_type=F32)
    o_ref[...] = h + _rms(y, go_ref[...])


def _cross_attention(h, gq, wq, k, v, wo, go, n_heads, scale, bm=512):
    B, S, D = h.shape
    M, W = k.shape[1:]
    const = lambda shape: pl.BlockSpec(shape, lambda b, i: (0, 0), pipeline_mode=pl.Buffered(1))
    return pl.pallas_call(
        functools.partial(_xattn_kernel, n_heads, scale),
        out_shape=jax.ShapeDtypeStruct((B, S, D), F32),
        grid=(B, S // bm),
        in_specs=[pl.BlockSpec((None, bm, D), lambda b, i: (b, i, 0)),
                  const((1, D)), const((D, W)),
                  pl.BlockSpec((None, M, W), lambda b, i: (b, 0, 0)),
                  pl.BlockSpec((None, M, W), lambda b, i: (b, 0, 0)),
                  const((W, D)), const((1, D))],
        out_specs=pl.BlockSpec((None, bm, D), lambda b, i: (b, i, 0)),
        scratch_shapes=[pltpu.VMEM((bm, W), BF16)],
        compiler_params=_cparams(2), name="cross_attention",
    )(h, gq, wq, k, v, wo, go)


def _rope_tables(pos, dim):
    inv = ROPE_THETA ** (-jnp.arange(0, dim, 2, dtype=F32) / dim)
    ang = pos.astype(F32)[..., None] * inv
    return jnp.cos(ang), jnp.sin(ang)


def _full_tables(pos):
    c, s = _rope_tables(pos, HEAD_DIM)
    return jnp.concatenate([c, c], -1), jnp.concatenate([-s, s], -1)


def _spread_rope64(t):
    z = jnp.zeros(t.shape[:-1] + (QK_ROPE // 2,), t.dtype)
    return jnp.concatenate([t[..., :QK_ROPE // 2], z, t[..., QK_ROPE // 2:], z], -1)


def _small_tables(pos):
    c, s = _rope_tables(pos, QK_ROPE)
    return _spread_rope64(jnp.concatenate([c, c], -1)), _spread_rope64(jnp.concatenate([-s, s], -1))


def _layer(h, mem, tabs, tabs_c, onehot, p):
    B, S, D = h.shape
    T = B * S
    dh = HEAD_DIM
    scale = dh ** -0.5 * LOG2E
    h2 = h.reshape(T, D)

    w_in = p["w_in"].T
    cuts = np.cumsum([NSA_HEADS * dh, 6 * dh, 3 * NSA_HEADS, Q_LORA, KV_LORA, QK_ROPE])
    w_q, w_kv, w_g, w_cq, w_ckv, w_kr, w_dil = jnp.split(w_in, cuts, axis=0)
    kv = [w_kv[k * dh:(k + 1) * dh] for k in range(6)]
    dw = DIL_HEADS * dh
    w_dq, w_dk, w_dv = w_dil[:dw], w_dil[dw:2 * dw], w_dil[2 * dw:]
    w_gpad = jnp.pad(w_g, ((0, LANE - w_g.shape[0]), (0, 0)))
    w_all = jnp.concatenate([w_q, kv[2], kv[4], w_dq, w_dk, _spread_rope64(w_kr.T).T,
                             kv[0], kv[1], kv[3], kv[5], w_gpad, w_cq, w_ckv, w_dv], 0).astype(BF16)
    one = lambda mode, n=1, s=1.0: ((mode,) * n, s)
    nsa_q, k_s, k_w, dq, dk, k_pe, k_cr, v_cr, v_s, v_w, gate, cq, ckv, dv = _proj(
        h2, p["g_mix_pre"], w_all,
        [one("rope", NSA_HEADS, scale), one("rope"), one("rope"), one("rope", DIL_HEADS, scale),
         one("rope", DIL_HEADS), one("rope_r"), one("none"), one("none"), one("none"), one("none"),
         one("none"), one("none", Q_LORA // LANE), one("none", KV_LORA // LANE),
         one("none", DIL_HEADS)],
        [BF16] * 10 + [F32, BF16, BF16, BF16], tabs=tabs, w_t=True, name="in_proj")

    n_chunk = S // CMP_STRIDE
    half = CMP_LEN // 2
    pe2 = p["cmp_pos_emb"].reshape(2, half * dh)
    cos_c, sin_c = tabs_c

    def compress(t, w1, w2, rope):
        out = _compress(t.reshape(B, n_chunk, CMP_STRIDE * dh), pe2,
                        w1[:half * dh].astype(BF16), w1[half * dh:].astype(BF16),
                        w2.astype(BF16), cos_c, sin_c, rope)
        return out.reshape(B, n_chunk // 4, 4, dh).transpose(0, 2, 1, 3).reshape(B, n_chunk, dh)

    k_c = compress(k_cr, p["w_cmp_k1"], p["w_cmp_k2"], True)
    v_c = compress(v_cr, p["w_cmp_v1"], p["w_cmp_v2"], False)
    q3 = nsa_q.reshape(B, S, NSA_HEADS * dh)
    o_cmp, pen = _nsa_compressed(q3, k_c, v_c)
    k_aug = jnp.concatenate([k_s.reshape(B, S, dh), onehot], axis=-1)
    o_sel = _nsa_selected(q3, pen, k_aug, v_s.reshape(B, S, dh))
    o_win = _nsa_window_attention(q3, k_w.reshape(B, S, dh), v_w.reshape(B, S, dh), NSA_WINDOW)

    dqk = QK_NOPE + QK_ROPE
    w_uq = p["w_uq"].reshape(Q_LORA, MLA_HEADS, dqk)
    w_uq = jnp.concatenate([w_uq[..., :QK_NOPE], _spread_rope64(w_uq[..., QK_NOPE:])], -1)
    w_uq = w_uq.reshape(Q_LORA, MLA_HEADS * 2 * dh).astype(BF16)
    (q_m,) = _proj(cq, p["g_q_lora"], w_uq, [(("none", "rope_r") * MLA_HEADS, dqk ** -0.5 * LOG2E)],
                   [BF16], tabs=tabs, bm=1024, name="mla_q_up")
    w_ukv = p["w_ukv"].reshape(KV_LORA, MLA_HEADS, 2 * dh)
    w_ukv = jnp.concatenate([w_ukv[..., :dh].reshape(KV_LORA, -1),
                             w_ukv[..., dh:].reshape(KV_LORA, -1)], 1).astype(BF16)
    k_m, v_m = _proj(ckv, p["g_kv_lora"], w_ukv,
                     [(("none", "extra") * MLA_HEADS, 1.0), (("none",) * MLA_HEADS, 1.0)],
                     [BF16, BF16], extra=k_pe, bm=1024, name="mla_kv_up")
    o_mla = _mla_attention(q_m.reshape(B, S, -1), k_m.reshape(B, S, -1),
                           v_m.reshape(B, S, -1), MLA_HEADS)

    o_dil = _dilated_attention(dq.reshape(B, S, dw), dk.reshape(B, S, dw), dv.reshape(B, S, dw),
                               DIL_PATTERNS, DIL_HEADS)

    h2 = _mix_out(o_cmp.reshape(T, -1), o_sel.reshape(T, -1), o_win.reshape(T, -1), gate,
                  o_mla.reshape(T, -1), o_dil.reshape(T, dw), p["w_out"].astype(BF16), h2,
                  p["g_mix_post"])

    xw = XATTN_HEADS * dh
    M = mem.shape[1]
    xk, xv = _proj(mem.reshape(B * M, D), p["g_mem_kv"], p["w_xkv"].astype(BF16),
                   [(("none",) * XATTN_HEADS, 1.0), (("none",) * XATTN_HEADS, 1.0)],
                   [BF16, BF16], bm=min(512, B * M), name="xattn_kv")
    h2 = _cross_attention(h2.reshape(B, S, D), p["g_mem_pre"], p["w_xq"].astype(BF16),
                          xk.reshape(B, M, xw), xv.reshape(B, M, xw), p["w_xo"].astype(BF16),
                          p["g_mem_post"], XATTN_HEADS, scale).reshape(T, D)

    up = _mlp_up(h2, p["g_mlp_pre"], p["w_up_all"], p["layer"])
    h2 = _out_proj(up, p["w_down_all"], h2, p["g_mlp_post"], layer=p["layer"], bm=1024, bk=1024,
                   name="mlp_down")
    return h2.reshape(B, S, D)


_LAYER_PARAMS = ("g_mix_pre", "w_in", "cmp_pos_emb", "w_cmp_k1", "w_cmp_k2", "w_cmp_v1", "w_cmp_v2",
                 "g_q_lora", "g_kv_lora", "w_uq", "w_ukv", "w_out", "g_mix_post", "g_mem_pre",
                 "g_mem_kv", "w_xq", "w_xkv", "w_xo", "g_mem_post", "g_mlp_pre", "w_up", "w_down",
                 "g_mlp_post")


def kernel(x, mem, positions, g_mix_pre, w_in, cmp_pos_emb, w_cmp_k1, w_cmp_k2, w_cmp_v1, w_cmp_v2, g_q_lora, g_kv_lora, w_uq, w_ukv, w_out, g_mix_post, g_mem_pre, g_mem_kv, w_xq, w_xkv, w_xo, g_mem_post, g_mlp_pre, w_up, w_down, g_mlp_post):
    stacked = dict(zip(_LAYER_PARAMS, (
        g_mix_pre, w_in, cmp_pos_emb, w_cmp_k1, w_cmp_k2, w_cmp_v1, w_cmp_v2, g_q_lora, g_kv_lora,
        w_uq, w_ukv, w_out, g_mix_post, g_mem_pre, g_mem_kv, w_xq, w_xkv, w_xo, g_mem_post,
        g_mlp_pre, w_up, w_down, g_mlp_post)))
    B, S, D = x.shape
    T = B * S
    assert S % PEN_CHUNK == 0
    cosf, sinf = _full_tables(positions)
    cosr, sinr = _small_tables(positions)
    tabs = tuple(t.reshape(T, LANE) for t in (cosf, sinf, cosr, sinr))
    n_chunk = S // CMP_STRIDE
    end = jnp.minimum(jnp.arange(n_chunk) * CMP_STRIDE + CMP_LEN - 1, S - 1)
    tabs_c = _full_tables(positions[:, end])
    blk = (jnp.arange(S) // SLC_BLOCK) % LANE
    onehot = jnp.broadcast_to((blk[:, None] == jnp.arange(LANE)[None, :]).astype(BF16)[None],
                              (B, S, LANE))
    h = x
    w_up_all, w_down_all = w_up.astype(BF16), w_down.astype(BF16)
    for layer in range(stacked["w_in"].shape[0]):
        p = {"layer": layer, "w_up_all": w_up_all, "w_down_all": w_down_all}
        for name, val in stacked.items():
            if name not in ("w_up", "w_down"):
                v = val[layer]
                p[name] = v[None, :] if name.startswith("g_") else v
        h = _layer(h, mem, tabs, tabs_c, onehot, p)
    return h
```

```python
import functools

import numpy as np
import jax
import jax.numpy as jnp
from jax import lax
from jax.experimental import pallas as pl
from jax.experimental.pallas import tpu as pltpu

F32 = jnp.float32
BF16 = jnp.bfloat16

LANE = 128
VMEM_LIMIT = 56 * 1024 * 1024

HEAD_DIM = 128
ROPE_THETA = 10000.0
NORM_EPS = 1e-6
NEG_INF = -1e30
BIG = 1e9
LOG2E = 1.4426950408889634
NSA_HEADS = 4
MLA_HEADS = 6
DIL_HEADS = 6
CMP_LEN = 32
CMP_STRIDE = 16
SLC_BLOCK = 64
SLC_TOPK = 16
NSA_WINDOW = 512
Q_LORA = 512
KV_LORA = 512
QK_NOPE = 128
QK_ROPE = 64
DIL_PATTERNS = ((128, 1), (512, 4), (2048, 16))
XATTN_HEADS = 4
PEN_CHUNK = LANE * SLC_BLOCK


def _cparams(n_grid):
    return pltpu.CompilerParams(
        dimension_semantics=("arbitrary",) * n_grid, vmem_limit_bytes=VMEM_LIMIT)


def _rms(x, g):
    return x * lax.rsqrt(jnp.mean(x * x, axis=-1, keepdims=True) + NORM_EPS) * g


def _dot_t(a, b):
    return lax.dot_general(a, b, (((1,), (1,)), ((), ())), preferred_element_type=F32)


def _proj_kernel(out_plan, has_norm, n_tab, has_extra, w_t, chunk, *refs):
    x_ref, g_ref, w_ref = refs[:3]
    tab_refs = refs[3:3 + n_tab]
    n_in = 3 + n_tab + int(has_extra)
    out_refs = refs[n_in:]
    x = x_ref[...].astype(F32)
    if has_norm:
        x = _rms(x, g_ref[...])
    xb = x.astype(BF16)
    n_cols = w_ref.shape[0 if w_t else 1]
    flat = []
    for oi, (modes, scale) in enumerate(out_plan):
        for k, mode in enumerate(modes):
            if mode == "extra":
                out_refs[oi][:, k * LANE:(k + 1) * LANE] = refs[n_in - 1][...]
            else:
                flat.append((oi, k * LANE, mode, scale))
    for c0 in range(0, n_cols, chunk):
        c1 = min(c0 + chunk, n_cols)
        if w_t:
            acc = _dot_t(xb, w_ref[c0:c1, :])
        else:
            acc = jnp.dot(xb, w_ref[:, c0:c1], preferred_element_type=F32)
        for s in range((c1 - c0) // LANE):
            oi, off, mode, scale = flat[c0 // LANE + s]
            a = acc[:, s * LANE:(s + 1) * LANE]
            if mode == "rope":
                a = a * tab_refs[0][...] + pltpu.roll(a, LANE // 2, 1) * tab_refs[1][...]
            elif mode == "rope_r":
                a = a * tab_refs[2][...] + pltpu.roll(a, LANE // 2, 1) * tab_refs[3][...]
            if scale != 1.0:
                a = a * scale
            out_refs[oi][:, off:off + LANE] = a.astype(out_refs[oi].dtype)


def _proj(x, g, w, out_plan, out_dtypes, tabs=(), extra=None, w_t=False, bm=512, chunk=512,
          name="proj"):
    T, K = x.shape
    N = w.shape[0 if w_t else 1]
    assert T % bm == 0 and N % LANE == 0
    assert sum(sum(md != "extra" for md in m) for m, _ in out_plan) * LANE == N
    has_norm = g is not None
    if g is None:
        g = jnp.ones((1, K), F32)
    extras = () if extra is None else (extra,)
    in_specs = [pl.BlockSpec((bm, K), lambda i: (i, 0)),
                pl.BlockSpec((1, K), lambda i: (0, 0)),
                pl.BlockSpec(w.shape, lambda i: (0, 0), pipeline_mode=pl.Buffered(1))]
    in_specs += [pl.BlockSpec((bm, LANE), lambda i: (i, 0)) for _ in tabs + extras]
    out_shape = [jax.ShapeDtypeStruct((T, len(m) * LANE), dt)
                 for (m, _), dt in zip(out_plan, out_dtypes)]
    out_specs = [pl.BlockSpec((bm, len(m) * LANE), lambda i: (i, 0)) for m, _ in out_plan]
    return pl.pallas_call(
        functools.partial(_proj_kernel, out_plan, has_norm, len(tabs), extra is not None, w_t,
                          chunk),
        out_shape=out_shape, grid=(T // bm,), in_specs=in_specs, out_specs=out_specs,
        compiler_params=_cparams(1), name=name,
    )(x, g, w, *tabs, *extras)


def _mlp_up_kernel(x_ref, g_ref, w_ref, o_ref, xn_ref):
    @pl.when(pl.program_id(1) == 0)
    def _():
        xn_ref[...] = _rms(x_ref[...], g_ref[...]).astype(BF16)
    a = jnp.dot(xn_ref[...], w_ref[...], preferred_element_type=F32)
    a = jnp.maximum(a, 0.0)
    o_ref[...] = (a * a).astype(o_ref.dtype)


def _mlp_up(x, g, w, layer, bm=1024, bn=2048):
    T, K = x.shape
    N = w.shape[2]
    return pl.pallas_call(
        _mlp_up_kernel,
        out_shape=jax.ShapeDtypeStruct((T, N), BF16),
        grid=(T // bm, N // bn),
        in_specs=[pl.BlockSpec((bm, K), lambda i, j: (i, 0)),
                  pl.BlockSpec((1, K), lambda i, j: (0, 0)),
                  pl.BlockSpec((None, K, bn), lambda i, j: (layer, 0, j))],
        out_specs=pl.BlockSpec((bm, bn), lambda i, j: (i, j)),
        scratch_shapes=[pltpu.VMEM((bm, K), BF16)],
        compiler_params=_cparams(2), name="mlp_up",
    )(x, g, w)


def _out_proj_kernel(n_k, a_ref, w_ref, h_ref, g_ref, o_ref):
    k = pl.program_id(1)

    def part(rows=slice(None)):
        return jnp.dot(a_ref[rows, :], w_ref[...], preferred_element_type=F32)

    def finish(first):
        half = o_ref.shape[0] // 2
        for r in range(2):
            rows = slice(r * half, (r + 1) * half)
            y = part(rows) if first else o_ref[rows, :] + part(rows)
            o_ref[rows, :] = h_ref[rows, :] + _rms(y, g_ref[...])

    if n_k == 1:
        finish(True)
    else:
        @pl.when(k == 0)
        def _():
            o_ref[...] = part()

        @pl.when(jnp.logical_and(k > 0, k < n_k - 1))
        def _():
            o_ref[...] += part()

        @pl.when(k == n_k - 1)
        def _():
            finish(False)


def _out_proj(a, w, h, g, layer, bm=512, bk=2048, name="out_proj"):
    T, K = a.shape
    N = w.shape[2]
    bk = min(bk, K)
    n_k = K // bk
    return pl.pallas_call(
        functools.partial(_out_proj_kernel, n_k),
        out_shape=jax.ShapeDtypeStruct((T, N), F32),
        grid=(T // bm, n_k),
        in_specs=[pl.BlockSpec((bm, bk), lambda i, k: (i, k)),
                  pl.BlockSpec((None, bk, N), lambda i, k: (layer, k, 0)),
                  pl.BlockSpec((bm, N), lambda i, k: (i, 0)),
                  pl.BlockSpec((1, N), lambda i, k: (0, 0))],
        out_specs=pl.BlockSpec((bm, N), lambda i, k: (i, 0)),
        compiler_params=_cparams(2), name=name,
    )(a, w, h, g)


def _online_step(carry, st, st_max, v):
    m, l, acc = carry
    m_new = jnp.maximum(m, st_max)
    alpha = jnp.exp2(m - m_new)
    p = jnp.exp2(st - m_new)
    l = alpha * l + jnp.sum(p, axis=0, keepdims=True)
    pv = lax.dot_general(v, p.astype(BF16), (((0,), (0,)), ((), ())), preferred_element_type=F32)
    return m_new, l, alpha * acc + pv


def _online_init(cols, dv):
    return (jnp.full((1, cols), NEG_INF, F32), jnp.zeros((1, cols), F32),
            jnp.zeros((dv, cols), F32))


def _flash_transposed(n_full, n_groups, scores, diag_mask, values, st_ref):
    groups = range(n_groups)

    def put(t, slot, masked=False):
        maxes = []
        for g in groups:
            st = scores(g, t)
            if masked:
                st = diag_mask(g, st)
            st_ref[g, slot] = st
            maxes.append(jnp.max(st, axis=0, keepdims=True))
        return tuple(maxes)

    def tile_at(step):
        return jnp.where(step == 0, n_full, step - 1)

    def softmax_step(step, stats, st_max, slot):
        t = tile_at(step)
        return tuple(_online_step(stats[g], st_ref[g, slot], st_max[g], values(g, t))
                     for g in groups)

    def half(step, carry, slot):
        stats, st_max = carry
        nxt = put(step, 1 - slot)
        return softmax_step(step, stats, st_max, slot), nxt

    def pair(u, carry):
        return half(2 * u + 1, half(2 * u, carry, 0), 1)

    init = tuple(_online_init(st_ref.shape[3], HEAD_DIM) for g in groups)
    carry = lax.fori_loop(0, n_full // 2, pair, (init, put(n_full, 0, masked=True)))
    odd = n_full & 1
    stats, st_max = lax.fori_loop(0, odd, lambda _, c: half(n_full - 1, c, 0), carry)
    stats = softmax_step(n_full, stats, st_max, odd)
    return [acc / l for _, l, acc in stats]


def _mla_kernel(tq, tk, hg, q_ref, k_ref, v_ref, o_ref, st_ref):
    t0 = pl.program_id(2) * tq
    dq = q_ref.shape[1] // hg
    last = t0 // tk

    def scores(g, t):
        start = pl.multiple_of(t * tk, tk)
        cols = slice(g * dq, (g + 1) * dq)
        return _dot_t(k_ref[pl.ds(start, tk), cols], q_ref[:, cols])

    def values(g, t):
        start = pl.multiple_of(t * tk, tk)
        return v_ref[pl.ds(start, tk), g * HEAD_DIM:(g + 1) * HEAD_DIM]

    def diag_mask(g, st):
        key = lax.broadcasted_iota(jnp.int32, (tk, tq), 0) + last * tk
        qry = lax.broadcasted_iota(jnp.int32, (tk, tq), 1) + t0
        return jnp.where(key <= qry, st, NEG_INF)

    outs = _flash_transposed(last, hg, scores, diag_mask, values, st_ref)
    for g in range(hg):
        o_ref[:, g * HEAD_DIM:(g + 1) * HEAD_DIM] = outs[g].T.astype(o_ref.dtype)


def _mla_attention(q, k, v, n_heads, tq=1024, tk=1024, hg=2):
    B, S, _ = q.shape
    dq = q.shape[2] // n_heads
    assert tk % tq == 0 and S % tk == 0 and n_heads % hg == 0
    resident = pl.Buffered(1)
    return pl.pallas_call(
        functools.partial(_mla_kernel, tq, tk, hg),
        out_shape=jax.ShapeDtypeStruct((B, S, n_heads * HEAD_DIM), BF16),
        grid=(B, n_heads // hg, S // tq),
        in_specs=[pl.BlockSpec((None, tq, hg * dq), lambda b, h, i: (b, i, h)),
                  pl.BlockSpec((None, S, hg * dq), lambda b, h, i: (b, 0, h),
                               pipeline_mode=resident),
                  pl.BlockSpec((None, S, hg * HEAD_DIM), lambda b, h, i: (b, 0, h),
                               pipeline_mode=resident)],
        out_specs=pl.BlockSpec((None, tq, hg * HEAD_DIM), lambda b, h, i: (b, i, h)),
        scratch_shapes=[pltpu.VMEM((hg, 2, tk, tq), F32)],
        compiler_params=_cparams(3), name="mla_attention",
    )(q, k, v)


def _band_bias(R, window, base=None):
    C = window + LANE
    r = lax.broadcasted_iota(jnp.int32, (R, C), 0) & (LANE - 1)
    c = lax.broadcasted_iota(jnp.int32, (R, C), 1)
    mask = (c >= r) & (c <= r + window)
    if base is not None:
        mask = mask & (c >= window - base)
    return jnp.where(mask, 0.0, NEG_INF)


def _band_block(q, kwin, vwin, bias):
    s = _dot_t(q, kwin) + bias
    m = jnp.max(s, axis=-1, keepdims=True)
    p = jnp.exp2(s - m)
    l = jnp.sum(p, axis=-1, keepdims=True)
    o = jnp.dot(p.astype(BF16), vwin, preferred_element_type=F32) / l
    return o, m + jnp.log2(l)


def _stack_heads(x, n):
    return jnp.concatenate([x[:, h * LANE:(h + 1) * LANE] for h in range(n)], axis=0)


def _nsa_window_kernel(tq, window, q_ref, kp_ref, kc_ref, vp_ref, vc_ref, o_ref):
    i = pl.program_id(1)
    kwin = jnp.concatenate([kp_ref[...], kc_ref[...]], axis=0)
    vwin = jnp.concatenate([vp_ref[...], vc_ref[...]], axis=0)
    for j in range(tq // LANE):
        q = _stack_heads(q_ref[j * LANE:(j + 1) * LANE, :], NSA_HEADS)
        lo = j * LANE
        o, _ = _band_block(q, kwin[lo:lo + window + LANE], vwin[lo:lo + window + LANE],
                           _band_bias(q.shape[0], window, i * tq + lo))
        for h in range(NSA_HEADS):
            o_ref[lo:lo + LANE, h * LANE:(h + 1) * LANE] = (
                o[h * LANE:(h + 1) * LANE].astype(o_ref.dtype))


def _nsa_window_attention(q, k, v, window):
    B, S, W = q.shape
    tq = window
    prev = lambda b, i: (b, jnp.maximum(i - 1, 0), 0)
    cur = lambda b, i: (b, i, 0)
    return pl.pallas_call(
        functools.partial(_nsa_window_kernel, tq, window),
        out_shape=jax.ShapeDtypeStruct((B, S, W), BF16),
        grid=(B, S // tq),
        in_specs=[pl.BlockSpec((None, tq, W), cur),
                  pl.BlockSpec((None, window, HEAD_DIM), prev),
                  pl.BlockSpec((None, tq, HEAD_DIM), cur),
                  pl.BlockSpec((None, window, HEAD_DIM), prev),
                  pl.BlockSpec((None, tq, HEAD_DIM), cur)],
        out_specs=pl.BlockSpec((None, tq, W), cur),
        compiler_params=_cparams(2), name="nsa_window_attention",
    )(q, k, k, v, v)


def _dilated_kernel(nb, patterns, q_ref, kp_ref, kc_ref, vp_ref, vc_ref, o_ref,
                    qf_ref, kf_ref, vf_ref, of_ref, lf_ref):
    i = pl.program_id(2)
    qf_ref[...] = q_ref[...].astype(F32)
    kf_ref[0:nb, :] = kp_ref[...].astype(F32)
    kf_ref[nb:2 * nb, :] = kc_ref[...].astype(F32)
    vf_ref[0:nb, :] = vp_ref[...].astype(F32)
    vf_ref[nb:2 * nb, :] = vc_ref[...].astype(F32)
    for pi, (window, dil) in enumerate(patterns):
        w = window // dil
        per_class = nb // dil
        inner = _band_bias(LANE, w)
        first = _band_bias(LANE, w, i * per_class)
        for r in range(dil):
            for j in range(per_class // LANE):
                q_lo = r + j * LANE * dil
                k_lo = nb + q_lo - w * dil
                rows_q = pl.ds(q_lo, LANE, stride=dil)
                rows_k = pl.ds(k_lo, w + LANE, stride=dil)
                o, lse = _band_block(qf_ref[rows_q, :].astype(BF16),
                                     kf_ref[rows_k, :].astype(BF16),
                                     vf_ref[rows_k, :].astype(BF16),
                                     first if j == 0 else inner)
                of_ref[pi, rows_q, :] = o
                lf_ref[pi, rows_q, :] = jnp.broadcast_to(lse, (LANE, LANE))
    n_pat = len(patterns)
    mx = lf_ref[0]
    for pi in range(1, n_pat):
        mx = jnp.maximum(mx, lf_ref[pi])
    es = [jnp.exp2(lf_ref[pi] - mx) for pi in range(n_pat)]
    den = es[0]
    num = es[0] * of_ref[0]
    for pi in range(1, n_pat):
        den = den + es[pi]
        num = num + es[pi] * of_ref[pi]
    o_ref[...] = (num / den).astype(o_ref.dtype)


def _dilated_attention(q, k, v, patterns, n_heads, nb=2048):
    B, S, W = q.shape
    for window, dil in patterns:
        assert window % dil == 0 and window // dil == LANE
        assert window <= nb and nb % (dil * LANE) == 0
    assert S % nb == 0
    prev = lambda b, h, i: (b, jnp.maximum(i - 1, 0), h)
    cur = lambda b, h, i: (b, i, h)
    blk = lambda index_map: pl.BlockSpec((None, nb, LANE), index_map)
    return pl.pallas_call(
        functools.partial(_dilated_kernel, nb, patterns),
        out_shape=jax.ShapeDtypeStruct((B, S, W), BF16),
        grid=(B, n_heads, S // nb),
        in_specs=[blk(cur), blk(prev), blk(cur), blk(prev), blk(cur)],
        out_specs=blk(cur),
        scratch_shapes=[pltpu.VMEM((nb, LANE), F32),
                        pltpu.VMEM((2 * nb, LANE), F32),
                        pltpu.VMEM((2 * nb, LANE), F32),
                        pltpu.VMEM((len(patterns), nb, LANE), F32),
                        pltpu.VMEM((len(patterns), nb, LANE), F32)],
        compiler_params=_cparams(3), name="dilated_attention",
    )(q, k, k, v, v)


def _gelu_tanh(x):
    return 0.5 * x * (1.0 + jnp.tanh(0.7978845608028654 * (x + 0.044715 * (x * x * x))))


def _compress_kernel(rope, t_ref, pe_ref, w1a_ref, w1b_ref, w2_ref, cos_ref, sin_ref, o_ref):
    t = t_ref[...].astype(F32)
    n = t.shape[0]
    first = jnp.dot((t + pe_ref[0:1, :]).astype(BF16), w1a_ref[...], preferred_element_type=F32)
    second = jnp.dot((t + pe_ref[1:2, :]).astype(BF16), w1b_ref[...], preferred_element_type=F32)
    hid = first + pltpu.roll(second, n - 1, 0)
    out = jnp.dot(_gelu_tanh(hid).astype(BF16), w2_ref[...], preferred_element_type=F32)
    if rope:
        out = out * cos_ref[...] + pltpu.roll(out, LANE // 2, 1) * sin_ref[...]
    o_ref[...] = out.astype(o_ref.dtype)


def _compress(t, pe2, w1a, w1b, w2, cos_c, sin_c, rope):
    B, n, K = t.shape
    whole = lambda b: (0, 0)
    per_b = lambda b: (b, 0, 0)
    return pl.pallas_call(
        functools.partial(_compress_kernel, rope),
        out_shape=jax.ShapeDtypeStruct((B, n, HEAD_DIM), BF16),
        grid=(B,),
        in_specs=[pl.BlockSpec((None, n, K), per_b),
                  pl.BlockSpec((2, K), whole),
                  pl.BlockSpec((K, HEAD_DIM), whole),
                  pl.BlockSpec((K, HEAD_DIM), whole),
                  pl.BlockSpec((HEAD_DIM, HEAD_DIM), whole),
                  pl.BlockSpec((None, n, HEAD_DIM), per_b),
                  pl.BlockSpec((None, n, HEAD_DIM), per_b)],
        out_specs=pl.BlockSpec((None, n, HEAD_DIM), per_b),
        compiler_params=_cparams(1), name="nsa_compress",
    )(t, pe2, w1a, w1b, w2, cos_c, sin_c)


def _nsa_cmp_block(tq, n_slc, n_sel, t0, q, kc, vc):
    n_cmp = kc.shape[0]
    per_blk = SLC_BLOCK // CMP_STRIDE
    slc_shift = n_slc.bit_length() - 1
    q = _stack_heads(q, NSA_HEADS)
    s = _dot_t(q, kc)
    R = NSA_HEADS * tq
    row = lax.broadcasted_iota(jnp.int32, (R, n_cmp), 0) & (tq - 1)
    col = lax.broadcasted_iota(jnp.int32, (R, n_cmp), 1)
    j_of = col & (n_slc - 1)
    r_of = col >> slc_shift
    cmp_end = (per_blk * j_of + r_of) * CMP_STRIDE + (CMP_LEN - 1)
    cmask = cmp_end <= row + t0
    s = jnp.where(cmask, s, NEG_INF)
    m = jnp.max(s, axis=-1, keepdims=True)
    e = jnp.exp2(s - m)
    p = jnp.where(cmask, e / jnp.sum(e, axis=-1, keepdims=True), 0.0)
    o = jnp.dot(p.astype(BF16), vc, preferred_element_type=F32)

    ph = p[0:tq]
    for h in range(1, NSA_HEADS):
        ph = ph + p[h * tq:(h + 1) * tq]
    groups = [ph[:, r * n_slc:(r + 1) * n_slc] for r in range(per_blk)]
    blk = lax.broadcasted_iota(jnp.int32, (tq, n_slc), 1)
    spill = jnp.where(blk == 0, 0.0, pltpu.roll(groups[per_blk - 1], 1, 1))
    imp = groups[0]
    for r in range(1, per_blk):
        imp = imp + groups[r]
    imp = imp + spill
    tpos = lax.broadcasted_iota(jnp.int32, (tq, n_slc), 0) + t0
    cur = tpos >> (SLC_BLOCK.bit_length() - 1)
    forced = (blk == 0) | (blk == cur) | (blk == cur - 1)
    valid = blk <= cur
    work = jnp.where(forced, BIG, jnp.where(valid, imp, -BIG))
    blk_f = blk.astype(F32)
    sel = jnp.zeros((tq, n_slc), F32)
    for _ in range(n_sel):
        mx = jnp.max(work, axis=-1, keepdims=True)
        first = jnp.min(jnp.where(work == mx, blk_f, float(n_slc)), axis=-1, keepdims=True)
        pick = blk_f == first
        sel = jnp.where(pick, 1.0, sel)
        work = jnp.where(pick, -jnp.inf, work)
    return o, jnp.where((sel > 0.0) & valid, 0.0, NEG_INF)


def _nsa_cmp_kernel(tq, n_sub, n_slc, n_sel, q_ref, kc_ref, vc_ref, o_ref, pen_ref):
    i = pl.program_id(1)
    per_blk = SLC_BLOCK // CMP_STRIDE

    def run(n_j):
        take = lambda ref: jnp.concatenate(
            [ref[r * n_slc:r * n_slc + n_j, :] for r in range(per_blk)], axis=0)
        kc, vc = (kc_ref[...], vc_ref[...]) if n_j == n_slc else (take(kc_ref), take(vc_ref))
        for u in range(n_sub):
            rows = slice(u * tq, (u + 1) * tq)
            o, pen = _nsa_cmp_block(tq, n_j, min(n_sel, n_j), (i * n_sub + u) * tq,
                                    q_ref[rows, :], kc, vc)
            for h in range(NSA_HEADS):
                o_ref[rows, h * LANE:(h + 1) * LANE] = o[h * tq:(h + 1) * tq].astype(o_ref.dtype)
            pen = pen.astype(pen_ref.dtype)
            for c in range(n_slc // LANE):
                if c < n_j // LANE:
                    pen_ref[c, rows, :] = pen[:, c * LANE:(c + 1) * LANE]
                else:
                    pen_ref[c, rows, :] = jnp.full((tq, LANE), NEG_INF, pen_ref.dtype)

    half = n_slc // 2
    if half % LANE == 0:
        early = (i + 1) * (tq * n_sub) <= half * SLC_BLOCK
        pl.when(early)(lambda: run(half))
        pl.when(jnp.logical_not(early))(lambda: run(n_slc))
    else:
        run(n_slc)


def _nsa_compressed(q, k_c, v_c, tq=128, n_sub=4):
    B, S, W = q.shape
    n_cmp = k_c.shape[1]
    n_slc = S // SLC_BLOCK
    n_sel = min(SLC_TOPK, n_slc)
    assert n_slc % LANE == 0 and n_slc & (n_slc - 1) == 0
    assert n_cmp == 4 * n_slc and tq & (tq - 1) == 0
    n_pc = n_slc // LANE
    bq = tq * n_sub
    return pl.pallas_call(
        functools.partial(_nsa_cmp_kernel, tq, n_sub, n_slc, n_sel),
        out_shape=[jax.ShapeDtypeStruct((B, S, W), BF16),
                   jax.ShapeDtypeStruct((B, n_pc, S, LANE), BF16)],
        grid=(B, S // bq),
        in_specs=[pl.BlockSpec((None, bq, W), lambda b, i: (b, i, 0)),
                  pl.BlockSpec((None, n_cmp, HEAD_DIM), lambda b, i: (b, 0, 0)),
                  pl.BlockSpec((None, n_cmp, HEAD_DIM), lambda b, i: (b, 0, 0))],
        out_specs=[pl.BlockSpec((None, bq, W), lambda b, i: (b, i, 0)),
                   pl.BlockSpec((None, n_pc, bq, LANE), lambda b, i: (b, 0, i, 0))],
        compiler_params=_cparams(2), name="nsa_compressed_topk",
    )(q, k_c, v_c)


def _nsa_sel_kernel(tq, tk, n_sub, q_ref, pen_ref, k_ref, v_ref, o_ref, qa_ref, st_ref):
    t0 = pl.program_id(1) * (tq * n_sub)
    R = NSA_HEADS * tq
    last = t0 // tk
    tiles_per_chunk = PEN_CHUNK // tk
    for u in range(n_sub):
        rows = slice(u * tq, (u + 1) * tq)
        q = _stack_heads(q_ref[rows, :], NSA_HEADS)
        for c in range(pen_ref.shape[0]):
            qa_ref[u, c, :, 0:LANE] = q
            qa_ref[u, c, :, LANE:2 * LANE] = jnp.concatenate([pen_ref[c, rows, :]] * NSA_HEADS,
                                                             axis=0)

    def scores(u, t):
        start = pl.multiple_of(t * tk, tk)
        return _dot_t(k_ref[pl.ds(start, tk), :], qa_ref[u, t // tiles_per_chunk])

    def diag_mask(u, st):
        key = lax.broadcasted_iota(jnp.int32, (tk, R), 0) + last * tk
        qry = (lax.broadcasted_iota(jnp.int32, (tk, R), 1) & (tq - 1)) + (t0 + u * tq)
        return jnp.where(key <= qry, st, NEG_INF)

    def values(u, t):
        return v_ref[pl.ds(pl.multiple_of(t * tk, tk), tk), :]

    outs = _flash_transposed(last, n_sub, scores, diag_mask, values, st_ref)
    for u in range(n_sub):
        for h in range(NSA_HEADS):
            o_ref[u * tq:(u + 1) * tq, h * LANE:(h + 1) * LANE] = (
                outs[u][:, h * tq:(h + 1) * tq].T.astype(o_ref.dtype))


def _nsa_selected(q, pen, k_aug, v, tq=128, n_sub=4, tk=1024):
    B, S, W = q.shape
    n_pc = pen.shape[1]
    bq = tq * n_sub
    assert tk % bq == 0 and PEN_CHUNK % tk == 0 and tq & (tq - 1) == 0
    return pl.pallas_call(
        functools.partial(_nsa_sel_kernel, tq, tk, n_sub),
        out_shape=jax.ShapeDtypeStruct((B, S, W), BF16),
        grid=(B, S // bq),
        in_specs=[pl.BlockSpec((None, bq, W), lambda b, i: (b, i, 0)),
                  pl.BlockSpec((None, n_pc, bq, LANE), lambda b, i: (b, 0, i, 0)),
                  pl.BlockSpec((None, S, 2 * HEAD_DIM), lambda b, i: (b, 0, 0)),
                  pl.BlockSpec((None, S, HEAD_DIM), lambda b, i: (b, 0, 0))],
        out_specs=pl.BlockSpec((None, bq, W), lambda b, i: (b, i, 0)),
        scratch_shapes=[pltpu.VMEM((n_sub, n_pc, NSA_HEADS * tq, 2 * HEAD_DIM), BF16),
                        pltpu.VMEM((n_sub, 2, tk, NSA_HEADS * tq), F32)],
        compiler_params=_cparams(2), name="nsa_selected_attention",
    )(q, pen, k_aug, v)


def _mix_out_kernel(oc_ref, os_ref, ow_ref, gate_ref, ob_ref, od_ref, w_ref, h_ref, g_ref,
                    o_ref, a_ref):
    half = a_ref.shape[0] // 2
    for part in range(2):
        rows = slice(part * half, (part + 1) * half)
        gate = jax.nn.sigmoid(gate_ref[rows, :])
        for h in range(NSA_HEADS):
            hs = slice(h * LANE, (h + 1) * LANE)
            a = (gate[:, 3 * h:3 * h + 1] * oc_ref[rows, hs].astype(F32)
                 + gate[:, 3 * h + 1:3 * h + 2] * os_ref[rows, hs].astype(F32)
                 + gate[:, 3 * h + 2:3 * h + 3] * ow_ref[rows, hs].astype(F32))
            a_ref[rows, hs] = a.astype(a_ref.dtype)
        off = NSA_HEADS * LANE
        a_ref[rows, off:off + MLA_HEADS * LANE] = ob_ref[rows, :]
        off += MLA_HEADS * LANE
        a_ref[rows, off:off + DIL_HEADS * LANE] = od_ref[rows, :]
        y = jnp.dot(a_ref[rows, :], w_ref[...], preferred_element_type=F32)
        o_ref[rows, :] = h_ref[rows, :] + _rms(y, g_ref[...])


def _mix_out(o_c, o_s, o_w, gate, o_b, o_d, w, h, g, bm=512):
    T, N = h.shape
    width = (NSA_HEADS + MLA_HEADS + DIL_HEADS) * LANE
    heads = (o_c, o_s, o_w, gate, o_b, o_d)
    row = lambda a: pl.BlockSpec((bm, a.shape[1]), lambda i: (i, 0))
    return pl.pallas_call(
        _mix_out_kernel,
        out_shape=jax.ShapeDtypeStruct((T, N), F32),
        grid=(T // bm,),
        in_specs=[row(a) for a in heads] + [
            pl.BlockSpec((width, N), lambda i: (0, 0), pipeline_mode=pl.Buffered(1)),
            row(h), pl.BlockSpec((1, N), lambda i: (0, 0))],
        out_specs=row(h),
        scratch_shapes=[pltpu.VMEM((bm, width), BF16)],
        compiler_params=_cparams(1), name="mix_out_proj",
    )(*heads, w, h, g)


def _xattn_kernel(n_heads, scale, h_ref, gq_ref, wq_ref, k_ref, v_ref, wo_ref, go_ref, o_ref,
                  a_ref):
    h = h_ref[...]
    x = _rms(h, gq_ref[...]).astype(BF16)
    q = (jnp.dot(x, wq_ref[...], preferred_element_type=F32) * scale).astype(BF16)
    for hd in range(n_heads):
        hs = slice(hd * LANE, (hd + 1) * LANE)
        s = _dot_t(q[:, hs], k_ref[:, hs])
        m = jnp.max(s, axis=-1, keepdims=True)
        p = jnp.exp2(s - m)
        l = jnp.sum(p, axis=-1, keepdims=True)
        o = jnp.dot(p.astype(BF16), v_ref[:, hs], preferred_element_type=F32) / l
        a_ref[:, hs] = o.astype(a_ref.dtype)
    y = jnp.dot(a_ref[...], wo_ref[...], preferred_element_type=F32)
    o_ref[...] = h + _rms(y, go_ref[...])


def _cross_attention(h, gq, wq, k, v, wo, go, n_heads, scale, bm=512):
    B, S, D = h.shape
    M, W = k.shape[1:]
    const = lambda shape: pl.BlockSpec(shape, lambda b, i: (0, 0), pipeline_mode=pl.Buffered(1))
    return pl.pallas_call(
        functools.partial(_xattn_kernel, n_heads, scale),
        out_shape=jax.ShapeDtypeStruct((B, S, D), F32),
        grid=(B, S // bm),
        in_specs=[pl.BlockSpec((None, bm, D), lambda b, i: (b, i, 0)),
                  const((1, D)), const((D, W)),
                  pl.BlockSpec((None, M, W), lambda b, i: (b, 0, 0)),
                  pl.BlockSpec((None, M, W), lambda b, i: (b, 0, 0)),
                  const((W, D)), const((1, D))],
        out_specs=pl.BlockSpec((None, bm, D), lambda b, i: (b, i, 0)),
        scratch_shapes=[pltpu.VMEM((bm, W), BF16)],
        compiler_params=_cparams(2), name="cross_attention",
    )(h, gq, wq, k, v, wo, go)


def _rope_tables(pos, dim):
    inv = ROPE_THETA ** (-jnp.arange(0, dim, 2, dtype=F32) / dim)
    ang = pos.astype(F32)[..., None] * inv
    return jnp.cos(ang), jnp.sin(ang)


def _full_tables(pos):
    c, s = _rope_tables(pos, HEAD_DIM)
    return jnp.concatenate([c, c], -1), jnp.concatenate([-s, s], -1)


def _spread_rope64(t):
    z = jnp.zeros(t.shape[:-1] + (QK_ROPE // 2,), t.dtype)
    return jnp.concatenate([t[..., :QK_ROPE // 2], z, t[..., QK_ROPE // 2:], z], -1)


def _small_tables(pos):
    c, s = _rope_tables(pos, QK_ROPE)
    return _spread_rope64(jnp.concatenate([c, c], -1)), _spread_rope64(jnp.concatenate([-s, s], -1))


def _layer(h, mem, tabs, tabs_c, onehot, p):
    B, S, D = h.shape
    T = B * S
    dh = HEAD_DIM
    scale = dh ** -0.5 * LOG2E
    h2 = h.reshape(T, D)

    w_in = p["w_in"].T
    cuts = np.cumsum([NSA_HEADS * dh, 6 * dh, 3 * NSA_HEADS, Q_LORA, KV_LORA, QK_ROPE])
    w_q, w_kv, w_g, w_cq, w_ckv, w_kr, w_dil = jnp.split(w_in, cuts, axis=0)
    kv = [w_kv[k * dh:(k + 1) * dh] for k in range(6)]
    dw = DIL_HEADS * dh
    w_dq, w_dk, w_dv = w_dil[:dw], w_dil[dw:2 * dw], w_dil[2 * dw:]
    w_gpad = jnp.pad(w_g, ((0, LANE - w_g.shape[0]), (0, 0)))
    w_all = jnp.concatenate([w_q, kv[2], kv[4], w_dq, w_dk, _spread_rope64(w_kr.T).T,
                             kv[0], kv[1], kv[3], kv[5], w_gpad, w_cq, w_ckv, w_dv], 0).astype(BF16)
    one = lambda mode, n=1, s=1.0: ((mode,) * n, s)
    nsa_q, k_s, k_w, dq, dk, k_pe, k_cr, v_cr, v_s, v_w, gate, cq, ckv, dv = _proj(
        h2, p["g_mix_pre"], w_all,
        [one("rope", NSA_HEADS, scale), one("rope"), one("rope"), one("rope", DIL_HEADS, scale),
         one("rope", DIL_HEADS), one("rope_r"), one("none"), one("none"), one("none"), one("none"),
         one("none"), one("none", Q_LORA // LANE), one("none", KV_LORA // LANE),
         one("none", DIL_HEADS)],
        [BF16] * 10 + [F32, BF16, BF16, BF16], tabs=tabs, w_t=True, name="in_proj")

    n_chunk = S // CMP_STRIDE
    half = CMP_LEN // 2
    pe2 = p["cmp_pos_emb"].reshape(2, half * dh)
    cos_c, sin_c = tabs_c

    def compress(t, w1, w2, rope):
        out = _compress(t.reshape(B, n_chunk, CMP_STRIDE * dh), pe2,
                        w1[:half * dh].astype(BF16), w1[half * dh:].astype(BF16),
                        w2.astype(BF16), cos_c, sin_c, rope)
        return out.reshape(B, n_chunk // 4, 4, dh).transpose(0, 2, 1, 3).reshape(B, n_chunk, dh)

    k_c = compress(k_cr, p["w_cmp_k1"], p["w_cmp_k2"], True)
    v_c = compress(v_cr, p["w_cmp_v1"], p["w_cmp_v2"], False)
    q3 = nsa_q.reshape(B, S, NSA_HEADS * dh)
    o_cmp, pen = _nsa_compressed(q3, k_c, v_c)
    k_aug = jnp.concatenate([k_s.reshape(B, S, dh), onehot], axis=-1)
    o_sel = _nsa_selected(q3, pen, k_aug, v_s.reshape(B, S, dh))
    o_win = _nsa_window_attention(q3, k_w.reshape(B, S, dh), v_w.reshape(B, S, dh), NSA_WINDOW)

    dqk = QK_NOPE + QK_ROPE
    w_uq = p["w_uq"].reshape(Q_LORA, MLA_HEADS, dqk)
    w_uq = jnp.concatenate([w_uq[..., :QK_NOPE], _spread_rope64(w_uq[..., QK_NOPE:])], -1)
    w_uq = w_uq.reshape(Q_LORA, MLA_HEADS * 2 * dh).astype(BF16)
    (q_m,) = _proj(cq, p["g_q_lora"], w_uq, [(("none", "rope_r") * MLA_HEADS, dqk ** -0.5 * LOG2E)],
                   [BF16], tabs=tabs, bm=1024, name="mla_q_up")
    w_ukv = p["w_ukv"].reshape(KV_LORA, MLA_HEADS, 2 * dh)
    w_ukv = jnp.concatenate([w_ukv[..., :dh].reshape(KV_LORA, -1),
                             w_ukv[..., dh:].reshape(KV_LORA, -1)], 1).astype(BF16)
    k_m, v_m = _proj(ckv, p["g_kv_lora"], w_ukv,
                     [(("none", "extra") * MLA_HEADS, 1.0), (("none",) * MLA_HEADS, 1.0)],
                     [BF16, BF16], extra=k_pe, bm=1024, name="mla_kv_up")
    o_mla = _mla_attention(q_m.reshape(B, S, -1), k_m.reshape(B, S, -1),
                           v_m.reshape(B, S, -1), MLA_HEADS)

    o_dil = _dilated_attention(dq.reshape(B, S, dw), dk.reshape(B, S, dw), dv.reshape(B, S, dw),
                               DIL_PATTERNS, DIL_HEADS)

    h2 = _mix_out(o_cmp.reshape(T, -1), o_sel.reshape(T, -1), o_win.reshape(T, -1), gate,
                  o_mla.reshape(T, -1), o_dil.reshape(T, dw), p["w_out"].astype(BF16), h2,
                  p["g_mix_post"])

    xw = XATTN_HEADS * dh
    M = mem.shape[1]
    xk, xv = _proj(mem.reshape(B * M, D), p["g_mem_kv"], p["w_xkv"].astype(BF16),
                   [(("none",) * XATTN_HEADS, 1.0), (("none",) * XATTN_HEADS, 1.0)],
                   [BF16, BF16], bm=min(512, B * M), name="xattn_kv")
    h2 = _cross_attention(h2.reshape(B, S, D), p["g_mem_pre"], p["w_xq"].astype(BF16),
                          xk.reshape(B, M, xw), xv.reshape(B, M, xw), p["w_xo"].astype(BF16),
                          p["g_mem_post"], XATTN_HEADS, scale).reshape(T, D)

    up = _mlp_up(h2, p["g_mlp_pre"], p["w_up_all"], p["layer"])
    h2 = _out_proj(up, p["w_down_all"], h2, p["g_mlp_post"], layer=p["layer"], bm=1024, bk=1024,
                   name="mlp_down")
    return h2.reshape(B, S, D)


_LAYER_PARAMS = ("g_mix_pre", "w_in", "cmp_pos_emb", "w_cmp_k1", "w_cmp_k2", "w_cmp_v1", "w_cmp_v2",
                 "g_q_lora", "g_kv_lora", "w_uq", "w_ukv", "w_out", "g_mix_post", "g_mem_pre",
                 "g_mem_kv", "w_xq", "w_xkv", "w_xo", "g_mem_post", "g_mlp_pre", "w_up", "w_down",
                 "g_mlp_post")


def kernel(x, mem, positions, g_mix_pre, w_in, cmp_pos_emb, w_cmp_k1, w_cmp_k2, w_cmp_v1, w_cmp_v2, g_q_lora, g_kv_lora, w_uq, w_ukv, w_out, g_mix_post, g_mem_pre, g_mem_kv, w_xq, w_xkv, w_xo, g_mem_post, g_mlp_pre, w_up, w_down, g_mlp_post):
    stacked = dict(zip(_LAYER_PARAMS, (
        g_mix_pre, w_in, cmp_pos_emb, w_cmp_k1, w_cmp_k2, w_cmp_v1, w_cmp_v2, g_q_lora, g_kv_lora,
        w_uq, w_ukv, w_out, g_mix_post, g_mem_pre, g_mem_kv, w_xq, w_xkv, w_xo, g_mem_post,
        g_mlp_pre, w_up, w_down, g_mlp_post)))
    B, S, D = x.shape
    T = B * S
    assert S % PEN_CHUNK == 0
    cosf, sinf = _full_tables(positions)
    cosr, sinr = _small_tables(positions)
    tabs = tuple(t.reshape(T, LANE) for t in (cosf, sinf, cosr, sinr))
    n_chunk = S // CMP_STRIDE
    end = jnp.minimum(jnp.arange(n_chunk) * CMP_STRIDE + CMP_LEN - 1, S - 1)
    tabs_c = _full_tables(positions[:, end])
    blk = (jnp.arange(S) // SLC_BLOCK) % LANE
    onehot = jnp.broadcast_to((blk[:, None] == jnp.arange(LANE)[None, :]).astype(BF16)[None],
                              (B, S, LANE))
    h = x
    w_up_all, w_down_all = w_up.astype(BF16), w_down.astype(BF16)
    for layer in range(stacked["w_in"].shape[0]):
        p = {"layer": layer, "w_up_all": w_up_all, "w_down_all": w_down_all}
        for name, val in stacked.items():
            if name not in ("w_up", "w_down"):
                v = val[layer]
                p[name] = v[None, :] if name.startswith("g_") else v
        h = _layer(h, mem, tabs, tabs_c, onehot, p)
    return h
```

```python
import functools

import numpy as np
import jax
import jax.numpy as jnp
from jax import lax
from jax.experimental import pallas as pl
from jax.experimental.pallas import tpu as pltpu

F32 = jnp.float32
BF16 = jnp.bfloat16

LANE = 128
VMEM_LIMIT = 56 * 1024 * 1024

HEAD_DIM = 128
ROPE_THETA = 10000.0
NORM_EPS = 1e-6
NEG_INF = -1e30
BIG = 1e9
LOG2E = 1.4426950408889634
NSA_HEADS = 4
MLA_HEADS = 6
DIL_HEADS = 6
CMP_LEN = 32
CMP_STRIDE = 16
SLC_BLOCK = 64
SLC_TOPK = 16
NSA_WINDOW = 512
Q_LORA = 512
KV_LORA = 512
QK_NOPE = 128
QK_ROPE = 64
DIL_PATTERNS = ((128, 1), (512, 4), (2048, 16))
XATTN_HEADS = 4
PEN_CHUNK = LANE * SLC_BLOCK


def _cparams(n_grid):
    return pltpu.CompilerParams(
        dimension_semantics=("arbitrary",) * n_grid, vmem_limit_bytes=VMEM_LIMIT)


def _rms(x, g):
    return x * lax.rsqrt(jnp.mean(x * x, axis=-1, keepdims=True) + NORM_EPS) * g


def _dot_t(a, b):
    return lax.dot_general(a, b, (((1,), (1,)), ((), ())), preferred_element_type=F32)


def _proj_kernel(out_plan, has_norm, n_tab, has_extra, w_t, chunk, *refs):
    x_ref, g_ref, w_ref = refs[:3]
    tab_refs = refs[3:3 + n_tab]
    n_in = 3 + n_tab + int(has_extra)
    out_refs = refs[n_in:]
    x = x_ref[...].astype(F32)
    if has_norm:
        x = _rms(x, g_ref[...])
    xb = x.astype(BF16)
    n_cols = w_ref.shape[0 if w_t else 1]
    flat = []
    for oi, (modes, scale) in enumerate(out_plan):
        for k, mode in enumerate(modes):
            if mode == "extra":
                out_refs[oi][:, k * LANE:(k + 1) * LANE] = refs[n_in - 1][...]
            else:
                flat.append((oi, k * LANE, mode, scale))
    for c0 in range(0, n_cols, chunk):
        c1 = min(c0 + chunk, n_cols)
        if w_t:
            acc = _dot_t(xb, w_ref[c0:c1, :])
        else:
            acc = jnp.dot(xb, w_ref[:, c0:c1], preferred_element_type=F32)
        for s in range((c1 - c0) // LANE):
            oi, off, mode, scale = flat[c0 // LANE + s]
            a = acc[:, s * LANE:(s + 1) * LANE]
            if mode == "rope":
                a = a * tab_refs[0][...] + pltpu.roll(a, LANE // 2, 1) * tab_refs[1][...]
            elif mode == "rope_r":
                a = a * tab_refs[2][...] + pltpu.roll(a, LANE // 2, 1) * tab_refs[3][...]
            if scale != 1.0:
                a = a * scale
            out_refs[oi][:, off:off + LANE] = a.astype(out_refs[oi].dtype)


def _proj(x, g, w, out_plan, out_dtypes, tabs=(), extra=None, w_t=False, bm=512, chunk=512,
          name="proj"):
    T, K = x.shape
    N = w.shape[0 if w_t else 1]
    assert T % bm == 0 and N % LANE == 0
    assert sum(sum(md != "extra" for md in m) for m, _ in out_plan) * LANE == N
    has_norm = g is not None
    if g is None:
        g = jnp.ones((1, K), F32)
    extras = () if extra is None else (extra,)
    in_specs = [pl.BlockSpec((bm, K), lambda i: (i, 0)),
                pl.BlockSpec((1, K), lambda i: (0, 0)),
                pl.BlockSpec(w.shape, lambda i: (0, 0), pipeline_mode=pl.Buffered(1))]
    in_specs += [pl.BlockSpec((bm, LANE), lambda i: (i, 0)) for _ in tabs + extras]
    out_shape = [jax.ShapeDtypeStruct((T, len(m) * LANE), dt)
                 for (m, _), dt in zip(out_plan, out_dtypes)]
    out_specs = [pl.BlockSpec((bm, len(m) * LANE), lambda i: (i, 0)) for m, _ in out_plan]
    return pl.pallas_call(
        functools.partial(_proj_kernel, out_plan, has_norm, len(tabs), extra is not None, w_t,
                          chunk),
        out_shape=out_shape, grid=(T // bm,), in_specs=in_specs, out_specs=out_specs,
        compiler_params=_cparams(1), name=name,
    )(x, g, w, *tabs, *extras)


def _mlp_up_kernel(x_ref, g_ref, w_ref, o_ref, xn_ref):
    @pl.when(pl.program_id(1) == 0)
    def _():
        xn_ref[...] = _rms(x_ref[...], g_ref[...]).astype(BF16)
    a = jnp.dot(xn_ref[...], w_ref[...], preferred_element_type=F32)
    a = jnp.maximum(a, 0.0)
    o_ref[...] = (a * a).astype(o_ref.dtype)


def _mlp_up(x, g, w, layer, bm=1024, bn=2048):
    T, K = x.shape
    N = w.shape[2]
    return pl.pallas_call(
        _mlp_up_kernel,
        out_shape=jax.ShapeDtypeStruct((T, N), BF16),
        grid=(T // bm, N // bn),
        in_specs=[pl.BlockSpec((bm, K), lambda i, j: (i, 0)),
                  pl.BlockSpec((1, K), lambda i, j: (0, 0)),
                  pl.BlockSpec((None, K, bn), lambda i, j: (layer, 0, j))],
        out_specs=pl.BlockSpec((bm, bn), lambda i, j: (i, j)),
        scratch_shapes=[pltpu.VMEM((bm, K), BF16)],
        compiler_params=_cparams(2), name="mlp_up",
    )(x, g, w)


def _out_proj_kernel(n_k, a_ref, w_ref, h_ref, g_ref, o_ref):
    k = pl.program_id(1)

    def part(rows=slice(None)):
        return jnp.dot(a_ref[rows, :], w_ref[...], preferred_element_type=F32)

    def finish(first):
        half = o_ref.shape[0] // 2
        for r in range(2):
            rows = slice(r * half, (r + 1) * half)
            y = part(rows) if first else o_ref[rows, :] + part(rows)
            o_ref[rows, :] = h_ref[rows, :] + _rms(y, g_ref[...])

    if n_k == 1:
        finish(True)
    else:
        @pl.when(k == 0)
        def _():
            o_ref[...] = part()

        @pl.when(jnp.logical_and(k > 0, k < n_k - 1))
        def _():
            o_ref[...] += part()

        @pl.when(k == n_k - 1)
        def _():
            finish(False)


def _out_proj(a, w, h, g, layer, bm=512, bk=2048, name="out_proj"):
    T, K = a.shape
    N = w.shape[2]
    bk = min(bk, K)
    n_k = K // bk
    return pl.pallas_call(
        functools.partial(_out_proj_kernel, n_k),
        out_shape=jax.ShapeDtypeStruct((T, N), F32),
        grid=(T // bm, n_k),
        in_specs=[pl.BlockSpec((bm, bk), lambda i, k: (i, k)),
                  pl.BlockSpec((None, bk, N), lambda i, k: (layer, k, 0)),
                  pl.BlockSpec((bm, N), lambda i, k: (i, 0)),
                  pl.BlockSpec((1, N), lambda i, k: (0, 0))],
        out_specs=pl.BlockSpec((bm, N), lambda i, k: (i, 0)),
        compiler_params=_cparams(2), name=name,
    )(a, w, h, g)


def _online_step(carry, st, st_max, v):
    m, l, acc = carry
    m_new = jnp.maximum(m, st_max)
    alpha = jnp.exp2(m - m_new)
    p = jnp.exp2(st - m_new)
    l = alpha * l + jnp.sum(p, axis=0, keepdims=True)
    pv = lax.dot_general(v, p.astype(BF16), (((0,), (0,)), ((), ())), preferred_element_type=F32)
    return m_new, l, alpha * acc + pv


def _online_init(cols, dv):
    return (jnp.full((1, cols), NEG_INF, F32), jnp.zeros((1, cols), F32),
            jnp.zeros((dv, cols), F32))


def _flash_transposed(n_full, n_groups, scores, diag_mask, values, st_ref):
    groups = range(n_groups)

    def put(t, slot, masked=False):
        maxes = []
        for g in groups:
            st = scores(g, t)
            if masked:
                st = diag_mask(g, st)
            st_ref[g, slot] = st
            maxes.append(jnp.max(st, axis=0, keepdims=True))
        return tuple(maxes)

    def tile_at(step):
        return jnp.where(step == 0, n_full, step - 1)

    def softmax_step(step, stats, st_max, slot):
        t = tile_at(step)
        return tuple(_online_step(stats[g], st_ref[g, slot], st_max[g], values(g, t))
                     for g in groups)

    def half(step, carry, slot):
        stats, st_max = carry
        nxt = put(step, 1 - slot)
        return softmax_step(step, stats, st_max, slot), nxt

    def pair(u, carry):
        return half(2 * u + 1, half(2 * u, carry, 0), 1)

    init = tuple(_online_init(st_ref.shape[3], HEAD_DIM) for g in groups)
    carry = lax.fori_loop(0, n_full // 2, pair, (init, put(n_full, 0, masked=True)))
    odd = n_full & 1
    stats, st_max = lax.fori_loop(0, odd, lambda _, c: half(n_full - 1, c, 0), carry)
    stats = softmax_step(n_full, stats, st_max, odd)
    return [acc / l for _, l, acc in stats]


def _mla_kernel(tq, tk, hg, q_ref, k_ref, v_ref, o_ref, st_ref):
    t0 = pl.program_id(2) * tq
    dq = q_ref.shape[1] // hg
    last = t0 // tk

    def scores(g, t):
        start = pl.multiple_of(t * tk, tk)
        cols = slice(g * dq, (g + 1) * dq)
        return _dot_t(k_ref[pl.ds(start, tk), cols], q_ref[:, cols])

    def values(g, t):
        start = pl.multiple_of(t * tk, tk)
        return v_ref[pl.ds(start, tk), g * HEAD_DIM:(g + 1) * HEAD_DIM]

    def diag_mask(g, st):
        key = lax.broadcasted_iota(jnp.int32, (tk, tq), 0) + last * tk
        qry = lax.broadcasted_iota(jnp.int32, (tk, tq), 1) + t0
        return jnp.where(key <= qry, st, NEG_INF)

    outs = _flash_transposed(last, hg, scores, diag_mask, values, st_ref)
    for g in range(hg):
        o_ref[:, g * HEAD_DIM:(g + 1) * HEAD_DIM] = outs[g].T.astype(o_ref.dtype)


def _mla_attention(q, k, v, n_heads, tq=1024, tk=1024, hg=2):
    B, S, _ = q.shape
    dq = q.shape[2] // n_heads
    assert tk % tq == 0 and S % tk == 0 and n_heads % hg == 0
    resident = pl.Buffered(1)
    return pl.pallas_call(
        functools.partial(_mla_kernel, tq, tk, hg),
        out_shape=jax.ShapeDtypeStruct((B, S, n_heads * HEAD_DIM), BF16),
        grid=(B, n_heads // hg, S // tq),
        in_specs=[pl.BlockSpec((None, tq, hg * dq), lambda b, h, i: (b, i, h)),
                  pl.BlockSpec((None, S, hg * dq), lambda b, h, i: (b, 0, h),
                               pipeline_mode=resident),
                  pl.BlockSpec((None, S, hg * HEAD_DIM), lambda b, h, i: (b, 0, h),
                               pipeline_mode=resident)],
        out_specs=pl.BlockSpec((None, tq, hg * HEAD_DIM), lambda b, h, i: (b, i, h)),
        scratch_shapes=[pltpu.VMEM((hg, 2, tk, tq), F32)],
        compiler_params=_cparams(3), name="mla_attention",
    )(q, k, v)


def _band_bias(R, window, base=None):
    C = window + LANE
    r = lax.broadcasted_iota(jnp.int32, (R, C), 0) & (LANE - 1)
    c = lax.broadcasted_iota(jnp.int32, (R, C), 1)
    mask = (c >= r) & (c <= r + window)
    if base is not None:
        mask = mask & (c >= window - base)
    return jnp.where(mask, 0.0, NEG_INF)


def _band_block(q, kwin, vwin, bias):
    s = _dot_t(q, kwin) + bias
    m = jnp.max(s, axis=-1, keepdims=True)
    p = jnp.exp2(s - m)
    l = jnp.sum(p, axis=-1, keepdims=True)
    o = jnp.dot(p.astype(BF16), vwin, preferred_element_type=F32) / l
    return o, m + jnp.log2(l)


def _stack_heads(x, n):
    return jnp.concatenate([x[:, h * LANE:(h + 1) * LANE] for h in range(n)], axis=0)


def _nsa_window_kernel(tq, window, q_ref, kp_ref, kc_ref, vp_ref, vc_ref, o_ref):
    i = pl.program_id(1)
    R = NSA_HEADS * LANE

    def run(first_step):
        kwin = jnp.concatenate([kp_ref[...], kc_ref[...]], axis=0)
        vwin = jnp.concatenate([vp_ref[...], vc_ref[...]], axis=0)
        inner = _band_bias(R, window)
        for j in range(tq // LANE):
            q = _stack_heads(q_ref[j * LANE:(j + 1) * LANE, :], NSA_HEADS)
            lo = j * LANE
            bias = _band_bias(R, window, lo) if first_step and lo < window else inner
            o, _ = _band_block(q, kwin[lo:lo + window + LANE], vwin[lo:lo + window + LANE], bias)
            for h in range(NSA_HEADS):
                o_ref[lo:lo + LANE, h * LANE:(h + 1) * LANE] = (
                    o[h * LANE:(h + 1) * LANE].astype(o_ref.dtype))

    pl.when(i == 0)(lambda: run(True))
    pl.when(i > 0)(lambda: run(False))


def _nsa_window_attention(q, k, v, window):
    B, S, W = q.shape
    tq = window
    prev = lambda b, i: (b, jnp.maximum(i - 1, 0), 0)
    cur = lambda b, i: (b, i, 0)
    return pl.pallas_call(
        functools.partial(_nsa_window_kernel, tq, window),
        out_shape=jax.ShapeDtypeStruct((B, S, W), BF16),
        grid=(B, S // tq),
        in_specs=[pl.BlockSpec((None, tq, W), cur),
                  pl.BlockSpec((None, window, HEAD_DIM), prev),
                  pl.BlockSpec((None, tq, HEAD_DIM), cur),
                  pl.BlockSpec((None, window, HEAD_DIM), prev),
                  pl.BlockSpec((None, tq, HEAD_DIM), cur)],
        out_specs=pl.BlockSpec((None, tq, W), cur),
        compiler_params=_cparams(2), name="nsa_window_attention",
    )(q, k, k, v, v)


def _dilated_kernel(nb, patterns, q_ref, kp_ref, kc_ref, vp_ref, vc_ref, o_ref,
                    qf_ref, kf_ref, vf_ref, of_ref, lf_ref):
    i = pl.program_id(2)
    qf_ref[...] = q_ref[...].astype(F32)
    kf_ref[0:nb, :] = kp_ref[...].astype(F32)
    kf_ref[nb:2 * nb, :] = kc_ref[...].astype(F32)
    vf_ref[0:nb, :] = vp_ref[...].astype(F32)
    vf_ref[nb:2 * nb, :] = vc_ref[...].astype(F32)
    for pi, (window, dil) in enumerate(patterns):
        w = window // dil
        per_class = nb // dil
        inner = _band_bias(LANE, w)
        first = _band_bias(LANE, w, i * per_class)
        for r in range(dil):
            for j in range(per_class // LANE):
                q_lo = r + j * LANE * dil
                k_lo = nb + q_lo - w * dil
                rows_q = pl.ds(q_lo, LANE, stride=dil)
                rows_k = pl.ds(k_lo, w + LANE, stride=dil)
                o, lse = _band_block(qf_ref[rows_q, :].astype(BF16),
                                     kf_ref[rows_k, :].astype(BF16),
                                     vf_ref[rows_k, :].astype(BF16),
                                     first if j == 0 else inner)
                of_ref[pi, rows_q, :] = o
                lf_ref[pi, rows_q, :] = jnp.broadcast_to(lse, (LANE, LANE))
    n_pat = len(patterns)
    mx = lf_ref[0]
    for pi in range(1, n_pat):
        mx = jnp.maximum(mx, lf_ref[pi])
    es = [jnp.exp2(lf_ref[pi] - mx) for pi in range(n_pat)]
    den = es[0]
    num = es[0] * of_ref[0]
    for pi in range(1, n_pat):
        den = den + es[pi]
        num = num + es[pi] * of_ref[pi]
    o_ref[...] = (num / den).astype(o_ref.dtype)


def _dilated_attention(q, k, v, patterns, n_heads, nb=2048):
    B, S, W = q.shape
    for window, dil in patterns:
        assert window % dil == 0 and window // dil == LANE
        assert window <= nb and nb % (dil * LANE) == 0
    assert S % nb == 0
    prev = lambda b, h, i: (b, jnp.maximum(i - 1, 0), h)
    cur = lambda b, h, i: (b, i, h)
    blk = lambda index_map: pl.BlockSpec((None, nb, LANE), index_map)
    return pl.pallas_call(
        functools.partial(_dilated_kernel, nb, patterns),
        out_shape=jax.ShapeDtypeStruct((B, S, W), BF16),
        grid=(B, n_heads, S // nb),
        in_specs=[blk(cur), blk(prev), blk(cur), blk(prev), blk(cur)],
        out_specs=blk(cur),
        scratch_shapes=[pltpu.VMEM((nb, LANE), F32),
                        pltpu.VMEM((2 * nb, LANE), F32),
                        pltpu.VMEM((2 * nb, LANE), F32),
                        pltpu.VMEM((len(patterns), nb, LANE), F32),
                        pltpu.VMEM((len(patterns), nb, LANE), F32)],
        compiler_params=_cparams(3), name="dilated_attention",
    )(q, k, k, v, v)


def _gelu_tanh(x):
    return 0.5 * x * (1.0 + jnp.tanh(0.7978845608028654 * (x + 0.044715 * (x * x * x))))


def _compress_kernel(rope, t_ref, pe_ref, w1a_ref, w1b_ref, w2_ref, cos_ref, sin_ref, o_ref):
    t = t_ref[...].astype(F32)
    n = t.shape[0]
    first = jnp.dot((t + pe_ref[0:1, :]).astype(BF16), w1a_ref[...], preferred_element_type=F32)
    second = jnp.dot((t + pe_ref[1:2, :]).astype(BF16), w1b_ref[...], preferred_element_type=F32)
    hid = first + pltpu.roll(second, n - 1, 0)
    out = jnp.dot(_gelu_tanh(hid).astype(BF16), w2_ref[...], preferred_element_type=F32)
    if rope:
        out = out * cos_ref[...] + pltpu.roll(out, LANE // 2, 1) * sin_ref[...]
    o_ref[...] = out.astype(o_ref.dtype)


def _compress(t, pe2, w1a, w1b, w2, cos_c, sin_c, rope):
    B, n, K = t.shape
    whole = lambda b: (0, 0)
    per_b = lambda b: (b, 0, 0)
    return pl.pallas_call(
        functools.partial(_compress_kernel, rope),
        out_shape=jax.ShapeDtypeStruct((B, n, HEAD_DIM), BF16),
        grid=(B,),
        in_specs=[pl.BlockSpec((None, n, K), per_b),
                  pl.BlockSpec((2, K), whole),
                  pl.BlockSpec((K, HEAD_DIM), whole),
                  pl.BlockSpec((K, HEAD_DIM), whole),
                  pl.BlockSpec((HEAD_DIM, HEAD_DIM), whole),
                  pl.BlockSpec((None, n, HEAD_DIM), per_b),
                  pl.BlockSpec((None, n, HEAD_DIM), per_b)],
        out_specs=pl.BlockSpec((None, n, HEAD_DIM), per_b),
        compiler_params=_cparams(1), name="nsa_compress",
    )(t, pe2, w1a, w1b, w2, cos_c, sin_c)


def _nsa_cmp_block(tq, n_slc, n_sel, t0, q, kc, vc):
    n_cmp = kc.shape[0]
    per_blk = SLC_BLOCK // CMP_STRIDE
    slc_shift = n_slc.bit_length() - 1
    q = _stack_heads(q, NSA_HEADS)
    s = _dot_t(q, kc)
    R = NSA_HEADS * tq
    row = lax.broadcasted_iota(jnp.int32, (R, n_cmp), 0) & (tq - 1)
    col = lax.broadcasted_iota(jnp.int32, (R, n_cmp), 1)
    j_of = col & (n_slc - 1)
    r_of = col >> slc_shift
    cmp_end = (per_blk * j_of + r_of) * CMP_STRIDE + (CMP_LEN - 1)
    cmask = cmp_end <= row + t0
    s = jnp.where(cmask, s, NEG_INF)
    m = jnp.max(s, axis=-1, keepdims=True)
    e = jnp.exp2(s - m)
    p = jnp.where(cmask, e / jnp.sum(e, axis=-1, keepdims=True), 0.0)
    o = jnp.dot(p.astype(BF16), vc, preferred_element_type=F32)

    ph = p[0:tq]
    for h in range(1, NSA_HEADS):
        ph = ph + p[h * tq:(h + 1) * tq]
    groups = [ph[:, r * n_slc:(r + 1) * n_slc] for r in range(per_blk)]
    blk = lax.broadcasted_iota(jnp.int32, (tq, n_slc), 1)
    spill = jnp.where(blk == 0, 0.0, pltpu.roll(groups[per_blk - 1], 1, 1))
    imp = groups[0]
    for r in range(1, per_blk):
        imp = imp + groups[r]
    imp = imp + spill
    tpos = lax.broadcasted_iota(jnp.int32, (tq, n_slc), 0) + t0
    cur = tpos >> (SLC_BLOCK.bit_length() - 1)
    forced = (blk == 0) | (blk == cur) | (blk == cur - 1)
    valid = blk <= cur
    work = jnp.where(forced, BIG, jnp.where(valid, imp, -BIG))
    blk_f = blk.astype(F32)
    sel = jnp.zeros((tq, n_slc), F32)
    for _ in range(n_sel):
        mx = jnp.max(work, axis=-1, keepdims=True)
        first = jnp.min(jnp.where(work == mx, blk_f, float(n_slc)), axis=-1, keepdims=True)
        pick = blk_f == first
        sel = jnp.where(pick, 1.0, sel)
        work = jnp.where(pick, -jnp.inf, work)
    return o, jnp.where((sel > 0.0) & valid, 0.0, NEG_INF)


def _nsa_cmp_kernel(tq, n_sub, n_slc, n_sel, q_ref, kc_ref, vc_ref, o_ref, pen_ref):
    i = pl.program_id(1)
    per_blk = SLC_BLOCK // CMP_STRIDE

    def run(n_j):
        take = lambda ref: jnp.concatenate(
            [ref[r * n_slc:r * n_slc + n_j, :] for r in range(per_blk)], axis=0)
        kc, vc = (kc_ref[...], vc_ref[...]) if n_j == n_slc else (take(kc_ref), take(vc_ref))
        for u in range(n_sub):
            rows = slice(u * tq, (u + 1) * tq)
            o, pen = _nsa_cmp_block(tq, n_j, min(n_sel, n_j), (i * n_sub + u) * tq,
                                    q_ref[rows, :], kc, vc)
            for h in range(NSA_HEADS):
                o_ref[rows, h * LANE:(h + 1) * LANE] = o[h * tq:(h + 1) * tq].astype(o_ref.dtype)
            pen = pen.astype(pen_ref.dtype)
            for c in range(n_slc // LANE):
                if c < n_j // LANE:
                    pen_ref[c, rows, :] = pen[:, c * LANE:(c + 1) * LANE]
                else:
                    pen_ref[c, rows, :] = jnp.full((tq, LANE), NEG_INF, pen_ref.dtype)

    half = n_slc // 2
    if half % LANE == 0:
        early = (i + 1) * (tq * n_sub) <= half * SLC_BLOCK
        pl.when(early)(lambda: run(half))
        pl.when(jnp.logical_not(early))(lambda: run(n_slc))
    else:
        run(n_slc)


def _nsa_compressed(q, k_c, v_c, tq=128, n_sub=4):
    B, S, W = q.shape
    n_cmp = k_c.shape[1]
    n_slc = S // SLC_BLOCK
    n_sel = min(SLC_TOPK, n_slc)
    assert n_slc % LANE == 0 and n_slc & (n_slc - 1) == 0
    assert n_cmp == 4 * n_slc and tq & (tq - 1) == 0
    n_pc = n_slc // LANE
    bq = tq * n_sub
    return pl.pallas_call(
        functools.partial(_nsa_cmp_kernel, tq, n_sub, n_slc, n_sel),
        out_shape=[jax.ShapeDtypeStruct((B, S, W), BF16),
                   jax.ShapeDtypeStruct((B, n_pc, S, LANE), BF16)],
        grid=(B, S // bq),
        in_specs=[pl.BlockSpec((None, bq, W), lambda b, i: (b, i, 0)),
                  pl.BlockSpec((None, n_cmp, HEAD_DIM), lambda b, i: (b, 0, 0)),
                  pl.BlockSpec((None, n_cmp, HEAD_DIM), lambda b, i: (b, 0, 0))],
        out_specs=[pl.BlockSpec((None, bq, W), lambda b, i: (b, i, 0)),
                   pl.BlockSpec((None, n_pc, bq, LANE), lambda b, i: (b, 0, i, 0))],
        compiler_params=_cparams(2), name="nsa_compressed_topk",
    )(q, k_c, v_c)


def _nsa_sel_kernel(tq, tk, n_sub, q_ref, pen_ref, k_ref, v_ref, o_ref, qa_ref, st_ref):
    t0 = pl.program_id(1) * (tq * n_sub)
    R = NSA_HEADS * tq
    last = t0 // tk
    tiles_per_chunk = PEN_CHUNK // tk
    for u in range(n_sub):
        rows = slice(u * tq, (u + 1) * tq)
        q = _stack_heads(q_ref[rows, :], NSA_HEADS)
        for c in range(pen_ref.shape[0]):
            qa_ref[u, c, :, 0:LANE] = q
            qa_ref[u, c, :, LANE:2 * LANE] = jnp.concatenate([pen_ref[c, rows, :]] * NSA_HEADS,
                                                             axis=0)

    def scores(u, t):
        start = pl.multiple_of(t * tk, tk)
        return _dot_t(k_ref[pl.ds(start, tk), :], qa_ref[u, t // tiles_per_chunk])

    def diag_mask(u, st):
        key = lax.broadcasted_iota(jnp.int32, (tk, R), 0) + last * tk
        qry = (lax.broadcasted_iota(jnp.int32, (tk, R), 1) & (tq - 1)) + (t0 + u * tq)
        return jnp.where(key <= qry, st, NEG_INF)

    def values(u, t):
        return v_ref[pl.ds(pl.multiple_of(t * tk, tk), tk), :]

    outs = _flash_transposed(last, n_sub, scores, diag_mask, values, st_ref)
    for u in range(n_sub):
        for h in range(NSA_HEADS):
            o_ref[u * tq:(u + 1) * tq, h * LANE:(h + 1) * LANE] = (
                outs[u][:, h * tq:(h + 1) * tq].T.astype(o_ref.dtype))


def _nsa_selected(q, pen, k_aug, v, tq=128, n_sub=4, tk=1024):
    B, S, W = q.shape
    n_pc = pen.shape[1]
    bq = tq * n_sub
    assert tk % bq == 0 and PEN_CHUNK % tk == 0 and tq & (tq - 1) == 0
    return pl.pallas_call(
        functools.partial(_nsa_sel_kernel, tq, tk, n_sub),
        out_shape=jax.ShapeDtypeStruct((B, S, W), BF16),
        grid=(B, S // bq),
        in_specs=[pl.BlockSpec((None, bq, W), lambda b, i: (b, i, 0)),
                  pl.BlockSpec((None, n_pc, bq, LANE), lambda b, i: (b, 0, i, 0)),
                  pl.BlockSpec((None, S, 2 * HEAD_DIM), lambda b, i: (b, 0, 0)),
                  pl.BlockSpec((None, S, HEAD_DIM), lambda b, i: (b, 0, 0))],
        out_specs=pl.BlockSpec((None, bq, W), lambda b, i: (b, i, 0)),
        scratch_shapes=[pltpu.VMEM((n_sub, n_pc, NSA_HEADS * tq, 2 * HEAD_DIM), BF16),
                        pltpu.VMEM((n_sub, 2, tk, NSA_HEADS * tq), F32)],
        compiler_params=_cparams(2), name="nsa_selected_attention",
    )(q, pen, k_aug, v)


def _mix_out_kernel(oc_ref, os_ref, ow_ref, gate_ref, ob_ref, od_ref, w_ref, h_ref, g_ref,
                    o_ref, a_ref):
    half = a_ref.shape[0] // 2
    for part in range(2):
        rows = slice(part * half, (part + 1) * half)
        gate = jax.nn.sigmoid(gate_ref[rows, :])
        for h in range(NSA_HEADS):
            hs = slice(h * LANE, (h + 1) * LANE)
            a = (gate[:, 3 * h:3 * h + 1] * oc_ref[rows, hs].astype(F32)
                 + gate[:, 3 * h + 1:3 * h + 2] * os_ref[rows, hs].astype(F32)
                 + gate[:, 3 * h + 2:3 * h + 3] * ow_ref[rows, hs].astype(F32))
            a_ref[rows, hs] = a.astype(a_ref.dtype)
        off = NSA_HEADS * LANE
        a_ref[rows, off:off + MLA_HEADS * LANE] = ob_ref[rows, :]
        off += MLA_HEADS * LANE
        a_ref[rows, off:off + DIL_HEADS * LANE] = od_ref[rows, :]
        y = jnp.dot(a_ref[rows, :], w_ref[...], preferred_element_type=F32)
        o_ref[rows, :] = h_ref[rows, :] + _rms(y, g_ref[...])


def _mix_out(o_c, o_s, o_w, gate, o_b, o_d, w, h, g, bm=512):
    T, N = h.shape
    width = (NSA_HEADS + MLA_HEADS + DIL_HEADS) * LANE
    heads = (o_c, o_s, o_w, gate, o_b, o_d)
    row = lambda a: pl.BlockSpec((bm, a.shape[1]), lambda i: (i, 0))
    return pl.pallas_call(
        _mix_out_kernel,
        out_shape=jax.ShapeDtypeStruct((T, N), F32),
        grid=(T // bm,),
        in_specs=[row(a) for a in heads] + [
            pl.BlockSpec((width, N), lambda i: (0, 0), pipeline_mode=pl.Buffered(1)),
            row(h), pl.BlockSpec((1, N), lambda i: (0, 0))],
        out_specs=row(h),
        scratch_shapes=[pltpu.VMEM((bm, width), BF16)],
        compiler_params=_cparams(1), name="mix_out_proj",
    )(*heads, w, h, g)


def _xattn_kernel(n_heads, scale, h_ref, gq_ref, wq_ref, k_ref, v_ref, wo_ref, go_ref, o_ref,
                  a_ref):
    h = h_ref[...]
    x = _rms(h, gq_ref[...]).astype(BF16)
    q = (jnp.dot(x, wq_ref[...], preferred_element_type=F32) * scale).astype(BF16)
    for hd in range(n_heads):
        hs = slice(hd * LANE, (hd + 1) * LANE)
        s = _dot_t(q[:, hs], k_ref[:, hs])
        m = jnp.max(s, axis=-1, keepdims=True)
        p = jnp.exp2(s - m)
        l = jnp.sum(p, axis=-1, keepdims=True)
        o = jnp.dot(p.astype(BF16), v_ref[:, hs], preferred_element_type=F32) / l
        a_ref[:, hs] = o.astype(a_ref.dtype)
    y = jnp.dot(a_ref[...], wo_ref[...], preferred_element_type=F32)
    o_ref[...] = h + _rms(y, go_ref[...])


def _cross_attention(h, gq, wq, k, v, wo, go, n_heads, scale, bm=512):
    B, S, D = h.shape
    M, W = k.shape[1:]
    const = lambda shape: pl.BlockSpec(shape, lambda b, i: (0, 0), pipeline_mode=pl.Buffered(1))
    return pl.pallas_call(
        functools.partial(_xattn_kernel, n_heads, scale),
        out_shape=jax.ShapeDtypeStruct((B, S, D), F32),
        grid=(B, S // bm),
        in_specs=[pl.BlockSpec((None, bm, D), lambda b, i: (b, i, 0)),
                  const((1, D)), const((D, W)),
                  pl.BlockSpec((None, M, W), lambda b, i: (b, 0, 0)),
                  pl.BlockSpec((None, M, W), lambda b, i: (b, 0, 0)),
                  const((W, D)), const((1, D))],
        out_specs=pl.BlockSpec((None, bm, D), lambda b, i: (b, i, 0)),
        scratch_shapes=[pltpu.VMEM((bm, W), BF16)],
        compiler_params=_cparams(2), name="cross_attention",
    )(h, gq, wq, k, v, wo, go)


def _rope_tables(pos, dim):
    inv = ROPE_THETA ** (-jnp.arange(0, dim, 2, dtype=F32) / dim)
    ang = pos.astype(F32)[..., None] * inv
    return jnp.cos(ang), jnp.sin(ang)


def _full_tables(pos):
    c, s = _rope_tables(pos, HEAD_DIM)
    return jnp.concatenate([c, c], -1), jnp.concatenate([-s, s], -1)


def _spread_rope64(t):
    z = jnp.zeros(t.shape[:-1] + (QK_ROPE // 2,), t.dtype)
    return jnp.concatenate([t[..., :QK_ROPE // 2], z, t[..., QK_ROPE // 2:], z], -1)


def _small_tables(pos):
    c, s = _rope_tables(pos, QK_ROPE)
    return _spread_rope64(jnp.concatenate([c, c], -1)), _spread_rope64(jnp.concatenate([-s, s], -1))


def _layer(h, mem, tabs, tabs_c, onehot, p):
    B, S, D = h.shape
    T = B * S
    dh = HEAD_DIM
    scale = dh ** -0.5 * LOG2E
    h2 = h.reshape(T, D)

    w_in = p["w_in"].T
    cuts = np.cumsum([NSA_HEADS * dh, 6 * dh, 3 * NSA_HEADS, Q_LORA, KV_LORA, QK_ROPE])
    w_q, w_kv, w_g, w_cq, w_ckv, w_kr, w_dil = jnp.split(w_in, cuts, axis=0)
    kv = [w_kv[k * dh:(k + 1) * dh] for k in range(6)]
    dw = DIL_HEADS * dh
    w_dq, w_dk, w_dv = w_dil[:dw], w_dil[dw:2 * dw], w_dil[2 * dw:]
    w_gpad = jnp.pad(w_g, ((0, LANE - w_g.shape[0]), (0, 0)))
    w_all = jnp.concatenate([w_q, kv[2], kv[4], w_dq, w_dk, _spread_rope64(w_kr.T).T,
                             kv[0], kv[1], kv[3], kv[5], w_gpad, w_cq, w_ckv, w_dv], 0).astype(BF16)
    one = lambda mode, n=1, s=1.0: ((mode,) * n, s)
    nsa_q, k_s, k_w, dq, dk, k_pe, k_cr, v_cr, v_s, v_w, gate, cq, ckv, dv = _proj(
        h2, p["g_mix_pre"], w_all,
        [one("rope", NSA_HEADS, scale), one("rope"), one("rope"), one("rope", DIL_HEADS, scale),
         one("rope", DIL_HEADS), one("rope_r"), one("none"), one("none"), one("none"), one("none"),
         one("none"), one("none", Q_LORA // LANE), one("none", KV_LORA // LANE),
         one("none", DIL_HEADS)],
        [BF16] * 10 + [F32, BF16, BF16, BF16], tabs=tabs, w_t=True, name="in_proj")

    n_chunk = S // CMP_STRIDE
    half = CMP_LEN // 2
    pe2 = p["cmp_pos_emb"].reshape(2, half * dh)
    cos_c, sin_c = tabs_c

    def compress(t, w1, w2, rope):
        out = _compress(t.reshape(B, n_chunk, CMP_STRIDE * dh), pe2,
                        w1[:half * dh].astype(BF16), w1[half * dh:].astype(BF16),
                        w2.astype(BF16), cos_c, sin_c, rope)
        return out.reshape(B, n_chunk // 4, 4, dh).transpose(0, 2, 1, 3).reshape(B, n_chunk, dh)

    k_c = compress(k_cr, p["w_cmp_k1"], p["w_cmp_k2"], True)
    v_c = compress(v_cr, p["w_cmp_v1"], p["w_cmp_v2"], False)
    q3 = nsa_q.reshape(B, S, NSA_HEADS * dh)
    o_cmp, pen = _nsa_compressed(q3, k_c, v_c)
    k_aug = jnp.concatenate([k_s.reshape(B, S, dh), onehot], axis=-1)
    o_sel = _nsa_selected(q3, pen, k_aug, v_s.reshape(B, S, dh))
    o_win = _nsa_window_attention(q3, k_w.reshape(B, S, dh), v_w.reshape(B, S, dh), NSA_WINDOW)

    dqk = QK_NOPE + QK_ROPE
    w_uq = p["w_uq"].reshape(Q_LORA, MLA_HEADS, dqk)
    w_uq = jnp.concatenate([w_uq[..., :QK_NOPE], _spread_rope64(w_uq[..., QK_NOPE:])], -1)
    w_uq = w_uq.reshape(Q_LORA, MLA_HEADS * 2 * dh).astype(BF16)
    (q_m,) = _proj(cq, p["g_q_lora"], w_uq, [(("none", "rope_r") * MLA_HEADS, dqk ** -0.5 * LOG2E)],
                   [BF16], tabs=tabs, bm=1024, name="mla_q_up")
    w_ukv = p["w_ukv"].reshape(KV_LORA, MLA_HEADS, 2 * dh)
    w_ukv = jnp.concatenate([w_ukv[..., :dh].reshape(KV_LORA, -1),
                             w_ukv[..., dh:].reshape(KV_LORA, -1)], 1).astype(BF16)
    k_m, v_m = _proj(ckv, p["g_kv_lora"], w_ukv,
                     [(("none", "extra") * MLA_HEADS, 1.0), (("none",) * MLA_HEADS, 1.0)],
                     [BF16, BF16], extra=k_pe, bm=1024, name="mla_kv_up")
    o_mla = _mla_attention(q_m.reshape(B, S, -1), k_m.reshape(B, S, -1),
                           v_m.reshape(B, S, -1), MLA_HEADS)

    o_dil = _dilated_attention(dq.reshape(B, S, dw), dk.reshape(B, S, dw), dv.reshape(B, S, dw),
                               DIL_PATTERNS, DIL_HEADS)

    h2 = _mix_out(o_cmp.reshape(T, -1), o_sel.reshape(T, -1), o_win.reshape(T, -1), gate,
                  o_mla.reshape(T, -1), o_dil.reshape(T, dw), p["w_out"].astype(BF16), h2,
                  p["g_mix_post"])

    xw = XATTN_HEADS * dh
    M = mem.shape[1]
    xk, xv = _proj(mem.reshape(B * M, D), p["g_mem_kv"], p["w_xkv"].astype(BF16),
                   [(("none",) * XATTN_HEADS, 1.0), (("none",) * XATTN_HEADS, 1.0)],
                   [BF16, BF16], bm=min(512, B * M), name="xattn_kv")
    h2 = _cross_attention(h2.reshape(B, S, D), p["g_mem_pre"], p["w_xq"].astype(BF16),
                          xk.reshape(B, M, xw), xv.reshape(B, M, xw), p["w_xo"].astype(BF16),
                          p["g_mem_post"], XATTN_HEADS, scale).reshape(T, D)

    up = _mlp_up(h2, p["g_mlp_pre"], p["w_up_all"], p["layer"])
    h2 = _out_proj(up, p["w_down_all"], h2, p["g_mlp_post"], layer=p["layer"], bm=1024, bk=1024,
                   name="mlp_down")
    return h2.reshape(B, S, D)


_LAYER_PARAMS = ("g_mix_pre", "w_in", "cmp_pos_emb", "w_cmp_k1", "w_cmp_k2", "w_cmp_v1", "w_cmp_v2",
                 "g_q_lora", "g_kv_lora", "w_uq", "w_ukv", "w_out", "g_mix_post", "g_mem_pre",
                 "g_mem_kv", "w_xq", "w_xkv", "w_xo", "g_mem_post", "g_mlp_pre", "w_up", "w_down",
                 "g_mlp_post")


def kernel(x, mem, positions, g_mix_pre, w_in, cmp_pos_emb, w_cmp_k1, w_cmp_k2, w_cmp_v1, w_cmp_v2, g_q_lora, g_kv_lora, w_uq, w_ukv, w_out, g_mix_post, g_mem_pre, g_mem_kv, w_xq, w_xkv, w_xo, g_mem_post, g_mlp_pre, w_up, w_down, g_mlp_post):
    stacked = dict(zip(_LAYER_PARAMS, (
        g_mix_pre, w_in, cmp_pos_emb, w_cmp_k1, w_cmp_k2, w_cmp_v1, w_cmp_v2, g_q_lora, g_kv_lora,
        w_uq, w_ukv, w_out, g_mix_post, g_mem_pre, g_mem_kv, w_xq, w_xkv, w_xo, g_mem_post,
        g_mlp_pre, w_up, w_down, g_mlp_post)))
    B, S, D = x.shape
    T = B * S
    assert S % PEN_CHUNK == 0
    cosf, sinf = _full_tables(positions)
    cosr, sinr = _small_tables(positions)
    tabs = tuple(t.reshape(T, LANE) for t in (cosf, sinf, cosr, sinr))
    n_chunk = S // CMP_STRIDE
    end = jnp.minimum(jnp.arange(n_chunk) * CMP_STRIDE + CMP_LEN - 1, S - 1)
    tabs_c = _full_tables(positions[:, end])
    blk = (jnp.arange(S) // SLC_BLOCK) % LANE
    onehot = jnp.broadcast_to((blk[:, None] == jnp.arange(LANE)[None, :]).astype(BF16)[None],
                              (B, S, LANE))
    h = x
    w_up_all, w_down_all = w_up.astype(BF16), w_down.astype(BF16)
    for layer in range(stacked["w_in"].shape[0]):
        p = {"layer": layer, "w_up_all": w_up_all, "w_down_all": w_down_all}
        for name, val in stacked.items():
            if name not in ("w_up", "w_down"):
                v = val[layer]
                p[name] = v[None, :] if name.startswith("g_") else v
        h = _layer(h, mem, tabs, tabs_c, onehot, p)
    return h
```

```python
import functools

import numpy as np
import jax
import jax.numpy as jnp
from jax import lax
from jax.experimental import pallas as pl
from jax.experimental.pallas import tpu as pltpu

F32 = jnp.float32
BF16 = jnp.bfloat16

LANE = 128
VMEM_LIMIT = 56 * 1024 * 1024

HEAD_DIM = 128
ROPE_THETA = 10000.0
NORM_EPS = 1e-6
NEG_INF = -1e30
BIG = 1e9
LOG2E = 1.4426950408889634
NSA_HEADS = 4
MLA_HEADS = 6
DIL_HEADS = 6
CMP_LEN = 32
CMP_STRIDE = 16
SLC_BLOCK = 64
SLC_TOPK = 16
NSA_WINDOW = 512
Q_LORA = 512
KV_LORA = 512
QK_NOPE = 128
QK_ROPE = 64
DIL_PATTERNS = ((128, 1), (512, 4), (2048, 16))
XATTN_HEADS = 4
PEN_CHUNK = LANE * SLC_BLOCK


def _cparams(n_grid):
    return pltpu.CompilerParams(
        dimension_semantics=("arbitrary",) * n_grid, vmem_limit_bytes=VMEM_LIMIT)


def _rms(x, g):
    return x * lax.rsqrt(jnp.mean(x * x, axis=-1, keepdims=True) + NORM_EPS) * g


def _dot_t(a, b):
    return lax.dot_general(a, b, (((1,), (1,)), ((), ())), preferred_element_type=F32)


def _proj_kernel(out_plan, has_norm, n_tab, has_extra, w_t, chunk, *refs):
    x_ref, g_ref, w_ref = refs[:3]
    tab_refs = refs[3:3 + n_tab]
    n_in = 3 + n_tab + int(has_extra)
    out_refs = refs[n_in:]
    x = x_ref[...].astype(F32)
    if has_norm:
        x = _rms(x, g_ref[...])
    xb = x.astype(BF16)
    n_cols = w_ref.shape[0 if w_t else 1]
    flat = []
    for oi, (modes, scale) in enumerate(out_plan):
        for k, mode in enumerate(modes):
            if mode == "extra":
                out_refs[oi][:, k * LANE:(k + 1) * LANE] = refs[n_in - 1][...]
            else:
                flat.append((oi, k * LANE, mode, scale))
    for c0 in range(0, n_cols, chunk):
        c1 = min(c0 + chunk, n_cols)
        if w_t:
            acc = _dot_t(xb, w_ref[c0:c1, :])
        else:
            acc = jnp.dot(xb, w_ref[:, c0:c1], preferred_element_type=F32)
        for s in range((c1 - c0) // LANE):
            oi, off, mode, scale = flat[c0 // LANE + s]
            a = acc[:, s * LANE:(s + 1) * LANE]
            if mode == "rope":
                a = a * tab_refs[0][...] + pltpu.roll(a, LANE // 2, 1) * tab_refs[1][...]
            elif mode == "rope_r":
                a = a * tab_refs[2][...] + pltpu.roll(a, LANE // 2, 1) * tab_refs[3][...]
            if scale != 1.0:
                a = a * scale
            out_refs[oi][:, off:off + LANE] = a.astype(out_refs[oi].dtype)


def _proj(x, g, w, out_plan, out_dtypes, tabs=(), extra=None, w_t=False, bm=512, chunk=512,
          name="proj"):
    T, K = x.shape
    N = w.shape[0 if w_t else 1]
    assert T % bm == 0 and N % LANE == 0
    assert sum(sum(md != "extra" for md in m) for m, _ in out_plan) * LANE == N
    has_norm = g is not None
    if g is None:
        g = jnp.ones((1, K), F32)
    extras = () if extra is None else (extra,)
    in_specs = [pl.BlockSpec((bm, K), lambda i: (i, 0)),
                pl.BlockSpec((1, K), lambda i: (0, 0)),
                pl.BlockSpec(w.shape, lambda i: (0, 0), pipeline_mode=pl.Buffered(1))]
    in_specs += [pl.BlockSpec((bm, LANE), lambda i: (i, 0)) for _ in tabs + extras]
    out_shape = [jax.ShapeDtypeStruct((T, len(m) * LANE), dt)
                 for (m, _), dt in zip(out_plan, out_dtypes)]
    out_specs = [pl.BlockSpec((bm, len(m) * LANE), lambda i: (i, 0)) for m, _ in out_plan]
    return pl.pallas_call(
        functools.partial(_proj_kernel, out_plan, has_norm, len(tabs), extra is not None, w_t,
                          chunk),
        out_shape=out_shape, grid=(T // bm,), in_specs=in_specs, out_specs=out_specs,
        compiler_params=_cparams(1), name=name,
    )(x, g, w, *tabs, *extras)


def _mlp_up_kernel(x_ref, g_ref, w_ref, o_ref, xn_ref):
    @pl.when(pl.program_id(1) == 0)
    def _():
        xn_ref[...] = _rms(x_ref[...], g_ref[...]).astype(BF16)
    a = jnp.dot(xn_ref[...], w_ref[...], preferred_element_type=F32)
    a = jnp.maximum(a, 0.0)
    o_ref[...] = (a * a).astype(o_ref.dtype)


def _mlp_up(x, g, w, layer, bm=1024, bn=2048):
    T, K = x.shape
    N = w.shape[2]
    return pl.pallas_call(
        _mlp_up_kernel,
        out_shape=jax.ShapeDtypeStruct((T, N), BF16),
        grid=(T // bm, N // bn),
        in_specs=[pl.BlockSpec((bm, K), lambda i, j: (i, 0)),
                  pl.BlockSpec((1, K), lambda i, j: (0, 0)),
                  pl.BlockSpec((None, K, bn), lambda i, j: (layer, 0, j))],
        out_specs=pl.BlockSpec((bm, bn), lambda i, j: (i, j)),
        scratch_shapes=[pltpu.VMEM((bm, K), BF16)],
        compiler_params=_cparams(2), name="mlp_up",
    )(x, g, w)


def _out_proj_kernel(n_k, a_ref, w_ref, h_ref, g_ref, o_ref):
    k = pl.program_id(1)

    def part(rows=slice(None)):
        return jnp.dot(a_ref[rows, :], w_ref[...], preferred_element_type=F32)

    def finish(first):
        half = o_ref.shape[0] // 2
        for r in range(2):
            rows = slice(r * half, (r + 1) * half)
            y = part(rows) if first else o_ref[rows, :] + part(rows)
            o_ref[rows, :] = h_ref[rows, :] + _rms(y, g_ref[...])

    if n_k == 1:
        finish(True)
    else:
        @pl.when(k == 0)
        def _():
            o_ref[...] = part()

        @pl.when(jnp.logical_and(k > 0, k < n_k - 1))
        def _():
            o_ref[...] += part()

        @pl.when(k == n_k - 1)
        def _():
            finish(False)


def _out_proj(a, w, h, g, layer, bm=512, bk=2048, name="out_proj"):
    T, K = a.shape
    N = w.shape[2]
    bk = min(bk, K)
    n_k = K // bk
    return pl.pallas_call(
        functools.partial(_out_proj_kernel, n_k),
        out_shape=jax.ShapeDtypeStruct((T, N), F32),
        grid=(T // bm, n_k),
        in_specs=[pl.BlockSpec((bm, bk), lambda i, k: (i, k)),
                  pl.BlockSpec((None, bk, N), lambda i, k: (layer, k, 0)),
                  pl.BlockSpec((bm, N), lambda i, k: (i, 0)),
                  pl.BlockSpec((1, N), lambda i, k: (0, 0))],
        out_specs=pl.BlockSpec((bm, N), lambda i, k: (i, 0)),
        compiler_params=_cparams(2), name=name,
    )(a, w, h, g)


def _online_step(carry, st, st_max, v):
    m, l, acc = carry
    m_new = jnp.maximum(m, st_max)
    alpha = jnp.exp2(m - m_new)
    p = jnp.exp2(st - m_new)
    l = alpha * l + jnp.sum(p, axis=0, keepdims=True)
    pv = lax.dot_general(v, p.astype(BF16), (((0,), (0,)), ((), ())), preferred_element_type=F32)
    return m_new, l, alpha * acc + pv


def _online_init(cols, dv):
    return (jnp.full((1, cols), NEG_INF, F32), jnp.zeros((1, cols), F32),
            jnp.zeros((dv, cols), F32))


def _flash_transposed(n_full, n_groups, scores, diag_mask, values, st_ref):
    groups = range(n_groups)

    def put(t, slot, masked=False):
        maxes = []
        for g in groups:
            st = scores(g, t)
            if masked:
                st = diag_mask(g, st)
            st_ref[g, slot] = st
            maxes.append(jnp.max(st, axis=0, keepdims=True))
        return tuple(maxes)

    def tile_at(step):
        return jnp.where(step == 0, n_full, step - 1)

    def softmax_step(step, stats, st_max, slot):
        t = tile_at(step)
        return tuple(_online_step(stats[g], st_ref[g, slot], st_max[g], values(g, t))
                     for g in groups)

    def half(step, carry, slot):
        stats, st_max = carry
        nxt = put(step, 1 - slot)
        return softmax_step(step, stats, st_max, slot), nxt

    def pair(u, carry):
        return half(2 * u + 1, half(2 * u, carry, 0), 1)

    init = tuple(_online_init(st_ref.shape[3], HEAD_DIM) for g in groups)
    carry = lax.fori_loop(0, n_full // 2, pair, (init, put(n_full, 0, masked=True)))
    odd = n_full & 1
    stats, st_max = lax.fori_loop(0, odd, lambda _, c: half(n_full - 1, c, 0), carry)
    stats = softmax_step(n_full, stats, st_max, odd)
    return [acc / l for _, l, acc in stats]


def _mla_kernel(tq, tk, hg, q_ref, k_ref, v_ref, o_ref, st_ref):
    t0 = pl.program_id(2) * tq
    dq = q_ref.shape[1] // hg
    last = t0 // tk

    def scores(g, t):
        start = pl.multiple_of(t * tk, tk)
        cols = slice(g * dq, (g + 1) * dq)
        return _dot_t(k_ref[pl.ds(start, tk), cols], q_ref[:, cols])

    def values(g, t):
        start = pl.multiple_of(t * tk, tk)
        return v_ref[pl.ds(start, tk), g * HEAD_DIM:(g + 1) * HEAD_DIM]

    def diag_mask(g, st):
        key = lax.broadcasted_iota(jnp.int32, (tk, tq), 0) + last * tk
        qry = lax.broadcasted_iota(jnp.int32, (tk, tq), 1) + t0
        return jnp.where(key <= qry, st, NEG_INF)

    outs = _flash_transposed(last, hg, scores, diag_mask, values, st_ref)
    for g in range(hg):
        o_ref[:, g * HEAD_DIM:(g + 1) * HEAD_DIM] = outs[g].T.astype(o_ref.dtype)


def _mla_attention(q, k, v, n_heads, tq=1024, tk=1024, hg=2):
    B, S, _ = q.shape
    dq = q.shape[2] // n_heads
    assert tk % tq == 0 and S % tk == 0 and n_heads % hg == 0
    resident = pl.Buffered(1)
    return pl.pallas_call(
        functools.partial(_mla_kernel, tq, tk, hg),
        out_shape=jax.ShapeDtypeStruct((B, S, n_heads * HEAD_DIM), BF16),
        grid=(B, n_heads // hg, S // tq),
        in_specs=[pl.BlockSpec((None, tq, hg * dq), lambda b, h, i: (b, i, h)),
                  pl.BlockSpec((None, S, hg * dq), lambda b, h, i: (b, 0, h),
                               pipeline_mode=resident),
                  pl.BlockSpec((None, S, hg * HEAD_DIM), lambda b, h, i: (b, 0, h),
                               pipeline_mode=resident)],
        out_specs=pl.BlockSpec((None, tq, hg * HEAD_DIM), lambda b, h, i: (b, i, h)),
        scratch_shapes=[pltpu.VMEM((hg, 2, tk, tq), F32)],
        compiler_params=_cparams(3), name="mla_attention",
    )(q, k, v)


def _band_bias(R, window, base=None):
    C = window + LANE
    r = lax.broadcasted_iota(jnp.int32, (R, C), 0) & (LANE - 1)
    c = lax.broadcasted_iota(jnp.int32, (R, C), 1)
    mask = (c >= r) & (c <= r + window)
    if base is not None:
        mask = mask & (c >= window - base)
    return jnp.where(mask, 0.0, NEG_INF)


def _band_block(q, kwin, vwin, bias):
    s = _dot_t(q, kwin) + bias
    m = jnp.max(s, axis=-1, keepdims=True)
    p = jnp.exp2(s - m)
    l = jnp.sum(p, axis=-1, keepdims=True)
    o = jnp.dot(p.astype(BF16), vwin, preferred_element_type=F32) / l
    return o, m + jnp.log2(l)


def _stack_heads(x, n):
    return jnp.concatenate([x[:, h * LANE:(h + 1) * LANE] for h in range(n)], axis=0)


def _nsa_window_kernel(tq, window, q_ref, kp_ref, kc_ref, vp_ref, vc_ref, o_ref):
    i = pl.program_id(1)
    R = NSA_HEADS * LANE

    def run(first_step):
        kwin = jnp.concatenate([kp_ref[...], kc_ref[...]], axis=0)
        vwin = jnp.concatenate([vp_ref[...], vc_ref[...]], axis=0)
        inner = _band_bias(R, window)
        for j in range(tq // LANE):
            q = _stack_heads(q_ref[j * LANE:(j + 1) * LANE, :], NSA_HEADS)
            lo = j * LANE
            bias = _band_bias(R, window, lo) if first_step and lo < window else inner
            o, _ = _band_block(q, kwin[lo:lo + window + LANE], vwin[lo:lo + window + LANE], bias)
            for h in range(NSA_HEADS):
                o_ref[lo:lo + LANE, h * LANE:(h + 1) * LANE] = (
                    o[h * LANE:(h + 1) * LANE].astype(o_ref.dtype))

    pl.when(i == 0)(lambda: run(True))
    pl.when(i > 0)(lambda: run(False))


def _nsa_window_attention(q, k, v, window):
    B, S, W = q.shape
    tq = window
    prev = lambda b, i: (b, jnp.maximum(i - 1, 0), 0)
    cur = lambda b, i: (b, i, 0)
    return pl.pallas_call(
        functools.partial(_nsa_window_kernel, tq, window),
        out_shape=jax.ShapeDtypeStruct((B, S, W), BF16),
        grid=(B, S // tq),
        in_specs=[pl.BlockSpec((None, tq, W), cur),
                  pl.BlockSpec((None, window, HEAD_DIM), prev),
                  pl.BlockSpec((None, tq, HEAD_DIM), cur),
                  pl.BlockSpec((None, window, HEAD_DIM), prev),
                  pl.BlockSpec((None, tq, HEAD_DIM), cur)],
        out_specs=pl.BlockSpec((None, tq, W), cur),
        compiler_params=_cparams(2), name="nsa_window_attention",
    )(q, k, k, v, v)


def _dilated_kernel(nb, patterns, q_ref, kp_ref, kc_ref, vp_ref, vc_ref, o_ref,
                    qf_ref, kf_ref, vf_ref, of_ref, lf_ref):
    i = pl.program_id(2)
    qf_ref[...] = q_ref[...].astype(F32)
    kf_ref[0:nb, :] = kp_ref[...].astype(F32)
    kf_ref[nb:2 * nb, :] = kc_ref[...].astype(F32)
    vf_ref[0:nb, :] = vp_ref[...].astype(F32)
    vf_ref[nb:2 * nb, :] = vc_ref[...].astype(F32)
    for pi, (window, dil) in enumerate(patterns):
        w = window // dil
        per_class = nb // dil
        inner = _band_bias(LANE, w)
        first = _band_bias(LANE, w, i * per_class)
        for r in range(dil):
            for j in range(per_class // LANE):
                q_lo = r + j * LANE * dil
                k_lo = nb + q_lo - w * dil
                rows_q = pl.ds(q_lo, LANE, stride=dil)
                rows_k = pl.ds(k_lo, w + LANE, stride=dil)
                o, lse = _band_block(qf_ref[rows_q, :].astype(BF16),
                                     kf_ref[rows_k, :].astype(BF16),
                                     vf_ref[rows_k, :].astype(BF16),
                                     first if j == 0 else inner)
                of_ref[pi, rows_q, :] = o
                lf_ref[pi, rows_q, :] = jnp.broadcast_to(lse, (LANE, LANE))
    n_pat = len(patterns)
    mx = lf_ref[0]
    for pi in range(1, n_pat):
        mx = jnp.maximum(mx, lf_ref[pi])
    es = [jnp.exp2(lf_ref[pi] - mx) for pi in range(n_pat)]
    den = es[0]
    num = es[0] * of_ref[0]
    for pi in range(1, n_pat):
        den = den + es[pi]
        num = num + es[pi] * of_ref[pi]
    o_ref[...] = (num / den).astype(o_ref.dtype)


def _dilated_attention(q, k, v, patterns, n_heads, nb=2048):
    B, S, W = q.shape
    for window, dil in patterns:
        assert window % dil == 0 and window // dil == LANE
        assert window <= nb and nb % (dil * LANE) == 0
    assert S % nb == 0
    prev = lambda b, h, i: (b, jnp.maximum(i - 1, 0), h)
    cur = lambda b, h, i: (b, i, h)
    blk = lambda index_map: pl.BlockSpec((None, nb, LANE), index_map)
    return pl.pallas_call(
        functools.partial(_dilated_kernel, nb, patterns),
        out_shape=jax.ShapeDtypeStruct((B, S, W), BF16),
        grid=(B, n_heads, S // nb),
        in_specs=[blk(cur), blk(prev), blk(cur), blk(prev), blk(cur)],
        out_specs=blk(cur),
        scratch_shapes=[pltpu.VMEM((nb, LANE), F32),
                        pltpu.VMEM((2 * nb, LANE), F32),
                        pltpu.VMEM((2 * nb, LANE), F32),
                        pltpu.VMEM((len(patterns), nb, LANE), F32),
                        pltpu.VMEM((len(patterns), nb, LANE), F32)],
        compiler_params=_cparams(3), name="dilated_attention",
    )(q, k, k, v, v)


def _gelu_tanh(x):
    return 0.5 * x * (1.0 + jnp.tanh(0.7978845608028654 * (x + 0.044715 * (x * x * x))))


def _compress_kernel(rope, t_ref, pe_ref, w1a_ref, w1b_ref, w2_ref, cos_ref, sin_ref, o_ref):
    t = t_ref[...].astype(F32)
    n = t.shape[0]
    first = jnp.dot((t + pe_ref[0:1, :]).astype(BF16), w1a_ref[...], preferred_element_type=F32)
    second = jnp.dot((t + pe_ref[1:2, :]).astype(BF16), w1b_ref[...], preferred_element_type=F32)
    hid = first + pltpu.roll(second, n - 1, 0)
    out = jnp.dot(_gelu_tanh(hid).astype(BF16), w2_ref[...], preferred_element_type=F32)
    if rope:
        out = out * cos_ref[...] + pltpu.roll(out, LANE // 2, 1) * sin_ref[...]
    o_ref[...] = out.astype(o_ref.dtype)


def _compress(t, pe2, w1a, w1b, w2, cos_c, sin_c, rope):
    B, n, K = t.shape
    whole = lambda b: (0, 0)
    per_b = lambda b: (b, 0, 0)
    return pl.pallas_call(
        functools.partial(_compress_kernel, rope),
        out_shape=jax.ShapeDtypeStruct((B, n, HEAD_DIM), BF16),
        grid=(B,),
        in_specs=[pl.BlockSpec((None, n, K), per_b),
                  pl.BlockSpec((2, K), whole),
                  pl.BlockSpec((K, HEAD_DIM), whole),
                  pl.BlockSpec((K, HEAD_DIM), whole),
                  pl.BlockSpec((HEAD_DIM, HEAD_DIM), whole),
                  pl.BlockSpec((None, n, HEAD_DIM), per_b),
                  pl.BlockSpec((None, n, HEAD_DIM), per_b)],
        out_specs=pl.BlockSpec((None, n, HEAD_DIM), per_b),
        compiler_params=_cparams(1), name="nsa_compress",
    )(t, pe2, w1a, w1b, w2, cos_c, sin_c)


def _nsa_cmp_block(tq, n_slc, n_sel, t0, q, kc, vc):
    n_cmp = kc.shape[0]
    per_blk = SLC_BLOCK // CMP_STRIDE
    slc_shift = n_slc.bit_length() - 1
    q = _stack_heads(q, NSA_HEADS)
    s = _dot_t(q, kc)
    R = NSA_HEADS * tq
    row = lax.broadcasted_iota(jnp.int32, (R, n_cmp), 0) & (tq - 1)
    col = lax.broadcasted_iota(jnp.int32, (1, n_cmp), 1)
    j_of = col & (n_slc - 1)
    r_of = col >> slc_shift
    cmp_end = (per_blk * j_of + r_of) * CMP_STRIDE + (CMP_LEN - 1)
    cmask = cmp_end <= row + t0
    s = jnp.where(cmask, s, NEG_INF)
    m = jnp.max(s, axis=-1, keepdims=True)
    e = jnp.exp2(s - m)
    inv = jnp.where(m > 0.5 * NEG_INF, 1.0 / jnp.sum(e, axis=-1, keepdims=True), 0.0)
    p = e * inv
    o = jnp.dot(p.astype(BF16), vc, preferred_element_type=F32)

    ph = p[0:tq]
    for h in range(1, NSA_HEADS):
        ph = ph + p[h * tq:(h + 1) * tq]
    groups = [ph[:, r * n_slc:(r + 1) * n_slc] for r in range(per_blk)]
    blk = lax.broadcasted_iota(jnp.int32, (tq, n_slc), 1)
    spill = jnp.where(blk == 0, 0.0, pltpu.roll(groups[per_blk - 1], 1, 1))
    imp = groups[0]
    for r in range(1, per_blk):
        imp = imp + groups[r]
    imp = imp + spill
    tpos = lax.broadcasted_iota(jnp.int32, (tq, n_slc), 0) + t0
    cur = tpos >> (SLC_BLOCK.bit_length() - 1)
    forced = (blk == 0) | (blk == cur) | (blk == cur - 1)
    valid = blk <= cur
    work = jnp.where(forced, BIG, jnp.where(valid, imp, -BIG))
    blk_f = blk.astype(F32)
    sel = jnp.zeros((tq, n_slc), F32)
    for _ in range(n_sel):
        mx = jnp.max(work, axis=-1, keepdims=True)
        first = jnp.min(jnp.where(work == mx, blk_f, float(n_slc)), axis=-1, keepdims=True)
        pick = blk_f == first
        sel = jnp.where(pick, 1.0, sel)
        work = jnp.where(pick, -jnp.inf, work)
    return o, jnp.where((sel > 0.0) & valid, 0.0, NEG_INF)


def _nsa_cmp_kernel(tq, n_sub, n_slc, n_sel, q_ref, kc_ref, vc_ref, o_ref, pen_ref):
    i = pl.program_id(1)
    per_blk = SLC_BLOCK // CMP_STRIDE

    def run(n_j):
        take = lambda ref: jnp.concatenate(
            [ref[r * n_slc:r * n_slc + n_j, :] for r in range(per_blk)], axis=0)
        kc, vc = (kc_ref[...], vc_ref[...]) if n_j == n_slc else (take(kc_ref), take(vc_ref))
        for u in range(n_sub):
            rows = slice(u * tq, (u + 1) * tq)
            o, pen = _nsa_cmp_block(tq, n_j, min(n_sel, n_j), (i * n_sub + u) * tq,
                                    q_ref[rows, :], kc, vc)
            for h in range(NSA_HEADS):
                o_ref[rows, h * LANE:(h + 1) * LANE] = o[h * tq:(h + 1) * tq].astype(o_ref.dtype)
            pen = pen.astype(pen_ref.dtype)
            for c in range(n_slc // LANE):
                if c < n_j // LANE:
                    pen_ref[c, rows, :] = pen[:, c * LANE:(c + 1) * LANE]
                else:
                    pen_ref[c, rows, :] = jnp.full((tq, LANE), NEG_INF, pen_ref.dtype)

    half = n_slc // 2
    if half % LANE == 0:
        early = (i + 1) * (tq * n_sub) <= half * SLC_BLOCK
        pl.when(early)(lambda: run(half))
        pl.when(jnp.logical_not(early))(lambda: run(n_slc))
    else:
        run(n_slc)


def _nsa_compressed(q, k_c, v_c, tq=128, n_sub=4):
    B, S, W = q.shape
    n_cmp = k_c.shape[1]
    n_slc = S // SLC_BLOCK
    n_sel = min(SLC_TOPK, n_slc)
    assert n_slc % LANE == 0 and n_slc & (n_slc - 1) == 0
    assert n_cmp == 4 * n_slc and tq & (tq - 1) == 0
    n_pc = n_slc // LANE
    bq = tq * n_sub
    return pl.pallas_call(
        functools.partial(_nsa_cmp_kernel, tq, n_sub, n_slc, n_sel),
        out_shape=[jax.ShapeDtypeStruct((B, S, W), BF16),
                   jax.ShapeDtypeStruct((B, n_pc, S, LANE), BF16)],
        grid=(B, S // bq),
        in_specs=[pl.BlockSpec((None, bq, W), lambda b, i: (b, i, 0)),
                  pl.BlockSpec((None, n_cmp, HEAD_DIM), lambda b, i: (b, 0, 0)),
                  pl.BlockSpec((None, n_cmp, HEAD_DIM), lambda b, i: (b, 0, 0))],
        out_specs=[pl.BlockSpec((None, bq, W), lambda b, i: (b, i, 0)),
                   pl.BlockSpec((None, n_pc, bq, LANE), lambda b, i: (b, 0, i, 0))],
        compiler_params=_cparams(2), name="nsa_compressed_topk",
    )(q, k_c, v_c)


def _nsa_sel_kernel(tq, tk, n_sub, q_ref, pen_ref, k_ref, v_ref, o_ref, qa_ref, st_ref):
    t0 = pl.program_id(1) * (tq * n_sub)
    R = NSA_HEADS * tq
    last = t0 // tk
    tiles_per_chunk = PEN_CHUNK // tk
    for u in range(n_sub):
        rows = slice(u * tq, (u + 1) * tq)
        q = _stack_heads(q_ref[rows, :], NSA_HEADS)
        for c in range(pen_ref.shape[0]):
            qa_ref[u, c, :, 0:LANE] = q
            qa_ref[u, c, :, LANE:2 * LANE] = jnp.concatenate([pen_ref[c, rows, :]] * NSA_HEADS,
                                                             axis=0)

    def scores(u, t):
        start = pl.multiple_of(t * tk, tk)
        return _dot_t(k_ref[pl.ds(start, tk), :], qa_ref[u, t // tiles_per_chunk])

    def diag_mask(u, st):
        key = lax.broadcasted_iota(jnp.int32, (tk, R), 0) + last * tk
        qry = (lax.broadcasted_iota(jnp.int32, (tk, R), 1) & (tq - 1)) + (t0 + u * tq)
        return jnp.where(key <= qry, st, NEG_INF)

    def values(u, t):
        return v_ref[pl.ds(pl.multiple_of(t * tk, tk), tk), :]

    outs = _flash_transposed(last, n_sub, scores, diag_mask, values, st_ref)
    for u in range(n_sub):
        for h in range(NSA_HEADS):
            o_ref[u * tq:(u + 1) * tq, h * LANE:(h + 1) * LANE] = (
                outs[u][:, h * tq:(h + 1) * tq].T.astype(o_ref.dtype))


def _nsa_selected(q, pen, k_aug, v, tq=128, n_sub=4, tk=1024):
    B, S, W = q.shape
    n_pc = pen.shape[1]
    bq = tq * n_sub
    assert tk % bq == 0 and PEN_CHUNK % tk == 0 and tq & (tq - 1) == 0
    return pl.pallas_call(
        functools.partial(_nsa_sel_kernel, tq, tk, n_sub),
        out_shape=jax.ShapeDtypeStruct((B, S, W), BF16),
        grid=(B, S // bq),
        in_specs=[pl.BlockSpec((None, bq, W), lambda b, i: (b, i, 0)),
                  pl.BlockSpec((None, n_pc, bq, LANE), lambda b, i: (b, 0, i, 0)),
                  pl.BlockSpec((None, S, 2 * HEAD_DIM), lambda b, i: (b, 0, 0)),
                  pl.BlockSpec((None, S, HEAD_DIM), lambda b, i: (b, 0, 0))],
        out_specs=pl.BlockSpec((None, bq, W), lambda b, i: (b, i, 0)),
        scratch_shapes=[pltpu.VMEM((n_sub, n_pc, NSA_HEADS * tq, 2 * HEAD_DIM), BF16),
                        pltpu.VMEM((n_sub, 2, tk, NSA_HEADS * tq), F32)],
        compiler_params=_cparams(2), name="nsa_selected_attention",
    )(q, pen, k_aug, v)


def _mix_out_kernel(oc_ref, os_ref, ow_ref, gate_ref, ob_ref, od_ref, w_ref, h_ref, g_ref,
                    o_ref, a_ref):
    half = a_ref.shape[0] // 2
    for part in range(2):
        rows = slice(part * half, (part + 1) * half)
        gate = jax.nn.sigmoid(gate_ref[rows, :])
        for h in range(NSA_HEADS):
            hs = slice(h * LANE, (h + 1) * LANE)
            a = (gate[:, 3 * h:3 * h + 1] * oc_ref[rows, hs].astype(F32)
                 + gate[:, 3 * h + 1:3 * h + 2] * os_ref[rows, hs].astype(F32)
                 + gate[:, 3 * h + 2:3 * h + 3] * ow_ref[rows, hs].astype(F32))
            a_ref[rows, hs] = a.astype(a_ref.dtype)
        off = NSA_HEADS * LANE
        a_ref[rows, off:off + MLA_HEADS * LANE] = ob_ref[rows, :]
        off += MLA_HEADS * LANE
        a_ref[rows, off:off + DIL_HEADS * LANE] = od_ref[rows, :]
        y = jnp.dot(a_ref[rows, :], w_ref[...], preferred_element_type=F32)
        o_ref[rows, :] = h_ref[rows, :] + _rms(y, g_ref[...])


def _mix_out(o_c, o_s, o_w, gate, o_b, o_d, w, h, g, bm=512):
    T, N = h.shape
    width = (NSA_HEADS + MLA_HEADS + DIL_HEADS) * LANE
    heads = (o_c, o_s, o_w, gate, o_b, o_d)
    row = lambda a: pl.BlockSpec((bm, a.shape[1]), lambda i: (i, 0))
    return pl.pallas_call(
        _mix_out_kernel,
        out_shape=jax.ShapeDtypeStruct((T, N), F32),
        grid=(T // bm,),
        in_specs=[row(a) for a in heads] + [
            pl.BlockSpec((width, N), lambda i: (0, 0), pipeline_mode=pl.Buffered(1)),
            row(h), pl.BlockSpec((1, N), lambda i: (0, 0))],
        out_specs=row(h),
        scratch_shapes=[pltpu.VMEM((bm, width), BF16)],
        compiler_params=_cparams(1), name="mix_out_proj",
    )(*heads, w, h, g)


def _xattn_kernel(n_heads, scale, h_ref, gq_ref, wq_ref, k_ref, v_ref, wo_ref, go_ref, o_ref,
                  a_ref):
    h = h_ref[...]
    x = _rms(h, gq_ref[...]).astype(BF16)
    q = (jnp.dot(x, wq_ref[...], preferred_element_type=F32) * scale).astype(BF16)
    for hd in range(n_heads):
        hs = slice(hd * LANE, (hd + 1) * LANE)
        s = _dot_t(q[:, hs], k_ref[:, hs])
        m = jnp.max(s, axis=-1, keepdims=True)
        p = jnp.exp2(s - m)
        l = jnp.sum(p, axis=-1, keepdims=True)
        o = jnp.dot(p.astype(BF16), v_ref[:, hs], preferred_element_type=F32) / l
        a_ref[:, hs] = o.astype(a_ref.dtype)
    y = jnp.dot(a_ref[...], wo_ref[...], preferred_element_type=F32)
    o_ref[...] = h + _rms(y, go_ref[...])


def _cross_attention(h, gq, wq, k, v, wo, go, n_heads, scale, bm=512):
    B, S, D = h.shape
    M, W = k.shape[1:]
    const = lambda shape: pl.BlockSpec(shape, lambda b, i: (0, 0), pipeline_mode=pl.Buffered(1))
    return pl.pallas_call(
        functools.partial(_xattn_kernel, n_heads, scale),
        out_shape=jax.ShapeDtypeStruct((B, S, D), F32),
        grid=(B, S // bm),
        in_specs=[pl.BlockSpec((None, bm, D), lambda b, i: (b, i, 0)),
                  const((1, D)), const((D, W)),
                  pl.BlockSpec((None, M, W), lambda b, i: (b, 0, 0)),
                  pl.BlockSpec((None, M, W), lambda b, i: (b, 0, 0)),
                  const((W, D)), const((1, D))],
        out_specs=pl.BlockSpec((None, bm, D), lambda b, i: (b, i, 0)),
        scratch_shapes=[pltpu.VMEM((bm, W), BF16)],
        compiler_params=_cparams(2), name="cross_attention",
    )(h, gq, wq, k, v, wo, go)


def _rope_tables(pos, dim):
    inv = ROPE_THETA ** (-jnp.arange(0, dim, 2, dtype=F32) / dim)
    ang = pos.astype(F32)[..., None] * inv
    return jnp.cos(ang), jnp.sin(ang)


def _full_tables(pos):
    c, s = _rope_tables(pos, HEAD_DIM)
    return jnp.concatenate([c, c], -1), jnp.concatenate([-s, s], -1)


def _spread_rope64(t):
    z = jnp.zeros(t.shape[:-1] + (QK_ROPE // 2,), t.dtype)
    return jnp.concatenate([t[..., :QK_ROPE // 2], z, t[..., QK_ROPE // 2:], z], -1)


def _small_tables(pos):
    c, s = _rope_tables(pos, QK_ROPE)
    return _spread_rope64(jnp.concatenate([c, c], -1)), _spread_rope64(jnp.concatenate([-s, s], -1))


def _layer(h, mem, tabs, tabs_c, onehot, p):
    B, S, D = h.shape
    T = B * S
    dh = HEAD_DIM
    scale = dh ** -0.5 * LOG2E
    h2 = h.reshape(T, D)

    w_in = p["w_in"].T
    cuts = np.cumsum([NSA_HEADS * dh, 6 * dh, 3 * NSA_HEADS, Q_LORA, KV_LORA, QK_ROPE])
    w_q, w_kv, w_g, w_cq, w_ckv, w_kr, w_dil = jnp.split(w_in, cuts, axis=0)
    kv = [w_kv[k * dh:(k + 1) * dh] for k in range(6)]
    dw = DIL_HEADS * dh
    w_dq, w_dk, w_dv = w_dil[:dw], w_dil[dw:2 * dw], w_dil[2 * dw:]
    w_gpad = jnp.pad(w_g, ((0, LANE - w_g.shape[0]), (0, 0)))
    w_all = jnp.concatenate([w_q, kv[2], kv[4], w_dq, w_dk, _spread_rope64(w_kr.T).T,
                             kv[0], kv[1], kv[3], kv[5], w_gpad, w_cq, w_ckv, w_dv], 0).astype(BF16)
    one = lambda mode, n=1, s=1.0: ((mode,) * n, s)
    nsa_q, k_s, k_w, dq, dk, k_pe, k_cr, v_cr, v_s, v_w, gate, cq, ckv, dv = _proj(
        h2, p["g_mix_pre"], w_all,
        [one("rope", NSA_HEADS, scale), one("rope"), one("rope"), one("rope", DIL_HEADS, scale),
         one("rope", DIL_HEADS), one("rope_r"), one("none"), one("none"), one("none"), one("none"),
         one("none"), one("none", Q_LORA // LANE), one("none", KV_LORA // LANE),
         one("none", DIL_HEADS)],
        [BF16] * 10 + [F32, BF16, BF16, BF16], tabs=tabs, w_t=True, name="in_proj")

    n_chunk = S // CMP_STRIDE
    half = CMP_LEN // 2
    pe2 = p["cmp_pos_emb"].reshape(2, half * dh)
    cos_c, sin_c = tabs_c

    def compress(t, w1, w2, rope):
        out = _compress(t.reshape(B, n_chunk, CMP_STRIDE * dh), pe2,
                        w1[:half * dh].astype(BF16), w1[half * dh:].astype(BF16),
                        w2.astype(BF16), cos_c, sin_c, rope)
        return out.reshape(B, n_chunk // 4, 4, dh).transpose(0, 2, 1, 3).reshape(B, n_chunk, dh)

    k_c = compress(k_cr, p["w_cmp_k1"], p["w_cmp_k2"], True)
    v_c = compress(v_cr, p["w_cmp_v1"], p["w_cmp_v2"], False)
    q3 = nsa_q.reshape(B, S, NSA_HEADS * dh)
    o_cmp, pen = _nsa_compressed(q3, k_c, v_c)
    k_aug = jnp.concatenate([k_s.reshape(B, S, dh), onehot], axis=-1)
    o_sel = _nsa_selected(q3, pen, k_aug, v_s.reshape(B, S, dh))
    o_win = _nsa_window_attention(q3, k_w.reshape(B, S, dh), v_w.reshape(B, S, dh), NSA_WINDOW)

    dqk = QK_NOPE + QK_ROPE
    w_uq = p["w_uq"].reshape(Q_LORA, MLA_HEADS, dqk)
    w_uq = jnp.concatenate([w_uq[..., :QK_NOPE], _spread_rope64(w_uq[..., QK_NOPE:])], -1)
    w_uq = w_uq.reshape(Q_LORA, MLA_HEADS * 2 * dh).astype(BF16)
    (q_m,) = _proj(cq, p["g_q_lora"], w_uq, [(("none", "rope_r") * MLA_HEADS, dqk ** -0.5 * LOG2E)],
                   [BF16], tabs=tabs, bm=1024, name="mla_q_up")
    w_ukv = p["w_ukv"].reshape(KV_LORA, MLA_HEADS, 2 * dh)
    w_ukv = jnp.concatenate([w_ukv[..., :dh].reshape(KV_LORA, -1),
                             w_ukv[..., dh:].reshape(KV_LORA, -1)], 1).astype(BF16)
    k_m, v_m = _proj(ckv, p["g_kv_lora"], w_ukv,
                     [(("none", "extra") * MLA_HEADS, 1.0), (("none",) * MLA_HEADS, 1.0)],
                     [BF16, BF16], extra=k_pe, bm=1024, name="mla_kv_up")
    o_mla = _mla_attention(q_m.reshape(B, S, -1), k_m.reshape(B, S, -1),
                           v_m.reshape(B, S, -1), MLA_HEADS)

    o_dil = _dilated_attention(dq.reshape(B, S, dw), dk.reshape(B, S, dw), dv.reshape(B, S, dw),
                               DIL_PATTERNS, DIL_HEADS)

    h2 = _mix_out(o_cmp.reshape(T, -1), o_sel.reshape(T, -1), o_win.reshape(T, -1), gate,
                  o_mla.reshape(T, -1), o_dil.reshape(T, dw), p["w_out"].astype(BF16), h2,
                  p["g_mix_post"])

    xw = XATTN_HEADS * dh
    M = mem.shape[1]
    xk, xv = _proj(mem.reshape(B * M, D), p["g_mem_kv"], p["w_xkv"].astype(BF16),
                   [(("none",) * XATTN_HEADS, 1.0), (("none",) * XATTN_HEADS, 1.0)],
                   [BF16, BF16], bm=min(512, B * M), name="xattn_kv")
    h2 = _cross_attention(h2.reshape(B, S, D), p["g_mem_pre"], p["w_xq"].astype(BF16),
                          xk.reshape(B, M, xw), xv.reshape(B, M, xw), p["w_xo"].astype(BF16),
                          p["g_mem_post"], XATTN_HEADS, scale).reshape(T, D)

    up = _mlp_up(h2, p["g_mlp_pre"], p["w_up_all"], p["layer"])
    h2 = _out_proj(up, p["w_down_all"], h2, p["g_mlp_post"], layer=p["layer"], bm=1024, bk=1024,
                   name="mlp_down")
    return h2.reshape(B, S, D)


_LAYER_PARAMS = ("g_mix_pre", "w_in", "cmp_pos_emb", "w_cmp_k1", "w_cmp_k2", "w_cmp_v1", "w_cmp_v2",
                 "g_q_lora", "g_kv_lora", "w_uq", "w_ukv", "w_out", "g_mix_post", "g_mem_pre",
                 "g_mem_kv", "w_xq", "w_xkv", "w_xo", "g_mem_post", "g_mlp_pre", "w_up", "w_down",
                 "g_mlp_post")


def kernel(x, mem, positions, g_mix_pre, w_in, cmp_pos_emb, w_cmp_k1, w_cmp_k2, w_cmp_v1, w_cmp_v2, g_q_lora, g_kv_lora, w_uq, w_ukv, w_out, g_mix_post, g_mem_pre, g_mem_kv, w_xq, w_xkv, w_xo, g_mem_post, g_mlp_pre, w_up, w_down, g_mlp_post):
    stacked = dict(zip(_LAYER_PARAMS, (
        g_mix_pre, w_in, cmp_pos_emb, w_cmp_k1, w_cmp_k2, w_cmp_v1, w_cmp_v2, g_q_lora, g_kv_lora,
        w_uq, w_ukv, w_out, g_mix_post, g_mem_pre, g_mem_kv, w_xq, w_xkv, w_xo, g_mem_post,
        g_mlp_pre, w_up, w_down, g_mlp_post)))
    B, S, D = x.shape
    T = B * S
    assert S % PEN_CHUNK == 0
    cosf, sinf = _full_tables(positions)
    cosr, sinr = _small_tables(positions)
    tabs = tuple(t.reshape(T, LANE) for t in (cosf, sinf, cosr, sinr))
    n_chunk = S // CMP_STRIDE
    end = jnp.minimum(jnp.arange(n_chunk) * CMP_STRIDE + CMP_LEN - 1, S - 1)
    tabs_c = _full_tables(positions[:, end])
    blk = (jnp.arange(S) // SLC_BLOCK) % LANE
    onehot = jnp.broadcast_to((blk[:, None] == jnp.arange(LANE)[None, :]).astype(BF16)[None],
                              (B, S, LANE))
    h = x
    w_up_all, w_down_all = w_up.astype(BF16), w_down.astype(BF16)
    for layer in range(stacked["w_in"].shape[0]):
        p = {"layer": layer, "w_up_all": w_up_all, "w_down_all": w_down_all}
        for name, val in stacked.items():
            if name not in ("w_up", "w_down"):
                v = val[layer]
                p[name] = v[None, :] if name.startswith("g_") else v
        h = _layer(h, mem, tabs, tabs_c, onehot, p)
    return h
```

```python
import functools

import numpy as np
import jax
import jax.numpy as jnp
from jax import lax
from jax.experimental import pallas as pl
from jax.experimental.pallas import tpu as pltpu

F32 = jnp.float32
BF16 = jnp.bfloat16

LANE = 128
VMEM_LIMIT = 56 * 1024 * 1024

HEAD_DIM = 128
ROPE_THETA = 10000.0
NORM_EPS = 1e-6
NEG_INF = -1e30
BIG = 1e9
LOG2E = 1.4426950408889634
NSA_HEADS = 4
MLA_HEADS = 6
DIL_HEADS = 6
CMP_LEN = 32
CMP_STRIDE = 16
SLC_BLOCK = 64
SLC_TOPK = 16
NSA_WINDOW = 512
Q_LORA = 512
KV_LORA = 512
QK_NOPE = 128
QK_ROPE = 64
DIL_PATTERNS = ((128, 1), (512, 4), (2048, 16))
XATTN_HEADS = 4
PEN_CHUNK = LANE * SLC_BLOCK


def _cparams(n_grid):
    return pltpu.CompilerParams(
        dimension_semantics=("arbitrary",) * n_grid, vmem_limit_bytes=VMEM_LIMIT)


def _rms(x, g):
    return x * lax.rsqrt(jnp.mean(x * x, axis=-1, keepdims=True) + NORM_EPS) * g


def _dot_t(a, b):
    return lax.dot_general(a, b, (((1,), (1,)), ((), ())), preferred_element_type=F32)


def _proj_kernel(out_plan, has_norm, n_tab, has_extra, w_t, chunk, *refs):
    x_ref, g_ref, w_ref = refs[:3]
    tab_refs = refs[3:3 + n_tab]
    n_in = 3 + n_tab + int(has_extra)
    out_refs = refs[n_in:]
    x = x_ref[...].astype(F32)
    if has_norm:
        x = _rms(x, g_ref[...])
    xb = x.astype(BF16)
    n_cols = w_ref.shape[0 if w_t else 1]
    flat = []
    for oi, (modes, scale) in enumerate(out_plan):
        for k, mode in enumerate(modes):
            if mode == "extra":
                out_refs[oi][:, k * LANE:(k + 1) * LANE] = refs[n_in - 1][...]
            else:
                flat.append((oi, k * LANE, mode, scale))
    for c0 in range(0, n_cols, chunk):
        c1 = min(c0 + chunk, n_cols)
        if w_t:
            acc = _dot_t(xb, w_ref[c0:c1, :])
        else:
            acc = jnp.dot(xb, w_ref[:, c0:c1], preferred_element_type=F32)
        for s in range((c1 - c0) // LANE):
            oi, off, mode, scale = flat[c0 // LANE + s]
            a = acc[:, s * LANE:(s + 1) * LANE]
            if mode == "rope":
                a = a * tab_refs[0][...] + pltpu.roll(a, LANE // 2, 1) * tab_refs[1][...]
            elif mode == "rope_r":
                a = a * tab_refs[2][...] + pltpu.roll(a, LANE // 2, 1) * tab_refs[3][...]
            if scale != 1.0:
                a = a * scale
            out_refs[oi][:, off:off + LANE] = a.astype(out_refs[oi].dtype)


def _proj(x, g, w, out_plan, out_dtypes, tabs=(), extra=None, w_t=False, bm=512, chunk=512,
          name="proj"):
    T, K = x.shape
    N = w.shape[0 if w_t else 1]
    assert T % bm == 0 and N % LANE == 0
    assert sum(sum(md != "extra" for md in m) for m, _ in out_plan) * LANE == N
    has_norm = g is not None
    if g is None:
        g = jnp.ones((1, K), F32)
    extras = () if extra is None else (extra,)
    in_specs = [pl.BlockSpec((bm, K), lambda i: (i, 0)),
                pl.BlockSpec((1, K), lambda i: (0, 0)),
                pl.BlockSpec(w.shape, lambda i: (0, 0), pipeline_mode=pl.Buffered(1))]
    in_specs += [pl.BlockSpec((bm, LANE), lambda i: (i, 0)) for _ in tabs + extras]
    out_shape = [jax.ShapeDtypeStruct((T, len(m) * LANE), dt)
                 for (m, _), dt in zip(out_plan, out_dtypes)]
    out_specs = [pl.BlockSpec((bm, len(m) * LANE), lambda i: (i, 0)) for m, _ in out_plan]
    return pl.pallas_call(
        functools.partial(_proj_kernel, out_plan, has_norm, len(tabs), extra is not None, w_t,
                          chunk),
        out_shape=out_shape, grid=(T // bm,), in_specs=in_specs, out_specs=out_specs,
        compiler_params=_cparams(1), name=name,
    )(x, g, w, *tabs, *extras)


def _mlp_up_kernel(x_ref, g_ref, w_ref, o_ref, xn_ref):
    @pl.when(pl.program_id(1) == 0)
    def _():
        xn_ref[...] = _rms(x_ref[...], g_ref[...]).astype(BF16)
    a = jnp.dot(xn_ref[...], w_ref[...], preferred_element_type=F32)
    a = jnp.maximum(a, 0.0)
    o_ref[...] = (a * a).astype(o_ref.dtype)


def _mlp_up(x, g, w, layer, bm=1024, bn=2048):
    T, K = x.shape
    N = w.shape[2]
    return pl.pallas_call(
        _mlp_up_kernel,
        out_shape=jax.ShapeDtypeStruct((T, N), BF16),
        grid=(T // bm, N // bn),
        in_specs=[pl.BlockSpec((bm, K), lambda i, j: (i, 0)),
                  pl.BlockSpec((1, K), lambda i, j: (0, 0)),
                  pl.BlockSpec((None, K, bn), lambda i, j: (layer, 0, j))],
        out_specs=pl.BlockSpec((bm, bn), lambda i, j: (i, j)),
        scratch_shapes=[pltpu.VMEM((bm, K), BF16)],
        compiler_params=_cparams(2), name="mlp_up",
    )(x, g, w)


def _out_proj_kernel(n_k, a_ref, w_ref, h_ref, g_ref, o_ref):
    k = pl.program_id(1)

    def part(rows=slice(None)):
        return jnp.dot(a_ref[rows, :], w_ref[...], preferred_element_type=F32)

    def finish(first):
        half = o_ref.shape[0] // 2
        for r in range(2):
            rows = slice(r * half, (r + 1) * half)
            y = part(rows) if first else o_ref[rows, :] + part(rows)
            o_ref[rows, :] = h_ref[rows, :] + _rms(y, g_ref[...])

    if n_k == 1:
        finish(True)
    else:
        @pl.when(k == 0)
        def _():
            o_ref[...] = part()

        @pl.when(jnp.logical_and(k > 0, k < n_k - 1))
        def _():
            o_ref[...] += part()

        @pl.when(k == n_k - 1)
        def _():
            finish(False)


def _out_proj(a, w, h, g, layer, bm=512, bk=2048, name="out_proj"):
    T, K = a.shape
    N = w.shape[2]
    bk = min(bk, K)
    n_k = K // bk
    return pl.pallas_call(
        functools.partial(_out_proj_kernel, n_k),
        out_shape=jax.ShapeDtypeStruct((T, N), F32),
        grid=(T // bm, n_k),
        in_specs=[pl.BlockSpec((bm, bk), lambda i, k: (i, k)),
                  pl.BlockSpec((None, bk, N), lambda i, k: (layer, k, 0)),
                  pl.BlockSpec((bm, N), lambda i, k: (i, 0)),
                  pl.BlockSpec((1, N), lambda i, k: (0, 0))],
        out_specs=pl.BlockSpec((bm, N), lambda i, k: (i, 0)),
        compiler_params=_cparams(2), name=name,
    )(a, w, h, g)


def _online_step(carry, st, st_max, v):
    m, l, acc = carry
    m_new = jnp.maximum(m, st_max)
    alpha = jnp.exp2(m - m_new)
    p = jnp.exp2(st - m_new)
    l = alpha * l + jnp.sum(p, axis=0, keepdims=True)
    pv = lax.dot_general(v, p.astype(BF16), (((0,), (0,)), ((), ())), preferred_element_type=F32)
    return m_new, l, alpha * acc + pv


def _online_init(cols, dv):
    return (jnp.full((1, cols), NEG_INF, F32), jnp.zeros((1, cols), F32),
            jnp.zeros((dv, cols), F32))


def _flash_transposed(n_full, n_groups, scores, diag_mask, values, st_ref):
    groups = range(n_groups)

    def put(t, slot, masked=False):
        maxes = []
        for g in groups:
            st = scores(g, t)
            if masked:
                st = diag_mask(g, st)
            st_ref[g, slot] = st
            maxes.append(jnp.max(st, axis=0, keepdims=True))
        return tuple(maxes)

    def tile_at(step):
        return jnp.where(step == 0, n_full, step - 1)

    def softmax_step(step, stats, st_max, slot):
        t = tile_at(step)
        return tuple(_online_step(stats[g], st_ref[g, slot], st_max[g], values(g, t))
                     for g in groups)

    def half(step, carry, slot):
        stats, st_max = carry
        nxt = put(step, 1 - slot)
        return softmax_step(step, stats, st_max, slot), nxt

    def pair(u, carry):
        return half(2 * u + 1, half(2 * u, carry, 0), 1)

    init = tuple(_online_init(st_ref.shape[3], HEAD_DIM) for g in groups)
    carry = lax.fori_loop(0, n_full // 2, pair, (init, put(n_full, 0, masked=True)))
    odd = n_full & 1
    stats, st_max = lax.fori_loop(0, odd, lambda _, c: half(n_full - 1, c, 0), carry)
    stats = softmax_step(n_full, stats, st_max, odd)
    return [acc / l for _, l, acc in stats]


def _mla_kernel(tq, tk, hg, q_ref, k_ref, v_ref, o_ref, st_ref):
    t0 = pl.program_id(2) * tq
    dq = q_ref.shape[1] // hg
    last = t0 // tk

    def scores(g, t):
        start = pl.multiple_of(t * tk, tk)
        cols = slice(g * dq, (g + 1) * dq)
        return _dot_t(k_ref[pl.ds(start, tk), cols], q_ref[:, cols])

    def values(g, t):
        start = pl.multiple_of(t * tk, tk)
        return v_ref[pl.ds(start, tk), g * HEAD_DIM:(g + 1) * HEAD_DIM]

    def diag_mask(g, st):
        key = lax.broadcasted_iota(jnp.int32, (tk, tq), 0) + last * tk
        qry = lax.broadcasted_iota(jnp.int32, (tk, tq), 1) + t0
        return jnp.where(key <= qry, st, NEG_INF)

    outs = _flash_transposed(last, hg, scores, diag_mask, values, st_ref)
    for g in range(hg):
        o_ref[:, g * HEAD_DIM:(g + 1) * HEAD_DIM] = outs[g].T.astype(o_ref.dtype)


def _mla_attention(q, k, v, n_heads, tq=1024, tk=1024, hg=2):
    B, S, _ = q.shape
    dq = q.shape[2] // n_heads
    assert tk % tq == 0 and S % tk == 0 and n_heads % hg == 0
    resident = pl.Buffered(1)
    return pl.pallas_call(
        functools.partial(_mla_kernel, tq, tk, hg),
        out_shape=jax.ShapeDtypeStruct((B, S, n_heads * HEAD_DIM), BF16),
        grid=(B, n_heads // hg, S // tq),
        in_specs=[pl.BlockSpec((None, tq, hg * dq), lambda b, h, i: (b, i, h)),
                  pl.BlockSpec((None, S, hg * dq), lambda b, h, i: (b, 0, h),
                               pipeline_mode=resident),
                  pl.BlockSpec((None, S, hg * HEAD_DIM), lambda b, h, i: (b, 0, h),
                               pipeline_mode=resident)],
        out_specs=pl.BlockSpec((None, tq, hg * HEAD_DIM), lambda b, h, i: (b, i, h)),
        scratch_shapes=[pltpu.VMEM((hg, 2, tk, tq), F32)],
        compiler_params=_cparams(3), name="mla_attention",
    )(q, k, v)


def _band_bias(R, window, base=None):
    C = window + LANE
    r = lax.broadcasted_iota(jnp.int32, (R, C), 0) & (LANE - 1)
    c = lax.broadcasted_iota(jnp.int32, (R, C), 1)
    mask = (c >= r) & (c <= r + window)
    if base is not None:
        mask = mask & (c >= window - base)
    return jnp.where(mask, 0.0, NEG_INF)


def _band_block(q, kwin, vwin, bias):
    s = _dot_t(q, kwin) + bias
    m = jnp.max(s, axis=-1, keepdims=True)
    p = jnp.exp2(s - m)
    l = jnp.sum(p, axis=-1, keepdims=True)
    o = jnp.dot(p.astype(BF16), vwin, preferred_element_type=F32) / l
    return o, m + jnp.log2(l)


def _stack_heads(x, n):
    return jnp.concatenate([x[:, h * LANE:(h + 1) * LANE] for h in range(n)], axis=0)


def _nsa_window_kernel(tq, window, q_ref, kp_ref, kc_ref, vp_ref, vc_ref, o_ref):
    i = pl.program_id(1)
    R = NSA_HEADS * LANE

    def run(first_step):
        kwin = jnp.concatenate([kp_ref[...], kc_ref[...]], axis=0)
        vwin = jnp.concatenate([vp_ref[...], vc_ref[...]], axis=0)
        inner = _band_bias(R, window)
        for j in range(tq // LANE):
            q = _stack_heads(q_ref[j * LANE:(j + 1) * LANE, :], NSA_HEADS)
            lo = j * LANE
            bias = _band_bias(R, window, lo) if first_step and lo < window else inner
            o, _ = _band_block(q, kwin[lo:lo + window + LANE], vwin[lo:lo + window + LANE], bias)
            for h in range(NSA_HEADS):
                o_ref[lo:lo + LANE, h * LANE:(h + 1) * LANE] = (
                    o[h * LANE:(h + 1) * LANE].astype(o_ref.dtype))

    pl.when(i == 0)(lambda: run(True))
    pl.when(i > 0)(lambda: run(False))


def _nsa_window_attention(q, k, v, window, tq=1024):
    B, S, W = q.shape
    assert tq % window == 0 and S % tq == 0
    prev = lambda b, i: (b, jnp.maximum(i * (tq // window) - 1, 0), 0)
    cur = lambda b, i: (b, i, 0)
    return pl.pallas_call(
        functools.partial(_nsa_window_kernel, tq, window),
        out_shape=jax.ShapeDtypeStruct((B, S, W), BF16),
        grid=(B, S // tq),
        in_specs=[pl.BlockSpec((None, tq, W), cur),
                  pl.BlockSpec((None, window, HEAD_DIM), prev),
                  pl.BlockSpec((None, tq, HEAD_DIM), cur),
                  pl.BlockSpec((None, window, HEAD_DIM), prev),
                  pl.BlockSpec((None, tq, HEAD_DIM), cur)],
        out_specs=pl.BlockSpec((None, tq, W), cur),
        compiler_params=_cparams(2), name="nsa_window_attention",
    )(q, k, k, v, v)


def _dilated_kernel(nb, patterns, q_ref, kp_ref, kc_ref, vp_ref, vc_ref, o_ref,
                    qf_ref, kf_ref, vf_ref, of_ref, lf_ref):
    i = pl.program_id(2)
    qf_ref[...] = q_ref[...].astype(F32)
    kf_ref[0:nb, :] = kp_ref[...].astype(F32)
    kf_ref[nb:2 * nb, :] = kc_ref[...].astype(F32)
    vf_ref[0:nb, :] = vp_ref[...].astype(F32)
    vf_ref[nb:2 * nb, :] = vc_ref[...].astype(F32)
    for pi, (window, dil) in enumerate(patterns):
        w = window // dil
        per_class = nb // dil
        inner = _band_bias(LANE, w)
        first = _band_bias(LANE, w, i * per_class)
        for r in range(dil):
            for j in range(per_class // LANE):
                q_lo = r + j * LANE * dil
                k_lo = nb + q_lo - w * dil
                rows_q = pl.ds(q_lo, LANE, stride=dil)
                rows_k = pl.ds(k_lo, w + LANE, stride=dil)
                o, lse = _band_block(qf_ref[rows_q, :].astype(BF16),
                                     kf_ref[rows_k, :].astype(BF16),
                                     vf_ref[rows_k, :].astype(BF16),
                                     first if j == 0 else inner)
                of_ref[pi, rows_q, :] = o
                lf_ref[pi, rows_q, :] = jnp.broadcast_to(lse, (LANE, LANE))
    n_pat = len(patterns)
    mx = lf_ref[0]
    for pi in range(1, n_pat):
        mx = jnp.maximum(mx, lf_ref[pi])
    es = [jnp.exp2(lf_ref[pi] - mx) for pi in range(n_pat)]
    den = es[0]
    num = es[0] * of_ref[0]
    for pi in range(1, n_pat):
        den = den + es[pi]
        num = num + es[pi] * of_ref[pi]
    o_ref[...] = (num / den).astype(o_ref.dtype)


def _dilated_attention(q, k, v, patterns, n_heads, nb=2048):
    B, S, W = q.shape
    for window, dil in patterns:
        assert window % dil == 0 and window // dil == LANE
        assert window <= nb and nb % (dil * LANE) == 0
    assert S % nb == 0
    prev = lambda b, h, i: (b, jnp.maximum(i - 1, 0), h)
    cur = lambda b, h, i: (b, i, h)
    blk = lambda index_map: pl.BlockSpec((None, nb, LANE), index_map)
    return pl.pallas_call(
        functools.partial(_dilated_kernel, nb, patterns),
        out_shape=jax.ShapeDtypeStruct((B, S, W), BF16),
        grid=(B, n_heads, S // nb),
        in_specs=[blk(cur), blk(prev), blk(cur), blk(prev), blk(cur)],
        out_specs=blk(cur),
        scratch_shapes=[pltpu.VMEM((nb, LANE), F32),
                        pltpu.VMEM((2 * nb, LANE), F32),
                        pltpu.VMEM((2 * nb, LANE), F32),
                        pltpu.VMEM((len(patterns), nb, LANE), F32),
                        pltpu.VMEM((len(patterns), nb, LANE), F32)],
        compiler_params=_cparams(3), name="dilated_attention",
    )(q, k, k, v, v)


def _gelu_tanh(x):
    return 0.5 * x * (1.0 + jnp.tanh(0.7978845608028654 * (x + 0.044715 * (x * x * x))))


def _compress_kernel(rope, t_ref, pe_ref, w1a_ref, w1b_ref, w2_ref, cos_ref, sin_ref, o_ref):
    t = t_ref[...].astype(F32)
    n = t.shape[0]
    first = jnp.dot((t + pe_ref[0:1, :]).astype(BF16), w1a_ref[...], preferred_element_type=F32)
    second = jnp.dot((t + pe_ref[1:2, :]).astype(BF16), w1b_ref[...], preferred_element_type=F32)
    hid = first + pltpu.roll(second, n - 1, 0)
    out = jnp.dot(_gelu_tanh(hid).astype(BF16), w2_ref[...], preferred_element_type=F32)
    if rope:
        out = out * cos_ref[...] + pltpu.roll(out, LANE // 2, 1) * sin_ref[...]
    o_ref[...] = out.astype(o_ref.dtype)


def _compress(t, pe2, w1a, w1b, w2, cos_c, sin_c, rope):
    B, n, K = t.shape
    whole = lambda b: (0, 0)
    per_b = lambda b: (b, 0, 0)
    return pl.pallas_call(
        functools.partial(_compress_kernel, rope),
        out_shape=jax.ShapeDtypeStruct((B, n, HEAD_DIM), BF16),
        grid=(B,),
        in_specs=[pl.BlockSpec((None, n, K), per_b),
                  pl.BlockSpec((2, K), whole),
                  pl.BlockSpec((K, HEAD_DIM), whole),
                  pl.BlockSpec((K, HEAD_DIM), whole),
                  pl.BlockSpec((HEAD_DIM, HEAD_DIM), whole),
                  pl.BlockSpec((None, n, HEAD_DIM), per_b),
                  pl.BlockSpec((None, n, HEAD_DIM), per_b)],
        out_specs=pl.BlockSpec((None, n, HEAD_DIM), per_b),
        compiler_params=_cparams(1), name="nsa_compress",
    )(t, pe2, w1a, w1b, w2, cos_c, sin_c)


def _nsa_cmp_block(tq, n_slc, n_sel, t0, q, kc, vc):
    n_cmp = kc.shape[0]
    per_blk = SLC_BLOCK // CMP_STRIDE
    slc_shift = n_slc.bit_length() - 1
    q = _stack_heads(q, NSA_HEADS)
    s = _dot_t(q, kc)
    R = NSA_HEADS * tq
    row = lax.broadcasted_iota(jnp.int32, (R, n_cmp), 0) & (tq - 1)
    col = lax.broadcasted_iota(jnp.int32, (1, n_cmp), 1)
    j_of = col & (n_slc - 1)
    r_of = col >> slc_shift
    cmp_end = (per_blk * j_of + r_of) * CMP_STRIDE + (CMP_LEN - 1)
    cmask = cmp_end <= row + t0
    s = jnp.where(cmask, s, NEG_INF)
    m = jnp.max(s, axis=-1, keepdims=True)
    e = jnp.exp2(s - m)
    inv = jnp.where(m > 0.5 * NEG_INF, 1.0 / jnp.sum(e, axis=-1, keepdims=True), 0.0)
    p = e * inv
    o = jnp.dot(p.astype(BF16), vc, preferred_element_type=F32)

    ph = p[0:tq]
    for h in range(1, NSA_HEADS):
        ph = ph + p[h * tq:(h + 1) * tq]
    groups = [ph[:, r * n_slc:(r + 1) * n_slc] for r in range(per_blk)]
    blk = lax.broadcasted_iota(jnp.int32, (tq, n_slc), 1)
    spill = jnp.where(blk == 0, 0.0, pltpu.roll(groups[per_blk - 1], 1, 1))
    imp = groups[0]
    for r in range(1, per_blk):
        imp = imp + groups[r]
    imp = imp + spill
    tpos = lax.broadcasted_iota(jnp.int32, (tq, n_slc), 0) + t0
    cur = tpos >> (SLC_BLOCK.bit_length() - 1)
    forced = (blk == 0) | (blk == cur) | (blk == cur - 1)
    valid = blk <= cur
    work = jnp.where(forced, BIG, jnp.where(valid, imp, -BIG))
    blk_f = blk.astype(F32)
    sel = jnp.zeros((tq, n_slc), F32)
    for _ in range(n_sel):
        mx = jnp.max(work, axis=-1, keepdims=True)
        first = jnp.min(jnp.where(work == mx, blk_f, float(n_slc)), axis=-1, keepdims=True)
        pick = blk_f == first
        sel = jnp.where(pick, 1.0, sel)
        work = jnp.where(pick, -jnp.inf, work)
    return o, jnp.where((sel > 0.0) & valid, 0.0, NEG_INF)


def _nsa_cmp_kernel(tq, n_sub, n_slc, n_sel, q_ref, kc_ref, vc_ref, o_ref, pen_ref):
    i = pl.program_id(1)
    per_blk = SLC_BLOCK // CMP_STRIDE

    def run(n_j):
        take = lambda ref: jnp.concatenate(
            [ref[r * n_slc:r * n_slc + n_j, :] for r in range(per_blk)], axis=0)
        kc, vc = (kc_ref[...], vc_ref[...]) if n_j == n_slc else (take(kc_ref), take(vc_ref))
        for u in range(n_sub):
            rows = slice(u * tq, (u + 1) * tq)
            o, pen = _nsa_cmp_block(tq, n_j, min(n_sel, n_j), (i * n_sub + u) * tq,
                                    q_ref[rows, :], kc, vc)
            for h in range(NSA_HEADS):
                o_ref[rows, h * LANE:(h + 1) * LANE] = o[h * tq:(h + 1) * tq].astype(o_ref.dtype)
            pen = pen.astype(pen_ref.dtype)
            for c in range(n_slc // LANE):
                if c < n_j // LANE:
                    pen_ref[c, rows, :] = pen[:, c * LANE:(c + 1) * LANE]
                else:
                    pen_ref[c, rows, :] = jnp.full((tq, LANE), NEG_INF, pen_ref.dtype)

    half = n_slc // 2
    if half % LANE == 0:
        early = (i + 1) * (tq * n_sub) <= half * SLC_BLOCK
        pl.when(early)(lambda: run(half))
        pl.when(jnp.logical_not(early))(lambda: run(n_slc))
    else:
        run(n_slc)


def _nsa_compressed(q, k_c, v_c, tq=128, n_sub=4):
    B, S, W = q.shape
    n_cmp = k_c.shape[1]
    n_slc = S // SLC_BLOCK
    n_sel = min(SLC_TOPK, n_slc)
    assert n_slc % LANE == 0 and n_slc & (n_slc - 1) == 0
    assert n_cmp == 4 * n_slc and tq & (tq - 1) == 0
    n_pc = n_slc // LANE
    bq = tq * n_sub
    return pl.pallas_call(
        functools.partial(_nsa_cmp_kernel, tq, n_sub, n_slc, n_sel),
        out_shape=[jax.ShapeDtypeStruct((B, S, W), BF16),
                   jax.ShapeDtypeStruct((B, n_pc, S, LANE), BF16)],
        grid=(B, S // bq),
        in_specs=[pl.BlockSpec((None, bq, W), lambda b, i: (b, i, 0)),
                  pl.BlockSpec((None, n_cmp, HEAD_DIM), lambda b, i: (b, 0, 0)),
                  pl.BlockSpec((None, n_cmp, HEAD_DIM), lambda b, i: (b, 0, 0))],
        out_specs=[pl.BlockSpec((None, bq, W), lambda b, i: (b, i, 0)),
                   pl.BlockSpec((None, n_pc, bq, LANE), lambda b, i: (b, 0, i, 0))],
        compiler_params=_cparams(2), name="nsa_compressed_topk",
    )(q, k_c, v_c)


def _nsa_sel_kernel(tq, tk, n_sub, q_ref, pen_ref, k_ref, v_ref, o_ref, qa_ref, st_ref):
    t0 = pl.program_id(1) * (tq * n_sub)
    R = NSA_HEADS * tq
    last = t0 // tk
    tiles_per_chunk = PEN_CHUNK // tk
    for u in range(n_sub):
        rows = slice(u * tq, (u + 1) * tq)
        q = _stack_heads(q_ref[rows, :], NSA_HEADS)
        for c in range(pen_ref.shape[0]):
            qa_ref[u, c, :, 0:LANE] = q
            qa_ref[u, c, :, LANE:2 * LANE] = jnp.concatenate([pen_ref[c, rows, :]] * NSA_HEADS,
                                                             axis=0)

    def scores(u, t):
        start = pl.multiple_of(t * tk, tk)
        return _dot_t(k_ref[pl.ds(start, tk), :], qa_ref[u, t // tiles_per_chunk])

    def diag_mask(u, st):
        key = lax.broadcasted_iota(jnp.int32, (tk, R), 0) + last * tk
        qry = (lax.broadcasted_iota(jnp.int32, (tk, R), 1) & (tq - 1)) + (t0 + u * tq)
        return jnp.where(key <= qry, st, NEG_INF)

    def values(u, t):
        return v_ref[pl.ds(pl.multiple_of(t * tk, tk), tk), :]

    outs = _flash_transposed(last, n_sub, scores, diag_mask, values, st_ref)
    for u in range(n_sub):
        for h in range(NSA_HEADS):
            o_ref[u * tq:(u + 1) * tq, h * LANE:(h + 1) * LANE] = (
                outs[u][:, h * tq:(h + 1) * tq].T.astype(o_ref.dtype))


def _nsa_selected(q, pen, k_aug, v, tq=128, n_sub=4, tk=1024):
    B, S, W = q.shape
    n_pc = pen.shape[1]
    bq = tq * n_sub
    assert tk % bq == 0 and PEN_CHUNK % tk == 0 and tq & (tq - 1) == 0
    return pl.pallas_call(
        functools.partial(_nsa_sel_kernel, tq, tk, n_sub),
        out_shape=jax.ShapeDtypeStruct((B, S, W), BF16),
        grid=(B, S // bq),
        in_specs=[pl.BlockSpec((None, bq, W), lambda b, i: (b, i, 0)),
                  pl.BlockSpec((None, n_pc, bq, LANE), lambda b, i: (b, 0, i, 0)),
                  pl.BlockSpec((None, S, 2 * HEAD_DIM), lambda b, i: (b, 0, 0)),
                  pl.BlockSpec((None, S, HEAD_DIM), lambda b, i: (b, 0, 0))],
        out_specs=pl.BlockSpec((None, bq, W), lambda b, i: (b, i, 0)),
        scratch_shapes=[pltpu.VMEM((n_sub, n_pc, NSA_HEADS * tq, 2 * HEAD_DIM), BF16),
                        pltpu.VMEM((n_sub, 2, tk, NSA_HEADS * tq), F32)],
        compiler_params=_cparams(2), name="nsa_selected_attention",
    )(q, pen, k_aug, v)


def _mix_out_kernel(oc_ref, os_ref, ow_ref, gate_ref, ob_ref, od_ref, w_ref, h_ref, g_ref,
                    o_ref, a_ref):
    half = a_ref.shape[0] // 2
    for part in range(2):
        rows = slice(part * half, (part + 1) * half)
        gate = jax.nn.sigmoid(gate_ref[rows, :])
        for h in range(NSA_HEADS):
            hs = slice(h * LANE, (h + 1) * LANE)
            a = (gate[:, 3 * h:3 * h + 1] * oc_ref[rows, hs].astype(F32)
                 + gate[:, 3 * h + 1:3 * h + 2] * os_ref[rows, hs].astype(F32)
                 + gate[:, 3 * h + 2:3 * h + 3] * ow_ref[rows, hs].astype(F32))
            a_ref[rows, hs] = a.astype(a_ref.dtype)
        off = NSA_HEADS * LANE
        a_ref[rows, off:off + MLA_HEADS * LANE] = ob_ref[rows, :]
        off += MLA_HEADS * LANE
        a_ref[rows, off:off + DIL_HEADS * LANE] = od_ref[rows, :]
        y = jnp.dot(a_ref[rows, :], w_ref[...], preferred_element_type=F32)
        o_ref[rows, :] = h_ref[rows, :] + _rms(y, g_ref[...])


def _mix_out(o_c, o_s, o_w, gate, o_b, o_d, w, h, g, bm=512):
    T, N = h.shape
    width = (NSA_HEADS + MLA_HEADS + DIL_HEADS) * LANE
    heads = (o_c, o_s, o_w, gate, o_b, o_d)
    row = lambda a: pl.BlockSpec((bm, a.shape[1]), lambda i: (i, 0))
    return pl.pallas_call(
        _mix_out_kernel,
        out_shape=jax.ShapeDtypeStruct((T, N), F32),
        grid=(T // bm,),
        in_specs=[row(a) for a in heads] + [
            pl.BlockSpec((width, N), lambda i: (0, 0), pipeline_mode=pl.Buffered(1)),
            row(h), pl.BlockSpec((1, N), lambda i: (0, 0))],
        out_specs=row(h),
        scratch_shapes=[pltpu.VMEM((bm, width), BF16)],
        compiler_params=_cparams(1), name="mix_out_proj",
    )(*heads, w, h, g)


def _xattn_kernel(n_heads, scale, h_ref, gq_ref, wq_ref, k_ref, v_ref, wo_ref, go_ref, o_ref,
                  a_ref):
    h = h_ref[...]
    x = _rms(h, gq_ref[...]).astype(BF16)
    q = (jnp.dot(x, wq_ref[...], preferred_element_type=F32) * scale).astype(BF16)
    for hd in range(n_heads):
        hs = slice(hd * LANE, (hd + 1) * LANE)
        s = _dot_t(q[:, hs], k_ref[:, hs])
        m = jnp.max(s, axis=-1, keepdims=True)
        p = jnp.exp2(s - m)
        l = jnp.sum(p, axis=-1, keepdims=True)
        o = jnp.dot(p.astype(BF16), v_ref[:, hs], preferred_element_type=F32) / l
        a_ref[:, hs] = o.astype(a_ref.dtype)
    y = jnp.dot(a_ref[...], wo_ref[...], preferred_element_type=F32)
    o_ref[...] = h + _rms(y, go_ref[...])


def _cross_attention(h, gq, wq, k, v, wo, go, n_heads, scale, bm=1024):
    B, S, D = h.shape
    M, W = k.shape[1:]
    const = lambda shape: pl.BlockSpec(shape, lambda b, i: (0, 0), pipeline_mode=pl.Buffered(1))
    return pl.pallas_call(
        functools.partial(_xattn_kernel, n_heads, scale),
        out_shape=jax.ShapeDtypeStruct((B, S, D), F32),
        grid=(B, S // bm),
        in_specs=[pl.BlockSpec((None, bm, D), lambda b, i: (b, i, 0)),
                  const((1, D)), const((D, W)),
                  pl.BlockSpec((None, M, W), lambda b, i: (b, 0, 0)),
                  pl.BlockSpec((None, M, W), lambda b, i: (b, 0, 0)),
                  const((W, D)), const((1, D))],
        out_specs=pl.BlockSpec((None, bm, D), lambda b, i: (b, i, 0)),
        scratch_shapes=[pltpu.VMEM((bm, W), BF16)],
        compiler_params=_cparams(2), name="cross_attention",
    )(h, gq, wq, k, v, wo, go)


def _rope_tables(pos, dim):
    inv = ROPE_THETA ** (-jnp.arange(0, dim, 2, dtype=F32) / dim)
    ang = pos.astype(F32)[..., None] * inv
    return jnp.cos(ang), jnp.sin(ang)


def _full_tables(pos):
    c, s = _rope_tables(pos, HEAD_DIM)
    return jnp.concatenate([c, c], -1), jnp.concatenate([-s, s], -1)


def _spread_rope64(t):
    z = jnp.zeros(t.shape[:-1] + (QK_ROPE // 2,), t.dtype)
    return jnp.concatenate([t[..., :QK_ROPE // 2], z, t[..., QK_ROPE // 2:], z], -1)


def _small_tables(pos):
    c, s = _rope_tables(pos, QK_ROPE)
    return _spread_rope64(jnp.concatenate([c, c], -1)), _spread_rope64(jnp.concatenate([-s, s], -1))


def _layer(h, mem, tabs, tabs_c, onehot, p):
    B, S, D = h.shape
    T = B * S
    dh = HEAD_DIM
    scale = dh ** -0.5 * LOG2E
    h2 = h.reshape(T, D)

    w_in = p["w_in"].T
    cuts = np.cumsum([NSA_HEADS * dh, 6 * dh, 3 * NSA_HEADS, Q_LORA, KV_LORA, QK_ROPE])
    w_q, w_kv, w_g, w_cq, w_ckv, w_kr, w_dil = jnp.split(w_in, cuts, axis=0)
    kv = [w_kv[k * dh:(k + 1) * dh] for k in range(6)]
    dw = DIL_HEADS * dh
    w_dq, w_dk, w_dv = w_dil[:dw], w_dil[dw:2 * dw], w_dil[2 * dw:]
    w_gpad = jnp.pad(w_g, ((0, LANE - w_g.shape[0]), (0, 0)))
    w_all = jnp.concatenate([w_q, kv[2], kv[4], w_dq, w_dk, _spread_rope64(w_kr.T).T,
                             kv[0], kv[1], kv[3], kv[5], w_gpad, w_cq, w_ckv, w_dv], 0).astype(BF16)
    one = lambda mode, n=1, s=1.0: ((mode,) * n, s)
    nsa_q, k_s, k_w, dq, dk, k_pe, k_cr, v_cr, v_s, v_w, gate, cq, ckv, dv = _proj(
        h2, p["g_mix_pre"], w_all,
        [one("rope", NSA_HEADS, scale), one("rope"), one("rope"), one("rope", DIL_HEADS, scale),
         one("rope", DIL_HEADS), one("rope_r"), one("none"), one("none"), one("none"), one("none"),
         one("none"), one("none", Q_LORA // LANE), one("none", KV_LORA // LANE),
         one("none", DIL_HEADS)],
        [BF16] * 10 + [F32, BF16, BF16, BF16], tabs=tabs, w_t=True, name="in_proj")

    n_chunk = S // CMP_STRIDE
    half = CMP_LEN // 2
    pe2 = p["cmp_pos_emb"].reshape(2, half * dh)
    cos_c, sin_c = tabs_c

    def compress(t, w1, w2, rope):
        out = _compress(t.reshape(B, n_chunk, CMP_STRIDE * dh), pe2,
                        w1[:half * dh].astype(BF16), w1[half * dh:].astype(BF16),
                        w2.astype(BF16), cos_c, sin_c, rope)
        return out.reshape(B, n_chunk // 4, 4, dh).transpose(0, 2, 1, 3).reshape(B, n_chunk, dh)

    k_c = compress(k_cr, p["w_cmp_k1"], p["w_cmp_k2"], True)
    v_c = compress(v_cr, p["w_cmp_v1"], p["w_cmp_v2"], False)
    q3 = nsa_q.reshape(B, S, NSA_HEADS * dh)
    o_cmp, pen = _nsa_compressed(q3, k_c, v_c)
    k_aug = jnp.concatenate([k_s.reshape(B, S, dh), onehot], axis=-1)
    o_sel = _nsa_selected(q3, pen, k_aug, v_s.reshape(B, S, dh))
    o_win = _nsa_window_attention(q3, k_w.reshape(B, S, dh), v_w.reshape(B, S, dh), NSA_WINDOW)

    dqk = QK_NOPE + QK_ROPE
    w_uq = p["w_uq"].reshape(Q_LORA, MLA_HEADS, dqk)
    w_uq = jnp.concatenate([w_uq[..., :QK_NOPE], _spread_rope64(w_uq[..., QK_NOPE:])], -1)
    w_uq = w_uq.reshape(Q_LORA, MLA_HEADS * 2 * dh).astype(BF16)
    (q_m,) = _proj(cq, p["g_q_lora"], w_uq, [(("none", "rope_r") * MLA_HEADS, dqk ** -0.5 * LOG2E)],
                   [BF16], tabs=tabs, bm=1024, name="mla_q_up")
    w_ukv = p["w_ukv"].reshape(KV_LORA, MLA_HEADS, 2 * dh)
    w_ukv = jnp.concatenate([w_ukv[..., :dh].reshape(KV_LORA, -1),
                             w_ukv[..., dh:].reshape(KV_LORA, -1)], 1).astype(BF16)
    k_m, v_m = _proj(ckv, p["g_kv_lora"], w_ukv,
                     [(("none", "extra") * MLA_HEADS, 1.0), (("none",) * MLA_HEADS, 1.0)],
                     [BF16, BF16], extra=k_pe, bm=1024, name="mla_kv_up")
    o_mla = _mla_attention(q_m.reshape(B, S, -1), k_m.reshape(B, S, -1),
                           v_m.reshape(B, S, -1), MLA_HEADS)

    o_dil = _dilated_attention(dq.reshape(B, S, dw), dk.reshape(B, S, dw), dv.reshape(B, S, dw),
                               DIL_PATTERNS, DIL_HEADS)

    h2 = _mix_out(o_cmp.reshape(T, -1), o_sel.reshape(T, -1), o_win.reshape(T, -1), gate,
                  o_mla.reshape(T, -1), o_dil.reshape(T, dw), p["w_out"].astype(BF16), h2,
                  p["g_mix_post"])

    xw = XATTN_HEADS * dh
    M = mem.shape[1]
    xk, xv = _proj(mem.reshape(B * M, D), p["g_mem_kv"], p["w_xkv"].astype(BF16),
                   [(("none",) * XATTN_HEADS, 1.0), (("none",) * XATTN_HEADS, 1.0)],
                   [BF16, BF16], bm=min(512, B * M), name="xattn_kv")
    h2 = _cross_attention(h2.reshape(B, S, D), p["g_mem_pre"], p["w_xq"].astype(BF16),
                          xk.reshape(B, M, xw), xv.reshape(B, M, xw), p["w_xo"].astype(BF16),
                          p["g_mem_post"], XATTN_HEADS, scale).reshape(T, D)

    up = _mlp_up(h2, p["g_mlp_pre"], p["w_up_all"], p["layer"])
    h2 = _out_proj(up, p["w_down_all"], h2, p["g_mlp_post"], layer=p["layer"], bm=1024, bk=1024,
                   name="mlp_down")
    return h2.reshape(B, S, D)


_LAYER_PARAMS = ("g_mix_pre", "w_in", "cmp_pos_emb", "w_cmp_k1", "w_cmp_k2", "w_cmp_v1", "w_cmp_v2",
                 "g_q_lora", "g_kv_lora", "w_uq", "w_ukv", "w_out", "g_mix_post", "g_mem_pre",
                 "g_mem_kv", "w_xq", "w_xkv", "w_xo", "g_mem_post", "g_mlp_pre", "w_up", "w_down",
                 "g_mlp_post")


def kernel(x, mem, positions, g_mix_pre, w_in, cmp_pos_emb, w_cmp_k1, w_cmp_k2, w_cmp_v1, w_cmp_v2, g_q_lora, g_kv_lora, w_uq, w_ukv, w_out, g_mix_post, g_mem_pre, g_mem_kv, w_xq, w_xkv, w_xo, g_mem_post, g_mlp_pre, w_up, w_down, g_mlp_post):
    stacked = dict(zip(_LAYER_PARAMS, (
        g_mix_pre, w_in, cmp_pos_emb, w_cmp_k1, w_cmp_k2, w_cmp_v1, w_cmp_v2, g_q_lora, g_kv_lora,
        w_uq, w_ukv, w_out, g_mix_post, g_mem_pre, g_mem_kv, w_xq, w_xkv, w_xo, g_mem_post,
        g_mlp_pre, w_up, w_down, g_mlp_post)))
    B, S, D = x.shape
    T = B * S
    assert S % PEN_CHUNK == 0
    cosf, sinf = _full_tables(positions)
    cosr, sinr = _small_tables(positions)
    tabs = tuple(t.reshape(T, LANE) for t in (cosf, sinf, cosr, sinr))
    n_chunk = S // CMP_STRIDE
    end = jnp.minimum(jnp.arange(n_chunk) * CMP_STRIDE + CMP_LEN - 1, S - 1)
    tabs_c = _full_tables(positions[:, end])
    blk = (jnp.arange(S) // SLC_BLOCK) % LANE
    onehot = jnp.broadcast_to((blk[:, None] == jnp.arange(LANE)[None, :]).astype(BF16)[None],
                              (B, S, LANE))
    h = x
    w_up_all, w_down_all = w_up.astype(BF16), w_down.astype(BF16)
    for layer in range(stacked["w_in"].shape[0]):
        p = {"layer": layer, "w_up_all": w_up_all, "w_down_all": w_down_all}
        for name, val in stacked.items():
            if name not in ("w_up", "w_down"):
                v = val[layer]
                p[name] = v[None, :] if name.startswith("g_") else v
        h = _layer(h, mem, tabs, tabs_c, onehot, p)
    return h
```
